```python
import math
import jax, jax.numpy as jnp
from jax import lax
import numpy as np

D_MODEL = 1024
BATCH = 8
SEQ = 2048
DEPTH = 4

CTX_LEN = 256
GRID_W = 64
EPS = 1e-6
NEG = -1e30
F32 = jnp.float32
N_MOD = 9

D_FF = 2816

D_RNN = 512
RNN_BLOCKS = 8
RNN_BLOCK = D_RNN // RNN_BLOCKS
CONV_W = 4
CONV_LEFT = 2
LRU_C = 8.0

ML_HEADS = 4
ML_DH = 128
ML_W = ML_HEADS * ML_DH
ML_CHUNK = 128

AT_HEADS = 8
AT_KV = 2
AT_DH = 64
AT_W = AT_HEADS * AT_DH
AT_KVW = AT_KV * AT_DH
WINDOW = 128
ATT_BLOCK = 128
ROPE_BASE = 10000.0

N_BRANCH = 3
BRANCH_W = 512

OFF_RG_X = 0
OFF_RG_G = OFF_RG_X + D_RNN
OFF_ML_Q = OFF_RG_G + D_RNN
OFF_ML_K = OFF_ML_Q + ML_W
OFF_ML_V = OFF_ML_K + ML_W
OFF_ML_O = OFF_ML_V + ML_W
OFF_ML_G = OFF_ML_O + ML_W
OFF_AT_Q = OFF_ML_G + 4 * ML_HEADS
OFF_AT_K = OFF_AT_Q + AT_W
OFF_AT_V = OFF_AT_K + AT_KVW
OFF_BR_G = OFF_AT_V + AT_KVW
D_IN = OFF_BR_G + N_BRANCH * D_MODEL

kernel_name = 'hybrid_rglru_mlstm_swa_prefix_dit'


def rmsnorm(x, g):
    xf = x.astype(F32)
    y = xf * lax.rsqrt(jnp.mean(xf * xf, axis=-1, keepdims=True) + EPS)
    return (y * g.astype(F32)).astype(x.dtype)


def ln_mod(x, g, shift, scale):
    return rmsnorm(x, g) * (1 + scale) + shift


def swiglu(h, w1, w3, w2):
    return (jax.nn.silu(h @ w1) * (h @ w3)) @ w2


def axial_angles(L):
    rows = L // GRID_W
    row = jnp.repeat(jnp.arange(rows), GRID_W).astype(F32)
    col = jnp.broadcast_to(jnp.arange(GRID_W), (rows, GRID_W)).reshape(-1).astype(F32)
    half = AT_DH // 2
    inv = ROPE_BASE ** (-jnp.arange(0, half, 2, dtype=F32) / half)
    ar = row[:, None] * inv
    ac = col[:, None] * inv
    ang = jnp.concatenate([ar, ar, ac, ac], axis=-1)
    return jnp.cos(ang), jnp.sin(ang)


def rotate_axial(x):
    a1, a2, b1, b2 = jnp.split(x, 4, axis=-1)
    return jnp.concatenate([-a2, a1, -b2, b1], axis=-1)


def apply_rope(x, cos, sin):
    return x * cos[:, None, :] + rotate_axial(x) * sin[:, None, :]


def dwconv(x, w, b):
    L = x.shape[1]
    xp = jnp.pad(x, ((0, 0), (CONV_LEFT, CONV_W - 1 - CONV_LEFT), (0, 0)))
    out = b
    for k in range(CONV_W):
        out = out + xp[:, k:k + L] * w[k]
    return out


def rglru_coeffs(u, wa, ba, wi, bi, lam):
    B, L, _ = u.shape
    ub = u.reshape(B, L, RNN_BLOCKS, RNN_BLOCK)
    r = jax.nn.sigmoid(jnp.einsum('blni,nij->blnj', ub, wa.astype(F32)).reshape(B, L, D_RNN) + ba.astype(F32))
    i = jax.nn.sigmoid(jnp.einsum('blni,nij->blnj', ub, wi.astype(F32)).reshape(B, L, D_RNN) + bi.astype(F32))
    log_a = -LRU_C * jax.nn.softplus(-lam.astype(F32)) * r
    a = jnp.exp(log_a)
    b = jnp.sqrt(-jnp.expm1(2.0 * log_a)) * (i * u)
    return a, b


def linear_scan(a, b, h0):
    b = b.at[:, 0].add(a[:, 0] * h0)
    def comb(lft, rgt):
        return lft[0] * rgt[0], rgt[0] * lft[1] + rgt[1]
    _, h = lax.associative_scan(comb, (a, b), axis=1)
    return h


def rglru_branch(pl, pc, conv_w, conv_b, wa, ba, wi, bi, lam, ctx_out):
    ul = dwconv(pl[..., OFF_RG_X:OFF_RG_G], conv_w, conv_b).astype(F32)
    uc = dwconv(pc[..., OFF_RG_X:OFF_RG_G], conv_w, conv_b).astype(F32)
    hl = 0.0
    hc = 0.0
    for d in range(2):
        al, bl = rglru_coeffs(ul, wa[d], ba[d], wi[d], bi[d], lam[d])
        ac, bc = rglru_coeffs(uc, wa[d], ba[d], wi[d], bi[d], lam[d])
        if d == 1:
            al, bl, ac, bc = (jnp.flip(z, axis=1) for z in (al, bl, ac, bc))
        h_c = linear_scan(ac, bc, jnp.zeros_like(ac[:, 0]))
        h_l = linear_scan(al, bl, h_c[:, -1])
        if d == 1:
            h_l, h_c = jnp.flip(h_l, axis=1), jnp.flip(h_c, axis=1)
        hl = hl + h_l
        hc = hc + h_c
    yl = (hl * jax.nn.gelu(pl[..., OFF_RG_G:OFF_ML_Q].astype(F32))).astype(pl.dtype)
    yc = (hc * jax.nn.gelu(pc[..., OFF_RG_G:OFF_ML_Q].astype(F32))).astype(pc.dtype) if ctx_out else None
    return yl, yc


def mlstm_scan(q, k, v, li, lf, state, with_out):
    B, H, T, Dh = q.shape
    nc = T // ML_CHUNK
    def chunks(z):
        return jnp.moveaxis(z.reshape(B, H, nc, ML_CHUNK, *z.shape[3:]), 2, 0)
    causal = jnp.tril(jnp.ones((ML_CHUNK, ML_CHUNK), dtype=bool))
    def step(carry, inp):
        C, n, m = carry
        qc, kc, vc, ic, fc = inp
        b = jnp.cumsum(fc, axis=-1)
        b_end = b[..., -1]
        w_end = b_end[..., None] - b + ic
        m_new = jnp.maximum(b_end + m, jnp.max(w_end, axis=-1))
        carry_decay = jnp.exp(b_end + m - m_new)
        w = jnp.exp(w_end - m_new[..., None])
        C_new = carry_decay[..., None, None] * C + jnp.einsum('bhs,bhsk,bhsv->bhkv', w, kc, vc)
        n_new = carry_decay[..., None] * n + jnp.einsum('bhs,bhsk->bhk', w, kc)
        if not with_out:
            return (C_new, n_new, m_new), None
        log_d = jnp.where(causal, b[..., :, None] - b[..., None, :] + ic[..., None, :], NEG)
        inter = b + m[..., None]
        m_t = jnp.maximum(inter, jnp.max(log_d, axis=-1))
        s = jnp.einsum('bhtd,bhsd->bhts', qc, kc) * jnp.exp(log_d - m_t[..., None])
        dec = jnp.exp(inter - m_t)
        num = jnp.einsum('bhts,bhsv->bhtv', s, vc) + dec[..., None] * jnp.einsum('bhtk,bhkv->bhtv', qc, C)
        den = jnp.sum(s, axis=-1) + dec * jnp.einsum('bhtk,bhk->bht', qc, n)
        h = num / jnp.maximum(jnp.abs(den), jnp.exp(-m_t))[..., None]
        return (C_new, n_new, m_new), h
    state, ys = lax.scan(step, state, (chunks(q), chunks(k), chunks(v), chunks(li), chunks(lf)))
    if not with_out:
        return state, None
    return state, jnp.moveaxis(ys, 0, 2).reshape(B, H, T, Dh)


def heads(z, h, d):
    return jnp.swapaxes(z.reshape(z.shape[0], z.shape[1], h, d), 1, 2)


def mlstm_branch(pl, pc, gate_b, norm_g, ctx_out):
    B = pl.shape[0]
    def prep(p):
        q = heads(p[..., OFF_ML_Q:OFF_ML_K], ML_HEADS, ML_DH).astype(F32)
        k = heads(p[..., OFF_ML_K:OFF_ML_V], ML_HEADS, ML_DH).astype(F32) * (ML_DH ** -0.5)
        v = heads(p[..., OFF_ML_V:OFF_ML_O], ML_HEADS, ML_DH).astype(F32)
        g = p[..., OFF_ML_G:OFF_AT_Q].astype(F32) + gate_b.astype(F32)
        g = jnp.moveaxis(g.reshape(p.shape[0], p.shape[1], 4, ML_HEADS), 1, -1)
        return q, k, v, g
    ql, kl, vl, gl = prep(pl)
    qc, kc, vc, gc = prep(pc)
    zero = (jnp.zeros((B, ML_HEADS, ML_DH, ML_DH), F32), jnp.zeros((B, ML_HEADS, ML_DH), F32),
            jnp.zeros((B, ML_HEADS), F32))
    hl = 0.0
    hc = 0.0
    for d in range(2):
        seq_l = (ql, kl, vl, gl[:, 2 * d], jax.nn.log_sigmoid(gl[:, 2 * d + 1]))
        seq_c = (qc, kc, vc, gc[:, 2 * d], jax.nn.log_sigmoid(gc[:, 2 * d + 1]))
        if d == 1:
            seq_l = tuple(jnp.flip(z, axis=2) for z in seq_l)
            seq_c = tuple(jnp.flip(z, axis=2) for z in seq_c)
        st, out_c = mlstm_scan(*seq_c, zero, ctx_out)
        _, out_l = mlstm_scan(*seq_l, st, True)
        if d == 1:
            out_l = jnp.flip(out_l, axis=2)
            out_c = jnp.flip(out_c, axis=2) if ctx_out else None
        hl = hl + out_l
        if ctx_out:
            hc = hc + out_c
    def finish(h, p):
        h = h * lax.rsqrt(jnp.mean(h * h, axis=-1, keepdims=True) + EPS)
        h = jnp.swapaxes(h, 1, 2).reshape(B, -1, ML_W) * norm_g.astype(F32)
        return (h * jax.nn.sigmoid(p[..., OFF_ML_O:OFF_ML_G].astype(F32))).astype(p.dtype)
    yl = finish(hl, pl)
    yc = finish(hc, pc) if ctx_out else None
    return yl, yc


def attention_branch(pl, pc, qn_g, kn_g, sink, cos, sin, ctx_out):
    B, L, _ = pl.shape
    Lc = pc.shape[1]
    G = AT_HEADS // AT_KV
    nb = L // ATT_BLOCK
    nk = 3 * ATT_BLOCK
    scale = AT_DH ** -0.5
    def qkv(p):
        q = rmsnorm(p[..., OFF_AT_Q:OFF_AT_K].reshape(B, -1, AT_HEADS, AT_DH), qn_g).astype(F32)
        k = rmsnorm(p[..., OFF_AT_K:OFF_AT_V].reshape(B, -1, AT_KV, AT_DH), kn_g).astype(F32)
        v = p[..., OFF_AT_V:OFF_BR_G].reshape(B, -1, AT_KV, AT_DH).astype(F32)
        return q, k, v
    ql, kl, vl = qkv(pl)
    qc, kc, vc = qkv(pc)
    ql = apply_rope(ql, cos, sin)
    kl = apply_rope(kl, cos, sin)
    sink_hg = sink.astype(F32).reshape(AT_KV, G)
    qb = ql.reshape(B, nb, ATT_BLOCK, AT_KV, G, AT_DH) * scale
    def band(z):
        zp = jnp.pad(z, ((0, 0), (ATT_BLOCK, ATT_BLOCK), (0, 0), (0, 0)))
        zp = zp.reshape(B, nb + 2, ATT_BLOCK, AT_KV, AT_DH)
        return jnp.concatenate([zp[:, :-2], zp[:, 1:-1], zp[:, 2:]], axis=2)
    kband, vband = band(kl), band(vl)
    r = jnp.arange(nk)
    i = jnp.arange(ATT_BLOCK)
    in_win = jnp.abs(r[None, :] - ATT_BLOCK - i[:, None]) <= WINDOW
    kblk = jnp.arange(nb)[:, None] - 1 + r[None, :] // ATT_BLOCK
    in_rng = (kblk >= 0) & (kblk < nb)
    mask = in_win[None] & in_rng[:, None, :]
    s_lat = jnp.einsum('bnqhgd,bnkhd->bnhgqk', qb, kband)
    s_lat = jnp.where(mask[None, :, None, None], s_lat, NEG)
    s_ctx = jnp.einsum('bnqhgd,bchd->bnhgqc', qb, kc)
    s_snk = jnp.broadcast_to(sink_hg[None, None, :, :, None, None], s_lat.shape[:-1] + (1,))
    p = jax.nn.softmax(jnp.concatenate([s_lat, s_ctx, s_snk], axis=-1), axis=-1)
    o = (jnp.einsum('bnhgqk,bnkhd->bnqhgd', p[..., :nk], vband)
         + jnp.einsum('bnhgqc,bchd->bnqhgd', p[..., nk:nk + Lc], vc))
    yl = o.reshape(B, L, AT_W).astype(pl.dtype)
    yc = None
    if ctx_out:
        qcg = qc.reshape(B, Lc, AT_KV, G, AT_DH) * scale
        s = jnp.einsum('bqhgd,bchd->bhgqc', qcg, kc)
        s_snk_c = jnp.broadcast_to(sink_hg[None, :, :, None, None], s.shape[:-1] + (1,))
        pc_ = jax.nn.softmax(jnp.concatenate([s, s_snk_c], axis=-1), axis=-1)
        oc = jnp.einsum('bhgqc,bchd->bqhgd', pc_[..., :Lc], vc)
        yc = oc.reshape(B, Lc, AT_W).astype(pc.dtype)
    return yl, yc


def token_mixer(hl, hc, w_in, rg_conv_w, rg_conv_b, rg_wa, rg_ba, rg_wi, rg_bi, rg_lam,
                ml_gate_b, ml_norm_g, at_qn_g, at_kn_g, at_sink, w_branch, w_out, cos, sin, ctx_out):
    pl = hl @ w_in
    pc = hc @ w_in
    ya_l, ya_c = rglru_branch(pl, pc, rg_conv_w, rg_conv_b, rg_wa, rg_ba, rg_wi, rg_bi, rg_lam, ctx_out)
    yb_l, yb_c = mlstm_branch(pl, pc, ml_gate_b, ml_norm_g, ctx_out)
    yc_l, yc_c = attention_branch(pl, pc, at_qn_g, at_kn_g, at_sink, cos, sin, ctx_out)
    def merge(p, ya, yb, yc):
        g = jax.nn.sigmoid(p[..., OFF_BR_G:]).reshape(p.shape[:-1] + (N_BRANCH, D_MODEL))
        m = (g[..., 0, :] * (ya @ w_branch[0]) + g[..., 1, :] * (yb @ w_branch[1])
             + g[..., 2, :] * (yc @ w_branch[2]))
        return m @ w_out
    yl = merge(pl, ya_l, yb_l, yc_l)
    yc = merge(pc, ya_c, yb_c, yc_c) if ctx_out else None
    return yl, yc


def setup_inputs(seed: int = 0) -> dict:
    key = jax.random.key(seed)
    ks = jax.random.split(key, 32)
    def nrm(k, shape, sc):
        return jax.random.normal(k, shape, F32) * sc
    x = nrm(ks[0], (BATCH, SEQ, D_MODEL), 1.0)
    c = nrm(ks[1], (BATCH, D_MODEL), 1.0)
    ctx = nrm(ks[2], (BATCH, CTX_LEN, D_MODEL), 1.0)
    c_ctx = nrm(ks[3], (D_MODEL,), 1.0)
    ada_w = nrm(ks[4], (DEPTH, D_MODEL, N_MOD * D_MODEL), 0.5 * D_MODEL ** -0.5)
    ada_b = nrm(ks[5], (DEPTH, N_MOD * D_MODEL), 0.02)
    norm_g = 1.0 + nrm(ks[6], (DEPTH, 3, D_MODEL), 0.02)
    ffn_w1 = nrm(ks[7], (DEPTH, 2, D_MODEL, D_FF), D_MODEL ** -0.5)
    ffn_w3 = nrm(ks[8], (DEPTH, 2, D_MODEL, D_FF), D_MODEL ** -0.5)
    ffn_w2 = nrm(ks[9], (DEPTH, 2, D_FF, D_MODEL), D_FF ** -0.5)
    w_in = nrm(ks[10], (DEPTH, D_MODEL, D_IN), D_MODEL ** -0.5)
    rg_conv_w = nrm(ks[11], (DEPTH, CONV_W, D_RNN), CONV_W ** -0.5)
    rg_conv_b = nrm(ks[12], (DEPTH, D_RNN), 0.01)
    rg_wa = nrm(ks[13], (DEPTH, 2, RNN_BLOCKS, RNN_BLOCK, RNN_BLOCK), RNN_BLOCK ** -0.5)
    rg_ba = nrm(ks[14], (DEPTH, 2, D_RNN), 0.01)
    rg_wi = nrm(ks[15], (DEPTH, 2, RNN_BLOCKS, RNN_BLOCK, RNN_BLOCK), RNN_BLOCK ** -0.5)
    rg_bi = nrm(ks[16], (DEPTH, 2, D_RNN), 0.01)
    a0 = jax.random.uniform(ks[17], (DEPTH, 2, D_RNN), F32, 0.9, 0.999)
    pa = a0 ** (1.0 / LRU_C)
    rg_lam = jnp.log(pa) - jnp.log1p(-pa)
    ib = nrm(ks[18], (DEPTH, 2, 1, ML_HEADS), 0.1)
    fb = jnp.linspace(3.0, 6.0, ML_HEADS, dtype=F32) + nrm(ks[19], (DEPTH, 2, 1, ML_HEADS), 0.01)
    ml_gate_b = jnp.concatenate([ib, fb], axis=2).reshape(DEPTH, 4 * ML_HEADS)
    ml_norm_g = 1.0 + nrm(ks[20], (DEPTH, ML_W), 0.02)
    at_qn_g = 1.0 + nrm(ks[21], (DEPTH, AT_DH), 0.02)
    at_kn_g = 1.0 + nrm(ks[22], (DEPTH, AT_DH), 0.02)
    at_sink = nrm(ks[23], (DEPTH, AT_HEADS), 0.5)
    w_branch = nrm(ks[24], (DEPTH, N_BRANCH, BRANCH_W, D_MODEL), BRANCH_W ** -0.5)
    w_out = nrm(ks[25], (DEPTH, D_MODEL, D_MODEL), D_MODEL ** -0.5)
    return {'x': x, 'c': c, 'ctx': ctx, 'c_ctx': c_ctx, 'ada_w': ada_w, 'ada_b': ada_b,
            'norm_g': norm_g, 'ffn_w1': ffn_w1, 'ffn_w3': ffn_w3, 'ffn_w2': ffn_w2, 'w_in': w_in,
            'rg_conv_w': rg_conv_w, 'rg_conv_b': rg_conv_b, 'rg_wa': rg_wa, 'rg_ba': rg_ba,
            'rg_wi': rg_wi, 'rg_bi': rg_bi, 'rg_lam': rg_lam, 'ml_gate_b': ml_gate_b,
            'ml_norm_g': ml_norm_g, 'at_qn_g': at_qn_g, 'at_kn_g': at_kn_g, 'at_sink': at_sink,
            'w_branch': w_branch, 'w_out': w_out}


def reference(x, c, ctx, c_ctx, ada_w, ada_b, norm_g, ffn_w1, ffn_w3, ffn_w2, w_in,
              rg_conv_w, rg_conv_b, rg_wa, rg_ba, rg_wi, rg_bi, rg_lam, ml_gate_b,
              ml_norm_g, at_qn_g, at_kn_g, at_sink, w_branch, w_out):
    B, L, _ = x.shape
    cos, sin = axial_angles(L)
    sc = jax.nn.silu(c)
    scc = jax.nn.silu(c_ctx)
    xl, xc = x, ctx
    for l in range(DEPTH):
        ctx_out = l < DEPTH - 1
        ml = (sc @ ada_w[l] + ada_b[l]).reshape(B, 1, N_MOD, D_MODEL)
        mc = (scc @ ada_w[l] + ada_b[l]).reshape(N_MOD, D_MODEL)
        xl = xl + 0.5 * ml[:, :, 2] * swiglu(ln_mod(xl, norm_g[l, 0], ml[:, :, 0], ml[:, :, 1]),
                                             ffn_w1[l, 0], ffn_w3[l, 0], ffn_w2[l, 0])
        xc = xc + 0.5 * mc[2] * swiglu(ln_mod(xc, norm_g[l, 0], mc[0], mc[1]),
                                       ffn_w1[l, 0], ffn_w3[l, 0], ffn_w2[l, 0])
        yl, yc = token_mixer(ln_mod(xl, norm_g[l, 1], ml[:, :, 3], ml[:, :, 4]),
                             ln_mod(xc, norm_g[l, 1], mc[3], mc[4]),
                             w_in[l], rg_conv_w[l], rg_conv_b[l], rg_wa[l], rg_ba[l], rg_wi[l],
                             rg_bi[l], rg_lam[l], ml_gate_b[l], ml_norm_g[l], at_qn_g[l],
                             at_kn_g[l], at_sink[l], w_branch[l], w_out[l], cos, sin, ctx_out)
        xl = xl + ml[:, :, 5] * yl
        xl = xl + 0.5 * ml[:, :, 8] * swiglu(ln_mod(xl, norm_g[l, 2], ml[:, :, 6], ml[:, :, 7]),
                                             ffn_w1[l, 1], ffn_w3[l, 1], ffn_w2[l, 1])
        if ctx_out:
            xc = xc + mc[5] * yc
            xc = xc + 0.5 * mc[8] * swiglu(ln_mod(xc, norm_g[l, 2], mc[6], mc[7]),
                                           ffn_w1[l, 1], ffn_w3[l, 1], ffn_w2[l, 1])
    return xl
```

```python
import functools

import jax
import jax.numpy as jnp
from jax import lax
from jax.experimental import pallas as pl
from jax.experimental.pallas import tpu as pltpu

F32 = jnp.float32
BF16 = jnp.bfloat16

EPS = 1e-6
NEG = -1e30
N_MOD = 9
GRID_W = 64
ROPE_BASE = 10000.0

D_RNN = 512
RNN_BLOCKS = 8
RNN_BLOCK = D_RNN // RNN_BLOCKS
CONV_W = 4
CONV_LEFT = 2
LRU_C = 8.0

ML_HEADS = 4
ML_DH = 128
ML_W = ML_HEADS * ML_DH
ML_CHUNK = 128

AT_HEADS = 8
AT_KV = 2
AT_DH = 64
AT_G = AT_HEADS // AT_KV
AT_W = AT_HEADS * AT_DH
AT_KVW = AT_KV * AT_DH
ATT_BLOCK = 128

N_BRANCH = 3
BRANCH_W = 512

LANES = 128
SUBLANES = 8
VMEM_LIMIT = 56 * 1024 * 1024

P_RGX = 0
P_RGG = P_RGX + D_RNN
P_MQ = P_RGG + D_RNN
P_MK = P_MQ + ML_W
P_MV = P_MK + ML_W
P_MO = P_MV + ML_W
P_AQ = P_MO + ML_W
P_AK = P_AQ + AT_W
P_AV = P_AK + 2 * AT_KVW
P_BR = P_AV + 2 * AT_KVW


def _cparams(sem):
    return pltpu.CompilerParams(dimension_semantics=sem, vmem_limit_bytes=VMEM_LIMIT)


def _const_spec(shape):
    nd = len(shape)
    return pl.BlockSpec(shape, lambda *_: (0,) * nd, pipeline_mode=pl.Buffered(1))


def _pick_tile(total, candidates):
    for c in candidates:
        if total % c == 0:
            return c
    raise ValueError(f"no tile for {total}")


def _sigmoid(x):
    return jax.nn.sigmoid(x)


def _ln_mod(x, g, shift, scale):
    ms = jnp.mean(x * x, axis=-1, keepdims=True)
    return x * lax.rsqrt(ms + EPS) * g * (1.0 + scale) + shift


def _row_mod(ml_ref, mc_ref, is_ctx, i):
    return jnp.where(is_ctx, mc_ref[i:i + 1, :], ml_ref[i:i + 1, :])


def _ada_kernel(cc_ref, w_ref, b_ref, o_ref):
    cc = cc_ref[...]
    s = cc * _sigmoid(cc)
    o_ref[...] = jnp.dot(s.astype(BF16), w_ref[...].astype(BF16),
                         preferred_element_type=F32) + b_ref[...]


def _ada_call(cc, ada_w, ada_b):
    depth, d, nout = ada_w.shape
    rows = cc.shape[0]
    tn = _pick_tile(nout, (1536, 1024, 512, 256, 128))
    return pl.pallas_call(
        _ada_kernel,
        grid=(depth, nout // tn),
        in_specs=[
            pl.BlockSpec((rows, d), lambda l, j: (0, 0)),
            pl.BlockSpec((None, d, tn), lambda l, j: (l, 0, j)),
            pl.BlockSpec((None, 1, tn), lambda l, j: (l, 0, j)),
        ],
        out_specs=pl.BlockSpec((None, rows, tn), lambda l, j: (l, 0, j)),
        out_shape=jax.ShapeDtypeStruct((depth, rows, nout), F32),
        compiler_params=_cparams(("arbitrary", "arbitrary")),
        name="ada_mod",
    )(cc, ada_w, ada_b.reshape(depth, 1, nout))


def _ffn_kernel(x_ref, ml_ref, mc_ref, g_ref, w1_ref, w3_ref, w2_ref, o_ref, act_ref,
                *, base, seq, tm, ck):
    t0 = pl.program_id(1) * tm
    rows = t0 + lax.broadcasted_iota(jnp.int32, (tm, 1), 0)
    is_ctx = rows >= seq
    x = x_ref[...]
    h = _ln_mod(x, g_ref[...], _row_mod(ml_ref, mc_ref, is_ctx, base),
                _row_mod(ml_ref, mc_ref, is_ctx, base + 1)).astype(BF16)
    dff = w1_ref.shape[1]
    for j in range(dff // ck):
        a = jnp.dot(h, w1_ref[:, j * ck:(j + 1) * ck], preferred_element_type=F32)
        b = jnp.dot(h, w3_ref[:, j * ck:(j + 1) * ck], preferred_element_type=F32)
        act_ref[:, j * ck:(j + 1) * ck] = (a * _sigmoid(a) * b).astype(BF16)
    y = jnp.dot(act_ref[...], w2_ref[...], preferred_element_type=F32)
    o_ref[...] = x + 0.5 * _row_mod(ml_ref, mc_ref, is_ctx, base + 2) * y


def _ffn_call(x, ml, mc, g, w1, w3, w2, *, base, seq, out_rows):
    bsz, t, d = x.shape
    dff = w1.shape[1]
    tm = _pick_tile(t, (768, 384, 128))
    ck = _pick_tile(dff, (256, 128))
    kern = functools.partial(_ffn_kernel, base=base, seq=seq, tm=tm, ck=ck)
    return pl.pallas_call(
        kern,
        grid=(bsz, pl.cdiv(out_rows, tm)),
        in_specs=[
            pl.BlockSpec((None, tm, d), lambda b, i: (b, i, 0)),
            pl.BlockSpec((None, N_MOD, d), lambda b, i: (b, 0, 0)),
            _const_spec((N_MOD, d)),
            _const_spec((1, d)),
            _const_spec((d, dff)),
            _const_spec((d, dff)),
            _const_spec((dff, d)),
        ],
        out_specs=pl.BlockSpec((None, tm, d), lambda b, i: (b, i, 0)),
        out_shape=jax.ShapeDtypeStruct((bsz, out_rows, d), F32),
        scratch_shapes=[pltpu.VMEM((tm, dff), BF16)],
        compiler_params=_cparams(("arbitrary", "arbitrary")),
        name="ffn",
    )(x, ml, mc, g, w1, w3, w2)


def _gelu_tanh(x):
    return 0.5 * x * (1.0 + jnp.tanh(0.7978845608028654 * (x + 0.044715 * (x * x * x))))


def _head_norm(x, gmat, g):
    sq = x * x
    hi = sq.astype(BF16)
    lo = (sq - hi.astype(F32)).astype(BF16)
    ms = (jnp.dot(hi, gmat, preferred_element_type=F32)
          + jnp.dot(lo, gmat, preferred_element_type=F32))
    return x * lax.rsqrt(ms + EPS) * g


def _rope(x, cos, sin_a, sin_b):
    parts = []
    for j in range(x.shape[1] // LANES):
        xj = x[:, j * LANES:(j + 1) * LANES]
        parts.append(xj * cos + pltpu.roll(xj, LANES - 16, 1) * sin_a
                     + pltpu.roll(xj, 16, 1) * sin_b)
    return jnp.concatenate(parts, axis=1)


def _proj_kernel(x_ref, ml_ref, mc_ref, g_ref, w_ref, gmat_ref, qg_ref, kg_ref, gb_ref,
                 cos_ref, sa_ref, sb_ref,
                 rgx_o, rgg_o, mq_o, mk_o, mv_o, mo_o, aq_o, ak_o, av_o, br_o, mg_o,
                 *, seq, tm, d_model):
    t0 = pl.program_id(1) * tm
    rows = t0 + lax.broadcasted_iota(jnp.int32, (tm, 1), 0)
    is_ctx = rows >= seq
    h = _ln_mod(x_ref[...], g_ref[...], _row_mod(ml_ref, mc_ref, is_ctx, 3),
                _row_mod(ml_ref, mc_ref, is_ctx, 4)).astype(BF16)

    def seg(c0, w):
        return jnp.dot(h, w_ref[:, c0:c0 + w], preferred_element_type=F32)

    rgx_o[...] = seg(P_RGX, D_RNN)
    rgg_o[...] = _gelu_tanh(seg(P_RGG, D_RNN)).astype(BF16)
    mq_o[...] = seg(P_MQ, ML_W).astype(BF16)
    mk_o[...] = (seg(P_MK, ML_W) * (ML_DH ** -0.5)).astype(BF16)
    mv_o[...] = seg(P_MV, ML_W).astype(BF16)
    mo_o[...] = _sigmoid(seg(P_MO, ML_W)).astype(BF16)
    cos, sin_a, sin_b = cos_ref[...], sa_ref[...], sb_ref[...]
    gmat = gmat_ref[...]
    q = _head_norm(seg(P_AQ, AT_W), gmat, qg_ref[...])
    aq_o[...] = (_rope(q, cos, sin_a, sin_b) * (AT_DH ** -0.5)).astype(BF16)
    k = _head_norm(seg(P_AK, 2 * AT_KVW), gmat[:2 * AT_KVW, :2 * AT_KVW], kg_ref[...])
    ak_o[...] = _rope(k, cos, sin_a, sin_b).astype(BF16)
    av_o[...] = seg(P_AV, 2 * AT_KVW).astype(BF16)
    nbr = N_BRANCH * d_model
    for c0 in range(0, nbr, 512):
        br_o[:, c0:c0 + 512] = _sigmoid(seg(P_BR + c0, 512)).astype(BF16)
    mg_o[...] = seg(P_BR + nbr, LANES) + gb_ref[...]


def _proj_call(x, ml, mc, g, w, gmat, qg, kg, gb, cos, sin_a, sin_b, *, seq):
    bsz, t, d = x.shape
    ncol = w.shape[1]
    tm = _pick_tile(t, (384, 128))
    kern = functools.partial(_proj_kernel, seq=seq, tm=tm, d_model=d)
    widths = [(D_RNN, F32), (D_RNN, BF16), (ML_W, BF16), (ML_W, BF16), (ML_W, BF16),
              (ML_W, BF16), (AT_W, BF16), (2 * AT_KVW, BF16), (2 * AT_KVW, BF16),
              (N_BRANCH * d, BF16), (LANES, F32)]
    tab_spec = pl.BlockSpec((tm, LANES), lambda b, i: (i, 0))
    return pl.pallas_call(
        kern,
        grid=(bsz, t // tm),
        in_specs=[
            pl.BlockSpec((None, tm, d), lambda b, i: (b, i, 0)),
            pl.BlockSpec((None, N_MOD, d), lambda b, i: (b, 0, 0)),
            _const_spec((N_MOD, d)),
            _const_spec((1, d)),
            _const_spec((d, ncol)),
            _const_spec((AT_W, AT_W)),
            _const_spec((1, AT_W)),
            _const_spec((1, 2 * AT_KVW)),
            _const_spec((1, LANES)),
            tab_spec, tab_spec, tab_spec,
        ],
        out_specs=[pl.BlockSpec((None, tm, wd), lambda b, i: (b, i, 0)) for wd, _ in widths],
        out_shape=[jax.ShapeDtypeStruct((bsz, t, wd), dt) for wd, dt in widths],
        compiler_params=_cparams(("arbitrary", "arbitrary")),
        name="in_proj",
    )(x, ml, mc, g, w, gmat, qg, kg, gb, cos, sin_a, sin_b)


def _scan_chunk(s, rev, nl, nc):
    if rev:
        return jnp.where(s < nc, nl + nc - 1 - s, nl - 1 - (s - nc))
    return jnp.where(s < nc, nl + s, s - nc)


def _rglru_kernel(*refs, rev, nl, nc, tc, bsz):
    if rev:
        (x_ref, xp_ref, xn_ref, cw_ref, cb_ref, w_ref, bias_ref, clam_ref, hf_ref, gg_ref,
         o_ref, ext, a_s, b_s, h_s, hstate) = refs
    else:
        (x_ref, xp_ref, xn_ref, cw_ref, cb_ref, w_ref, bias_ref, clam_ref,
         o_ref, ext, a_s, b_s, h_s, hstate) = refs
    s = pl.program_id(0)
    chunk = _scan_chunk(s, rev, nl, nc)
    first = jnp.logical_or(chunk == 0, chunk == nl)
    last = jnp.logical_or(chunk == nl - 1, chunk == nl + nc - 1)
    halo = SUBLANES
    nlb = a_s.shape[0]

    @pl.when(s == 0)
    def _():
        hstate[...] = jnp.zeros_like(hstate)

    cw = cw_ref[...]
    cb = cb_ref[...]
    clam = clam_ref[...]
    bias = bias_ref[...]

    def gates(b, carry):
        ext[halo:halo + tc, :] = x_ref[b]
        ext[0:halo, :] = jnp.where(first, 0.0, xp_ref[b])
        ext[halo + tc:halo + tc + halo, :] = jnp.where(last, 0.0, xn_ref[b])
        u = cb
        for k in range(CONV_W):
            o = halo - CONV_LEFT + k
            u = u + ext[o:o + tc, :] * cw[k:k + 1, :]
        z = jnp.dot(u.astype(BF16), w_ref[...], preferred_element_type=F32) + bias
        r = _sigmoid(z[:, :D_RNN])
        i = _sigmoid(z[:, D_RNN:])
        a = jnp.exp(clam * r)
        bb = jnp.sqrt(1.0 - a * a) * (i * u)
        r0 = pl.multiple_of(b * tc, tc)
        for j in range(nlb):
            a_s[j, pl.ds(r0, tc), :] = a[:, j * LANES:(j + 1) * LANES]
            b_s[j, pl.ds(r0, tc), :] = bb[:, j * LANES:(j + 1) * LANES]
        return carry

    lax.fori_loop(0, bsz, gates, 0)

    def step(k, hs):
        t = (tc - 1 - k) if rev else k
        out = []
        for j in range(nlb):
            h = (a_s[j, pl.ds(t, bsz, stride=tc), :] * hs[j]
                 + b_s[j, pl.ds(t, bsz, stride=tc), :])
            h_s[j, pl.ds(t, bsz, stride=tc), :] = h
            out.append(h)
        return tuple(out)

    hs = lax.fori_loop(0, tc, step, tuple(hstate[j] for j in range(nlb)), unroll=8)
    for j in range(nlb):
        hstate[j] = hs[j]

    def finish(b, carry):
        r0 = pl.multiple_of(b * tc, tc)
        h = jnp.concatenate([h_s[j, pl.ds(r0, tc), :] for j in range(nlb)], axis=1)
        if rev:
            h = (hf_ref[b] + h) * gg_ref[b].astype(F32)
        o_ref[b] = h.astype(o_ref.dtype)
        return carry

    lax.fori_loop(0, bsz, finish, 0)


def _rglru_call(rgx, cw, cb, w, bias, clam, hf, gg, *, rev, seq):
    bsz, t, c = rgx.shape
    tc = _pick_tile(t - seq, (256, 128))
    assert seq % tc == 0
    nl, nc = seq // tc, (t - seq) // tc
    hb = tc // SUBLANES
    nblk8 = t // SUBLANES
    chunk = functools.partial(_scan_chunk, rev=rev, nl=nl, nc=nc)
    main = pl.BlockSpec((bsz, tc, c), lambda s: (0, chunk(s), 0))
    in_specs = [
        main,
        pl.BlockSpec((bsz, SUBLANES, c), lambda s: (0, jnp.maximum(chunk(s) * hb - 1, 0), 0)),
        pl.BlockSpec((bsz, SUBLANES, c),
                     lambda s: (0, jnp.minimum((chunk(s) + 1) * hb, nblk8 - 1), 0)),
        _const_spec((CONV_W, c)),
        _const_spec((1, c)),
        _const_spec((c, 2 * c)),
        _const_spec((1, 2 * c)),
        _const_spec((1, c)),
    ]
    args = [rgx, rgx, rgx, cw, cb, w, bias, clam]
    if rev:
        in_specs += [main, main]
        args += [hf, gg]
    kern = functools.partial(_rglru_kernel, rev=rev, nl=nl, nc=nc, tc=tc, bsz=bsz)
    return pl.pallas_call(
        kern,
        grid=(nl + nc,),
        in_specs=in_specs,
        out_specs=main,
        out_shape=jax.ShapeDtypeStruct((bsz, t, c), BF16 if rev else F32),
        scratch_shapes=[
            pltpu.VMEM((tc + 2 * SUBLANES, c), F32),
            pltpu.VMEM((c // LANES, bsz * tc, LANES), F32),
            pltpu.VMEM((c // LANES, bsz * tc, LANES), F32),
            pltpu.VMEM((c // LANES, bsz * tc, LANES), F32),
            pltpu.VMEM((c // LANES, bsz, LANES), F32),
        ],
        compiler_params=_cparams(("arbitrary",)),
        name="rglru_bwd" if rev else "rglru_fwd",
    )(*args)


def _log_sigmoid(x):
    return -(jnp.maximum(-x, 0.0) + jnp.log1p(jnp.exp(-jnp.abs(x))))


def _lane_cumsum(x, rev):
    lane = lax.broadcasted_iota(jnp.int32, x.shape, 1)
    n = x.shape[1]
    sh = 1
    while sh < n:
        if rev:
            x = x + jnp.where(lane < n - sh, pltpu.roll(x, n - sh, 1), 0.0)
        else:
            x = x + jnp.where(lane >= sh, pltpu.roll(x, sh, 1), 0.0)
        sh *= 2
    return x


def _mlstm_unit(q, kt, v_ext, li, lf, cb, c_old, m_old, rev):
    L = q.shape[0]
    ti = lax.broadcasted_iota(jnp.int32, (L, L), 0)
    si = lax.broadcasted_iota(jnp.int32, (L, L), 1)
    tri = (si >= ti) if rev else (si <= ti)
    cb_col = jnp.sum(jnp.where(si == ti, cb, 0.0), axis=1, keepdims=True)
    tot = jnp.sum(lf, axis=1, keepdims=True)
    w_end = tot - cb + li
    m_new = jnp.maximum(tot + m_old, jnp.max(w_end, axis=1, keepdims=True))
    carry_decay = jnp.exp(tot + m_old - m_new)
    w = jnp.exp(w_end - m_new)
    ktw = (kt.astype(F32) * w).astype(BF16)
    c_new = carry_decay * c_old + jnp.dot(ktw, v_ext, preferred_element_type=F32)
    log_d = jnp.where(tri, cb_col - cb + li, NEG)
    inter = cb_col + m_old
    m_t = jnp.maximum(inter, jnp.max(log_d, axis=1, keepdims=True))
    sc = jnp.dot(q, kt, preferred_element_type=F32) * jnp.exp(log_d - m_t)
    dec = jnp.exp(inter - m_t)
    ab = (jnp.dot(sc.astype(BF16), v_ext, preferred_element_type=F32)
          + dec * jnp.dot(q, c_old.astype(BF16), preferred_element_type=F32))
    dh = q.shape[1]
    h = ab[:, :dh] / jnp.maximum(jnp.abs(ab[:, dh:]), jnp.exp(-m_t))
    return h, c_new, m_new


def _mlstm_kernel(qf_ref, ktf_ref, vf_ref, gf_ref, qb_ref, ktb_ref, vb_ref, gb_ref,
                  hf_ref, hb_ref, c_st, m_st):
    s = pl.program_id(1)

    @pl.when(s == 0)
    def _():
        c_st[...] = jnp.zeros_like(c_st)
        m_st[...] = jnp.zeros_like(m_st)

    ones = jnp.ones((ML_CHUNK, ML_DH), BF16)
    for d, (q_ref, kt_ref, v_ref, g_ref, o_ref) in enumerate(
            ((qf_ref, ktf_ref, vf_ref, gf_ref, hf_ref), (qb_ref, ktb_ref, vb_ref, gb_ref, hb_ref))):
        rev = d == 1
        g = g_ref[...]
        li_all = g[2 * d * ML_HEADS:(2 * d + 1) * ML_HEADS, :]
        lf_all = _log_sigmoid(g[(2 * d + 1) * ML_HEADS:(2 * d + 2) * ML_HEADS, :])
        cb_all = _lane_cumsum(lf_all, rev)
        for hd in range(ML_HEADS):
            cols = slice(hd * ML_DH, (hd + 1) * ML_DH)
            u = d * ML_HEADS + hd
            v_ext = jnp.concatenate([v_ref[:, cols], ones], axis=1)
            h, c_new, m_new = _mlstm_unit(
                q_ref[:, cols], kt_ref[cols, :], v_ext,
                li_all[hd:hd + 1, :], lf_all[hd:hd + 1, :], cb_all[hd:hd + 1, :],
                c_st[u], m_st[u][:, 0:1], rev)
            o_ref[:, cols] = h
            c_st[u] = c_new
            m_st[u] = jnp.broadcast_to(m_new, (1, LANES))


def _mlstm_call(mq, mkt, mv, gt, *, seq):
    bsz, t, w = mq.shape
    nl, nc = seq // ML_CHUNK, (t - seq) // ML_CHUNK
    cf = functools.partial(_scan_chunk, rev=False, nl=nl, nc=nc)
    cr = functools.partial(_scan_chunk, rev=True, nl=nl, nc=nc)

    def tok(ch):
        return pl.BlockSpec((None, ML_CHUNK, w), lambda b, s: (b, ch(s), 0))

    def tr(ch, rows):
        return pl.BlockSpec((None, rows, ML_CHUNK), lambda b, s: (b, 0, ch(s)))

    ng = gt.shape[1]
    return pl.pallas_call(
        _mlstm_kernel,
        grid=(bsz, nl + nc),
        in_specs=[tok(cf), tr(cf, w), tok(cf), tr(cf, ng), tok(cr), tr(cr, w), tok(cr), tr(cr, ng)],
        out_specs=[tok(cf), tok(cr)],
        out_shape=[jax.ShapeDtypeStruct((bsz, t, w), F32)] * 2,
        scratch_shapes=[
            pltpu.VMEM((2 * ML_HEADS, ML_DH, 2 * ML_DH), F32),
            pltpu.VMEM((2 * ML_HEADS, 1, LANES), F32),
        ],
        compiler_params=_cparams(("arbitrary", "arbitrary")),
        name="mlstm",
    )(mq, mkt, mv, gt, mq, mkt, mv, gt)


def _attn_kernel(sink_ref, q_ref, kp_ref, ks_ref, kn_ref, kc_ref, vp_ref, vs_ref, vn_ref, vc_ref,
                 o_ref, k_buf, v_buf, p_buf, *, nlb, ctx):
    kv = pl.program_id(1)
    n = pl.program_id(2)
    blk = ATT_BLOCK
    nk = 3 * blk + ctx
    k_buf[0:blk, :] = kp_ref[...]
    k_buf[blk:2 * blk, :] = ks_ref[...]
    k_buf[2 * blk:3 * blk, :] = kn_ref[...]
    k_buf[3 * blk:nk, :] = kc_ref[...]
    v_buf[0:blk, :] = vp_ref[...]
    v_buf[blk:2 * blk, :] = vs_ref[...]
    v_buf[2 * blk:3 * blk, :] = vn_ref[...]
    v_buf[3 * blk:nk, :] = vc_ref[...]

    lane = lax.broadcasted_iota(jnp.int32, (blk, LANES), 1)
    low = lane < AT_DH
    zero = jnp.zeros((blk, LANES), BF16)
    qs = []
    for j in range(AT_G // 2):
        qj = q_ref[:, j * LANES:(j + 1) * LANES]
        qs += [jnp.where(low, qj, zero), jnp.where(low, zero, qj)]
    q_all = jnp.concatenate(qs, axis=0)
    sc = lax.dot_general(q_all, k_buf[...], (((1,), (1,)), ((), ())),
                         preferred_element_type=F32)

    rows = AT_G * blk
    ti = lax.broadcasted_iota(jnp.int32, (rows, nk), 0) & (blk - 1)
    ci = lax.broadcasted_iota(jnp.int32, (rows, nk), 1)
    lo_n = jnp.where(n >= nlb, 3 * blk, jnp.where(n >= 1, 0, blk))
    hi_n = jnp.where(n < nlb - 1, 3 * blk - 1, 2 * blk - 1)
    in_band = jnp.logical_and(ci >= jnp.maximum(ti, lo_n), ci <= jnp.minimum(ti + 2 * blk, hi_n))
    sc = jnp.where(jnp.logical_or(in_band, ci >= 3 * blk), sc, NEG)

    gi = lax.broadcasted_iota(jnp.int32, (rows, 1), 0) // blk
    snk = jnp.zeros((rows, 1), F32)
    for g in range(AT_G):
        snk = jnp.where(gi == g, sink_ref[kv * AT_G + g], snk)
    m = jnp.maximum(jnp.max(sc, axis=1, keepdims=True), snk)
    p = jnp.exp(sc - m)
    den = jnp.sum(p, axis=1, keepdims=True) + jnp.exp(snk - m)
    p_buf[...] = p.astype(BF16)
    o2 = jnp.dot(p_buf[...], v_buf[...], preferred_element_type=F32) / den
    for j in range(AT_G // 2):
        a = o2[(2 * j) * blk:(2 * j + 1) * blk, :]
        b = o2[(2 * j + 1) * blk:(2 * j + 2) * blk, :]
        o_ref[:, j * LANES:(j + 1) * LANES] = jnp.where(low, a, b).astype(o_ref.dtype)


def _attn_call(sink, aq, akd, avd, *, seq):
    bsz, t, _ = aq.shape
    ctx = t - seq
    blk = ATT_BLOCK
    nlb = seq // blk
    nb = t // blk
    gw = AT_G * AT_DH
    cblk = seq // ctx

    def kv_spec(fn):
        return pl.BlockSpec((None, blk, LANES), lambda b, kv, n: (b, fn(n), kv))

    prev = kv_spec(lambda n: jnp.maximum(n - 1, 0))
    cur = kv_spec(lambda n: n)
    nxt = kv_spec(lambda n: jnp.minimum(n + 1, nb - 1))
    cspec = pl.BlockSpec((None, ctx, LANES), lambda b, kv, n: (b, cblk, kv))
    kern = functools.partial(_attn_kernel, nlb=nlb, ctx=ctx)
    return pl.pallas_call(
        kern,
        grid=(bsz, AT_KV, nb),
        in_specs=[
            pl.BlockSpec(memory_space=pltpu.SMEM),
            pl.BlockSpec((None, blk, gw), lambda b, kv, n: (b, n, kv)),
            prev, cur, nxt, cspec, prev, cur, nxt, cspec,
        ],
        out_specs=pl.BlockSpec((None, blk, gw), lambda b, kv, n: (b, n, kv)),
        out_shape=jax.ShapeDtypeStruct((bsz, t, AT_W), BF16),
        scratch_shapes=[
            pltpu.VMEM((3 * blk + ctx, LANES), BF16),
            pltpu.VMEM((3 * blk + ctx, LANES), BF16),
            pltpu.VMEM((AT_G * blk, 3 * blk + ctx), BF16),
        ],
        compiler_params=_cparams(("arbitrary", "arbitrary", "arbitrary")),
        name="attn",
    )(sink, aq, akd, akd, akd, akd, avd, avd, avd, avd)


def _merge_kernel(x_ref, ml_ref, mc_ref, ya_ref, hf_ref, hb_ref, mo_ref, yc_ref, br_ref,
                  ng_ref, wb_ref, wo_ref, o_ref, *, seq, tm):
    t0 = pl.program_id(1) * tm
    rows = t0 + lax.broadcasted_iota(jnp.int32, (tm, 1), 0)
    is_ctx = rows >= seq
    d = x_ref.shape[1]
    hsum = hf_ref[...] + hb_ref[...]
    parts = []
    for hd in range(ML_HEADS):
        hh = hsum[:, hd * ML_DH:(hd + 1) * ML_DH]
        parts.append(hh * lax.rsqrt(jnp.mean(hh * hh, axis=-1, keepdims=True) + EPS))
    yb = (jnp.concatenate(parts, axis=1) * ng_ref[...] * mo_ref[...].astype(F32)).astype(BF16)
    m = (br_ref[:, 0:d].astype(F32) * jnp.dot(ya_ref[...], wb_ref[0], preferred_element_type=F32)
         + br_ref[:, d:2 * d].astype(F32) * jnp.dot(yb, wb_ref[1], preferred_element_type=F32)
         + br_ref[:, 2 * d:3 * d].astype(F32) * jnp.dot(yc_ref[...], wb_ref[2],
                                                       preferred_element_type=F32))
    y = jnp.dot(m.astype(BF16), wo_ref[...], preferred_element_type=F32)
    o_ref[...] = x_ref[...] + _row_mod(ml_ref, mc_ref, is_ctx, 5) * y


def _merge_call(x, ml, mc, ya, hf, hb, mo, yc, br, ng, wb, wo, *, seq):
    bsz, t, d = x.shape
    tm = _pick_tile(t, (384, 128))
    kern = functools.partial(_merge_kernel, seq=seq, tm=tm)

    def tok(wd):
        return pl.BlockSpec((None, tm, wd), lambda b, i: (b, i, 0))

    return pl.pallas_call(
        kern,
        grid=(bsz, t // tm),
        in_specs=[
            tok(d),
            pl.BlockSpec((None, N_MOD, d), lambda b, i: (b, 0, 0)),
            _const_spec((N_MOD, d)),
            tok(BRANCH_W), tok(ML_W), tok(ML_W), tok(ML_W), tok(AT_W), tok(N_BRANCH * d),
            _const_spec((1, ML_W)),
            _const_spec((N_BRANCH, BRANCH_W, d)),
            _const_spec((d, d)),
        ],
        out_specs=tok(d),
        out_shape=jax.ShapeDtypeStruct((bsz, t, d), F32),
        compiler_params=_cparams(("arbitrary", "arbitrary")),
        name="merge",
    )(x, ml, mc, ya, hf, hb, mo, yc, br, ng, wb, wo)


def _pack_w_in(w_in, d_model):
    o_rg_x = 0
    o_ml_g = 2 * D_RNN + 4 * ML_W
    o_at_q = o_ml_g + 4 * ML_HEADS
    o_at_k = o_at_q + AT_W
    o_at_v = o_at_k + AT_KVW
    o_br = o_at_v + AT_KVW

    def dup(c0):
        hs = [w_in[:, c0 + i * AT_DH:c0 + (i + 1) * AT_DH] for i in range(AT_KV)]
        return [hs[i // 2] for i in range(2 * AT_KV)]

    gate = w_in[:, o_ml_g:o_at_q]
    pad = jnp.zeros((w_in.shape[0], LANES - gate.shape[1]), w_in.dtype)
    cols = ([w_in[:, o_rg_x:o_ml_g], w_in[:, o_at_q:o_at_k]] + dup(o_at_k) + dup(o_at_v)
            + [w_in[:, o_br:o_br + N_BRANCH * d_model], gate, pad])
    return jnp.concatenate(cols, axis=1).astype(BF16)


def _block_diag(w):
    n, bi, bj = w.shape
    eye = jnp.eye(n, dtype=w.dtype)
    return (eye[:, None, :, None] * w[:, :, None, :]).reshape(n * bi, n * bj)


def _rope_tables(seq, t):
    rows = seq // GRID_W
    row = jnp.repeat(jnp.arange(rows), GRID_W).astype(F32)
    col = jnp.broadcast_to(jnp.arange(GRID_W), (rows, GRID_W)).reshape(-1).astype(F32)
    half = AT_DH // 2
    inv = ROPE_BASE ** (-jnp.arange(0, half, 2, dtype=F32) / half)
    ar = row[:, None] * inv
    ac = col[:, None] * inv
    ang = jnp.concatenate([ar, ar, ac, ac], axis=-1)
    cos = jnp.concatenate([jnp.cos(ang), jnp.ones((t - seq, AT_DH), F32)], axis=0)
    sin = jnp.concatenate([jnp.sin(ang), jnp.zeros((t - seq, AT_DH), F32)], axis=0)
    cos = jnp.tile(cos, (1, LANES // AT_DH))
    sin = jnp.tile(sin, (1, LANES // AT_DH))
    first_half = (jnp.arange(LANES) % 32) < 16
    sin_a = jnp.where(first_half, -sin, 0.0)
    sin_b = jnp.where(first_half, 0.0, sin)
    return cos, sin_a, sin_b


def kernel(x, c, ctx, c_ctx, ada_w, ada_b, norm_g, ffn_w1, ffn_w3, ffn_w2, w_in, rg_conv_w,
           rg_conv_b, rg_wa, rg_ba, rg_wi, rg_bi, rg_lam, ml_gate_b, ml_norm_g, at_qn_g,
           at_kn_g, at_sink, w_branch, w_out):
    bsz, seq, d = x.shape
    nctx = ctx.shape[1]
    t = seq + nctx
    depth = ada_w.shape[0]

    xs = jnp.concatenate([x, ctx], axis=1)
    mod_rows = 2 * SUBLANES
    cc = jnp.concatenate([c, c_ctx[None, :], jnp.zeros((mod_rows - bsz - 1, d), F32)], axis=0)
    mods = _ada_call(cc, ada_w, ada_b).reshape(depth, mod_rows, N_MOD, d)
    cos, sin_a, sin_b = _rope_tables(seq, t)
    gmat = _block_diag(jnp.full((AT_HEADS, AT_DH, AT_DH), 1.0 / AT_DH, F32)).astype(BF16)

    for l in range(depth):
        ml = mods[l, :bsz]
        mc = mods[l, bsz]
        last = l == depth - 1
        w1 = [ffn_w1[l, i].astype(BF16) for i in range(2)]
        w3 = [ffn_w3[l, i].astype(BF16) for i in range(2)]
        w2 = [ffn_w2[l, i].astype(BF16) for i in range(2)]

        xs = _ffn_call(xs, ml, mc, norm_g[l, 0][None, :], w1[0], w3[0], w2[0],
                       base=0, seq=seq, out_rows=t)

        qg = jnp.tile(at_qn_g[l], AT_HEADS)[None, :]
        kg = jnp.tile(at_kn_g[l], 2 * AT_KV)[None, :]
        gb = jnp.concatenate([ml_gate_b[l], jnp.zeros((LANES - 4 * ML_HEADS,), F32)])[None, :]
        (rgx, rgg, mq, mk, mv, mo, aq, akd, avd, br, mg) = _proj_call(
            xs, ml, mc, norm_g[l, 1][None, :], _pack_w_in(w_in[l], d), gmat, qg, kg, gb,
            cos, sin_a, sin_b, seq=seq)

        hf = None
        for dr in range(2):
            wcat = jnp.concatenate([_block_diag(rg_wa[l, dr]), _block_diag(rg_wi[l, dr])],
                                   axis=1).astype(BF16)
            bias = jnp.concatenate([rg_ba[l, dr], rg_bi[l, dr]])[None, :]
            clam = (-LRU_C * jax.nn.softplus(-rg_lam[l, dr]))[None, :]
            res = _rglru_call(rgx, rg_conv_w[l], rg_conv_b[l][None, :], wcat, bias, clam,
                              hf, rgg, rev=dr == 1, seq=seq)
            if dr == 0:
                hf = res
        ya = res

        gt = jnp.swapaxes(mg[:, :, :4 * ML_HEADS], 1, 2)
        mkt = jnp.swapaxes(mk, 1, 2)
        mhf, mhb = _mlstm_call(mq, mkt, mv, gt, seq=seq)

        yc = _attn_call(at_sink[l], aq, akd, avd, seq=seq)

        xs = _merge_call(xs, ml, mc, ya, mhf, mhb, mo, yc, br, ml_norm_g[l][None, :],
                         w_branch[l].astype(BF16), w_out[l].astype(BF16), seq=seq)

        xs = _ffn_call(xs, ml, mc, norm_g[l, 2][None, :], w1[1], w3[1], w2[1],
                       base=6, seq=seq, out_rows=seq if last else t)
    return xs
```

```python
import functools

import jax
import jax.numpy as jnp
from jax import lax
from jax.experimental import pallas as pl
from jax.experimental.pallas import tpu as pltpu

F32 = jnp.float32
BF16 = jnp.bfloat16

EPS = 1e-6
NEG = -1e30
N_MOD = 9
GRID_W = 64
ROPE_BASE = 10000.0

D_RNN = 512
RNN_BLOCKS = 8
RNN_BLOCK = D_RNN // RNN_BLOCKS
CONV_W = 4
CONV_LEFT = 2
CONV_RIGHT = CONV_W - 1 - CONV_LEFT
LRU_C = 8.0

ML_HEADS = 4
ML_DH = 128
ML_W = ML_HEADS * ML_DH
ML_CHUNK = 128
ML_UNITS = 2 * ML_HEADS

AT_HEADS = 8
AT_KV = 2
AT_DH = 64
AT_G = AT_HEADS // AT_KV
AT_W = AT_HEADS * AT_DH
AT_KVW = AT_KV * AT_DH
ATT_BLOCK = 128

N_BRANCH = 3
BRANCH_W = 512

LANES = 128
SUBLANES = 8
VMEM_LIMIT = 56 * 1024 * 1024

P_RGX = 0
P_RGG = P_RGX + D_RNN
P_MQ = P_RGG + D_RNN
P_MK = P_MQ + ML_W
P_MV = P_MK + ML_W
P_MO = P_MV + ML_W
P_AQ = P_MO + ML_W
P_AK = P_AQ + AT_W
P_AV = P_AK + 2 * AT_KVW
P_BR = P_AV + 2 * AT_KVW


def _cparams(sem):
    return pltpu.CompilerParams(dimension_semantics=sem, vmem_limit_bytes=VMEM_LIMIT)


def _const_spec(shape):
    nd = len(shape)
    return pl.BlockSpec(shape, lambda *_: (0,) * nd, pipeline_mode=pl.Buffered(1))


def _pick_tile(total, candidates):
    for c in candidates:
        if total % c == 0:
            return c
    raise ValueError(f"no tile for {total}")


def _sigmoid(x):
    return jax.nn.sigmoid(x)


def _ln_mod(x, g, shift, scale):
    ms = jnp.mean(x * x, axis=-1, keepdims=True)
    return x * lax.rsqrt(ms + EPS) * g * (1.0 + scale) + shift


def _row_mod(ml_ref, mc_ref, is_ctx, i):
    return jnp.where(is_ctx, mc_ref[i:i + 1, :], ml_ref[i:i + 1, :])


def _ada_kernel(cc_ref, w_ref, b_ref, o_ref):
    cc = cc_ref[...]
    s = cc * _sigmoid(cc)
    o_ref[...] = jnp.dot(s.astype(BF16), w_ref[...].astype(BF16),
                         preferred_element_type=F32) + b_ref[...]


def _ada_call(cc, ada_w, ada_b):
    depth, d, nout = ada_w.shape
    rows = cc.shape[0]
    tn = _pick_tile(nout, (1536, 1024, 512, 256, 128))
    return pl.pallas_call(
        _ada_kernel,
        grid=(depth, nout // tn),
        in_specs=[
            pl.BlockSpec((rows, d), lambda l, j: (0, 0)),
            pl.BlockSpec((None, d, tn), lambda l, j: (l, 0, j)),
            pl.BlockSpec((None, 1, tn), lambda l, j: (l, 0, j)),
        ],
        out_specs=pl.BlockSpec((None, rows, tn), lambda l, j: (l, 0, j)),
        out_shape=jax.ShapeDtypeStruct((depth, rows, nout), F32),
        compiler_params=_cparams(("arbitrary", "arbitrary")),
        name="ada_mod",
    )(cc, ada_w, ada_b.reshape(depth, 1, nout))


def _ffn_kernel(x_ref, ml_ref, mc_ref, g_ref, w1_ref, w3_ref, w2_ref, o_ref, act_ref,
                *, base, seq, tm, ck):
    t0 = pl.program_id(1) * tm
    rows = t0 + lax.broadcasted_iota(jnp.int32, (tm, 1), 0)
    is_ctx = rows >= seq
    x = x_ref[...]
    h = _ln_mod(x, g_ref[...], _row_mod(ml_ref, mc_ref, is_ctx, base),
                _row_mod(ml_ref, mc_ref, is_ctx, base + 1)).astype(BF16)
    dff = w1_ref.shape[1]
    for j in range(dff // ck):
        a = jnp.dot(h, w1_ref[:, j * ck:(j + 1) * ck], preferred_element_type=F32)
        b = jnp.dot(h, w3_ref[:, j * ck:(j + 1) * ck], preferred_element_type=F32)
        act_ref[:, j * ck:(j + 1) * ck] = (a * _sigmoid(a) * b).astype(BF16)
    y = jnp.dot(act_ref[...], w2_ref[...], preferred_element_type=F32)
    o_ref[...] = x + 0.5 * _row_mod(ml_ref, mc_ref, is_ctx, base + 2) * y


def _ffn_call(x, ml, mc, g, w1, w3, w2, *, base, seq, out_rows):
    bsz, t, d = x.shape
    dff = w1.shape[1]
    tm = _pick_tile(t, (768, 384, 128))
    ck = _pick_tile(dff, (256, 128))
    kern = functools.partial(_ffn_kernel, base=base, seq=seq, tm=tm, ck=ck)
    return pl.pallas_call(
        kern,
        grid=(bsz, pl.cdiv(out_rows, tm)),
        in_specs=[
            pl.BlockSpec((None, tm, d), lambda b, i: (b, i, 0)),
            pl.BlockSpec((None, N_MOD, d), lambda b, i: (b, 0, 0)),
            _const_spec((N_MOD, d)),
            _const_spec((1, d)),
            _const_spec((d, dff)),
            _const_spec((d, dff)),
            _const_spec((dff, d)),
        ],
        out_specs=pl.BlockSpec((None, tm, d), lambda b, i: (b, i, 0)),
        out_shape=jax.ShapeDtypeStruct((bsz, out_rows, d), F32),
        scratch_shapes=[pltpu.VMEM((tm, dff), BF16)],
        compiler_params=_cparams(("arbitrary", "arbitrary")),
        name="ffn",
    )(x, ml, mc, g, w1, w3, w2)


def _gelu_tanh(x):
    return 0.5 * x * (1.0 + jnp.tanh(0.7978845608028654 * (x + 0.044715 * (x * x * x))))


def _head_norm(x, gmat, g):
    sq = x * x
    hi = sq.astype(BF16)
    lo = (sq - hi.astype(F32)).astype(BF16)
    ms = (jnp.dot(hi, gmat, preferred_element_type=F32)
          + jnp.dot(lo, gmat, preferred_element_type=F32))
    return x * lax.rsqrt(ms + EPS) * g


def _rope(x, cos, sin_a, sin_b):
    parts = []
    for j in range(x.shape[1] // LANES):
        xj = x[:, j * LANES:(j + 1) * LANES]
        parts.append(xj * cos + pltpu.roll(xj, LANES - 16, 1) * sin_a
                     + pltpu.roll(xj, 16, 1) * sin_b)
    return jnp.concatenate(parts, axis=1)


def _proj_kernel(x_ref, ml_ref, mc_ref, g_ref, w_ref, gmat_ref, qg_ref, kg_ref, gb_ref,
                 cos_ref, sa_ref, sb_ref,
                 rgx_o, rgg_o, mq_o, mk_o, mv_o, mo_o, aq_o, ak_o, av_o, br_o, mg_o,
                 *, seq, tm, d_model):
    t0 = pl.program_id(1) * tm
    rows = t0 + lax.broadcasted_iota(jnp.int32, (tm, 1), 0)
    is_ctx = rows >= seq
    h = _ln_mod(x_ref[...], g_ref[...], _row_mod(ml_ref, mc_ref, is_ctx, 3),
                _row_mod(ml_ref, mc_ref, is_ctx, 4)).astype(BF16)

    def seg(c0, w):
        return jnp.dot(h, w_ref[:, c0:c0 + w], preferred_element_type=F32)

    rgx_o[...] = seg(P_RGX, D_RNN)
    rgg_o[...] = _gelu_tanh(seg(P_RGG, D_RNN)).astype(BF16)
    mq_o[...] = seg(P_MQ, ML_W).astype(BF16)
    mk_o[...] = (seg(P_MK, ML_W) * (ML_DH ** -0.5)).astype(BF16)
    mv_o[...] = seg(P_MV, ML_W).astype(BF16)
    mo_o[...] = _sigmoid(seg(P_MO, ML_W)).astype(BF16)
    cos, sin_a, sin_b = cos_ref[...], sa_ref[...], sb_ref[...]
    gmat = gmat_ref[...]
    q = _head_norm(seg(P_AQ, AT_W), gmat, qg_ref[...])
    aq_o[...] = (_rope(q, cos, sin_a, sin_b) * (AT_DH ** -0.5)).astype(BF16)
    k = _head_norm(seg(P_AK, 2 * AT_KVW), gmat[:2 * AT_KVW, :2 * AT_KVW], kg_ref[...])
    ak_o[...] = _rope(k, cos, sin_a, sin_b).astype(BF16)
    av_o[...] = seg(P_AV, 2 * AT_KVW).astype(BF16)
    nbr = N_BRANCH * d_model
    for c0 in range(0, nbr, 512):
        br_o[:, c0:c0 + 512] = _sigmoid(seg(P_BR + c0, 512)).astype(BF16)
    mg_o[...] = seg(P_BR + nbr, LANES) + gb_ref[...]


def _proj_call(x, ml, mc, g, w, gmat, qg, kg, gb, cos, sin_a, sin_b, *, seq):
    bsz, t, d = x.shape
    ncol = w.shape[1]
    tm = _pick_tile(t, (384, 128))
    kern = functools.partial(_proj_kernel, seq=seq, tm=tm, d_model=d)
    widths = [(D_RNN, F32), (D_RNN, BF16), (ML_W, BF16), (ML_W, BF16), (ML_W, BF16),
              (ML_W, BF16), (AT_W, BF16), (2 * AT_KVW, BF16), (2 * AT_KVW, BF16),
              (N_BRANCH * d, BF16), (LANES, F32)]
    tab_spec = pl.BlockSpec((tm, LANES), lambda b, i: (i, 0))
    return pl.pallas_call(
        kern,
        grid=(bsz, t // tm),
        in_specs=[
            pl.BlockSpec((None, tm, d), lambda b, i: (b, i, 0)),
            pl.BlockSpec((None, N_MOD, d), lambda b, i: (b, 0, 0)),
            _const_spec((N_MOD, d)),
            _const_spec((1, d)),
            _const_spec((d, ncol)),
            _const_spec((AT_W, AT_W)),
            _const_spec((1, AT_W)),
            _const_spec((1, 2 * AT_KVW)),
            _const_spec((1, LANES)),
            tab_spec, tab_spec, tab_spec,
        ],
        out_specs=[pl.BlockSpec((None, tm, wd), lambda b, i: (b, i, 0)) for wd, _ in widths],
        out_shape=[jax.ShapeDtypeStruct((bsz, t, wd), dt) for wd, dt in widths],
        compiler_params=_cparams(("arbitrary", "arbitrary")),
        name="in_proj",
    )(x, ml, mc, g, w, gmat, qg, kg, gb, cos, sin_a, sin_b)


def _scan_chunk(s, rev, nl, nc):
    if rev:
        return jnp.where(s < nc, nl + nc - 1 - s, nl - 1 - (s - nc))
    return jnp.where(s < nc, nl + s, s - nc)


def _rglru_kernel(xf_ref, xfp_ref, xfn_ref, xb_ref, xbp_ref, xbn_ref, cw_ref, cb_ref, w_ref,
                  bias_ref, clam_ref, hf_ref, hb_ref, ext, a_s, b_s, hstate,
                  *, nl, nc, tc, sub):
    s = pl.program_id(0)
    bsz, c = hstate.shape[1], hstate.shape[2]

    @pl.when(s == 0)
    def _():
        hstate[...] = jnp.zeros_like(hstate)

    cw = cw_ref[...]
    cb = cb_ref[...]
    for d, (x_ref, xp_ref, xn_ref) in enumerate(((xf_ref, xfp_ref, xfn_ref),
                                                (xb_ref, xbp_ref, xbn_ref))):
        chunk = _scan_chunk(s, d == 1, nl, nc)
        first = jnp.logical_or(chunk == 0, chunk == nl)
        last = jnp.logical_or(chunk == nl - 1, chunk == nl + nc - 1)
        ext[d, 0:CONV_LEFT] = jnp.where(first, 0.0, xp_ref[...])
        ext[d, CONV_LEFT:CONV_LEFT + tc] = x_ref[...]
        ext[d, CONV_LEFT + tc:CONV_LEFT + tc + CONV_RIGHT] = jnp.where(last, 0.0, xn_ref[...])

        def gates(i, carry, d=d):
            t0 = pl.multiple_of(i * sub, sub)
            u = cb
            for k in range(CONV_W):
                u = u + ext[d, pl.ds(t0 + k, sub)] * cw[k:k + 1, :]
            u2 = u.reshape(sub * bsz, c)
            z = jnp.dot(u2.astype(BF16), w_ref[d], preferred_element_type=F32) + bias_ref[d]
            r = 0.5 * jnp.tanh(0.5 * z[:, :c]) + 0.5
            gi = 0.5 * jnp.tanh(0.5 * z[:, c:]) + 0.5
            a = jnp.exp(clam_ref[d] * r)
            s1 = 1.0 - a * a
            bb = (s1 * lax.rsqrt(jnp.maximum(s1, 1e-36))) * (gi * u2)
            a_s[d, pl.ds(t0, sub)] = a.reshape(sub, bsz, c)
            b_s[d, pl.ds(t0, sub)] = bb.reshape(sub, bsz, c)
            return carry

        lax.fori_loop(0, tc // sub, gates, 0)

    def step(k, hs):
        hf, hb = hs
        kb = tc - 1 - k
        hf = a_s[0, k] * hf + b_s[0, k]
        hb = a_s[1, kb] * hb + b_s[1, kb]
        hf_ref[k] = hf
        hb_ref[kb] = hb
        return hf, hb

    hf, hb = lax.fori_loop(0, tc, step, (hstate[0], hstate[1]), unroll=8)
    hstate[0] = hf
    hstate[1] = hb


def _rglru_call(rgx_t, cw, cb, w, bias, clam, *, seq):
    t, bsz, c = rgx_t.shape
    tc = 128
    sub = 32
    assert seq % tc == 0 and (t - seq) % tc == 0
    nl, nc = seq // tc, (t - seq) // tc

    def specs(rev):
        chunk = functools.partial(_scan_chunk, rev=rev, nl=nl, nc=nc)
        return [
            pl.BlockSpec((tc, bsz, c), lambda s: (chunk(s), 0, 0)),
            pl.BlockSpec((CONV_LEFT, bsz, c),
                         lambda s: (jnp.maximum(chunk(s) * (tc // CONV_LEFT) - 1, 0), 0, 0)),
            pl.BlockSpec((CONV_RIGHT, bsz, c),
                         lambda s: (jnp.minimum((chunk(s) + 1) * (tc // CONV_RIGHT),
                                                t // CONV_RIGHT - 1), 0, 0)),
        ]

    kern = functools.partial(_rglru_kernel, nl=nl, nc=nc, tc=tc, sub=sub)
    fspec, bspec = specs(False), specs(True)
    return pl.pallas_call(
        kern,
        grid=(nl + nc,),
        in_specs=fspec + bspec + [
            _const_spec((CONV_W, c)),
            _const_spec((1, c)),
            _const_spec((2, c, 2 * c)),
            _const_spec((2, 1, 2 * c)),
            _const_spec((2, 1, c)),
        ],
        out_specs=[fspec[0], bspec[0]],
        out_shape=[jax.ShapeDtypeStruct((t, bsz, c), F32)] * 2,
        scratch_shapes=[
            pltpu.VMEM((2, tc + CONV_W - 1, bsz, c), F32),
            pltpu.VMEM((2, tc, bsz, c), F32),
            pltpu.VMEM((2, tc, bsz, c), F32),
            pltpu.VMEM((2, bsz, c), F32),
        ],
        compiler_params=_cparams(("arbitrary",)),
        name="rglru",
    )(rgx_t, rgx_t, rgx_t, rgx_t, rgx_t, rgx_t, cw, cb, w, bias, clam)


def _log_sigmoid(x):
    return -(jnp.maximum(-x, 0.0) + jnp.log1p(jnp.exp(-jnp.abs(x))))


def _lane_scan(x, op, ident, rev):
    n = x.shape[1]
    lane = lax.broadcasted_iota(jnp.int32, x.shape, 1)
    sh = 1
    while sh < n:
        if rev:
            x = op(x, jnp.where(lane < n - sh, pltpu.roll(x, n - sh, 1), ident))
        else:
            x = op(x, jnp.where(lane >= sh, pltpu.roll(x, sh, 1), ident))
        sh *= 2
    return x


def _mlgate_kernel(g_ref, row_o, col_o, cb_s, gg_s, ml_s, tot_s, wm_s, mo_s, mn_s, *, nl, nc):
    L = ML_CHUNK
    H = ML_HEADS
    U = ML_UNITS
    is_bwd = lax.broadcasted_iota(jnp.int32, (U, L), 0) >= H
    for c in range(nl + nc):
        lanes = slice(c * L, (c + 1) * L)
        g16 = g_ref[:, lanes]
        li = jnp.concatenate([g16[0:H], g16[2 * H:3 * H]], axis=0)
        lf = _log_sigmoid(jnp.concatenate([g16[H:2 * H], g16[3 * H:4 * H]], axis=0))
        pre = _lane_scan(lf, jnp.add, 0.0, False)
        suf = _lane_scan(lf, jnp.add, 0.0, True)
        tot = pre + suf - lf
        cb = jnp.where(is_bwd, suf, pre)
        gg = li - cb
        pmax = _lane_scan(gg, jnp.maximum, NEG, False)
        smax = _lane_scan(gg, jnp.maximum, NEG, True)
        cb_s[:, lanes] = cb
        gg_s[:, lanes] = gg
        ml_s[:, lanes] = cb + jnp.where(is_bwd, smax, pmax)
        tot_s[:, lanes] = tot
        wm_s[:, lanes] = tot + jnp.maximum(pmax, smax)

    m = jnp.zeros((U, L), F32)
    for s in range(nl + nc):
        cf = nl + s if s < nc else s - nc
        cr = nl + nc - 1 - s if s < nc else nl - 1 - (s - nc)
        lf_, lr_ = slice(cf * L, (cf + 1) * L), slice(cr * L, (cr + 1) * L)
        tot = jnp.where(is_bwd, tot_s[:, lr_], tot_s[:, lf_])
        wm = jnp.where(is_bwd, wm_s[:, lr_], wm_s[:, lf_])
        m_new = jnp.maximum(tot + m, wm)
        mo_s[0:H, lf_] = m[0:H]
        mo_s[H:U, lr_] = m[H:U]
        mn_s[0:H, lf_] = m_new[0:H]
        mn_s[H:U, lr_] = m_new[H:U]
        m = m_new

    cb = cb_s[...]
    gg = gg_s[...]
    tot = tot_s[...]
    mo = mo_s[...]
    mn = mn_s[...]
    m_t = jnp.maximum(cb + mo, ml_s[...])
    row_o[0:U, :] = gg
    row_o[U:2 * U, :] = jnp.exp(tot + gg - mn)
    row_o[2 * U:3 * U, :] = jnp.exp(tot + mo - mn)
    col_o[0:U, :] = cb - m_t
    col_o[U:2 * U, :] = jnp.exp(cb + mo - m_t)
    col_o[2 * U:3 * U, :] = jnp.exp(-m_t)


def _mlgate_call(gt, *, seq):
    bsz, ng, t = gt.shape
    nl, nc = seq // ML_CHUNK, (t - seq) // ML_CHUNK
    kern = functools.partial(_mlgate_kernel, nl=nl, nc=nc)
    out = pl.BlockSpec((None, 3 * ML_UNITS, t), lambda b: (b, 0, 0))
    return pl.pallas_call(
        kern,
        grid=(bsz,),
        in_specs=[pl.BlockSpec((None, ng, t), lambda b: (b, 0, 0))],
        out_specs=[out, out],
        out_shape=[jax.ShapeDtypeStruct((bsz, 3 * ML_UNITS, t), F32)] * 2,
        scratch_shapes=[pltpu.VMEM((ML_UNITS, t), F32)] * 7,
        compiler_params=_cparams(("arbitrary",)),
        name="mlstm_gates",
    )(gt)


def _mlstm_kernel(qf_ref, ktf_ref, vf_ref, rf_ref, cf_ref, qb_ref, ktb_ref, vb_ref, rb_ref, cb_ref,
                  hf_ref, hb_ref, c_st):
    s = pl.program_id(1)

    @pl.when(s == 0)
    def _():
        c_st[...] = jnp.zeros_like(c_st)

    L, dh, U = ML_CHUNK, ML_DH, ML_UNITS
    ti = lax.broadcasted_iota(jnp.int32, (L, L), 0)
    si = lax.broadcasted_iota(jnp.int32, (L, L), 1)
    ones = jnp.ones((L, dh), BF16)
    for d, (q_ref, kt_ref, v_ref, r_ref, c_ref, o_ref) in enumerate(
            ((qf_ref, ktf_ref, vf_ref, rf_ref, cf_ref, hf_ref),
             (qb_ref, ktb_ref, vb_ref, rb_ref, cb_ref, hb_ref))):
        tri = (si >= ti) if d == 1 else (si <= ti)
        rows = r_ref[...]
        cols = c_ref[...]
        for hd in range(ML_HEADS):
            u = d * ML_HEADS + hd
            sl = slice(hd * dh, (hd + 1) * dh)
            q = q_ref[:, sl]
            kt = kt_ref[sl, :]
            v_ext = jnp.concatenate([v_ref[:, sl], ones], axis=1)
            g_row = rows[u:u + 1, :]
            w_row = rows[U + u:U + u + 1, :]
            a_row = rows[2 * U + u:2 * U + u + 1, :]
            x_col = cols[:, u:u + 1]
            dec = cols[:, U + u:U + u + 1]
            emt = cols[:, 2 * U + u:2 * U + u + 1]
            c_old = c_st[u]
            dm = jnp.exp(jnp.where(tri, x_col + g_row, NEG))
            sc = jnp.dot(q, kt, preferred_element_type=F32) * dm
            ab = (jnp.dot(sc.astype(BF16), v_ext, preferred_element_type=F32)
                  + dec * jnp.dot(q, c_old.astype(BF16), preferred_element_type=F32))
            o_ref[:, sl] = ab[:, :dh] / jnp.maximum(jnp.abs(ab[:, dh:]), emt)
            ktw = (kt.astype(F32) * w_row).astype(BF16)
            c_st[u] = (jnp.concatenate([a_row, a_row], axis=1) * c_old
                       + jnp.dot(ktw, v_ext, preferred_element_type=F32))


def _mlstm_call(mq, mkt, mv, rowp, colp, *, seq):
    bsz, t, w = mq.shape
    nl, nc = seq // ML_CHUNK, (t - seq) // ML_CHUNK
    cf = functools.partial(_scan_chunk, rev=False, nl=nl, nc=nc)
    cr = functools.partial(_scan_chunk, rev=True, nl=nl, nc=nc)
    nr = rowp.shape[1]

    def tok(ch, wd):
        return pl.BlockSpec((None, ML_CHUNK, wd), lambda b, s: (b, ch(s), 0))

    def tr(ch, rows):
        return pl.BlockSpec((None, rows, ML_CHUNK), lambda b, s: (b, 0, ch(s)))

    def side(ch):
        return [tok(ch, w), tr(ch, w), tok(ch, w), tr(ch, nr), tok(ch, nr)]

    return pl.pallas_call(
        _mlstm_kernel,
        grid=(bsz, nl + nc),
        in_specs=side(cf) + side(cr),
        out_specs=[tok(cf, w), tok(cr, w)],
        out_shape=[jax.ShapeDtypeStruct((bsz, t, w), F32)] * 2,
        scratch_shapes=[pltpu.VMEM((ML_UNITS, ML_DH, 2 * ML_DH), F32)],
        compiler_params=_cparams(("arbitrary", "arbitrary")),
        name="mlstm",
    )(mq, mkt, mv, rowp, colp, mq, mkt, mv, rowp, colp)


def _attn_kernel(sink_ref, q_ref, kp_ref, ks_ref, kn_ref, kc_ref, vp_ref, vs_ref, vn_ref, vc_ref,
                 o_ref, k_buf, v_buf, p_buf, *, nlb, ctx):
    kv = pl.program_id(1)
    n = pl.program_id(2)
    blk = ATT_BLOCK
    nk = 3 * blk + ctx
    k_buf[0:blk, :] = kp_ref[...]
    k_buf[blk:2 * blk, :] = ks_ref[...]
    k_buf[2 * blk:3 * blk, :] = kn_ref[...]
    k_buf[3 * blk:nk, :] = kc_ref[...]
    v_buf[0:blk, :] = vp_ref[...]
    v_buf[blk:2 * blk, :] = vs_ref[...]
    v_buf[2 * blk:3 * blk, :] = vn_ref[...]
    v_buf[3 * blk:nk, :] = vc_ref[...]

    lane = lax.broadcasted_iota(jnp.int32, (blk, LANES), 1)
    low = lane < AT_DH
    zero = jnp.zeros((blk, LANES), BF16)
    qs = []
    for j in range(AT_G // 2):
        qj = q_ref[:, j * LANES:(j + 1) * LANES]
        qs += [jnp.where(low, qj, zero), jnp.where(low, zero, qj)]
    q_all = jnp.concatenate(qs, axis=0)
    sc = lax.dot_general(q_all, k_buf[...], (((1,), (1,)), ((), ())),
                         preferred_element_type=F32)

    rows = AT_G * blk
    ti = lax.broadcasted_iota(jnp.int32, (rows, nk), 0) & (blk - 1)
    ci = lax.broadcasted_iota(jnp.int32, (rows, nk), 1)
    lo_n = jnp.where(n >= nlb, 3 * blk, jnp.where(n >= 1, 0, blk))
    hi_n = jnp.where(n < nlb - 1, 3 * blk - 1, 2 * blk - 1)
    in_band = jnp.logical_and(ci >= jnp.maximum(ti, lo_n), ci <= jnp.minimum(ti + 2 * blk, hi_n))
    sc = jnp.where(jnp.logical_or(in_band, ci >= 3 * blk), sc, NEG)

    gi = lax.broadcasted_iota(jnp.int32, (rows, 1), 0) // blk
    snk = jnp.zeros((rows, 1), F32)
    for g in range(AT_G):
        snk = jnp.where(gi == g, sink_ref[kv * AT_G + g], snk)
    m = jnp.maximum(jnp.max(sc, axis=1, keepdims=True), snk)
    p = jnp.exp(sc - m)
    den = jnp.sum(p, axis=1, keepdims=True) + jnp.exp(snk - m)
    p_buf[...] = p.astype(BF16)
    o2 = jnp.dot(p_buf[...], v_buf[...], preferred_element_type=F32) / den
    for j in range(AT_G // 2):
        a = o2[(2 * j) * blk:(2 * j + 1) * blk, :]
        b = o2[(2 * j + 1) * blk:(2 * j + 2) * blk, :]
        o_ref[:, j * LANES:(j + 1) * LANES] = jnp.where(low, a, b).astype(o_ref.dtype)


def _attn_call(sink, aq, akd, avd, *, seq):
    bsz, t, _ = aq.shape
    ctx = t - seq
    blk = ATT_BLOCK
    nlb = seq // blk
    nb = t // blk
    gw = AT_G * AT_DH
    cblk = seq // ctx

    def kv_spec(fn):
        return pl.BlockSpec((None, blk, LANES), lambda b, kv, n: (b, fn(n), kv))

    prev = kv_spec(lambda n: jnp.maximum(n - 1, 0))
    cur = kv_spec(lambda n: n)
    nxt = kv_spec(lambda n: jnp.minimum(n + 1, nb - 1))
    cspec = pl.BlockSpec((None, ctx, LANES), lambda b, kv, n: (b, cblk, kv))
    kern = functools.partial(_attn_kernel, nlb=nlb, ctx=ctx)
    return pl.pallas_call(
        kern,
        grid=(bsz, AT_KV, nb),
        in_specs=[
            pl.BlockSpec(memory_space=pltpu.SMEM),
            pl.BlockSpec((None, blk, gw), lambda b, kv, n: (b, n, kv)),
            prev, cur, nxt, cspec, prev, cur, nxt, cspec,
        ],
        out_specs=pl.BlockSpec((None, blk, gw), lambda b, kv, n: (b, n, kv)),
        out_shape=jax.ShapeDtypeStruct((bsz, t, AT_W), BF16),
        scratch_shapes=[
            pltpu.VMEM((3 * blk + ctx, LANES), BF16),
            pltpu.VMEM((3 * blk + ctx, LANES), BF16),
            pltpu.VMEM((AT_G * blk, 3 * blk + ctx), BF16),
        ],
        compiler_params=_cparams(("arbitrary", "arbitrary", "arbitrary")),
        name="attn",
    )(sink, aq, akd, akd, akd, akd, avd, avd, avd, avd)


def _merge_kernel(x_ref, ml_ref, mc_ref, hs_ref, gg_ref, hf_ref, hb_ref, mo_ref, yc_ref, br_ref,
                  ng_ref, wb_ref, wo_ref, o_ref, *, seq, tm):
    t0 = pl.program_id(1) * tm
    rows = t0 + lax.broadcasted_iota(jnp.int32, (tm, 1), 0)
    is_ctx = rows >= seq
    d = x_ref.shape[1]
    ya = (hs_ref[...].astype(F32) * gg_ref[...].astype(F32)).astype(BF16)
    hsum = hf_ref[...] + hb_ref[...]
    parts = []
    for hd in range(ML_HEADS):
        hh = hsum[:, hd * ML_DH:(hd + 1) * ML_DH]
        parts.append(hh * lax.rsqrt(jnp.mean(hh * hh, axis=-1, keepdims=True) + EPS))
    yb = (jnp.concatenate(parts, axis=1) * ng_ref[...] * mo_ref[...].astype(F32)).astype(BF16)
    m = (br_ref[:, 0:d].astype(F32) * jnp.dot(ya, wb_ref[0], preferred_element_type=F32)
         + br_ref[:, d:2 * d].astype(F32) * jnp.dot(yb, wb_ref[1], preferred_element_type=F32)
         + br_ref[:, 2 * d:3 * d].astype(F32) * jnp.dot(yc_ref[...], wb_ref[2],
                                                       preferred_element_type=F32))
    y = jnp.dot(m.astype(BF16), wo_ref[...], preferred_element_type=F32)
    o_ref[...] = x_ref[...] + _row_mod(ml_ref, mc_ref, is_ctx, 5) * y


def _merge_call(x, ml, mc, hs, gg, hf, hb, mo, yc, br, ng, wb, wo, *, seq):
    bsz, t, d = x.shape
    tm = _pick_tile(t, (384, 128))
    kern = functools.partial(_merge_kernel, seq=seq, tm=tm)

    def tok(wd):
        return pl.BlockSpec((None, tm, wd), lambda b, i: (b, i, 0))

    return pl.pallas_call(
        kern,
        grid=(bsz, t // tm),
        in_specs=[
            tok(d),
            pl.BlockSpec((None, N_MOD, d), lambda b, i: (b, 0, 0)),
            _const_spec((N_MOD, d)),
            tok(D_RNN), tok(D_RNN), tok(ML_W), tok(ML_W), tok(ML_W), tok(AT_W),
            tok(N_BRANCH * d),
            _const_spec((1, ML_W)),
            _const_spec((N_BRANCH, BRANCH_W, d)),
            _const_spec((d, d)),
        ],
        out_specs=tok(d),
        out_shape=jax.ShapeDtypeStruct((bsz, t, d), F32),
        compiler_params=_cparams(("arbitrary", "arbitrary")),
        name="merge",
    )(x, ml, mc, hs, gg, hf, hb, mo, yc, br, ng, wb, wo)


def _pack_w_in(w_in, d_model):
    o_rg_x = 0
    o_ml_g = 2 * D_RNN + 4 * ML_W
    o_at_q = o_ml_g + 4 * ML_HEADS
    o_at_k = o_at_q + AT_W
    o_at_v = o_at_k + AT_KVW
    o_br = o_at_v + AT_KVW

    def dup(c0):
        hs = [w_in[:, c0 + i * AT_DH:c0 + (i + 1) * AT_DH] for i in range(AT_KV)]
        return [hs[i // 2] for i in range(2 * AT_KV)]

    gate = w_in[:, o_ml_g:o_at_q]
    pad = jnp.zeros((w_in.shape[0], LANES - gate.shape[1]), w_in.dtype)
    cols = ([w_in[:, o_rg_x:o_ml_g], w_in[:, o_at_q:o_at_k]] + dup(o_at_k) + dup(o_at_v)
            + [w_in[:, o_br:o_br + N_BRANCH * d_model], gate, pad])
    return jnp.concatenate(cols, axis=1).astype(BF16)


def _block_diag(w):
    n, bi, bj = w.shape
    eye = jnp.eye(n, dtype=w.dtype)
    return (eye[:, None, :, None] * w[:, :, None, :]).reshape(n * bi, n * bj)


def _rope_tables(seq, t):
    rows = seq // GRID_W
    row = jnp.repeat(jnp.arange(rows), GRID_W).astype(F32)
    col = jnp.broadcast_to(jnp.arange(GRID_W), (rows, GRID_W)).reshape(-1).astype(F32)
    half = AT_DH // 2
    inv = ROPE_BASE ** (-jnp.arange(0, half, 2, dtype=F32) / half)
    ar = row[:, None] * inv
    ac = col[:, None] * inv
    ang = jnp.concatenate([ar, ar, ac, ac], axis=-1)
    cos = jnp.concatenate([jnp.cos(ang), jnp.ones((t - seq, AT_DH), F32)], axis=0)
    sin = jnp.concatenate([jnp.sin(ang), jnp.zeros((t - seq, AT_DH), F32)], axis=0)
    cos = jnp.tile(cos, (1, LANES // AT_DH))
    sin = jnp.tile(sin, (1, LANES // AT_DH))
    first_half = (jnp.arange(LANES) % 32) < 16
    sin_a = jnp.where(first_half, -sin, 0.0)
    sin_b = jnp.where(first_half, 0.0, sin)
    return cos, sin_a, sin_b


def kernel(x, c, ctx, c_ctx, ada_w, ada_b, norm_g, ffn_w1, ffn_w3, ffn_w2, w_in, rg_conv_w,
           rg_conv_b, rg_wa, rg_ba, rg_wi, rg_bi, rg_lam, ml_gate_b, ml_norm_g, at_qn_g,
           at_kn_g, at_sink, w_branch, w_out):
    bsz, seq, d = x.shape
    nctx = ctx.shape[1]
    t = seq + nctx
    depth = ada_w.shape[0]

    xs = jnp.concatenate([x, ctx], axis=1)
    mod_rows = 2 * SUBLANES
    cc = jnp.concatenate([c, c_ctx[None, :], jnp.zeros((mod_rows - bsz - 1, d), F32)], axis=0)
    mods = _ada_call(cc, ada_w, ada_b).reshape(depth, mod_rows, N_MOD, d)
    cos, sin_a, sin_b = _rope_tables(seq, t)
    gmat = _block_diag(jnp.full((AT_HEADS, AT_DH, AT_DH), 1.0 / AT_DH, F32)).astype(BF16)

    for l in range(depth):
        ml = mods[l, :bsz]
        mc = mods[l, bsz]
        last = l == depth - 1
        w1 = [ffn_w1[l, i].astype(BF16) for i in range(2)]
        w3 = [ffn_w3[l, i].astype(BF16) for i in range(2)]
        w2 = [ffn_w2[l, i].astype(BF16) for i in range(2)]

        xs = _ffn_call(xs, ml, mc, norm_g[l, 0][None, :], w1[0], w3[0], w2[0],
                       base=0, seq=seq, out_rows=t)

        qg = jnp.tile(at_qn_g[l], AT_HEADS)[None, :]
        kg = jnp.tile(at_kn_g[l], 2 * AT_KV)[None, :]
        gb = jnp.concatenate([ml_gate_b[l], jnp.zeros((LANES - 4 * ML_HEADS,), F32)])[None, :]
        (rgx, rgg, mq, mk, mv, mo, aq, akd, avd, br, mg) = _proj_call(
            xs, ml, mc, norm_g[l, 1][None, :], _pack_w_in(w_in[l], d), gmat, qg, kg, gb,
            cos, sin_a, sin_b, seq=seq)

        wcat = jnp.stack([jnp.concatenate([_block_diag(rg_wa[l, dr]), _block_diag(rg_wi[l, dr])],
                                          axis=1) for dr in range(2)]).astype(BF16)
        bias = jnp.concatenate([rg_ba[l], rg_bi[l]], axis=1)[:, None, :]
        clam = (-LRU_C * jax.nn.softplus(-rg_lam[l]))[:, None, :]
        rhf, rhb = _rglru_call(jnp.swapaxes(rgx, 0, 1), rg_conv_w[l], rg_conv_b[l][None, :],
                               wcat, bias, clam, seq=seq)
        hs = jnp.swapaxes(rhf + rhb, 0, 1).astype(BF16)

        gt = jnp.swapaxes(mg[:, :, :4 * ML_HEADS], 1, 2)
        rowp, colsrc = _mlgate_call(gt, seq=seq)
        colp = jnp.swapaxes(colsrc, 1, 2)
        mkt = jnp.swapaxes(mk, 1, 2)
        mhf, mhb = _mlstm_call(mq, mkt, mv, rowp, colp, seq=seq)

        yc = _attn_call(at_sink[l], aq, akd, avd, seq=seq)

        xs = _merge_call(xs, ml, mc, hs, rgg, mhf, mhb, mo, yc, br, ml_norm_g[l][None, :],
                         w_branch[l].astype(BF16), w_out[l].astype(BF16), seq=seq)

        xs = _ffn_call(xs, ml, mc, norm_g[l, 2][None, :], w1[1], w3[1], w2[1],
                       base=6, seq=seq, out_rows=seq if last else t)
    return xs
```

```python
import functools

import jax
import jax.numpy as jnp
from jax import lax
from jax.experimental import pallas as pl
from jax.experimental.pallas import tpu as pltpu

F32 = jnp.float32
BF16 = jnp.bfloat16

EPS = 1e-6
NEG = -1e30
LOG2E = 1.4426950408889634
N_MOD = 9
GRID_W = 64
ROPE_BASE = 10000.0

D_RNN = 512
RNN_BLOCKS = 8
RNN_BLOCK = D_RNN // RNN_BLOCKS
CONV_W = 4
CONV_LEFT = 2
CONV_RIGHT = CONV_W - 1 - CONV_LEFT
LRU_C = 8.0

ML_HEADS = 4
ML_DH = 128
ML_W = ML_HEADS * ML_DH
ML_CHUNK = 128
ML_UNITS = 2 * ML_HEADS

AT_HEADS = 8
AT_KV = 2
AT_DH = 64
AT_G = AT_HEADS // AT_KV
AT_W = AT_HEADS * AT_DH
AT_KVW = AT_KV * AT_DH
ATT_BLOCK = 128

N_BRANCH = 3
BRANCH_W = 512

LANES = 128
SUBLANES = 8
VMEM_LIMIT = 56 * 1024 * 1024

P_RGX = 0
P_RGG = P_RGX + D_RNN
P_MQ = P_RGG + D_RNN
P_MK = P_MQ + ML_W
P_MV = P_MK + ML_W
P_MO = P_MV + ML_W
P_AQ = P_MO + ML_W
P_AK = P_AQ + AT_W
P_AV = P_AK + 2 * AT_KVW
P_BR = P_AV + 2 * AT_KVW


def _cparams(sem):
    return pltpu.CompilerParams(dimension_semantics=sem, vmem_limit_bytes=VMEM_LIMIT)


def _const_spec(shape):
    nd = len(shape)
    return pl.BlockSpec(shape, lambda *_: (0,) * nd, pipeline_mode=pl.Buffered(1))


def _pick_tile(total, candidates):
    for c in candidates:
        if total % c == 0:
            return c
    raise ValueError(f"no tile for {total}")


def _sigmoid(x):
    return jax.nn.sigmoid(x)


def _ln_mod(x, g, shift, scale):
    ms = jnp.mean(x * x, axis=-1, keepdims=True)
    return x * lax.rsqrt(ms + EPS) * g * (1.0 + scale) + shift


def _row_mod(ml_ref, mc_ref, is_ctx, i):
    return jnp.where(is_ctx, mc_ref[i:i + 1, :], ml_ref[i:i + 1, :])


def _ada_kernel(cc_ref, w_ref, b_ref, o_ref):
    cc = cc_ref[...]
    s = cc * _sigmoid(cc)
    o_ref[...] = jnp.dot(s.astype(BF16), w_ref[...].astype(BF16),
                         preferred_element_type=F32) + b_ref[...]


def _ada_call(cc, ada_w, ada_b):
    depth, d, nout = ada_w.shape
    rows = cc.shape[0]
    tn = _pick_tile(nout, (1536, 1024, 512, 256, 128))
    return pl.pallas_call(
        _ada_kernel,
        grid=(depth, nout // tn),
        in_specs=[
            pl.BlockSpec((rows, d), lambda l, j: (0, 0)),
            pl.BlockSpec((None, d, tn), lambda l, j: (l, 0, j)),
            pl.BlockSpec((None, 1, tn), lambda l, j: (l, 0, j)),
        ],
        out_specs=pl.BlockSpec((None, rows, tn), lambda l, j: (l, 0, j)),
        out_shape=jax.ShapeDtypeStruct((depth, rows, nout), F32),
        compiler_params=_cparams(("arbitrary", "arbitrary")),
        name="ada_mod",
    )(cc, ada_w, ada_b.reshape(depth, 1, nout))


def _ffn_kernel(x_ref, ml_ref, mc_ref, g_ref, w1_ref, w3_ref, w2_ref, o_ref, act_ref,
                *, base, seq, tm, ck):
    t0 = pl.program_id(1) * tm
    rows = t0 + lax.broadcasted_iota(jnp.int32, (tm, 1), 0)
    is_ctx = rows >= seq
    x = x_ref[...]
    h = _ln_mod(x, g_ref[...], _row_mod(ml_ref, mc_ref, is_ctx, base),
                _row_mod(ml_ref, mc_ref, is_ctx, base + 1)).astype(BF16)
    dff = w1_ref.shape[1]
    for j in range(dff // ck):
        a = jnp.dot(h, w1_ref[:, j * ck:(j + 1) * ck], preferred_element_type=F32)
        b = jnp.dot(h, w3_ref[:, j * ck:(j + 1) * ck], preferred_element_type=F32)
        act_ref[:, j * ck:(j + 1) * ck] = (a * _sigmoid(a) * b).astype(BF16)
    y = jnp.dot(act_ref[...], w2_ref[...], preferred_element_type=F32)
    o_ref[...] = x + 0.5 * _row_mod(ml_ref, mc_ref, is_ctx, base + 2) * y


def _ffn_call(x, ml, mc, g, w1, w3, w2, *, base, seq, out_rows):
    bsz, t, d = x.shape
    dff = w1.shape[1]
    tm = _pick_tile(t, (768, 384, 128))
    ck = _pick_tile(dff, (256, 128))
    kern = functools.partial(_ffn_kernel, base=base, seq=seq, tm=tm, ck=ck)
    return pl.pallas_call(
        kern,
        grid=(bsz, pl.cdiv(out_rows, tm)),
        in_specs=[
            pl.BlockSpec((None, tm, d), lambda b, i: (b, i, 0)),
            pl.BlockSpec((None, N_MOD, d), lambda b, i: (b, 0, 0)),
            _const_spec((N_MOD, d)),
            _const_spec((1, d)),
            _const_spec((d, dff)),
            _const_spec((d, dff)),
            _const_spec((dff, d)),
        ],
        out_specs=pl.BlockSpec((None, tm, d), lambda b, i: (b, i, 0)),
        out_shape=jax.ShapeDtypeStruct((bsz, out_rows, d), F32),
        scratch_shapes=[pltpu.VMEM((tm, dff), BF16)],
        compiler_params=_cparams(("arbitrary", "arbitrary")),
        name="ffn",
    )(x, ml, mc, g, w1, w3, w2)


def _gelu_tanh(x):
    return 0.5 * x * (1.0 + jnp.tanh(0.7978845608028654 * (x + 0.044715 * (x * x * x))))


def _head_norm(x, gmat, g):
    sq = x * x
    hi = sq.astype(BF16)
    lo = (sq - hi.astype(F32)).astype(BF16)
    ms = (jnp.dot(hi, gmat, preferred_element_type=F32)
          + jnp.dot(lo, gmat, preferred_element_type=F32))
    return x * lax.rsqrt(ms + EPS) * g


def _rope(x, cos, sin_a, sin_b):
    parts = []
    for j in range(x.shape[1] // LANES):
        xj = x[:, j * LANES:(j + 1) * LANES]
        parts.append(xj * cos + pltpu.roll(xj, LANES - 16, 1) * sin_a
                     + pltpu.roll(xj, 16, 1) * sin_b)
    return jnp.concatenate(parts, axis=1)


def _proj_kernel(x_ref, ml_ref, mc_ref, g_ref, w_ref, gmat_ref, qg_ref, kg_ref, gb_ref,
                 cos_ref, sa_ref, sb_ref,
                 rgx_o, rgg_o, mq_o, mk_o, mv_o, mo_o, aq_o, ak_o, av_o, br_o, mg_o,
                 *, seq, tm, d_model):
    t0 = pl.program_id(1) * tm
    rows = t0 + lax.broadcasted_iota(jnp.int32, (tm, 1), 0)
    is_ctx = rows >= seq
    h = _ln_mod(x_ref[...], g_ref[...], _row_mod(ml_ref, mc_ref, is_ctx, 3),
                _row_mod(ml_ref, mc_ref, is_ctx, 4)).astype(BF16)

    def seg(c0, w):
        return jnp.dot(h, w_ref[:, c0:c0 + w], preferred_element_type=F32)

    rgx_o[...] = seg(P_RGX, D_RNN).reshape(rgx_o.shape)
    rgg_o[...] = _gelu_tanh(seg(P_RGG, D_RNN)).astype(BF16)
    mq_o[...] = seg(P_MQ, ML_W).astype(BF16)
    mk_o[...] = (seg(P_MK, ML_W) * (ML_DH ** -0.5)).astype(BF16)
    mv_o[...] = seg(P_MV, ML_W).astype(BF16)
    mo_o[...] = _sigmoid(seg(P_MO, ML_W)).astype(BF16)
    cos, sin_a, sin_b = cos_ref[...], sa_ref[...], sb_ref[...]
    gmat = gmat_ref[...]
    q = _head_norm(seg(P_AQ, AT_W), gmat, qg_ref[...])
    aq_o[...] = (_rope(q, cos, sin_a, sin_b) * (AT_DH ** -0.5 * LOG2E)).astype(BF16)
    k = _head_norm(seg(P_AK, 2 * AT_KVW), gmat[:2 * AT_KVW, :2 * AT_KVW], kg_ref[...])
    ak_o[...] = _rope(k, cos, sin_a, sin_b).astype(BF16)
    av_o[...] = seg(P_AV, 2 * AT_KVW).astype(BF16)
    nbr = N_BRANCH * d_model
    for c0 in range(0, nbr, 512):
        br_o[:, c0:c0 + 512] = _sigmoid(seg(P_BR + c0, 512)).astype(BF16)
    mg_o[...] = seg(P_BR + nbr, LANES) + gb_ref[...]


def _proj_call(x, ml, mc, g, w, gmat, qg, kg, gb, cos, sin_a, sin_b, *, seq):
    bsz, t, d = x.shape
    ncol = w.shape[1]
    tm = _pick_tile(t, (384, 128))
    kern = functools.partial(_proj_kernel, seq=seq, tm=tm, d_model=d)
    widths = [(D_RNN, F32), (D_RNN, BF16), (ML_W, BF16), (ML_W, BF16), (ML_W, BF16),
              (ML_W, BF16), (AT_W, BF16), (2 * AT_KVW, BF16), (2 * AT_KVW, BF16),
              (N_BRANCH * d, BF16), (LANES, F32)]
    tab_spec = pl.BlockSpec((tm, LANES), lambda b, i: (i, 0))
    out_specs = [pl.BlockSpec((None, tm, wd), lambda b, i: (b, i, 0)) for wd, _ in widths]
    out_shape = [jax.ShapeDtypeStruct((bsz, t, wd), dt) for wd, dt in widths]
    out_specs[0] = pl.BlockSpec((tm // SUBLANES, None, SUBLANES, D_RNN), lambda b, i: (i, b, 0, 0))
    out_shape[0] = jax.ShapeDtypeStruct((t // SUBLANES, bsz, SUBLANES, D_RNN), F32)
    return pl.pallas_call(
        kern,
        grid=(bsz, t // tm),
        in_specs=[
            pl.BlockSpec((None, tm, d), lambda b, i: (b, i, 0)),
            pl.BlockSpec((None, N_MOD, d), lambda b, i: (b, 0, 0)),
            _const_spec((N_MOD, d)),
            _const_spec((1, d)),
            _const_spec((d, ncol)),
            _const_spec((AT_W, AT_W)),
            _const_spec((1, AT_W)),
            _const_spec((1, 2 * AT_KVW)),
            _const_spec((1, LANES)),
            tab_spec, tab_spec, tab_spec,
        ],
        out_specs=out_specs,
        out_shape=out_shape,
        compiler_params=_cparams(("arbitrary", "arbitrary")),
        name="in_proj",
    )(x, ml, mc, g, w, gmat, qg, kg, gb, cos, sin_a, sin_b)


def _scan_chunk(s, rev, nl, nc):
    if rev:
        return jnp.where(s < nc, nl + nc - 1 - s, nl - 1 - (s - nc))
    return jnp.where(s < nc, nl + s, s - nc)


def _rglru_kernel(xf_ref, xfp_ref, xfn_ref, xb_ref, xbp_ref, xbn_ref, cw_ref, cb_ref, w_ref,
                  bias_ref, clam_ref, hf_ref, hb_ref, ext, a_s, b_s, h_s, hstate,
                  *, nl, nc, tc, sub):
    s = pl.program_id(0)
    bsz, c = hstate.shape[1], hstate.shape[2]
    ntb = tc // SUBLANES

    @pl.when(s == 0)
    def _():
        hstate[...] = jnp.zeros_like(hstate)

    cw = cw_ref[...]
    cb = cb_ref[...]
    for d, (x_ref, xp_ref, xn_ref) in enumerate(((xf_ref, xfp_ref, xfn_ref),
                                                (xb_ref, xbp_ref, xbn_ref))):
        chunk = _scan_chunk(s, d == 1, nl, nc)
        first = jnp.logical_or(chunk == 0, chunk == nl)
        last = jnp.logical_or(chunk == nl - 1, chunk == nl + nc - 1)
        prev = jnp.swapaxes(xp_ref[0], 0, 1)[SUBLANES - CONV_LEFT:]
        nxt = jnp.swapaxes(xn_ref[0], 0, 1)[:CONV_RIGHT]
        ext[d, 0:CONV_LEFT] = jnp.where(first, 0.0, prev)
        for i in range(ntb):
            o = CONV_LEFT + i * SUBLANES
            ext[d, o:o + SUBLANES] = jnp.swapaxes(x_ref[i], 0, 1)
        ext[d, CONV_LEFT + tc:CONV_LEFT + tc + CONV_RIGHT] = jnp.where(last, 0.0, nxt)

        def gates(i, carry, d=d):
            t0 = pl.multiple_of(i * sub, sub)
            u = cb
            for k in range(CONV_W):
                u = u + ext[d, pl.ds(t0 + k, sub)] * cw[k:k + 1, :]
            u2 = u.reshape(sub * bsz, c)
            z = jnp.dot(u2.astype(BF16), w_ref[d], preferred_element_type=F32) + bias_ref[d]
            r = 0.5 * jnp.tanh(0.5 * z[:, :c]) + 0.5
            gi = 0.5 * jnp.tanh(0.5 * z[:, c:]) + 0.5
            a = jnp.exp(clam_ref[d] * r)
            s1 = 1.0 - a * a
            bb = (s1 * lax.rsqrt(jnp.maximum(s1, 1e-36))) * (gi * u2)
            a_s[d, pl.ds(t0, sub)] = a.reshape(sub, bsz, c)
            b_s[d, pl.ds(t0, sub)] = bb.reshape(sub, bsz, c)
            return carry

        lax.fori_loop(0, tc // sub, gates, 0)

    def step(k, hs):
        hf, hb = hs
        kb = tc - 1 - k
        hf = a_s[0, k] * hf + b_s[0, k]
        hb = a_s[1, kb] * hb + b_s[1, kb]
        h_s[0, k] = hf
        h_s[1, kb] = hb
        return hf, hb

    hf, hb = lax.fori_loop(0, tc, step, (hstate[0], hstate[1]), unroll=8)
    hstate[0] = hf
    hstate[1] = hb
    for d, o_ref in enumerate((hf_ref, hb_ref)):
        for i in range(ntb):
            o_ref[i] = jnp.swapaxes(h_s[d, i * SUBLANES:(i + 1) * SUBLANES], 0, 1)


def _rglru_call(rgx, cw, cb, w, bias, clam, *, seq):
    ntb_all, bsz, _, c = rgx.shape
    t = ntb_all * SUBLANES
    tc = 128
    sub = 32
    assert seq % tc == 0 and (t - seq) % tc == 0
    nl, nc = seq // tc, (t - seq) // tc
    ntb = tc // SUBLANES

    def specs(rev):
        chunk = functools.partial(_scan_chunk, rev=rev, nl=nl, nc=nc)
        return [
            pl.BlockSpec((ntb, bsz, SUBLANES, c), lambda s: (chunk(s), 0, 0, 0)),
            pl.BlockSpec((1, bsz, SUBLANES, c),
                         lambda s: (jnp.maximum(chunk(s) * ntb - 1, 0), 0, 0, 0)),
            pl.BlockSpec((1, bsz, SUBLANES, c),
                         lambda s: (jnp.minimum((chunk(s) + 1) * ntb, ntb_all - 1), 0, 0, 0)),
        ]

    kern = functools.partial(_rglru_kernel, nl=nl, nc=nc, tc=tc, sub=sub)
    fspec, bspec = specs(False), specs(True)
    return pl.pallas_call(
        kern,
        grid=(nl + nc,),
        in_specs=fspec + bspec + [
            _const_spec((CONV_W, c)),
            _const_spec((1, c)),
            _const_spec((2, c, 2 * c)),
            _const_spec((2, 1, 2 * c)),
            _const_spec((2, 1, c)),
        ],
        out_specs=[fspec[0], bspec[0]],
        out_shape=[jax.ShapeDtypeStruct(rgx.shape, F32)] * 2,
        scratch_shapes=[
            pltpu.VMEM((2, tc + CONV_W - 1, bsz, c), F32),
            pltpu.VMEM((2, tc, bsz, c), F32),
            pltpu.VMEM((2, tc, bsz, c), F32),
            pltpu.VMEM((2, tc, bsz, c), F32),
            pltpu.VMEM((2, bsz, c), F32),
        ],
        compiler_params=_cparams(("arbitrary",)),
        name="rglru",
    )(rgx, rgx, rgx, rgx, rgx, rgx, cw, cb, w, bias, clam)


def _log_sigmoid(x):
    return -(jnp.maximum(-x, 0.0) + jnp.log1p(jnp.exp(-jnp.abs(x))))


def _lane_scan(x, op, ident, rev):
    n = x.shape[1]
    lane = lax.broadcasted_iota(jnp.int32, x.shape, 1)
    sh = 1
    while sh < n:
        if rev:
            x = op(x, jnp.where(lane < n - sh, pltpu.roll(x, n - sh, 1), ident))
        else:
            x = op(x, jnp.where(lane >= sh, pltpu.roll(x, sh, 1), ident))
        sh *= 2
    return x


def _mlgate_kernel(g_ref, row_o, col_o, cb_s, gg_s, ml_s, tot_s, wm_s, mo_s, mn_s, *, nl, nc):
    L = ML_CHUNK
    H = ML_HEADS
    U = ML_UNITS
    is_bwd = lax.broadcasted_iota(jnp.int32, (U, L), 0) >= H
    for c in range(nl + nc):
        lanes = slice(c * L, (c + 1) * L)
        g16 = g_ref[:, lanes]
        li = jnp.concatenate([g16[0:H], g16[2 * H:3 * H]], axis=0)
        lf = _log_sigmoid(jnp.concatenate([g16[H:2 * H], g16[3 * H:4 * H]], axis=0))
        pre = _lane_scan(lf, jnp.add, 0.0, False)
        suf = _lane_scan(lf, jnp.add, 0.0, True)
        tot = pre + suf - lf
        cb = jnp.where(is_bwd, suf, pre)
        gg = li - cb
        pmax = _lane_scan(gg, jnp.maximum, NEG, False)
        smax = _lane_scan(gg, jnp.maximum, NEG, True)
        cb_s[:, lanes] = cb
        gg_s[:, lanes] = gg
        ml_s[:, lanes] = cb + jnp.where(is_bwd, smax, pmax)
        tot_s[:, lanes] = tot
        wm_s[:, lanes] = tot + jnp.maximum(pmax, smax)

    m = jnp.zeros((U, L), F32)
    for s in range(nl + nc):
        cf = nl + s if s < nc else s - nc
        cr = nl + nc - 1 - s if s < nc else nl - 1 - (s - nc)
        lf_, lr_ = slice(cf * L, (cf + 1) * L), slice(cr * L, (cr + 1) * L)
        tot = jnp.where(is_bwd, tot_s[:, lr_], tot_s[:, lf_])
        wm = jnp.where(is_bwd, wm_s[:, lr_], wm_s[:, lf_])
        m_new = jnp.maximum(tot + m, wm)
        mo_s[0:H, lf_] = m[0:H]
        mo_s[H:U, lr_] = m[H:U]
        mn_s[0:H, lf_] = m_new[0:H]
        mn_s[H:U, lr_] = m_new[H:U]
        m = m_new

    cb = cb_s[...]
    gg = gg_s[...]
    tot = tot_s[...]
    mo = mo_s[...]
    mn = mn_s[...]
    m_t = jnp.maximum(cb + mo, ml_s[...])
    row_o[0:U, :] = gg
    row_o[U:2 * U, :] = jnp.exp(tot + gg - mn)
    row_o[2 * U:3 * U, :] = jnp.exp(tot + mo - mn)
    col_o[0:U, :] = cb - m_t
    col_o[U:2 * U, :] = jnp.exp(cb + mo - m_t)
    col_o[2 * U:3 * U, :] = jnp.exp(-m_t)


def _mlgate_call(gt, *, seq):
    bsz, ng, t = gt.shape
    nl, nc = seq // ML_CHUNK, (t - seq) // ML_CHUNK
    kern = functools.partial(_mlgate_kernel, nl=nl, nc=nc)
    out = pl.BlockSpec((None, 3 * ML_UNITS, t), lambda b: (b, 0, 0))
    return pl.pallas_call(
        kern,
        grid=(bsz,),
        in_specs=[pl.BlockSpec((None, ng, t), lambda b: (b, 0, 0))],
        out_specs=[out, out],
        out_shape=[jax.ShapeDtypeStruct((bsz, 3 * ML_UNITS, t), F32)] * 2,
        scratch_shapes=[pltpu.VMEM((ML_UNITS, t), F32)] * 7,
        compiler_params=_cparams(("arbitrary",)),
        name="mlstm_gates",
    )(gt)


def _mlstm_kernel(qf_ref, ktf_ref, vf_ref, rf_ref, cf_ref, qb_ref, ktb_ref, vb_ref, rb_ref, cb_ref,
                  hf_ref, hb_ref, c_st):
    s = pl.program_id(1)

    @pl.when(s == 0)
    def _():
        c_st[...] = jnp.zeros_like(c_st)

    L, dh, U = ML_CHUNK, ML_DH, ML_UNITS
    ti = lax.broadcasted_iota(jnp.int32, (L, L), 0)
    si = lax.broadcasted_iota(jnp.int32, (L, L), 1)
    ones = jnp.ones((L, dh), BF16)
    for d, (q_ref, kt_ref, v_ref, r_ref, c_ref, o_ref) in enumerate(
            ((qf_ref, ktf_ref, vf_ref, rf_ref, cf_ref, hf_ref),
             (qb_ref, ktb_ref, vb_ref, rb_ref, cb_ref, hb_ref))):
        tri = (si >= ti) if d == 1 else (si <= ti)
        rows = r_ref[...]
        cols = c_ref[...]
        for hd in range(ML_HEADS):
            u = d * ML_HEADS + hd
            sl = slice(hd * dh, (hd + 1) * dh)
            q = q_ref[:, sl]
            kt = kt_ref[sl, :]
            v_ext = jnp.concatenate([v_ref[:, sl], ones], axis=1)
            g_row = rows[u:u + 1, :]
            w_row = rows[U + u:U + u + 1, :]
            a_row = rows[2 * U + u:2 * U + u + 1, :]
            x_col = cols[:, u:u + 1]
            dec = cols[:, U + u:U + u + 1]
            emt = cols[:, 2 * U + u:2 * U + u + 1]
            c_old = c_st[u]
            dm = jnp.exp(jnp.where(tri, x_col + g_row, NEG))
            sc = jnp.dot(q, kt, preferred_element_type=F32) * dm
            ab = (jnp.dot(sc.astype(BF16), v_ext, preferred_element_type=F32)
                  + dec * jnp.dot(q, c_old.astype(BF16), preferred_element_type=F32))
            o_ref[:, sl] = ab[:, :dh] / jnp.maximum(jnp.abs(ab[:, dh:]), emt)
            ktw = (kt.astype(F32) * w_row).astype(BF16)
            c_st[u] = (jnp.concatenate([a_row, a_row], axis=1) * c_old
                       + jnp.dot(ktw, v_ext, preferred_element_type=F32))


def _mlstm_call(mq, mkt, mv, rowp, colp, *, seq):
    bsz, t, w = mq.shape
    nl, nc = seq // ML_CHUNK, (t - seq) // ML_CHUNK
    cf = functools.partial(_scan_chunk, rev=False, nl=nl, nc=nc)
    cr = functools.partial(_scan_chunk, rev=True, nl=nl, nc=nc)
    nr = rowp.shape[1]

    def tok(ch, wd):
        return pl.BlockSpec((None, ML_CHUNK, wd), lambda b, s: (b, ch(s), 0))

    def tr(ch, rows):
        return pl.BlockSpec((None, rows, ML_CHUNK), lambda b, s: (b, 0, ch(s)))

    def side(ch):
        return [tok(ch, w), tr(ch, w), tok(ch, w), tr(ch, nr), tok(ch, nr)]

    return pl.pallas_call(
        _mlstm_kernel,
        grid=(bsz, nl + nc),
        in_specs=side(cf) + side(cr),
        out_specs=[tok(cf, w), tok(cr, w)],
        out_shape=[jax.ShapeDtypeStruct((bsz, t, w), F32)] * 2,
        scratch_shapes=[pltpu.VMEM((ML_UNITS, ML_DH, 2 * ML_DH), F32)],
        compiler_params=_cparams(("arbitrary", "arbitrary")),
        name="mlstm",
    )(mq, mkt, mv, rowp, colp, mq, mkt, mv, rowp, colp)


def _attn_kernel(sink_ref, q_ref, k_ref, v_ref, o_ref, vext, *, seq, qbs):
    kv = pl.program_id(1)
    i = pl.program_id(2)
    blk = ATT_BLOCK
    t = k_ref.shape[0]
    ctx = t - seq
    nlb = seq // blk
    band = 3 * blk

    @pl.when(i == 0)
    def _():
        vext[:, 0:LANES] = v_ref[...]
        vext[:, LANES:2 * LANES] = jnp.ones((t, LANES), BF16)

    lane = lax.broadcasted_iota(jnp.int32, (blk, LANES), 1)
    low = lane < AT_DH
    zero = jnp.zeros((blk, LANES), BF16)
    rows = AT_G * blk
    gi = lax.broadcasted_iota(jnp.int32, (rows, 1), 0) // blk
    snk = jnp.zeros((rows, 1), F32)
    for g in range(AT_G):
        snk = jnp.where(gi == g, sink_ref[kv * AT_G + g] * LOG2E, snk)
    ti = lax.broadcasted_iota(jnp.int32, (blk, band), 0)
    ci = lax.broadcasted_iota(jnp.int32, (blk, band), 1)
    nt = (((1,), (1,)), ((), ()))
    k_ctx = k_ref[seq:t, :]
    v_ctx = vext[seq:t, :]

    for j in range(qbs):
        qb = i * qbs + j
        s0 = pl.multiple_of(jnp.clip((qb - 1) * blk, 0, t - band), blk)
        rel = ci - ti + (s0 - qb * blk + blk)
        lim = jnp.where(qb < nlb, seq - s0, 0)
        ok = jnp.logical_and(jnp.logical_and(rel >= 0, rel <= 2 * blk), ci < lim)
        bias = jnp.where(ok, 0.0, NEG)
        bias = jnp.concatenate([bias] * AT_G, axis=0)

        qs = []
        for h2 in range(AT_G // 2):
            qj = q_ref[j * blk:(j + 1) * blk, h2 * LANES:(h2 + 1) * LANES]
            qs += [jnp.where(low, qj, zero), jnp.where(low, zero, qj)]
        q_all = jnp.concatenate(qs, axis=0)
        s_band = lax.dot_general(q_all, k_ref[pl.ds(s0, band), :], nt,
                                 preferred_element_type=F32) + bias
        s_ctx = lax.dot_general(q_all, k_ctx, nt, preferred_element_type=F32)
        m = jnp.maximum(jnp.maximum(jnp.max(s_band, axis=1, keepdims=True),
                                    jnp.max(s_ctx, axis=1, keepdims=True)), snk)
        p_band = jnp.exp2(s_band - m).astype(BF16)
        p_ctx = jnp.exp2(s_ctx - m).astype(BF16)
        o2 = (jnp.dot(p_band, vext[pl.ds(s0, band), :], preferred_element_type=F32)
              + jnp.dot(p_ctx, v_ctx, preferred_element_type=F32))
        o2 = o2[:, :LANES] / (o2[:, LANES:] + jnp.exp2(snk - m))
        for h2 in range(AT_G // 2):
            a = o2[(2 * h2) * blk:(2 * h2 + 1) * blk, :]
            b = o2[(2 * h2 + 1) * blk:(2 * h2 + 2) * blk, :]
            o_ref[j * blk:(j + 1) * blk, h2 * LANES:(h2 + 1) * LANES] = (
                jnp.where(low, a, b).astype(o_ref.dtype))


def _attn_call(sink, aq, akd, avd, *, seq):
    bsz, t, _ = aq.shape
    blk = ATT_BLOCK
    qbs = 2
    assert t % (qbs * blk) == 0 and t >= 3 * blk
    gw = AT_G * AT_DH
    kern = functools.partial(_attn_kernel, seq=seq, qbs=qbs)
    whole = pl.BlockSpec((None, t, LANES), lambda b, kv, i: (b, 0, kv))
    return pl.pallas_call(
        kern,
        grid=(bsz, AT_KV, t // (qbs * blk)),
        in_specs=[
            pl.BlockSpec(memory_space=pltpu.SMEM),
            pl.BlockSpec((None, qbs * blk, gw), lambda b, kv, i: (b, i, kv)),
            whole, whole,
        ],
        out_specs=pl.BlockSpec((None, qbs * blk, gw), lambda b, kv, i: (b, i, kv)),
        out_shape=jax.ShapeDtypeStruct((bsz, t, AT_W), BF16),
        scratch_shapes=[pltpu.VMEM((t, 2 * LANES), BF16)],
        compiler_params=_cparams(("arbitrary", "arbitrary", "arbitrary")),
        name="attn",
    )(sink, aq, akd, avd)


def _merge_kernel(x_ref, ml_ref, mc_ref, rf_ref, rb_ref, gg_ref, hf_ref, hb_ref, mo_ref, yc_ref,
                  br_ref, ng_ref, wb_ref, wo_ref, o_ref, *, seq, tm):
    t0 = pl.program_id(1) * tm
    rows = t0 + lax.broadcasted_iota(jnp.int32, (tm, 1), 0)
    is_ctx = rows >= seq
    d = x_ref.shape[1]
    rsum = (rf_ref[...] + rb_ref[...]).reshape(tm, D_RNN)
    ya = (rsum * gg_ref[...].astype(F32)).astype(BF16)
    hsum = hf_ref[...] + hb_ref[...]
    parts = []
    for hd in range(ML_HEADS):
        hh = hsum[:, hd * ML_DH:(hd + 1) * ML_DH]
        parts.append(hh * lax.rsqrt(jnp.mean(hh * hh, axis=-1, keepdims=True) + EPS))
    yb = (jnp.concatenate(parts, axis=1) * ng_ref[...] * mo_ref[...].astype(F32)).astype(BF16)
    m = (br_ref[:, 0:d].astype(F32) * jnp.dot(ya, wb_ref[0], preferred_element_type=F32)
         + br_ref[:, d:2 * d].astype(F32) * jnp.dot(yb, wb_ref[1], preferred_element_type=F32)
         + br_ref[:, 2 * d:3 * d].astype(F32) * jnp.dot(yc_ref[...], wb_ref[2],
                                                       preferred_element_type=F32))
    y = jnp.dot(m.astype(BF16), wo_ref[...], preferred_element_type=F32)
    o_ref[...] = x_ref[...] + _row_mod(ml_ref, mc_ref, is_ctx, 5) * y


def _merge_call(x, ml, mc, rf, rb, gg, hf, hb, mo, yc, br, ng, wb, wo, *, seq):
    bsz, t, d = x.shape
    tm = _pick_tile(t, (384, 128))
    kern = functools.partial(_merge_kernel, seq=seq, tm=tm)

    def tok(wd):
        return pl.BlockSpec((None, tm, wd), lambda b, i: (b, i, 0))

    rspec = pl.BlockSpec((tm // SUBLANES, None, SUBLANES, D_RNN), lambda b, i: (i, b, 0, 0))

    return pl.pallas_call(
        kern,
        grid=(bsz, t // tm),
        in_specs=[
            tok(d),
            pl.BlockSpec((None, N_MOD, d), lambda b, i: (b, 0, 0)),
            _const_spec((N_MOD, d)),
            rspec, rspec, tok(D_RNN), tok(ML_W), tok(ML_W), tok(ML_W), tok(AT_W),
            tok(N_BRANCH * d),
            _const_spec((1, ML_W)),
            _const_spec((N_BRANCH, BRANCH_W, d)),
            _const_spec((d, d)),
        ],
        out_specs=tok(d),
        out_shape=jax.ShapeDtypeStruct((bsz, t, d), F32),
        compiler_params=_cparams(("arbitrary", "arbitrary")),
        name="merge",
    )(x, ml, mc, rf, rb, gg, hf, hb, mo, yc, br, ng, wb, wo)


def _pack_w_in(w_in, d_model):
    o_rg_x = 0
    o_ml_g = 2 * D_RNN + 4 * ML_W
    o_at_q = o_ml_g + 4 * ML_HEADS
    o_at_k = o_at_q + AT_W
    o_at_v = o_at_k + AT_KVW
    o_br = o_at_v + AT_KVW

    def dup(c0):
        hs = [w_in[:, c0 + i * AT_DH:c0 + (i + 1) * AT_DH] for i in range(AT_KV)]
        return [hs[i // 2] for i in range(2 * AT_KV)]

    gate = w_in[:, o_ml_g:o_at_q]
    pad = jnp.zeros((w_in.shape[0], LANES - gate.shape[1]), w_in.dtype)
    cols = ([w_in[:, o_rg_x:o_ml_g], w_in[:, o_at_q:o_at_k]] + dup(o_at_k) + dup(o_at_v)
            + [w_in[:, o_br:o_br + N_BRANCH * d_model], gate, pad])
    return jnp.concatenate(cols, axis=1).astype(BF16)


def _block_diag(w):
    n, bi, bj = w.shape
    eye = jnp.eye(n, dtype=w.dtype)
    return (eye[:, None, :, None] * w[:, :, None, :]).reshape(n * bi, n * bj)


def _rope_tables(seq, t):
    rows = seq // GRID_W
    row = jnp.repeat(jnp.arange(rows), GRID_W).astype(F32)
    col = jnp.broadcast_to(jnp.arange(GRID_W), (rows, GRID_W)).reshape(-1).astype(F32)
    half = AT_DH // 2
    inv = ROPE_BASE ** (-jnp.arange(0, half, 2, dtype=F32) / half)
    ar = row[:, None] * inv
    ac = col[:, None] * inv
    ang = jnp.concatenate([ar, ar, ac, ac], axis=-1)
    cos = jnp.concatenate([jnp.cos(ang), jnp.ones((t - seq, AT_DH), F32)], axis=0)
    sin = jnp.concatenate([jnp.sin(ang), jnp.zeros((t - seq, AT_DH), F32)], axis=0)
    cos = jnp.tile(cos, (1, LANES // AT_DH))
    sin = jnp.tile(sin, (1, LANES // AT_DH))
    first_half = (jnp.arange(LANES) % 32) < 16
    sin_a = jnp.where(first_half, -sin, 0.0)
    sin_b = jnp.where(first_half, 0.0, sin)
    return cos, sin_a, sin_b


def kernel(x, c, ctx, c_ctx, ada_w, ada_b, norm_g, ffn_w1, ffn_w3, ffn_w2, w_in, rg_conv_w,
           rg_conv_b, rg_wa, rg_ba, rg_wi, rg_bi, rg_lam, ml_gate_b, ml_norm_g, at_qn_g,
           at_kn_g, at_sink, w_branch, w_out):
    bsz, seq, d = x.shape
    nctx = ctx.shape[1]
    t = seq + nctx
    depth = ada_w.shape[0]

    xs = jnp.concatenate([x, ctx], axis=1)
    mod_rows = 2 * SUBLANES
    cc = jnp.concatenate([c, c_ctx[None, :], jnp.zeros((mod_rows - bsz - 1, d), F32)], axis=0)
    mods = _ada_call(cc, ada_w, ada_b).reshape(depth, mod_rows, N_MOD, d)
    cos, sin_a, sin_b = _rope_tables(seq, t)
    gmat = _block_diag(jnp.full((AT_HEADS, AT_DH, AT_DH), 1.0 / AT_DH, F32)).astype(BF16)

    for l in range(depth):
        ml = mods[l, :bsz]
        mc = mods[l, bsz]
        last = l == depth - 1
        w1 = [ffn_w1[l, i].astype(BF16) for i in range(2)]
        w3 = [ffn_w3[l, i].astype(BF16) for i in range(2)]
        w2 = [ffn_w2[l, i].astype(BF16) for i in range(2)]

        xs = _ffn_call(xs, ml, mc, norm_g[l, 0][None, :], w1[0], w3[0], w2[0],
                       base=0, seq=seq, out_rows=t)

        qg = jnp.tile(at_qn_g[l], AT_HEADS)[None, :]
        kg = jnp.tile(at_kn_g[l], 2 * AT_KV)[None, :]
        gb = jnp.concatenate([ml_gate_b[l], jnp.zeros((LANES - 4 * ML_HEADS,), F32)])[None, :]
        (rgx, rgg, mq, mk, mv, mo, aq, akd, avd, br, mg) = _proj_call(
            xs, ml, mc, norm_g[l, 1][None, :], _pack_w_in(w_in[l], d), gmat, qg, kg, gb,
            cos, sin_a, sin_b, seq=seq)

        wcat = jnp.stack([jnp.concatenate([_block_diag(rg_wa[l, dr]), _block_diag(rg_wi[l, dr])],
                                          axis=1) for dr in range(2)]).astype(BF16)
        bias = jnp.concatenate([rg_ba[l], rg_bi[l]], axis=1)[:, None, :]
        clam = (-LRU_C * jax.nn.softplus(-rg_lam[l]))[:, None, :]
        rhf, rhb = _rglru_call(rgx, rg_conv_w[l], rg_conv_b[l][None, :], wcat, bias, clam,
                               seq=seq)

        gt = jnp.swapaxes(mg[:, :, :4 * ML_HEADS], 1, 2)
        rowp, colsrc = _mlgate_call(gt, seq=seq)
        colp = jnp.swapaxes(colsrc, 1, 2)
        mkt = jnp.swapaxes(mk, 1, 2)
        mhf, mhb = _mlstm_call(mq, mkt, mv, rowp, colp, seq=seq)

        yc = _attn_call(at_sink[l], aq, akd, avd, seq=seq)

        xs = _merge_call(xs, ml, mc, rhf, rhb, rgg, mhf, mhb, mo, yc, br, ml_norm_g[l][None, :],
                         w_branch[l].astype(BF16), w_out[l].astype(BF16), seq=seq)

        xs = _ffn_call(xs, ml, mc, norm_g[l, 2][None, :], w1[1], w3[1], w2[1],
                       base=6, seq=seq, out_rows=seq if last else t)
    return xs
```

```python
import functools

import jax
import jax.numpy as jnp
from jax import lax
from jax.experimental import pallas as pl
from jax.experimental.pallas import tpu as pltpu

F32 = jnp.float32
BF16 = jnp.bfloat16

EPS = 1e-6
NEG = -1e30
LOG2E = 1.4426950408889634
N_MOD = 9
GRID_W = 64
ROPE_BASE = 10000.0

D_RNN = 512
RNN_BLOCKS = 8
RNN_BLOCK = D_RNN // RNN_BLOCKS
CONV_W = 4
CONV_LEFT = 2
CONV_RIGHT = CONV_W - 1 - CONV_LEFT
LRU_C = 8.0

ML_HEADS = 4
ML_DH = 128
ML_W = ML_HEADS * ML_DH
ML_CHUNK = 128
ML_UNITS = 2 * ML_HEADS

AT_HEADS = 8
AT_KV = 2
AT_DH = 64
AT_G = AT_HEADS // AT_KV
AT_W = AT_HEADS * AT_DH
AT_KVW = AT_KV * AT_DH
ATT_BLOCK = 128

N_BRANCH = 3
BRANCH_W = 512

LANES = 128
SUBLANES = 8
VMEM_LIMIT = 56 * 1024 * 1024

P_RGX = 0
P_RGG = P_RGX + D_RNN
P_MQ = P_RGG + D_RNN
P_MK = P_MQ + ML_W
P_MV = P_MK + ML_W
P_MO = P_MV + ML_W
P_AQ = P_MO + ML_W
P_AK = P_AQ + AT_W
P_AV = P_AK + 2 * AT_KVW
P_BR = P_AV + 2 * AT_KVW


def _cparams(sem):
    return pltpu.CompilerParams(dimension_semantics=sem, vmem_limit_bytes=VMEM_LIMIT)


def _const_spec(shape, lead=()):
    nd = len(shape)
    idx = tuple(lead) + (0,) * nd
    return pl.BlockSpec((None,) * len(lead) + tuple(shape), lambda *_: idx,
                        pipeline_mode=pl.Buffered(1))


def _pick_tile(total, candidates):
    for c in candidates:
        if total % c == 0:
            return c
    raise ValueError(f"no tile for {total}")


def _sigmoid(x):
    return jax.nn.sigmoid(x)


def _ln_mod(x, g, shift, scale):
    ms = jnp.mean(x * x, axis=-1, keepdims=True)
    return x * lax.rsqrt(ms + EPS) * g * (1.0 + scale) + shift


def _row_mod(ml_ref, mc_ref, is_ctx, i):
    return jnp.where(is_ctx, mc_ref[i:i + 1, :], ml_ref[i:i + 1, :])


def _ada_kernel(cc_ref, w_ref, b_ref, o_ref):
    cc = cc_ref[...]
    s = cc * _sigmoid(cc)
    o_ref[...] = jnp.dot(s.astype(BF16), w_ref[...].astype(BF16),
                         preferred_element_type=F32) + b_ref[...]


def _ada_call(cc, ada_w, ada_b):
    depth, d, nout = ada_w.shape
    rows = cc.shape[0]
    tn = _pick_tile(nout, (1536, 1024, 512, 256, 128))
    return pl.pallas_call(
        _ada_kernel,
        grid=(depth, nout // tn),
        in_specs=[
            pl.BlockSpec((rows, d), lambda l, j: (0, 0)),
            pl.BlockSpec((None, d, tn), lambda l, j: (l, 0, j)),
            pl.BlockSpec((None, 1, tn), lambda l, j: (l, 0, j)),
        ],
        out_specs=pl.BlockSpec((None, rows, tn), lambda l, j: (l, 0, j)),
        out_shape=jax.ShapeDtypeStruct((depth, rows, nout), F32),
        compiler_params=_cparams(("arbitrary", "arbitrary")),
        name="ada_mod",
    )(cc, ada_w, ada_b.reshape(depth, 1, nout))


def _ffn_kernel(x_ref, ml_ref, mc_ref, g_ref, w1_ref, w3_ref, w2_ref, o_ref, act_ref,
                *, base, seq, tm, ck):
    t0 = pl.program_id(1) * tm
    rows = t0 + lax.broadcasted_iota(jnp.int32, (tm, 1), 0)
    is_ctx = rows >= seq
    x = x_ref[...]
    h = _ln_mod(x, g_ref[...], _row_mod(ml_ref, mc_ref, is_ctx, base),
                _row_mod(ml_ref, mc_ref, is_ctx, base + 1)).astype(BF16)
    dff = w1_ref.shape[1]
    for j in range(dff // ck):
        a = jnp.dot(h, w1_ref[:, j * ck:(j + 1) * ck], preferred_element_type=F32)
        b = jnp.dot(h, w3_ref[:, j * ck:(j + 1) * ck], preferred_element_type=F32)
        act_ref[:, j * ck:(j + 1) * ck] = (a * _sigmoid(a) * b).astype(BF16)
    y = jnp.dot(act_ref[...], w2_ref[...], preferred_element_type=F32)
    o_ref[...] = x + 0.5 * _row_mod(ml_ref, mc_ref, is_ctx, base + 2) * y


def _ffn_call(x, ml, mc, g, w1, w3, w2, *, lead, base, seq, out_rows):
    bsz, t, d = x.shape
    dff = w1.shape[-1]
    tm = _pick_tile(t, (768, 384, 128))
    ck = _pick_tile(dff, (256, 128))
    kern = functools.partial(_ffn_kernel, base=base, seq=seq, tm=tm, ck=ck)
    return pl.pallas_call(
        kern,
        grid=(bsz, pl.cdiv(out_rows, tm)),
        in_specs=[
            pl.BlockSpec((None, tm, d), lambda b, i: (b, i, 0)),
            pl.BlockSpec((None, N_MOD, d), lambda b, i: (b, 0, 0)),
            _const_spec((N_MOD, d)),
            _const_spec((1, d)),
            _const_spec((d, dff), lead),
            _const_spec((d, dff), lead),
            _const_spec((dff, d), lead),
        ],
        out_specs=pl.BlockSpec((None, tm, d), lambda b, i: (b, i, 0)),
        out_shape=jax.ShapeDtypeStruct((bsz, out_rows, d), F32),
        scratch_shapes=[pltpu.VMEM((tm, dff), BF16)],
        compiler_params=_cparams(("arbitrary", "arbitrary")),
        name="ffn",
    )(x, ml, mc, g, w1, w3, w2)


def _gelu_tanh(x):
    return 0.5 * x * (1.0 + jnp.tanh(0.7978845608028654 * (x + 0.044715 * (x * x * x))))


def _head_norm(x, gmat, g):
    sq = x * x
    hi = sq.astype(BF16)
    lo = (sq - hi.astype(F32)).astype(BF16)
    ms = (jnp.dot(hi, gmat, preferred_element_type=F32)
          + jnp.dot(lo, gmat, preferred_element_type=F32))
    return x * lax.rsqrt(ms + EPS) * g


def _rope(x, cos, sin_a, sin_b):
    parts = []
    for j in range(x.shape[1] // LANES):
        xj = x[:, j * LANES:(j + 1) * LANES]
        parts.append(xj * cos + pltpu.roll(xj, LANES - 16, 1) * sin_a
                     + pltpu.roll(xj, 16, 1) * sin_b)
    return jnp.concatenate(parts, axis=1)


def _proj_kernel(x_ref, ml_ref, mc_ref, g_ref, w_ref, gmat_ref, qg_ref, kg_ref, gb_ref,
                 cos_ref, sa_ref, sb_ref,
                 rgx_o, rgg_o, mq_o, mk_o, mv_o, mo_o, aq_o, ak_o, av_o, br_o, mg_o,
                 *, seq, tm, d_model):
    t0 = pl.program_id(1) * tm
    rows = t0 + lax.broadcasted_iota(jnp.int32, (tm, 1), 0)
    is_ctx = rows >= seq
    h = _ln_mod(x_ref[...], g_ref[...], _row_mod(ml_ref, mc_ref, is_ctx, 3),
                _row_mod(ml_ref, mc_ref, is_ctx, 4)).astype(BF16)

    def seg(c0, w):
        return jnp.dot(h, w_ref[:, c0:c0 + w], preferred_element_type=F32)

    rgx_o[...] = seg(P_RGX, D_RNN).reshape(rgx_o.shape)
    rgg_o[...] = _gelu_tanh(seg(P_RGG, D_RNN)).astype(BF16)
    mq_o[...] = seg(P_MQ, ML_W).astype(BF16)
    mk_o[...] = (seg(P_MK, ML_W) * (ML_DH ** -0.5)).astype(BF16)
    mv_o[...] = seg(P_MV, ML_W).astype(BF16)
    mo_o[...] = _sigmoid(seg(P_MO, ML_W)).astype(BF16)
    cos, sin_a, sin_b = cos_ref[...], sa_ref[...], sb_ref[...]
    gmat = gmat_ref[...]
    q = _head_norm(seg(P_AQ, AT_W), gmat, qg_ref[...])
    aq_o[...] = (_rope(q, cos, sin_a, sin_b) * (AT_DH ** -0.5 * LOG2E)).astype(BF16)
    k = _head_norm(seg(P_AK, 2 * AT_KVW), gmat[:2 * AT_KVW, :2 * AT_KVW], kg_ref[...])
    ak_o[...] = _rope(k, cos, sin_a, sin_b).astype(BF16)
    av_o[...] = seg(P_AV, 2 * AT_KVW).astype(BF16)
    nbr = N_BRANCH * d_model
    for c0 in range(0, nbr, 512):
        br_o[:, c0:c0 + 512] = _sigmoid(seg(P_BR + c0, 512)).astype(BF16)
    mg_o[...] = seg(P_BR + nbr, LANES) + gb_ref[...]


def _proj_call(x, ml, mc, g, w, gmat, qg, kg, gb, cos, sin_a, sin_b, *, layer, seq):
    bsz, t, d = x.shape
    ncol = w.shape[-1]
    tm = _pick_tile(t, (384, 128))
    kern = functools.partial(_proj_kernel, seq=seq, tm=tm, d_model=d)
    widths = [(D_RNN, F32), (D_RNN, BF16), (ML_W, BF16), (ML_W, BF16), (ML_W, BF16),
              (ML_W, BF16), (AT_W, BF16), (2 * AT_KVW, BF16), (2 * AT_KVW, BF16),
              (N_BRANCH * d, BF16), (LANES, F32)]
    tab_spec = pl.BlockSpec((tm, LANES), lambda b, i: (i, 0))
    out_specs = [pl.BlockSpec((None, tm, wd), lambda b, i: (b, i, 0)) for wd, _ in widths]
    out_shape = [jax.ShapeDtypeStruct((bsz, t, wd), dt) for wd, dt in widths]
    out_specs[0] = pl.BlockSpec((tm // SUBLANES, None, SUBLANES, D_RNN), lambda b, i: (i, b, 0, 0))
    out_shape[0] = jax.ShapeDtypeStruct((t // SUBLANES, bsz, SUBLANES, D_RNN), F32)
    return pl.pallas_call(
        kern,
        grid=(bsz, t // tm),
        in_specs=[
            pl.BlockSpec((None, tm, d), lambda b, i: (b, i, 0)),
            pl.BlockSpec((None, N_MOD, d), lambda b, i: (b, 0, 0)),
            _const_spec((N_MOD, d)),
            _const_spec((1, d)),
            _const_spec((d, ncol), (layer,)),
            _const_spec((AT_W, AT_W)),
            _const_spec((1, AT_W)),
            _const_spec((1, 2 * AT_KVW)),
            _const_spec((1, LANES)),
            tab_spec, tab_spec, tab_spec,
        ],
        out_specs=out_specs,
        out_shape=out_shape,
        compiler_params=_cparams(("arbitrary", "arbitrary")),
        name="in_proj",
    )(x, ml, mc, g, w, gmat, qg, kg, gb, cos, sin_a, sin_b)


def _scan_chunk(s, rev, nl, nc):
    if rev:
        return jnp.where(s < nc, nl + nc - 1 - s, nl - 1 - (s - nc))
    return jnp.where(s < nc, nl + s, s - nc)


def _rglru_kernel(xf_ref, xfp_ref, xfn_ref, xb_ref, xbp_ref, xbn_ref, cw_ref, cb_ref, w_ref,
                  bias_ref, clam_ref, hf_ref, hb_ref, ext, a_s, b_s, h_s, hstate,
                  *, nl, nc, tc, sub):
    s = pl.program_id(0)
    bsz, c = hstate.shape[1], hstate.shape[2]
    ntb = tc // SUBLANES

    @pl.when(s == 0)
    def _():
        hstate[...] = jnp.zeros_like(hstate)

    cw = cw_ref[...]
    cb = cb_ref[...]
    for d, (x_ref, xp_ref, xn_ref) in enumerate(((xf_ref, xfp_ref, xfn_ref),
                                                (xb_ref, xbp_ref, xbn_ref))):
        chunk = _scan_chunk(s, d == 1, nl, nc)
        first = jnp.logical_or(chunk == 0, chunk == nl)
        last = jnp.logical_or(chunk == nl - 1, chunk == nl + nc - 1)
        prev = jnp.swapaxes(xp_ref[0], 0, 1)[SUBLANES - CONV_LEFT:]
        nxt = jnp.swapaxes(xn_ref[0], 0, 1)[:CONV_RIGHT]
        ext[d, 0:CONV_LEFT] = jnp.where(first, 0.0, prev)
        for i in range(ntb):
            o = CONV_LEFT + i * SUBLANES
            ext[d, o:o + SUBLANES] = jnp.swapaxes(x_ref[i], 0, 1)
        ext[d, CONV_LEFT + tc:CONV_LEFT + tc + CONV_RIGHT] = jnp.where(last, 0.0, nxt)

        def gates(i, carry, d=d):
            t0 = pl.multiple_of(i * sub, sub)
            u = cb
            for k in range(CONV_W):
                u = u + ext[d, pl.ds(t0 + k, sub)] * cw[k:k + 1, :]
            u2 = u.reshape(sub * bsz, c)
            z = jnp.dot(u2.astype(BF16), w_ref[d], preferred_element_type=F32) + bias_ref[d]
            r = 0.5 * jnp.tanh(0.5 * z[:, :c]) + 0.5
            gi = 0.5 * jnp.tanh(0.5 * z[:, c:]) + 0.5
            a = jnp.exp(clam_ref[d] * r)
            s1 = 1.0 - a * a
            bb = (s1 * lax.rsqrt(jnp.maximum(s1, 1e-36))) * (gi * u2)
            a_s[d, pl.ds(t0, sub)] = a.reshape(sub, bsz, c)
            b_s[d, pl.ds(t0, sub)] = bb.reshape(sub, bsz, c)
            return carry

        lax.fori_loop(0, tc // sub, gates, 0)

    def step(k, hs):
        hf, hb = hs
        kb = tc - 1 - k
        hf = a_s[0, k] * hf + b_s[0, k]
        hb = a_s[1, kb] * hb + b_s[1, kb]
        h_s[0, k] = hf
        h_s[1, kb] = hb
        return hf, hb

    hf, hb = lax.fori_loop(0, tc, step, (hstate[0], hstate[1]), unroll=8)
    hstate[0] = hf
    hstate[1] = hb
    for d, o_ref in enumerate((hf_ref, hb_ref)):
        for i in range(ntb):
            o_ref[i] = jnp.swapaxes(h_s[d, i * SUBLANES:(i + 1) * SUBLANES], 0, 1)


def _rglru_call(rgx, cw, cb, w, bias, clam, *, seq):
    ntb_all, bsz, _, c = rgx.shape
    t = ntb_all * SUBLANES
    tc = 128
    sub = 32
    assert seq % tc == 0 and (t - seq) % tc == 0
    nl, nc = seq // tc, (t - seq) // tc
    ntb = tc // SUBLANES

    def specs(rev):
        chunk = functools.partial(_scan_chunk, rev=rev, nl=nl, nc=nc)
        return [
            pl.BlockSpec((ntb, bsz, SUBLANES, c), lambda s: (chunk(s), 0, 0, 0)),
            pl.BlockSpec((1, bsz, SUBLANES, c),
                         lambda s: (jnp.maximum(chunk(s) * ntb - 1, 0), 0, 0, 0)),
            pl.BlockSpec((1, bsz, SUBLANES, c),
                         lambda s: (jnp.minimum((chunk(s) + 1) * ntb, ntb_all - 1), 0, 0, 0)),
        ]

    kern = functools.partial(_rglru_kernel, nl=nl, nc=nc, tc=tc, sub=sub)
    fspec, bspec = specs(False), specs(True)
    return pl.pallas_call(
        kern,
        grid=(nl + nc,),
        in_specs=fspec + bspec + [
            _const_spec((CONV_W, c)),
            _const_spec((1, c)),
            _const_spec((2, c, 2 * c)),
            _const_spec((2, 1, 2 * c)),
            _const_spec((2, 1, c)),
        ],
        out_specs=[fspec[0], bspec[0]],
        out_shape=[jax.ShapeDtypeStruct(rgx.shape, F32)] * 2,
        scratch_shapes=[
            pltpu.VMEM((2, tc + CONV_W - 1, bsz, c), F32),
            pltpu.VMEM((2, tc, bsz, c), F32),
            pltpu.VMEM((2, tc, bsz, c), F32),
            pltpu.VMEM((2, tc, bsz, c), F32),
            pltpu.VMEM((2, bsz, c), F32),
        ],
        compiler_params=_cparams(("arbitrary",)),
        name="rglru",
    )(rgx, rgx, rgx, rgx, rgx, rgx, cw, cb, w, bias, clam)


def _log_sigmoid(x):
    return -(jnp.maximum(-x, 0.0) + jnp.log1p(jnp.exp(-jnp.abs(x))))


def _lane_scan(x, op, ident, rev):
    n = x.shape[1]
    lane = lax.broadcasted_iota(jnp.int32, x.shape, 1)
    sh = 1
    while sh < n:
        if rev:
            x = op(x, jnp.where(lane < n - sh, pltpu.roll(x, n - sh, 1), ident))
        else:
            x = op(x, jnp.where(lane >= sh, pltpu.roll(x, sh, 1), ident))
        sh *= 2
    return x


def _mlgate_kernel(g_ref, row_o, col_o, cb_s, gg_s, ml_s, tot_s, wm_s, mo_s, mn_s, *, nl, nc):
    L = ML_CHUNK
    H = ML_HEADS
    U = ML_UNITS
    is_bwd = lax.broadcasted_iota(jnp.int32, (U, L), 0) >= H
    for c in range(nl + nc):
        lanes = slice(c * L, (c + 1) * L)
        g16 = g_ref[:, lanes]
        li = jnp.concatenate([g16[0:H], g16[2 * H:3 * H]], axis=0)
        lf = _log_sigmoid(jnp.concatenate([g16[H:2 * H], g16[3 * H:4 * H]], axis=0))
        pre = _lane_scan(lf, jnp.add, 0.0, False)
        suf = _lane_scan(lf, jnp.add, 0.0, True)
        tot = pre + suf - lf
        cb = jnp.where(is_bwd, suf, pre)
        gg = li - cb
        pmax = _lane_scan(gg, jnp.maximum, NEG, False)
        smax = _lane_scan(gg, jnp.maximum, NEG, True)
        cb_s[:, lanes] = cb
        gg_s[:, lanes] = gg
        ml_s[:, lanes] = cb + jnp.where(is_bwd, smax, pmax)
        tot_s[:, lanes] = tot
        wm_s[:, lanes] = tot + jnp.maximum(pmax, smax)

    m = jnp.zeros((U, L), F32)
    for s in range(nl + nc):
        cf = nl + s if s < nc else s - nc
        cr = nl + nc - 1 - s if s < nc else nl - 1 - (s - nc)
        lf_, lr_ = slice(cf * L, (cf + 1) * L), slice(cr * L, (cr + 1) * L)
        tot = jnp.where(is_bwd, tot_s[:, lr_], tot_s[:, lf_])
        wm = jnp.where(is_bwd, wm_s[:, lr_], wm_s[:, lf_])
        m_new = jnp.maximum(tot + m, wm)
        mo_s[0:H, lf_] = m[0:H]
        mo_s[H:U, lr_] = m[H:U]
        mn_s[0:H, lf_] = m_new[0:H]
        mn_s[H:U, lr_] = m_new[H:U]
        m = m_new

    cb = cb_s[...]
    gg = gg_s[...]
    tot = tot_s[...]
    mo = mo_s[...]
    mn = mn_s[...]
    m_t = jnp.maximum(cb + mo, ml_s[...])
    row_o[0:U, :] = gg
    row_o[U:2 * U, :] = jnp.exp(tot + gg - mn)
    row_o[2 * U:3 * U, :] = jnp.exp(tot + mo - mn)
    col_o[0:U, :] = cb - m_t
    col_o[U:2 * U, :] = jnp.exp(cb + mo - m_t)
    col_o[2 * U:3 * U, :] = jnp.exp(-m_t)


def _mlgate_call(gt, *, seq):
    bsz, ng, t = gt.shape
    nl, nc = seq // ML_CHUNK, (t - seq) // ML_CHUNK
    kern = functools.partial(_mlgate_kernel, nl=nl, nc=nc)
    out = pl.BlockSpec((None, 3 * ML_UNITS, t), lambda b: (b, 0, 0))
    return pl.pallas_call(
        kern,
        grid=(bsz,),
        in_specs=[pl.BlockSpec((None, ng, t), lambda b: (b, 0, 0))],
        out_specs=[out, out],
        out_shape=[jax.ShapeDtypeStruct((bsz, 3 * ML_UNITS, t), F32)] * 2,
        scratch_shapes=[pltpu.VMEM((ML_UNITS, t), F32)] * 7,
        compiler_params=_cparams(("arbitrary",)),
        name="mlstm_gates",
    )(gt)


def _mlstm_kernel(qf_ref, ktf_ref, vf_ref, rf_ref, cf_ref, qb_ref, ktb_ref, vb_ref, rb_ref, cb_ref,
                  hf_ref, hb_ref, c_st):
    s = pl.program_id(1)

    @pl.when(s == 0)
    def _():
        c_st[...] = jnp.zeros_like(c_st)

    L, dh, U = ML_CHUNK, ML_DH, ML_UNITS
    ti = lax.broadcasted_iota(jnp.int32, (L, L), 0)
    si = lax.broadcasted_iota(jnp.int32, (L, L), 1)
    ones = jnp.ones((L, dh), BF16)
    for bb in range(qf_ref.shape[0]):
        for d, (q_ref, kt_ref, v_ref, r_ref, c_ref, o_ref) in enumerate(
                ((qf_ref, ktf_ref, vf_ref, rf_ref, cf_ref, hf_ref),
                 (qb_ref, ktb_ref, vb_ref, rb_ref, cb_ref, hb_ref))):
            tri = (si >= ti) if d == 1 else (si <= ti)
            rows = r_ref[bb]
            cols = c_ref[bb]
            for hd in range(ML_HEADS):
                u = d * ML_HEADS + hd
                sl = slice(hd * dh, (hd + 1) * dh)
                q = q_ref[bb, :, sl]
                kt = kt_ref[bb, sl, :]
                v_ext = jnp.concatenate([v_ref[bb, :, sl], ones], axis=1)
                g_row = rows[u:u + 1, :]
                w_row = rows[U + u:U + u + 1, :]
                a_row = rows[2 * U + u:2 * U + u + 1, :]
                x_col = cols[:, u:u + 1]
                dec = cols[:, U + u:U + u + 1]
                emt = cols[:, 2 * U + u:2 * U + u + 1]
                c_old = c_st[bb * U + u]
                qkc = jnp.dot(q, jnp.concatenate([kt, c_old.astype(BF16)], axis=1),
                              preferred_element_type=F32)
                dm = jnp.exp(jnp.where(tri, x_col + g_row, NEG))
                sc = (qkc[:, :L] * dm).astype(BF16)
                ktw = (kt.astype(F32) * w_row).astype(BF16)
                sv = jnp.dot(jnp.concatenate([sc, ktw], axis=0), v_ext,
                             preferred_element_type=F32)
                ab = sv[:L] + dec * qkc[:, L:]
                o_ref[bb, :, sl] = ab[:, :dh] / jnp.maximum(jnp.abs(ab[:, dh:]), emt)
                c_st[bb * U + u] = jnp.concatenate([a_row, a_row], axis=1) * c_old + sv[L:]


def _mlstm_call(mq, mkt, mv, rowp, colp, *, seq):
    bsz, t, w = mq.shape
    nl, nc = seq // ML_CHUNK, (t - seq) // ML_CHUNK
    cf = functools.partial(_scan_chunk, rev=False, nl=nl, nc=nc)
    cr = functools.partial(_scan_chunk, rev=True, nl=nl, nc=nc)
    nr = rowp.shape[1]
    bpb = 2 if bsz % 2 == 0 else 1

    def tok(ch, wd):
        return pl.BlockSpec((bpb, ML_CHUNK, wd), lambda b, s: (b, ch(s), 0))

    def tr(ch, rows):
        return pl.BlockSpec((bpb, rows, ML_CHUNK), lambda b, s: (b, 0, ch(s)))

    def side(ch):
        return [tok(ch, w), tr(ch, w), tok(ch, w), tr(ch, nr), tok(ch, nr)]

    return pl.pallas_call(
        _mlstm_kernel,
        grid=(bsz // bpb, nl + nc),
        in_specs=side(cf) + side(cr),
        out_specs=[tok(cf, w), tok(cr, w)],
        out_shape=[jax.ShapeDtypeStruct((bsz, t, w), F32)] * 2,
        scratch_shapes=[pltpu.VMEM((bpb * ML_UNITS, ML_DH, 2 * ML_DH), F32)],
        compiler_params=_cparams(("arbitrary", "arbitrary")),
        name="mlstm",
    )(mq, mkt, mv, rowp, colp, mq, mkt, mv, rowp, colp)


def _attn_kernel(sink_ref, q_ref, k_ref, v_ref, o_ref, vext, *, seq, qbs):
    kv = pl.program_id(1)
    i = pl.program_id(2)
    blk = ATT_BLOCK
    t = k_ref.shape[0]
    ctx = t - seq
    nlb = seq // blk
    band = 3 * blk

    @pl.when(i == 0)
    def _():
        vext[:, 0:LANES] = v_ref[...]
        vext[:, LANES:2 * LANES] = jnp.ones((t, LANES), BF16)

    lane = lax.broadcasted_iota(jnp.int32, (blk, LANES), 1)
    low = lane < AT_DH
    zero = jnp.zeros((blk, LANES), BF16)
    rows = AT_G * blk
    gi = lax.broadcasted_iota(jnp.int32, (rows, 1), 0) // blk
    snk = jnp.zeros((rows, 1), F32)
    for g in range(AT_G):
        snk = jnp.where(gi == g, sink_ref[kv * AT_G + g] * LOG2E, snk)
    ti = lax.broadcasted_iota(jnp.int32, (blk, band), 0)
    ci = lax.broadcasted_iota(jnp.int32, (blk, band), 1)
    nt = (((1,), (1,)), ((), ()))
    k_ctx = k_ref[seq:t, :]
    v_ctx = vext[seq:t, :]

    for j in range(qbs):
        qb = i * qbs + j
        s0 = pl.multiple_of(jnp.clip((qb - 1) * blk, 0, t - band), blk)
        rel = ci - ti + (s0 - qb * blk + blk)
        lim = jnp.where(qb < nlb, seq - s0, 0)
        ok = jnp.logical_and(jnp.logical_and(rel >= 0, rel <= 2 * blk), ci < lim)
        bias = jnp.where(ok, 0.0, NEG)
        bias = jnp.concatenate([bias] * AT_G, axis=0)

        qs = []
        for h2 in range(AT_G // 2):
            qj = q_ref[j * blk:(j + 1) * blk, h2 * LANES:(h2 + 1) * LANES]
            qs += [jnp.where(low, qj, zero), jnp.where(low, zero, qj)]
        q_all = jnp.concatenate(qs, axis=0)
        s_band = lax.dot_general(q_all, k_ref[pl.ds(s0, band), :], nt,
                                 preferred_element_type=F32) + bias
        s_ctx = lax.dot_general(q_all, k_ctx, nt, preferred_element_type=F32)
        m = jnp.maximum(jnp.maximum(jnp.max(s_band, axis=1, keepdims=True),
                                    jnp.max(s_ctx, axis=1, keepdims=True)), snk)
        p_band = jnp.exp2(s_band - m).astype(BF16)
        p_ctx = jnp.exp2(s_ctx - m).astype(BF16)
        o2 = (jnp.dot(p_band, vext[pl.ds(s0, band), :], preferred_element_type=F32)
              + jnp.dot(p_ctx, v_ctx, preferred_element_type=F32))
        o2 = o2[:, :LANES] / (o2[:, LANES:] + jnp.exp2(snk - m))
        for h2 in range(AT_G // 2):
            a = o2[(2 * h2) * blk:(2 * h2 + 1) * blk, :]
            b = o2[(2 * h2 + 1) * blk:(2 * h2 + 2) * blk, :]
            o_ref[j * blk:(j + 1) * blk, h2 * LANES:(h2 + 1) * LANES] = (
                jnp.where(low, a, b).astype(o_ref.dtype))


def _attn_call(sink, aq, akd, avd, *, seq):
    bsz, t, _ = aq.shape
    blk = ATT_BLOCK
    qbs = 2
    assert t % (qbs * blk) == 0 and t >= 3 * blk
    gw = AT_G * AT_DH
    kern = functools.partial(_attn_kernel, seq=seq, qbs=qbs)
    whole = pl.BlockSpec((None, t, LANES), lambda b, kv, i: (b, 0, kv))
    return pl.pallas_call(
        kern,
        grid=(bsz, AT_KV, t // (qbs * blk)),
        in_specs=[
            pl.BlockSpec(memory_space=pltpu.SMEM),
            pl.BlockSpec((None, qbs * blk, gw), lambda b, kv, i: (b, i, kv)),
            whole, whole,
        ],
        out_specs=pl.BlockSpec((None, qbs * blk, gw), lambda b, kv, i: (b, i, kv)),
        out_shape=jax.ShapeDtypeStruct((bsz, t, AT_W), BF16),
        scratch_shapes=[pltpu.VMEM((t, 2 * LANES), BF16)],
        compiler_params=_cparams(("arbitrary", "arbitrary", "arbitrary")),
        name="attn",
    )(sink, aq, akd, avd)


def _merge_kernel(x_ref, ml_ref, mc_ref, rf_ref, rb_ref, gg_ref, hf_ref, hb_ref, mo_ref, yc_ref,
                  br_ref, ng_ref, wb_ref, wo_ref, o_ref, *, seq, tm):
    t0 = pl.program_id(1) * tm
    rows = t0 + lax.broadcasted_iota(jnp.int32, (tm, 1), 0)
    is_ctx = rows >= seq
    d = x_ref.shape[1]
    rsum = (rf_ref[...] + rb_ref[...]).reshape(tm, D_RNN)
    ya = (rsum * gg_ref[...].astype(F32)).astype(BF16)
    hsum = hf_ref[...] + hb_ref[...]
    parts = []
    for hd in range(ML_HEADS):
        hh = hsum[:, hd * ML_DH:(hd + 1) * ML_DH]
        parts.append(hh * lax.rsqrt(jnp.mean(hh * hh, axis=-1, keepdims=True) + EPS))
    yb = (jnp.concatenate(parts, axis=1) * ng_ref[...] * mo_ref[...].astype(F32)).astype(BF16)
    m = (br_ref[:, 0:d].astype(F32) * jnp.dot(ya, wb_ref[0], preferred_element_type=F32)
         + br_ref[:, d:2 * d].astype(F32) * jnp.dot(yb, wb_ref[1], preferred_element_type=F32)
         + br_ref[:, 2 * d:3 * d].astype(F32) * jnp.dot(yc_ref[...], wb_ref[2],
                                                       preferred_element_type=F32))
    y = jnp.dot(m.astype(BF16), wo_ref[...], preferred_element_type=F32)
    o_ref[...] = x_ref[...] + _row_mod(ml_ref, mc_ref, is_ctx, 5) * y


def _merge_call(x, ml, mc, rf, rb, gg, hf, hb, mo, yc, br, ng, wb, wo, *, layer, seq):
    bsz, t, d = x.shape
    tm = _pick_tile(t, (384, 128))
    kern = functools.partial(_merge_kernel, seq=seq, tm=tm)

    def tok(wd):
        return pl.BlockSpec((None, tm, wd), lambda b, i: (b, i, 0))

    rspec = pl.BlockSpec((tm // SUBLANES, None, SUBLANES, D_RNN), lambda b, i: (i, b, 0, 0))

    return pl.pallas_call(
        kern,
        grid=(bsz, t // tm),
        in_specs=[
            tok(d),
            pl.BlockSpec((None, N_MOD, d), lambda b, i: (b, 0, 0)),
            _const_spec((N_MOD, d)),
            rspec, rspec, tok(D_RNN), tok(ML_W), tok(ML_W), tok(ML_W), tok(AT_W),
            tok(N_BRANCH * d),
            _const_spec((1, ML_W)),
            _const_spec((N_BRANCH, BRANCH_W, d), (layer,)),
            _const_spec((d, d), (layer,)),
        ],
        out_specs=tok(d),
        out_shape=jax.ShapeDtypeStruct((bsz, t, d), F32),
        compiler_params=_cparams(("arbitrary", "arbitrary")),
        name="merge",
    )(x, ml, mc, rf, rb, gg, hf, hb, mo, yc, br, ng, wb, wo)


def _pack_w_in(w_in, d_model):
    o_rg_x = 0
    o_ml_g = 2 * D_RNN + 4 * ML_W
    o_at_q = o_ml_g + 4 * ML_HEADS
    o_at_k = o_at_q + AT_W
    o_at_v = o_at_k + AT_KVW
    o_br = o_at_v + AT_KVW

    w_in = w_in.astype(BF16)

    def dup(c0):
        hs = [w_in[..., c0 + i * AT_DH:c0 + (i + 1) * AT_DH] for i in range(AT_KV)]
        return [hs[i // 2] for i in range(2 * AT_KV)]

    gate = w_in[..., o_ml_g:o_at_q]
    pad = jnp.zeros(w_in.shape[:-1] + (LANES - gate.shape[-1],), w_in.dtype)
    cols = ([w_in[..., o_rg_x:o_ml_g], w_in[..., o_at_q:o_at_k]] + dup(o_at_k) + dup(o_at_v)
            + [w_in[..., o_br:o_br + N_BRANCH * d_model], gate, pad])
    return jnp.concatenate(cols, axis=-1)


def _block_diag(w):
    n, bi, bj = w.shape
    eye = jnp.eye(n, dtype=w.dtype)
    return (eye[:, None, :, None] * w[:, :, None, :]).reshape(n * bi, n * bj)


def _rope_tables(seq, t):
    rows = seq // GRID_W
    row = jnp.repeat(jnp.arange(rows), GRID_W).astype(F32)
    col = jnp.broadcast_to(jnp.arange(GRID_W), (rows, GRID_W)).reshape(-1).astype(F32)
    half = AT_DH // 2
    inv = ROPE_BASE ** (-jnp.arange(0, half, 2, dtype=F32) / half)
    ar = row[:, None] * inv
    ac = col[:, None] * inv
    ang = jnp.concatenate([ar, ar, ac, ac], axis=-1)
    cos = jnp.concatenate([jnp.cos(ang), jnp.ones((t - seq, AT_DH), F32)], axis=0)
    sin = jnp.concatenate([jnp.sin(ang), jnp.zeros((t - seq, AT_DH), F32)], axis=0)
    cos = jnp.tile(cos, (1, LANES // AT_DH))
    sin = jnp.tile(sin, (1, LANES // AT_DH))
    first_half = (jnp.arange(LANES) % 32) < 16
    sin_a = jnp.where(first_half, -sin, 0.0)
    sin_b = jnp.where(first_half, 0.0, sin)
    return cos, sin_a, sin_b


def kernel(x, c, ctx, c_ctx, ada_w, ada_b, norm_g, ffn_w1, ffn_w3, ffn_w2, w_in, rg_conv_w,
           rg_conv_b, rg_wa, rg_ba, rg_wi, rg_bi, rg_lam, ml_gate_b, ml_norm_g, at_qn_g,
           at_kn_g, at_sink, w_branch, w_out):
    bsz, seq, d = x.shape
    nctx = ctx.shape[1]
    t = seq + nctx
    depth = ada_w.shape[0]

    xs = jnp.concatenate([x, ctx], axis=1)
    mod_rows = 2 * SUBLANES
    cc = jnp.concatenate([c, c_ctx[None, :], jnp.zeros((mod_rows - bsz - 1, d), F32)], axis=0)
    mods = _ada_call(cc, ada_w, ada_b).reshape(depth, mod_rows, N_MOD, d)
    cos, sin_a, sin_b = _rope_tables(seq, t)
    gmat = _block_diag(jnp.full((AT_HEADS, AT_DH, AT_DH), 1.0 / AT_DH, F32)).astype(BF16)
    w1, w3, w2 = ffn_w1.astype(BF16), ffn_w3.astype(BF16), ffn_w2.astype(BF16)
    w_in_p = _pack_w_in(w_in, d)
    w_br, w_o = w_branch.astype(BF16), w_out.astype(BF16)

    for l in range(depth):
        ml = mods[l, :bsz]
        mc = mods[l, bsz]
        last = l == depth - 1

        xs = _ffn_call(xs, ml, mc, norm_g[l, 0][None, :], w1, w3, w2,
                       lead=(l, 0), base=0, seq=seq, out_rows=t)

        qg = jnp.tile(at_qn_g[l], AT_HEADS)[None, :]
        kg = jnp.tile(at_kn_g[l], 2 * AT_KV)[None, :]
        gb = jnp.concatenate([ml_gate_b[l], jnp.zeros((LANES - 4 * ML_HEADS,), F32)])[None, :]
        (rgx, rgg, mq, mk, mv, mo, aq, akd, avd, br, mg) = _proj_call(
            xs, ml, mc, norm_g[l, 1][None, :], w_in_p, gmat, qg, kg, gb,
            cos, sin_a, sin_b, layer=l, seq=seq)

        wcat = jnp.stack([jnp.concatenate([_block_diag(rg_wa[l, dr]), _block_diag(rg_wi[l, dr])],
                                          axis=1) for dr in range(2)]).astype(BF16)
        bias = jnp.concatenate([rg_ba[l], rg_bi[l]], axis=1)[:, None, :]
        clam = (-LRU_C * jax.nn.softplus(-rg_lam[l]))[:, None, :]
        rhf, rhb = _rglru_call(rgx, rg_conv_w[l], rg_conv_b[l][None, :], wcat, bias, clam,
                               seq=seq)

        gt = jnp.swapaxes(mg[:, :, :4 * ML_HEADS], 1, 2)
        rowp, colsrc = _mlgate_call(gt, seq=seq)
        colp = jnp.swapaxes(colsrc, 1, 2)
        mkt = jnp.swapaxes(mk, 1, 2)
        mhf, mhb = _mlstm_call(mq, mkt, mv, rowp, colp, seq=seq)

        yc = _attn_call(at_sink[l], aq, akd, avd, seq=seq)

        xs = _merge_call(xs, ml, mc, rhf, rhb, rgg, mhf, mhb, mo, yc, br, ml_norm_g[l][None, :],
                         w_br, w_o, layer=l, seq=seq)

        xs = _ffn_call(xs, ml, mc, norm_g[l, 2][None, :], w1, w3, w2,
                       lead=(l, 1), base=6, seq=seq, out_rows=seq if last else t)
    return xs
```

```python
import functools

import jax
import jax.numpy as jnp
from jax import lax
from jax.experimental import pallas as pl
from jax.experimental.pallas import tpu as pltpu

F32 = jnp.float32
BF16 = jnp.bfloat16

EPS = 1e-6
NEG = -1e30
LOG2E = 1.4426950408889634
N_MOD = 9
GRID_W = 64
ROPE_BASE = 10000.0

D_RNN = 512
RNN_BLOCKS = 8
RNN_BLOCK = D_RNN // RNN_BLOCKS
CONV_W = 4
CONV_LEFT = 2
CONV_RIGHT = CONV_W - 1 - CONV_LEFT
LRU_C = 8.0

ML_HEADS = 4
ML_DH = 128
ML_W = ML_HEADS * ML_DH
ML_CHUNK = 128
ML_UNITS = 2 * ML_HEADS

AT_HEADS = 8
AT_KV = 2
AT_DH = 64
AT_G = AT_HEADS // AT_KV
AT_W = AT_HEADS * AT_DH
AT_KVW = AT_KV * AT_DH
ATT_BLOCK = 128

N_BRANCH = 3
BRANCH_W = 512

LANES = 128
SUBLANES = 8
VMEM_LIMIT = 56 * 1024 * 1024

P_RGX = 0
P_RGG = P_RGX + D_RNN
P_MQ = P_RGG + D_RNN
P_MK = P_MQ + ML_W
P_MV = P_MK + ML_W
P_MO = P_MV + ML_W
P_HEAD = P_MO + ML_W
N_GATE = 4 * ML_HEADS
T_AQ = 0
T_AK = T_AQ + AT_W
T_AV = T_AK + AT_KVW
T_BR = T_AV + AT_KVW


def _cparams(sem):
    return pltpu.CompilerParams(dimension_semantics=sem, vmem_limit_bytes=VMEM_LIMIT)


def _const_spec(shape, lead=()):
    nd = len(shape)
    idx = tuple(lead) + (0,) * nd
    return pl.BlockSpec((None,) * len(lead) + tuple(shape), lambda *_: idx,
                        pipeline_mode=pl.Buffered(1))


def _pick_tile(total, candidates):
    for c in candidates:
        if total % c == 0:
            return c
    raise ValueError(f"no tile for {total}")


def _sigmoid(x):
    return jax.nn.sigmoid(x)


def _ln_mod(x, g, shift, scale):
    ms = jnp.mean(x * x, axis=-1, keepdims=True)
    return x * lax.rsqrt(ms + EPS) * g * (1.0 + scale) + shift


def _row_mod(ml_ref, mc_ref, is_ctx, i):
    return jnp.where(is_ctx, mc_ref[i:i + 1, :], ml_ref[i:i + 1, :])


def _ada_kernel(cc_ref, w_ref, b_ref, o_ref):
    cc = cc_ref[...]
    s = cc * _sigmoid(cc)
    o_ref[...] = jnp.dot(s.astype(BF16), w_ref[...].astype(BF16),
                         preferred_element_type=F32) + b_ref[...]


def _ada_call(cc, ada_w, ada_b):
    depth, d, nout = ada_w.shape
    rows = cc.shape[0]
    tn = _pick_tile(nout, (1536, 1024, 512, 256, 128))
    return pl.pallas_call(
        _ada_kernel,
        grid=(depth, nout // tn),
        in_specs=[
            pl.BlockSpec((rows, d), lambda l, j: (0, 0)),
            pl.BlockSpec((None, d, tn), lambda l, j: (l, 0, j)),
            pl.BlockSpec((None, 1, tn), lambda l, j: (l, 0, j)),
        ],
        out_specs=pl.BlockSpec((None, rows, tn), lambda l, j: (l, 0, j)),
        out_shape=jax.ShapeDtypeStruct((depth, rows, nout), F32),
        compiler_params=_cparams(("arbitrary", "arbitrary")),
        name="ada_mod",
    )(cc, ada_w, ada_b.reshape(depth, 1, nout))


def _ffn_body(x, ml_ref, mc_ref, is_ctx, g_ref, w1_ref, w3_ref, w2_ref, act_ref, *, base, ck):
    h = _ln_mod(x, g_ref[...], _row_mod(ml_ref, mc_ref, is_ctx, base),
                _row_mod(ml_ref, mc_ref, is_ctx, base + 1)).astype(BF16)
    dff = w1_ref.shape[1]
    for j in range(dff // ck):
        a = jnp.dot(h, w1_ref[:, j * ck:(j + 1) * ck], preferred_element_type=F32)
        b = jnp.dot(h, w3_ref[:, j * ck:(j + 1) * ck], preferred_element_type=F32)
        act_ref[:, j * ck:(j + 1) * ck] = (a * _sigmoid(a) * b).astype(BF16)
    y = jnp.dot(act_ref[...], w2_ref[...], preferred_element_type=F32)
    return x + 0.5 * _row_mod(ml_ref, mc_ref, is_ctx, base + 2) * y


def _ffn_kernel(x_ref, ml_ref, mc_ref, g_ref, w1_ref, w3_ref, w2_ref, o_ref, act_ref,
                *, base, seq, tm, ck):
    t0 = pl.program_id(1) * tm
    rows = t0 + lax.broadcasted_iota(jnp.int32, (tm, 1), 0)
    is_ctx = rows >= seq
    o_ref[...] = _ffn_body(x_ref[...], ml_ref, mc_ref, is_ctx, g_ref, w1_ref, w3_ref, w2_ref,
                           act_ref, base=base, ck=ck)


def _ffn_call(x, ml, mc, g, w1, w3, w2, *, lead, base, seq, out_rows):
    bsz, t, d = x.shape
    dff = w1.shape[-1]
    tm = _pick_tile(t, (768, 384, 128))
    ck = _pick_tile(dff, (256, 128))
    kern = functools.partial(_ffn_kernel, base=base, seq=seq, tm=tm, ck=ck)
    return pl.pallas_call(
        kern,
        grid=(bsz, pl.cdiv(out_rows, tm)),
        in_specs=[
            pl.BlockSpec((None, tm, d), lambda b, i: (b, i, 0)),
            pl.BlockSpec((None, N_MOD, d), lambda b, i: (b, 0, 0)),
            _const_spec((N_MOD, d)),
            _const_spec((1, d)),
            _const_spec((d, dff), lead),
            _const_spec((d, dff), lead),
            _const_spec((dff, d), lead),
        ],
        out_specs=pl.BlockSpec((None, tm, d), lambda b, i: (b, i, 0)),
        out_shape=jax.ShapeDtypeStruct((bsz, out_rows, d), F32),
        scratch_shapes=[pltpu.VMEM((tm, dff), BF16)],
        compiler_params=_cparams(("arbitrary", "arbitrary")),
        name="ffn",
    )(x, ml, mc, g, w1, w3, w2)


def _gelu_tanh(x):
    return 0.5 * x * (1.0 + jnp.tanh(0.7978845608028654 * (x + 0.044715 * (x * x * x))))


def _head_norm(x, gmat, g):
    ms = jnp.dot((x * x).astype(BF16), gmat, preferred_element_type=F32)
    return x * lax.rsqrt(ms + EPS) * g


def _dup_heads(x):
    low = lax.broadcasted_iota(jnp.int32, x.shape, 1) < AT_DH
    sw = pltpu.roll(x, AT_DH, 1)
    return jnp.concatenate([jnp.where(low, x, sw), jnp.where(low, sw, x)], axis=1)


def _rope(x, cos, sin_a, sin_b):
    parts = []
    for j in range(x.shape[1] // LANES):
        xj = x[:, j * LANES:(j + 1) * LANES]
        parts.append(xj * cos + pltpu.roll(xj, LANES - 16, 1) * sin_a
                     + pltpu.roll(xj, 16, 1) * sin_b)
    return jnp.concatenate(parts, axis=1)


def _proj_kernel(x_ref, ml_ref, mc_ref, g_ref, wh_ref, wt_ref, gmat_ref, qg_ref, kg_ref, gb_ref,
                 cos_ref, sa_ref, sb_ref,
                 rgx_o, rgg_o, mq_o, mk_o, mv_o, mo_o, aq_o, ak_o, av_o, br_o, mg_o,
                 tail, *, seq, tm, d_model):
    @pl.when(jnp.logical_and(pl.program_id(0) == 0, pl.program_id(1) == 0))
    def _():
        for r in range(0, d_model, LANES):
            tail[r:r + LANES, :] = wt_ref[r:r + LANES, N_GATE:]

    t0 = pl.program_id(1) * tm
    rows = t0 + lax.broadcasted_iota(jnp.int32, (tm, 1), 0)
    is_ctx = rows >= seq
    h = _ln_mod(x_ref[...], g_ref[...], _row_mod(ml_ref, mc_ref, is_ctx, 3),
                _row_mod(ml_ref, mc_ref, is_ctx, 4)).astype(BF16)

    def seg(c0, w, ref=wh_ref):
        return jnp.dot(h, ref[:, c0:c0 + w], preferred_element_type=F32)

    rgx_o[...] = seg(P_RGX, D_RNN).reshape(rgx_o.shape)
    rgg_o[...] = _gelu_tanh(seg(P_RGG, D_RNN)).astype(BF16)
    mq_o[...] = seg(P_MQ, ML_W).astype(BF16)
    mk_o[...] = (seg(P_MK, ML_W) * (ML_DH ** -0.5)).astype(BF16)
    mv_o[...] = seg(P_MV, ML_W).astype(BF16)
    mo_o[...] = _sigmoid(seg(P_MO, ML_W)).astype(BF16)
    cos, sin_a, sin_b = cos_ref[...], sa_ref[...], sb_ref[...]
    gmat = gmat_ref[...]
    q = _head_norm(seg(T_AQ, AT_W, tail), gmat, qg_ref[...])
    aq_o[...] = (_rope(q, cos, sin_a, sin_b) * (AT_DH ** -0.5 * LOG2E)).astype(BF16)
    k = _head_norm(seg(T_AK, AT_KVW, tail), gmat[:AT_KVW, :AT_KVW], kg_ref[...])
    ak_o[...] = _dup_heads(_rope(k, cos, sin_a, sin_b)).astype(BF16)
    av_o[...] = _dup_heads(seg(T_AV, AT_KVW, tail)).astype(BF16)
    for c0 in range(0, N_BRANCH * d_model, 512):
        br_o[:, c0:c0 + 512] = _sigmoid(seg(T_BR + c0, 512, tail)).astype(BF16)
    mg_o[...] = seg(0, LANES, wt_ref) + gb_ref[...]


def _proj_call(x, ml, mc, g, wh, wt, gmat, qg, kg, gb, cos, sin_a, sin_b, *, layer, seq):
    bsz, t, d = x.shape
    ntail = wt.shape[-1] - N_GATE
    tm = _pick_tile(t, (384, 128))
    kern = functools.partial(_proj_kernel, seq=seq, tm=tm, d_model=d)
    widths = [(D_RNN, F32), (D_RNN, BF16), (ML_W, BF16), (ML_W, BF16), (ML_W, BF16),
              (ML_W, BF16), (AT_W, BF16), (2 * AT_KVW, BF16), (2 * AT_KVW, BF16),
              (N_BRANCH * d, BF16), (LANES, F32)]
    tab_spec = pl.BlockSpec((tm, LANES), lambda b, i: (i, 0))
    out_specs = [pl.BlockSpec((None, tm, wd), lambda b, i: (b, i, 0)) for wd, _ in widths]
    out_shape = [jax.ShapeDtypeStruct((bsz, t, wd), dt) for wd, dt in widths]
    out_specs[0] = pl.BlockSpec((tm // SUBLANES, None, SUBLANES, D_RNN), lambda b, i: (i, b, 0, 0))
    out_shape[0] = jax.ShapeDtypeStruct((t // SUBLANES, bsz, SUBLANES, D_RNN), F32)
    return pl.pallas_call(
        kern,
        grid=(bsz, t // tm),
        in_specs=[
            pl.BlockSpec((None, tm, d), lambda b, i: (b, i, 0)),
            pl.BlockSpec((None, N_MOD, d), lambda b, i: (b, 0, 0)),
            _const_spec((N_MOD, d)),
            _const_spec((1, d)),
            _const_spec((d, wh.shape[-1]), (layer,)),
            _const_spec((d, wt.shape[-1]), (layer,)),
            _const_spec((AT_W, AT_W)),
            _const_spec((1, AT_W)),
            _const_spec((1, AT_KVW)),
            _const_spec((1, LANES)),
            tab_spec, tab_spec, tab_spec,
        ],
        out_specs=out_specs,
        out_shape=out_shape,
        scratch_shapes=[pltpu.VMEM((d, ntail), BF16)],
        compiler_params=_cparams(("arbitrary", "arbitrary")),
        name="in_proj",
    )(x, ml, mc, g, wh, wt, gmat, qg, kg, gb, cos, sin_a, sin_b)


def _scan_chunk(s, rev, nl, nc):
    if rev:
        return jnp.where(s < nc, nl + nc - 1 - s, nl - 1 - (s - nc))
    return jnp.where(s < nc, nl + s, s - nc)


def _rglru_kernel(xf_ref, xfp_ref, xfn_ref, xb_ref, xbp_ref, xbn_ref, cw_ref, cb_ref, w_ref,
                  bias_ref, clam_ref, hf_ref, hb_ref, ext, a_s, b_s, h_s, hstate,
                  *, nl, nc, tc, sub):
    s = pl.program_id(0)
    bsz, c = hstate.shape[1], hstate.shape[2]
    ntb = tc // SUBLANES

    @pl.when(s == 0)
    def _():
        hstate[...] = jnp.zeros_like(hstate)

    cw = cw_ref[...]
    cb = cb_ref[...]
    for d, (x_ref, xp_ref, xn_ref) in enumerate(((xf_ref, xfp_ref, xfn_ref),
                                                (xb_ref, xbp_ref, xbn_ref))):
        chunk = _scan_chunk(s, d == 1, nl, nc)
        first = jnp.logical_or(chunk == 0, chunk == nl)
        last = jnp.logical_or(chunk == nl - 1, chunk == nl + nc - 1)
        prev = jnp.swapaxes(xp_ref[0], 0, 1)[SUBLANES - CONV_LEFT:]
        nxt = jnp.swapaxes(xn_ref[0], 0, 1)[:CONV_RIGHT]
        ext[d, 0:CONV_LEFT] = jnp.where(first, 0.0, prev)
        for i in range(ntb):
            o = CONV_LEFT + i * SUBLANES
            ext[d, o:o + SUBLANES] = jnp.swapaxes(x_ref[i], 0, 1)
        ext[d, CONV_LEFT + tc:CONV_LEFT + tc + CONV_RIGHT] = jnp.where(last, 0.0, nxt)

        def gates(i, carry, d=d):
            t0 = pl.multiple_of(i * sub, sub)
            u = cb
            for k in range(CONV_W):
                u = u + ext[d, pl.ds(t0 + k, sub)] * cw[k:k + 1, :]
            u2 = u.reshape(sub * bsz, c)
            z = jnp.dot(u2.astype(BF16), w_ref[d], preferred_element_type=F32) + bias_ref[d]
            a = jnp.exp2(clam_ref[d] * jnp.tanh(z[:, :c]) + clam_ref[d])
            gi = 0.5 * jnp.tanh(z[:, c:]) + 0.5
            s1 = 1.0 - a * a
            bb = (s1 * lax.rsqrt(jnp.maximum(s1, 1e-36))) * (gi * u2)
            a_s[d, pl.ds(t0, sub)] = a.reshape(sub, bsz, c)
            b_s[d, pl.ds(t0, sub)] = bb.reshape(sub, bsz, c)
            return carry

        lax.fori_loop(0, tc // sub, gates, 0)

    def step(k, hs):
        hf, hb = hs
        kb = tc - 1 - k
        hf = a_s[0, k] * hf + b_s[0, k]
        hb = a_s[1, kb] * hb + b_s[1, kb]
        h_s[0, k] = hf
        h_s[1, kb] = hb
        return hf, hb

    hf, hb = lax.fori_loop(0, tc, step, (hstate[0], hstate[1]), unroll=8)
    hstate[0] = hf
    hstate[1] = hb
    for d, o_ref in enumerate((hf_ref, hb_ref)):
        for i in range(ntb):
            o_ref[i] = jnp.swapaxes(h_s[d, i * SUBLANES:(i + 1) * SUBLANES], 0, 1)


def _rglru_call(rgx, cw, cb, w, bias, clam, *, seq):
    ntb_all, bsz, _, c = rgx.shape
    t = ntb_all * SUBLANES
    tc = 128
    sub = 32
    assert seq % tc == 0 and (t - seq) % tc == 0
    nl, nc = seq // tc, (t - seq) // tc
    ntb = tc // SUBLANES

    def specs(rev):
        chunk = functools.partial(_scan_chunk, rev=rev, nl=nl, nc=nc)
        return [
            pl.BlockSpec((ntb, bsz, SUBLANES, c), lambda s: (chunk(s), 0, 0, 0)),
            pl.BlockSpec((1, bsz, SUBLANES, c),
                         lambda s: (jnp.maximum(chunk(s) * ntb - 1, 0), 0, 0, 0)),
            pl.BlockSpec((1, bsz, SUBLANES, c),
                         lambda s: (jnp.minimum((chunk(s) + 1) * ntb, ntb_all - 1), 0, 0, 0)),
        ]

    kern = functools.partial(_rglru_kernel, nl=nl, nc=nc, tc=tc, sub=sub)
    fspec, bspec = specs(False), specs(True)
    return pl.pallas_call(
        kern,
        grid=(nl + nc,),
        in_specs=fspec + bspec + [
            _const_spec((CONV_W, c)),
            _const_spec((1, c)),
            _const_spec((2, c, 2 * c)),
            _const_spec((2, 1, 2 * c)),
            _const_spec((2, 1, c)),
        ],
        out_specs=[fspec[0], bspec[0]],
        out_shape=[jax.ShapeDtypeStruct(rgx.shape, F32)] * 2,
        scratch_shapes=[
            pltpu.VMEM((2, tc + CONV_W - 1, bsz, c), F32),
            pltpu.VMEM((2, tc, bsz, c), F32),
            pltpu.VMEM((2, tc, bsz, c), F32),
            pltpu.VMEM((2, tc, bsz, c), F32),
            pltpu.VMEM((2, bsz, c), F32),
        ],
        compiler_params=_cparams(("arbitrary",)),
        name="rglru",
    )(rgx, rgx, rgx, rgx, rgx, rgx, cw, cb, w, bias, clam)


def _log_sigmoid(x):
    return -(jnp.maximum(-x, 0.0) + jnp.log1p(jnp.exp(-jnp.abs(x))))


def _lane_scan(x, op, ident, rev):
    n = x.shape[1]
    lane = lax.broadcasted_iota(jnp.int32, x.shape, 1)
    sh = 1
    while sh < n:
        if rev:
            x = op(x, jnp.where(lane < n - sh, pltpu.roll(x, n - sh, 1), ident))
        else:
            x = op(x, jnp.where(lane >= sh, pltpu.roll(x, sh, 1), ident))
        sh *= 2
    return x


def _mlgate_kernel(g_ref, row_o, col_o, cb_s, gg_s, ml_s, tot_s, wm_s, mo_s, mn_s, *, nl, nc):
    L = ML_CHUNK
    H = ML_HEADS
    U = ML_UNITS
    is_bwd = lax.broadcasted_iota(jnp.int32, (U, L), 0) >= H
    for c in range(nl + nc):
        lanes = slice(c * L, (c + 1) * L)
        g16 = g_ref[:, lanes]
        li = jnp.concatenate([g16[0:H], g16[2 * H:3 * H]], axis=0)
        lf = _log_sigmoid(jnp.concatenate([g16[H:2 * H], g16[3 * H:4 * H]], axis=0))
        pre = _lane_scan(lf, jnp.add, 0.0, False)
        suf = _lane_scan(lf, jnp.add, 0.0, True)
        tot = pre + suf - lf
        cb = jnp.where(is_bwd, suf, pre)
        gg = li - cb
        pmax = _lane_scan(gg, jnp.maximum, NEG, False)
        smax = _lane_scan(gg, jnp.maximum, NEG, True)
        cb_s[:, lanes] = cb
        gg_s[:, lanes] = gg
        ml_s[:, lanes] = cb + jnp.where(is_bwd, smax, pmax)
        tot_s[:, lanes] = tot
        wm_s[:, lanes] = tot + jnp.maximum(pmax, smax)

    m = jnp.zeros((U, L), F32)
    for s in range(nl + nc):
        cf = nl + s if s < nc else s - nc
        cr = nl + nc - 1 - s if s < nc else nl - 1 - (s - nc)
        lf_, lr_ = slice(cf * L, (cf + 1) * L), slice(cr * L, (cr + 1) * L)
        tot = jnp.where(is_bwd, tot_s[:, lr_], tot_s[:, lf_])
        wm = jnp.where(is_bwd, wm_s[:, lr_], wm_s[:, lf_])
        m_new = jnp.maximum(tot + m, wm)
        mo_s[0:H, lf_] = m[0:H]
        mo_s[H:U, lr_] = m[H:U]
        mn_s[0:H, lf_] = m_new[0:H]
        mn_s[H:U, lr_] = m_new[H:U]
        m = m_new

    cb = cb_s[...]
    gg = gg_s[...]
    tot = tot_s[...]
    mo = mo_s[...]
    mn = mn_s[...]
    m_t = jnp.maximum(cb + mo, ml_s[...])
    row_o[0:U, :] = gg
    row_o[U:2 * U, :] = jnp.exp(tot + gg - mn)
    row_o[2 * U:3 * U, :] = jnp.exp(tot + mo - mn)
    row_o[3 * U:4 * U, :] = mo
    col_o[0:U, :] = cb - m_t
    col_o[U:2 * U, :] = jnp.exp(-m_t)


def _mlgate_call(gt, *, seq):
    bsz, ng, t = gt.shape
    nl, nc = seq // ML_CHUNK, (t - seq) // ML_CHUNK
    kern = functools.partial(_mlgate_kernel, nl=nl, nc=nc)
    rout = pl.BlockSpec((None, 4 * ML_UNITS, t), lambda b: (b, 0, 0))
    cout = pl.BlockSpec((None, 2 * ML_UNITS, t), lambda b: (b, 0, 0))
    return pl.pallas_call(
        kern,
        grid=(bsz,),
        in_specs=[pl.BlockSpec((None, ng, t), lambda b: (b, 0, 0))],
        out_specs=[rout, cout],
        out_shape=[jax.ShapeDtypeStruct((bsz, 4 * ML_UNITS, t), F32),
                   jax.ShapeDtypeStruct((bsz, 2 * ML_UNITS, t), F32)],
        scratch_shapes=[pltpu.VMEM((ML_UNITS, t), F32)] * 7,
        compiler_params=_cparams(("arbitrary",)),
        name="mlstm_gates",
    )(gt)


def _mlstm_kernel(qf_ref, ktf_ref, vf_ref, rf_ref, cf_ref, qb_ref, ktb_ref, vb_ref, rb_ref, cb_ref,
                  hf_ref, hb_ref, c_st):
    s = pl.program_id(1)

    @pl.when(s == 0)
    def _():
        c_st[...] = jnp.zeros_like(c_st)

    L, dh, U = ML_CHUNK, ML_DH, ML_UNITS
    ti = lax.broadcasted_iota(jnp.int32, (L, L), 0)
    si = lax.broadcasted_iota(jnp.int32, (L, L), 1)
    ones = jnp.ones((L, dh), BF16)
    for bb in range(qf_ref.shape[0]):
        for d, (q_ref, kt_ref, v_ref, r_ref, c_ref, o_ref) in enumerate(
                ((qf_ref, ktf_ref, vf_ref, rf_ref, cf_ref, hf_ref),
                 (qb_ref, ktb_ref, vb_ref, rb_ref, cb_ref, hb_ref))):
            tri = (si >= ti) if d == 1 else (si <= ti)
            rows = r_ref[bb]
            cols = c_ref[bb]
            for hd in range(ML_HEADS):
                u = d * ML_HEADS + hd
                sl = slice(hd * dh, (hd + 1) * dh)
                q = q_ref[bb, :, sl]
                kt = kt_ref[bb, sl, :]
                v_ext = jnp.concatenate([v_ref[bb, :, sl], ones], axis=1)
                g_row = rows[u:u + 1, :]
                w_row = rows[U + u:U + u + 1, :]
                a_row = rows[2 * U + u:2 * U + u + 1, :]
                m_row = rows[3 * U + u:3 * U + u + 1, :]
                xb = jnp.broadcast_to(cols[:, u:u + 1], (L, L))
                emt = cols[:, U + u:U + u + 1]
                c_old = c_st[bb * U + u]
                qkc = jnp.dot(q, jnp.concatenate([kt, c_old.astype(BF16)], axis=1),
                              preferred_element_type=F32)
                dm = jnp.exp(jnp.where(tri, xb + g_row, NEG))
                dec = jnp.exp(xb + m_row)
                sc = (qkc[:, :L] * dm).astype(BF16)
                ktw = (kt.astype(F32) * w_row).astype(BF16)
                sv = jnp.dot(jnp.concatenate([sc, ktw], axis=0), v_ext,
                             preferred_element_type=F32)
                ab = sv[:L] + jnp.concatenate([dec, dec], axis=1) * qkc[:, L:]
                o_ref[bb, :, sl] = ab[:, :dh] / jnp.maximum(jnp.abs(ab[:, dh:]), emt)
                c_st[bb * U + u] = jnp.concatenate([a_row, a_row], axis=1) * c_old + sv[L:]


def _mlstm_call(mq, mkt, mv, rowp, colp, *, seq):
    bsz, t, w = mq.shape
    nl, nc = seq // ML_CHUNK, (t - seq) // ML_CHUNK
    cf = functools.partial(_scan_chunk, rev=False, nl=nl, nc=nc)
    cr = functools.partial(_scan_chunk, rev=True, nl=nl, nc=nc)
    nr = rowp.shape[1]
    bpb = 8 if bsz % 8 == 0 else 1

    def tok(ch, wd):
        return pl.BlockSpec((bpb, ML_CHUNK, wd), lambda b, s: (b, ch(s), 0))

    def tr(ch, rows):
        return pl.BlockSpec((bpb, rows, ML_CHUNK), lambda b, s: (b, 0, ch(s)))

    def side(ch):
        return [tok(ch, w), tr(ch, w), tok(ch, w), tr(ch, nr), tok(ch, colp.shape[2])]

    return pl.pallas_call(
        _mlstm_kernel,
        grid=(bsz // bpb, nl + nc),
        in_specs=side(cf) + side(cr),
        out_specs=[tok(cf, w), tok(cr, w)],
        out_shape=[jax.ShapeDtypeStruct((bsz, t, w), F32)] * 2,
        scratch_shapes=[pltpu.VMEM((bpb * ML_UNITS, ML_DH, 2 * ML_DH), F32)],
        compiler_params=_cparams(("arbitrary", "arbitrary")),
        name="mlstm",
    )(mq, mkt, mv, rowp, colp, mq, mkt, mv, rowp, colp)


def _attn_kernel(sink_ref, q_ref, k_ref, v_ref, o_ref, vext, *, seq, qbs):
    kv = pl.program_id(1)
    i = pl.program_id(2)
    blk = ATT_BLOCK
    t = k_ref.shape[0]
    ctx = t - seq
    nlb = seq // blk
    band = 3 * blk

    @pl.when(i == 0)
    def _():
        vext[:, 0:LANES] = v_ref[...]
        vext[:, LANES:2 * LANES] = jnp.ones((t, LANES), BF16)

    lane = lax.broadcasted_iota(jnp.int32, (blk, LANES), 1)
    low = lane < AT_DH
    zero = jnp.zeros((blk, LANES), BF16)
    rows = AT_G * blk
    gi = lax.broadcasted_iota(jnp.int32, (rows, 1), 0) // blk
    snk = jnp.zeros((rows, 1), F32)
    for g in range(AT_G):
        snk = jnp.where(gi == g, sink_ref[kv * AT_G + g] * LOG2E, snk)
    ti = lax.broadcasted_iota(jnp.int32, (blk, band), 0)
    ci = lax.broadcasted_iota(jnp.int32, (blk, band), 1)
    nt = (((1,), (1,)), ((), ()))
    k_ctx = k_ref[seq:t, :]
    v_ctx = vext[seq:t, :]

    for j in range(qbs):
        qb = i * qbs + j
        s0 = pl.multiple_of(jnp.clip((qb - 1) * blk, 0, t - band), blk)
        rel = ci - ti + (s0 - qb * blk + blk)
        lim = jnp.where(qb < nlb, seq - s0, 0)
        ok = jnp.logical_and(jnp.logical_and(rel >= 0, rel <= 2 * blk), ci < lim)
        bias = jnp.where(ok, 0.0, NEG)
        bias = jnp.concatenate([bias] * AT_G, axis=0)

        qs = []
        for h2 in range(AT_G // 2):
            qj = q_ref[j * blk:(j + 1) * blk, h2 * LANES:(h2 + 1) * LANES]
            qs += [jnp.where(low, qj, zero), jnp.where(low, zero, qj)]
        q_all = jnp.concatenate(qs, axis=0)
        s_band = lax.dot_general(q_all, k_ref[pl.ds(s0, band), :], nt,
                                 preferred_element_type=F32) + bias
        s_ctx = lax.dot_general(q_all, k_ctx, nt, preferred_element_type=F32)
        m = jnp.maximum(jnp.maximum(jnp.max(s_band, axis=1, keepdims=True),
                                    jnp.max(s_ctx, axis=1, keepdims=True)), snk)
        p_band = jnp.exp2(s_band - m).astype(BF16)
        p_ctx = jnp.exp2(s_ctx - m).astype(BF16)
        o2 = (jnp.dot(p_band, vext[pl.ds(s0, band), :], preferred_element_type=F32)
              + jnp.dot(p_ctx, v_ctx, preferred_element_type=F32))
        o2 = o2[:, :LANES] / (o2[:, LANES:] + jnp.exp2(snk - m))
        for h2 in range(AT_G // 2):
            a = o2[(2 * h2) * blk:(2 * h2 + 1) * blk, :]
            b = o2[(2 * h2 + 1) * blk:(2 * h2 + 2) * blk, :]
            o_ref[j * blk:(j + 1) * blk, h2 * LANES:(h2 + 1) * LANES] = (
                jnp.where(low, a, b).astype(o_ref.dtype))


def _attn_call(sink, aq, akd, avd, *, seq):
    bsz, t, _ = aq.shape
    blk = ATT_BLOCK
    qbs = 3
    assert t % (qbs * blk) == 0 and t >= 3 * blk
    gw = AT_G * AT_DH
    kern = functools.partial(_attn_kernel, seq=seq, qbs=qbs)
    whole = pl.BlockSpec((None, t, LANES), lambda b, kv, i: (b, 0, kv))
    return pl.pallas_call(
        kern,
        grid=(bsz, AT_KV, t // (qbs * blk)),
        in_specs=[
            pl.BlockSpec(memory_space=pltpu.SMEM),
            pl.BlockSpec((None, qbs * blk, gw), lambda b, kv, i: (b, i, kv)),
            whole, whole,
        ],
        out_specs=pl.BlockSpec((None, qbs * blk, gw), lambda b, kv, i: (b, i, kv)),
        out_shape=jax.ShapeDtypeStruct((bsz, t, AT_W), BF16),
        scratch_shapes=[pltpu.VMEM((t, 2 * LANES), BF16)],
        compiler_params=_cparams(("arbitrary", "arbitrary", "arbitrary")),
        name="attn",
    )(sink, aq, akd, avd)


def _merge_kernel(x_ref, ml_ref, mc_ref, rf_ref, rb_ref, gg_ref, hf_ref, hb_ref, mo_ref, yc_ref,
                  br_ref, ng_ref, wb_ref, wo_ref, g_ref, w1_ref, w3_ref, w2_ref, o_ref, act_ref,
                  *, seq, tm, ck):
    t0 = pl.program_id(1) * tm
    rows = t0 + lax.broadcasted_iota(jnp.int32, (tm, 1), 0)
    is_ctx = rows >= seq
    d = x_ref.shape[1]
    rsum = (rf_ref[...] + rb_ref[...]).reshape(tm, D_RNN)
    ya = (rsum * gg_ref[...].astype(F32)).astype(BF16)
    hsum = hf_ref[...] + hb_ref[...]
    parts = []
    for hd in range(ML_HEADS):
        hh = hsum[:, hd * ML_DH:(hd + 1) * ML_DH]
        parts.append(hh * lax.rsqrt(jnp.mean(hh * hh, axis=-1, keepdims=True) + EPS))
    yb = (jnp.concatenate(parts, axis=1) * ng_ref[...] * mo_ref[...].astype(F32)).astype(BF16)
    m = (br_ref[:, 0:d].astype(F32) * jnp.dot(ya, wb_ref[0], preferred_element_type=F32)
         + br_ref[:, d:2 * d].astype(F32) * jnp.dot(yb, wb_ref[1], preferred_element_type=F32)
         + br_ref[:, 2 * d:3 * d].astype(F32) * jnp.dot(yc_ref[...], wb_ref[2],
                                                       preferred_element_type=F32))
    y = jnp.dot(m.astype(BF16), wo_ref[...], preferred_element_type=F32)
    x1 = x_ref[...] + _row_mod(ml_ref, mc_ref, is_ctx, 5) * y
    o_ref[...] = _ffn_body(x1, ml_ref, mc_ref, is_ctx, g_ref, w1_ref, w3_ref, w2_ref, act_ref,
                           base=6, ck=ck)


def _merge_call(x, ml, mc, rf, rb, gg, hf, hb, mo, yc, br, ng, wb, wo, g, w1, w3, w2,
                *, layer, seq, out_rows):
    bsz, t, d = x.shape
    dff = w1.shape[-1]
    tm = _pick_tile(t, (384, 128))
    ck = _pick_tile(dff, (256, 128))
    kern = functools.partial(_merge_kernel, seq=seq, tm=tm, ck=ck)

    def tok(wd):
        return pl.BlockSpec((None, tm, wd), lambda b, i: (b, i, 0))

    rspec = pl.BlockSpec((tm // SUBLANES, None, SUBLANES, D_RNN), lambda b, i: (i, b, 0, 0))

    return pl.pallas_call(
        kern,
        grid=(bsz, pl.cdiv(out_rows, tm)),
        in_specs=[
            tok(d),
            pl.BlockSpec((None, N_MOD, d), lambda b, i: (b, 0, 0)),
            _const_spec((N_MOD, d)),
            rspec, rspec, tok(D_RNN), tok(ML_W), tok(ML_W), tok(ML_W), tok(AT_W),
            tok(N_BRANCH * d),
            _const_spec((1, ML_W)),
            _const_spec((N_BRANCH, BRANCH_W, d), (layer,)),
            _const_spec((d, d), (layer,)),
            _const_spec((1, d)),
            _const_spec((d, dff), (layer, 1)),
            _const_spec((d, dff), (layer, 1)),
            _const_spec((dff, d), (layer, 1)),
        ],
        out_specs=tok(d),
        out_shape=jax.ShapeDtypeStruct((bsz, out_rows, d), F32),
        scratch_shapes=[pltpu.VMEM((tm, dff), BF16)],
        compiler_params=_cparams(("arbitrary", "arbitrary")),
        name="merge_ffn",
    )(x, ml, mc, rf, rb, gg, hf, hb, mo, yc, br, ng, wb, wo, g, w1, w3, w2)


def _block_diag(w):
    n, bi, bj = w.shape
    eye = jnp.eye(n, dtype=w.dtype)
    return (eye[:, None, :, None] * w[:, :, None, :]).reshape(n * bi, n * bj)


def _rope_tables(seq, t):
    rows = seq // GRID_W
    row = jnp.repeat(jnp.arange(rows), GRID_W).astype(F32)
    col = jnp.broadcast_to(jnp.arange(GRID_W), (rows, GRID_W)).reshape(-1).astype(F32)
    half = AT_DH // 2
    inv = ROPE_BASE ** (-jnp.arange(0, half, 2, dtype=F32) / half)
    ar = row[:, None] * inv
    ac = col[:, None] * inv
    ang = jnp.concatenate([ar, ar, ac, ac], axis=-1)
    cos = jnp.concatenate([jnp.cos(ang), jnp.ones((t - seq, AT_DH), F32)], axis=0)
    sin = jnp.concatenate([jnp.sin(ang), jnp.zeros((t - seq, AT_DH), F32)], axis=0)
    cos = jnp.tile(cos, (1, LANES // AT_DH))
    sin = jnp.tile(sin, (1, LANES // AT_DH))
    first_half = (jnp.arange(LANES) % 32) < 16
    sin_a = jnp.where(first_half, -sin, 0.0)
    sin_b = jnp.where(first_half, 0.0, sin)
    return cos, sin_a, sin_b


def kernel(x, c, ctx, c_ctx, ada_w, ada_b, norm_g, ffn_w1, ffn_w3, ffn_w2, w_in, rg_conv_w,
           rg_conv_b, rg_wa, rg_ba, rg_wi, rg_bi, rg_lam, ml_gate_b, ml_norm_g, at_qn_g,
           at_kn_g, at_sink, w_branch, w_out):
    bsz, seq, d = x.shape
    nctx = ctx.shape[1]
    t = seq + nctx
    depth = ada_w.shape[0]

    xs = jnp.concatenate([x, ctx], axis=1)
    mod_rows = 2 * SUBLANES
    cc = jnp.concatenate([c, c_ctx[None, :], jnp.zeros((mod_rows - bsz - 1, d), F32)], axis=0)
    mods = _ada_call(cc, ada_w, ada_b).reshape(depth, mod_rows, N_MOD, d)
    cos, sin_a, sin_b = _rope_tables(seq, t)
    gmat = _block_diag(jnp.full((AT_HEADS, AT_DH, AT_DH), 1.0 / AT_DH, F32)).astype(BF16)
    w1, w3, w2 = ffn_w1.astype(BF16), ffn_w3.astype(BF16), ffn_w2.astype(BF16)
    w_head, w_tail = w_in[..., :P_HEAD].astype(BF16), w_in[..., P_HEAD:].astype(BF16)
    w_br, w_o = w_branch.astype(BF16), w_out.astype(BF16)

    for l in range(depth):
        ml = mods[l, :bsz]
        mc = mods[l, bsz]
        last = l == depth - 1

        xs = _ffn_call(xs, ml, mc, norm_g[l, 0][None, :], w1, w3, w2,
                       lead=(l, 0), base=0, seq=seq, out_rows=t)

        qg = jnp.tile(at_qn_g[l], AT_HEADS)[None, :]
        kg = jnp.tile(at_kn_g[l], AT_KV)[None, :]
        gb = jnp.concatenate([ml_gate_b[l], jnp.zeros((LANES - N_GATE,), F32)])[None, :]
        (rgx, rgg, mq, mk, mv, mo, aq, akd, avd, br, mg) = _proj_call(
            xs, ml, mc, norm_g[l, 1][None, :], w_head, w_tail, gmat, qg, kg, gb,
            cos, sin_a, sin_b, layer=l, seq=seq)

        wcat = jnp.stack([jnp.concatenate([_block_diag(rg_wa[l, dr]), _block_diag(rg_wi[l, dr])],
                                          axis=1) for dr in range(2)])
        wcat = (0.5 * wcat).astype(BF16)
        bias = 0.5 * jnp.concatenate([rg_ba[l], rg_bi[l]], axis=1)[:, None, :]
        clam = (-0.5 * LOG2E * LRU_C * jax.nn.softplus(-rg_lam[l]))[:, None, :]
        rhf, rhb = _rglru_call(rgx, rg_conv_w[l], rg_conv_b[l][None, :], wcat, bias, clam,
                               seq=seq)

        gt = jnp.swapaxes(mg[:, :, :N_GATE], 1, 2)
        rowp, colsrc = _mlgate_call(gt, seq=seq)
        colp = jnp.swapaxes(colsrc, 1, 2)
        mkt = jnp.swapaxes(mk, 1, 2)
        mhf, mhb = _mlstm_call(mq, mkt, mv, rowp, colp, seq=seq)

        yc = _attn_call(at_sink[l], aq, akd, avd, seq=seq)

        xs = _merge_call(xs, ml, mc, rhf, rhb, rgg, mhf, mhb, mo, yc, br, ml_norm_g[l][None, :],
                         w_br, w_o, norm_g[l, 2][None, :], w1, w3, w2,
                         layer=l, seq=seq, out_rows=seq if last else t)
    return xs
```

```python
import functools

import jax
import jax.numpy as jnp
from jax import lax
from jax.experimental import pallas as pl
from jax.experimental.pallas import tpu as pltpu

F32 = jnp.float32
BF16 = jnp.bfloat16

EPS = 1e-6
NEG = -1e30
LOG2E = 1.4426950408889634
N_MOD = 9
GRID_W = 64
ROPE_BASE = 10000.0

D_RNN = 512
RNN_BLOCKS = 8
RNN_BLOCK = D_RNN // RNN_BLOCKS
CONV_W = 4
CONV_LEFT = 2
CONV_RIGHT = CONV_W - 1 - CONV_LEFT
LRU_C = 8.0

ML_HEADS = 4
ML_DH = 128
ML_W = ML_HEADS * ML_DH
ML_CHUNK = 128
ML_UNITS = 2 * ML_HEADS

AT_HEADS = 8
AT_KV = 2
AT_DH = 64
AT_G = AT_HEADS // AT_KV
AT_W = AT_HEADS * AT_DH
AT_KVW = AT_KV * AT_DH
ATT_BLOCK = 128

N_BRANCH = 3
BRANCH_W = 512

LANES = 128
SUBLANES = 8
VMEM_LIMIT = 56 * 1024 * 1024

P_RGX = 0
P_RGG = P_RGX + D_RNN
P_MQ = P_RGG + D_RNN
P_MK = P_MQ + ML_W
P_MV = P_MK + ML_W
P_MO = P_MV + ML_W
P_HEAD = P_MO + ML_W
N_GATE = 4 * ML_HEADS
T_AQ = 0
T_AK = T_AQ + AT_W
T_AV = T_AK + AT_KVW
T_BR = T_AV + AT_KVW


def _cparams(sem):
    return pltpu.CompilerParams(dimension_semantics=sem, vmem_limit_bytes=VMEM_LIMIT)


def _const_spec(shape, lead=()):
    nd = len(shape)
    idx = tuple(lead) + (0,) * nd
    return pl.BlockSpec((None,) * len(lead) + tuple(shape), lambda *_: idx,
                        pipeline_mode=pl.Buffered(1))


def _pick_tile(total, candidates):
    for c in candidates:
        if total % c == 0:
            return c
    raise ValueError(f"no tile for {total}")


def _sigmoid(x):
    return jax.nn.sigmoid(x)


def _ln_mod(x, g, shift, scale):
    ms = jnp.mean(x * x, axis=-1, keepdims=True)
    return x * lax.rsqrt(ms + EPS) * g * (1.0 + scale) + shift


def _row_mod(ml_ref, mc_ref, is_ctx, i):
    return jnp.where(is_ctx, mc_ref[i:i + 1, :], ml_ref[i:i + 1, :])


def _ada_kernel(cc_ref, w_ref, b_ref, o_ref):
    cc = cc_ref[...]
    s = cc * _sigmoid(cc)
    o_ref[...] = jnp.dot(s.astype(BF16), w_ref[...].astype(BF16),
                         preferred_element_type=F32) + b_ref[...]


def _ada_call(cc, ada_w, ada_b):
    depth, d, nout = ada_w.shape
    rows = cc.shape[0]
    tn = _pick_tile(nout, (1536, 1024, 512, 256, 128))
    return pl.pallas_call(
        _ada_kernel,
        grid=(depth, nout // tn),
        in_specs=[
            pl.BlockSpec((rows, d), lambda l, j: (0, 0)),
            pl.BlockSpec((None, d, tn), lambda l, j: (l, 0, j)),
            pl.BlockSpec((None, 1, tn), lambda l, j: (l, 0, j)),
        ],
        out_specs=pl.BlockSpec((None, rows, tn), lambda l, j: (l, 0, j)),
        out_shape=jax.ShapeDtypeStruct((depth, rows, nout), F32),
        compiler_params=_cparams(("arbitrary", "arbitrary")),
        name="ada_mod",
    )(cc, ada_w, ada_b.reshape(depth, 1, nout))


def _ffn_body(x, ml_ref, mc_ref, is_ctx, g_ref, w1_ref, w3_ref, w2_ref, act_ref, *, base, ck):
    h = _ln_mod(x, g_ref[...], _row_mod(ml_ref, mc_ref, is_ctx, base),
                _row_mod(ml_ref, mc_ref, is_ctx, base + 1)).astype(BF16)
    dff = w1_ref.shape[1]
    for j in range(dff // ck):
        a = jnp.dot(h, w1_ref[:, j * ck:(j + 1) * ck], preferred_element_type=F32)
        b = jnp.dot(h, w3_ref[:, j * ck:(j + 1) * ck], preferred_element_type=F32)
        act_ref[:, j * ck:(j + 1) * ck] = (a * _sigmoid(a) * b).astype(BF16)
    y = jnp.dot(act_ref[...], w2_ref[...], preferred_element_type=F32)
    return x + 0.5 * _row_mod(ml_ref, mc_ref, is_ctx, base + 2) * y


def _ffn_weight_scratch(d, dff, ck):
    return [pltpu.VMEM((d, dff), BF16), pltpu.VMEM((d, dff), BF16), pltpu.VMEM((dff, d), BF16),
            pltpu.VMEM((2, d, ck), F32), pltpu.VMEM((2, ck, d), F32),
            pltpu.SemaphoreType.DMA((2,))]


def _load_ffn_weights(w1_hbm, w3_hbm, w2_hbm, lead, w1b, w3b, w2b, stg_c, stg_r, sem, ck):
    dff = w1b.shape[1]
    nck = dff // ck
    jobs = []
    for src, dst in ((w1_hbm, w1b), (w3_hbm, w3b)):
        for j in range(nck):
            cols = slice(j * ck, (j + 1) * ck)
            jobs.append((src.at[lead[0], lead[1], :, cols], stg_c, dst.at[:, cols]))
    for j in range(nck):
        rws = slice(j * ck, (j + 1) * ck)
        jobs.append((w2_hbm.at[lead[0], lead[1], rws, :], stg_r, w2b.at[rws, :]))
    copies = [pltpu.make_async_copy(src, stg.at[k % 2], sem.at[k % 2])
              for k, (src, stg, _) in enumerate(jobs)]
    copies[0].start()
    for k, (_, stg, dst) in enumerate(jobs):
        if k + 1 < len(jobs):
            copies[k + 1].start()
        copies[k].wait()
        dst[...] = stg[k % 2].astype(BF16)


def _ffn_kernel(x_ref, ml_ref, mc_ref, g_ref, w1_hbm, w3_hbm, w2_hbm, o_ref, act_ref,
                w1b, w3b, w2b, stg_c, stg_r, sem, *, lead, base, seq, tm, ck):
    @pl.when(jnp.logical_and(pl.program_id(0) == 0, pl.program_id(1) == 0))
    def _():
        _load_ffn_weights(w1_hbm, w3_hbm, w2_hbm, lead, w1b, w3b, w2b, stg_c, stg_r, sem, ck)

    t0 = pl.program_id(1) * tm
    rows = t0 + lax.broadcasted_iota(jnp.int32, (tm, 1), 0)
    is_ctx = rows >= seq
    o_ref[...] = _ffn_body(x_ref[...], ml_ref, mc_ref, is_ctx, g_ref, w1b, w3b, w2b,
                           act_ref, base=base, ck=ck)


def _ffn_call(x, ml, mc, g, w1, w3, w2, *, lead, base, seq, out_rows):
    bsz, t, d = x.shape
    dff = w1.shape[-1]
    tm = _pick_tile(t, (768, 384, 128))
    ck = _pick_tile(dff, (256, 128))
    kern = functools.partial(_ffn_kernel, lead=lead, base=base, seq=seq, tm=tm, ck=ck)
    hbm = pl.BlockSpec(memory_space=pl.ANY)
    return pl.pallas_call(
        kern,
        grid=(bsz, pl.cdiv(out_rows, tm)),
        in_specs=[
            pl.BlockSpec((None, tm, d), lambda b, i: (b, i, 0)),
            pl.BlockSpec((None, N_MOD, d), lambda b, i: (b, 0, 0)),
            _const_spec((N_MOD, d)),
            _const_spec((1, d)),
            hbm, hbm, hbm,
        ],
        out_specs=pl.BlockSpec((None, tm, d), lambda b, i: (b, i, 0)),
        out_shape=jax.ShapeDtypeStruct((bsz, out_rows, d), F32),
        scratch_shapes=[pltpu.VMEM((tm, dff), BF16)] + _ffn_weight_scratch(d, dff, ck),
        compiler_params=_cparams(("arbitrary", "arbitrary")),
        name="ffn",
    )(x, ml, mc, g, w1, w3, w2)


def _gelu_tanh(x):
    return 0.5 * x * (1.0 + jnp.tanh(0.7978845608028654 * (x + 0.044715 * (x * x * x))))


def _head_norm(x, gmat, g):
    ms = jnp.dot((x * x).astype(BF16), gmat, preferred_element_type=F32)
    return x * lax.rsqrt(ms + EPS) * g


def _dup_heads(x):
    low = lax.broadcasted_iota(jnp.int32, x.shape, 1) < AT_DH
    sw = pltpu.roll(x, AT_DH, 1)
    return jnp.concatenate([jnp.where(low, x, sw), jnp.where(low, sw, x)], axis=1)


def _rope(x, cos, sin_a, sin_b):
    parts = []
    for j in range(x.shape[1] // LANES):
        xj = x[:, j * LANES:(j + 1) * LANES]
        parts.append(xj * cos + pltpu.roll(xj, LANES - 16, 1) * sin_a
                     + pltpu.roll(xj, 16, 1) * sin_b)
    return jnp.concatenate(parts, axis=1)


def _proj_kernel(x_ref, ml_ref, mc_ref, g_ref, wh_ref, wt_ref, gmat_ref, qg_ref, kg_ref, gb_ref,
                 cos_ref, sa_ref, sb_ref,
                 rgx_o, rgg_o, mq_o, mk_o, mv_o, mo_o, aq_o, ak_o, av_o, br_o, mg_o,
                 tail, *, seq, tm, d_model):
    @pl.when(jnp.logical_and(pl.program_id(0) == 0, pl.program_id(1) == 0))
    def _():
        for r in range(0, d_model, LANES):
            tail[r:r + LANES, :] = wt_ref[r:r + LANES, N_GATE:]

    t0 = pl.program_id(1) * tm
    rows = t0 + lax.broadcasted_iota(jnp.int32, (tm, 1), 0)
    is_ctx = rows >= seq
    h = _ln_mod(x_ref[...], g_ref[...], _row_mod(ml_ref, mc_ref, is_ctx, 3),
                _row_mod(ml_ref, mc_ref, is_ctx, 4)).astype(BF16)

    def seg(c0, w, ref=wh_ref):
        return jnp.dot(h, ref[:, c0:c0 + w], preferred_element_type=F32)

    rgx_o[...] = seg(P_RGX, D_RNN).reshape(rgx_o.shape)
    rgg_o[...] = _gelu_tanh(seg(P_RGG, D_RNN)).astype(BF16)
    mq_o[...] = seg(P_MQ, ML_W).astype(BF16)
    mk_o[...] = (seg(P_MK, ML_W) * (ML_DH ** -0.5)).astype(BF16)
    mv_o[...] = seg(P_MV, ML_W).astype(BF16)
    mo_o[...] = _sigmoid(seg(P_MO, ML_W)).astype(BF16)
    cos, sin_a, sin_b = cos_ref[...], sa_ref[...], sb_ref[...]
    gmat = gmat_ref[...]
    q = _head_norm(seg(T_AQ, AT_W, tail), gmat, qg_ref[...])
    aq_o[...] = (_rope(q, cos, sin_a, sin_b) * (AT_DH ** -0.5 * LOG2E)).astype(BF16)
    k = _head_norm(seg(T_AK, AT_KVW, tail), gmat[:AT_KVW, :AT_KVW], kg_ref[...])
    ak_o[...] = _dup_heads(_rope(k, cos, sin_a, sin_b)).astype(BF16)
    av_o[...] = _dup_heads(seg(T_AV, AT_KVW, tail)).astype(BF16)
    for c0 in range(0, N_BRANCH * d_model, 512):
        br_o[:, c0:c0 + 512] = _sigmoid(seg(T_BR + c0, 512, tail)).astype(BF16)
    mg_o[...] = seg(0, LANES, wt_ref) + gb_ref[...]


def _proj_call(x, ml, mc, g, wh, wt, gmat, qg, kg, gb, cos, sin_a, sin_b, *, layer, seq):
    bsz, t, d = x.shape
    ntail = wt.shape[-1] - N_GATE
    tm = _pick_tile(t, (384, 128))
    kern = functools.partial(_proj_kernel, seq=seq, tm=tm, d_model=d)
    widths = [(D_RNN, F32), (D_RNN, BF16), (ML_W, BF16), (ML_W, BF16), (ML_W, BF16),
              (ML_W, BF16), (AT_W, BF16), (2 * AT_KVW, BF16), (2 * AT_KVW, BF16),
              (N_BRANCH * d, BF16), (LANES, F32)]
    tab_spec = pl.BlockSpec((tm, LANES), lambda b, i: (i, 0))
    out_specs = [pl.BlockSpec((None, tm, wd), lambda b, i: (b, i, 0)) for wd, _ in widths]
    out_shape = [jax.ShapeDtypeStruct((bsz, t, wd), dt) for wd, dt in widths]
    out_specs[0] = pl.BlockSpec((tm // SUBLANES, None, SUBLANES, D_RNN), lambda b, i: (i, b, 0, 0))
    out_shape[0] = jax.ShapeDtypeStruct((t // SUBLANES, bsz, SUBLANES, D_RNN), F32)
    return pl.pallas_call(
        kern,
        grid=(bsz, t // tm),
        in_specs=[
            pl.BlockSpec((None, tm, d), lambda b, i: (b, i, 0)),
            pl.BlockSpec((None, N_MOD, d), lambda b, i: (b, 0, 0)),
            _const_spec((N_MOD, d)),
            _const_spec((1, d)),
            _const_spec((d, wh.shape[-1]), (layer,)),
            _const_spec((d, wt.shape[-1]), (layer,)),
            _const_spec((AT_W, AT_W)),
            _const_spec((1, AT_W)),
            _const_spec((1, AT_KVW)),
            _const_spec((1, LANES)),
            tab_spec, tab_spec, tab_spec,
        ],
        out_specs=out_specs,
        out_shape=out_shape,
        scratch_shapes=[pltpu.VMEM((d, ntail), BF16)],
        compiler_params=_cparams(("arbitrary", "arbitrary")),
        name="in_proj",
    )(x, ml, mc, g, wh, wt, gmat, qg, kg, gb, cos, sin_a, sin_b)


def _scan_chunk(s, rev, nl, nc):
    if rev:
        return jnp.where(s < nc, nl + nc - 1 - s, nl - 1 - (s - nc))
    return jnp.where(s < nc, nl + s, s - nc)


def _rglru_kernel(xf_ref, xfp_ref, xfn_ref, xb_ref, xbp_ref, xbn_ref, cw_ref, cb_ref, w_ref,
                  bias_ref, clam_ref, hf_ref, hb_ref, ext, a_s, b_s, h_s, hstate,
                  *, nl, nc, tc, sub):
    s = pl.program_id(0)
    bsz, c = hstate.shape[1], hstate.shape[2]
    ntb = tc // SUBLANES

    @pl.when(s == 0)
    def _():
        hstate[...] = jnp.zeros_like(hstate)

    cw = cw_ref[...]
    cb = cb_ref[...]
    for d, (x_ref, xp_ref, xn_ref) in enumerate(((xf_ref, xfp_ref, xfn_ref),
                                                (xb_ref, xbp_ref, xbn_ref))):
        chunk = _scan_chunk(s, d == 1, nl, nc)
        first = jnp.logical_or(chunk == 0, chunk == nl)
        last = jnp.logical_or(chunk == nl - 1, chunk == nl + nc - 1)
        prev = jnp.swapaxes(xp_ref[0], 0, 1)[SUBLANES - CONV_LEFT:]
        nxt = jnp.swapaxes(xn_ref[0], 0, 1)[:CONV_RIGHT]
        ext[d, 0:CONV_LEFT] = jnp.where(first, 0.0, prev)
        for i in range(ntb):
            o = CONV_LEFT + i * SUBLANES
            ext[d, o:o + SUBLANES] = jnp.swapaxes(x_ref[i], 0, 1)
        ext[d, CONV_LEFT + tc:CONV_LEFT + tc + CONV_RIGHT] = jnp.where(last, 0.0, nxt)

        def gates(i, carry, d=d):
            t0 = pl.multiple_of(i * sub, sub)
            u = cb
            for k in range(CONV_W):
                u = u + ext[d, pl.ds(t0 + k, sub)] * cw[k:k + 1, :]
            u2 = u.reshape(sub * bsz, c)
            z = jnp.dot(u2.astype(BF16), w_ref[d], preferred_element_type=F32) + bias_ref[d]
            a = jnp.exp2(clam_ref[d] * jnp.tanh(z[:, :c]) + clam_ref[d])
            gi = 0.5 * jnp.tanh(z[:, c:]) + 0.5
            s1 = 1.0 - a * a
            bb = (s1 * lax.rsqrt(jnp.maximum(s1, 1e-36))) * (gi * u2)
            a_s[d, pl.ds(t0, sub)] = a.reshape(sub, bsz, c)
            b_s[d, pl.ds(t0, sub)] = bb.reshape(sub, bsz, c)
            return carry

        lax.fori_loop(0, tc // sub, gates, 0)

    def step(k, hs):
        hf, hb = hs
        kb = tc - 1 - k
        hf = a_s[0, k] * hf + b_s[0, k]
        hb = a_s[1, kb] * hb + b_s[1, kb]
        h_s[0, k] = hf
        h_s[1, kb] = hb
        return hf, hb

    hf, hb = lax.fori_loop(0, tc, step, (hstate[0], hstate[1]), unroll=8)
    hstate[0] = hf
    hstate[1] = hb
    for d, o_ref in enumerate((hf_ref, hb_ref)):
        for i in range(ntb):
            o_ref[i] = jnp.swapaxes(h_s[d, i * SUBLANES:(i + 1) * SUBLANES], 0, 1)


def _rglru_call(rgx, cw, cb, w, bias, clam, *, seq):
    ntb_all, bsz, _, c = rgx.shape
    t = ntb_all * SUBLANES
    tc = 128
    sub = 32
    assert seq % tc == 0 and (t - seq) % tc == 0
    nl, nc = seq // tc, (t - seq) // tc
    ntb = tc // SUBLANES

    def specs(rev):
        chunk = functools.partial(_scan_chunk, rev=rev, nl=nl, nc=nc)
        return [
            pl.BlockSpec((ntb, bsz, SUBLANES, c), lambda s: (chunk(s), 0, 0, 0)),
            pl.BlockSpec((1, bsz, SUBLANES, c),
                         lambda s: (jnp.maximum(chunk(s) * ntb - 1, 0), 0, 0, 0)),
            pl.BlockSpec((1, bsz, SUBLANES, c),
                         lambda s: (jnp.minimum((chunk(s) + 1) * ntb, ntb_all - 1), 0, 0, 0)),
        ]

    kern = functools.partial(_rglru_kernel, nl=nl, nc=nc, tc=tc, sub=sub)
    fspec, bspec = specs(False), specs(True)
    return pl.pallas_call(
        kern,
        grid=(nl + nc,),
        in_specs=fspec + bspec + [
            _const_spec((CONV_W, c)),
            _const_spec((1, c)),
            _const_spec((2, c, 2 * c)),
            _const_spec((2, 1, 2 * c)),
            _const_spec((2, 1, c)),
        ],
        out_specs=[fspec[0], bspec[0]],
        out_shape=[jax.ShapeDtypeStruct(rgx.shape, F32)] * 2,
        scratch_shapes=[
            pltpu.VMEM((2, tc + CONV_W - 1, bsz, c), F32),
            pltpu.VMEM((2, tc, bsz, c), F32),
            pltpu.VMEM((2, tc, bsz, c), F32),
            pltpu.VMEM((2, tc, bsz, c), F32),
            pltpu.VMEM((2, bsz, c), F32),
        ],
        compiler_params=_cparams(("arbitrary",)),
        name="rglru",
    )(rgx, rgx, rgx, rgx, rgx, rgx, cw, cb, w, bias, clam)


def _log_sigmoid(x):
    return -(jnp.maximum(-x, 0.0) + jnp.log1p(jnp.exp(-jnp.abs(x))))


def _lane_scan(x, op, ident, rev):
    n = x.shape[1]
    lane = lax.broadcasted_iota(jnp.int32, x.shape, 1)
    sh = 1
    while sh < n:
        if rev:
            x = op(x, jnp.where(lane < n - sh, pltpu.roll(x, n - sh, 1), ident))
        else:
            x = op(x, jnp.where(lane >= sh, pltpu.roll(x, sh, 1), ident))
        sh *= 2
    return x


def _mlgate_kernel(g_ref, row_o, col_o, cb_s, gg_s, ml_s, tot_s, wm_s, mo_s, mn_s, *, nl, nc):
    L = ML_CHUNK
    H = ML_HEADS
    U = ML_UNITS
    is_bwd = lax.broadcasted_iota(jnp.int32, (U, L), 0) >= H
    for c in range(nl + nc):
        lanes = slice(c * L, (c + 1) * L)
        g16 = g_ref[:, lanes]
        li = jnp.concatenate([g16[0:H], g16[2 * H:3 * H]], axis=0)
        lf = _log_sigmoid(jnp.concatenate([g16[H:2 * H], g16[3 * H:4 * H]], axis=0))
        pre = _lane_scan(lf, jnp.add, 0.0, False)
        suf = _lane_scan(lf, jnp.add, 0.0, True)
        tot = pre + suf - lf
        cb = jnp.where(is_bwd, suf, pre)
        gg = li - cb
        pmax = _lane_scan(gg, jnp.maximum, NEG, False)
        smax = _lane_scan(gg, jnp.maximum, NEG, True)
        cb_s[:, lanes] = cb
        gg_s[:, lanes] = gg
        ml_s[:, lanes] = cb + jnp.where(is_bwd, smax, pmax)
        tot_s[:, lanes] = tot
        wm_s[:, lanes] = tot + jnp.maximum(pmax, smax)

    m = jnp.zeros((U, L), F32)
    for s in range(nl + nc):
        cf = nl + s if s < nc else s - nc
        cr = nl + nc - 1 - s if s < nc else nl - 1 - (s - nc)
        lf_, lr_ = slice(cf * L, (cf + 1) * L), slice(cr * L, (cr + 1) * L)
        tot = jnp.where(is_bwd, tot_s[:, lr_], tot_s[:, lf_])
        wm = jnp.where(is_bwd, wm_s[:, lr_], wm_s[:, lf_])
        m_new = jnp.maximum(tot + m, wm)
        mo_s[0:H, lf_] = m[0:H]
        mo_s[H:U, lr_] = m[H:U]
        mn_s[0:H, lf_] = m_new[0:H]
        mn_s[H:U, lr_] = m_new[H:U]
        m = m_new

    cb = cb_s[...]
    gg = gg_s[...]
    tot = tot_s[...]
    mo = mo_s[...]
    mn = mn_s[...]
    m_t = jnp.maximum(cb + mo, ml_s[...])
    row_o[0:U, :] = gg
    row_o[U:2 * U, :] = jnp.exp(tot + gg - mn)
    row_o[2 * U:3 * U, :] = jnp.exp(tot + mo - mn)
    row_o[3 * U:4 * U, :] = mo
    col_o[0:U, :] = cb - m_t
    col_o[U:2 * U, :] = jnp.exp(-m_t)


def _mlgate_call(gt, *, seq):
    bsz, ng, t = gt.shape
    nl, nc = seq // ML_CHUNK, (t - seq) // ML_CHUNK
    kern = functools.partial(_mlgate_kernel, nl=nl, nc=nc)
    rout = pl.BlockSpec((None, 4 * ML_UNITS, t), lambda b: (b, 0, 0))
    cout = pl.BlockSpec((None, 2 * ML_UNITS, t), lambda b: (b, 0, 0))
    return pl.pallas_call(
        kern,
        grid=(bsz,),
        in_specs=[pl.BlockSpec((None, ng, t), lambda b: (b, 0, 0))],
        out_specs=[rout, cout],
        out_shape=[jax.ShapeDtypeStruct((bsz, 4 * ML_UNITS, t), F32),
                   jax.ShapeDtypeStruct((bsz, 2 * ML_UNITS, t), F32)],
        scratch_shapes=[pltpu.VMEM((ML_UNITS, t), F32)] * 7,
        compiler_params=_cparams(("arbitrary",)),
        name="mlstm_gates",
    )(gt)


def _mlstm_kernel(qf_ref, ktf_ref, vf_ref, rf_ref, cf_ref, qb_ref, ktb_ref, vb_ref, rb_ref, cb_ref,
                  hf_ref, hb_ref, c_st):
    s = pl.program_id(1)

    @pl.when(s == 0)
    def _():
        c_st[...] = jnp.zeros_like(c_st)

    L, dh, U = ML_CHUNK, ML_DH, ML_UNITS
    ti = lax.broadcasted_iota(jnp.int32, (L, L), 0)
    si = lax.broadcasted_iota(jnp.int32, (L, L), 1)
    ones = jnp.ones((L, dh), BF16)
    for bb in range(qf_ref.shape[0]):
        for d, (q_ref, kt_ref, v_ref, r_ref, c_ref, o_ref) in enumerate(
                ((qf_ref, ktf_ref, vf_ref, rf_ref, cf_ref, hf_ref),
                 (qb_ref, ktb_ref, vb_ref, rb_ref, cb_ref, hb_ref))):
            tri = (si >= ti) if d == 1 else (si <= ti)
            rows = r_ref[bb]
            cols = c_ref[bb]
            for hd in range(ML_HEADS):
                u = d * ML_HEADS + hd
                sl = slice(hd * dh, (hd + 1) * dh)
                q = q_ref[bb, :, sl]
                kt = kt_ref[bb, sl, :]
                v_ext = jnp.concatenate([v_ref[bb, :, sl], ones], axis=1)
                g_row = rows[u:u + 1, :]
                w_row = rows[U + u:U + u + 1, :]
                a_row = rows[2 * U + u:2 * U + u + 1, :]
                m_row = rows[3 * U + u:3 * U + u + 1, :]
                xb = jnp.broadcast_to(cols[:, u:u + 1], (L, L))
                emt = cols[:, U + u:U + u + 1]
                c_old = c_st[bb * U + u]
                qkc = jnp.dot(q, jnp.concatenate([kt, c_old.astype(BF16)], axis=1),
                              preferred_element_type=F32)
                dm = jnp.exp(jnp.where(tri, xb + g_row, NEG))
                dec = jnp.exp(xb + m_row)
                sc = (qkc[:, :L] * dm).astype(BF16)
                ktw = (kt.astype(F32) * w_row).astype(BF16)
                sv = jnp.dot(jnp.concatenate([sc, ktw], axis=0), v_ext,
                             preferred_element_type=F32)
                ab = sv[:L] + jnp.concatenate([dec, dec], axis=1) * qkc[:, L:]
                o_ref[bb, :, sl] = ab[:, :dh] / jnp.maximum(jnp.abs(ab[:, dh:]), emt)
                c_st[bb * U + u] = jnp.concatenate([a_row, a_row], axis=1) * c_old + sv[L:]


def _mlstm_call(mq, mkt, mv, rowp, colp, *, seq):
    bsz, t, w = mq.shape
    nl, nc = seq // ML_CHUNK, (t - seq) // ML_CHUNK
    cf = functools.partial(_scan_chunk, rev=False, nl=nl, nc=nc)
    cr = functools.partial(_scan_chunk, rev=True, nl=nl, nc=nc)
    nr = rowp.shape[1]
    bpb = 8 if bsz % 8 == 0 else 1

    def tok(ch, wd):
        return pl.BlockSpec((bpb, ML_CHUNK, wd), lambda b, s: (b, ch(s), 0))

    def tr(ch, rows):
        return pl.BlockSpec((bpb, rows, ML_CHUNK), lambda b, s: (b, 0, ch(s)))

    def side(ch):
        return [tok(ch, w), tr(ch, w), tok(ch, w), tr(ch, nr), tok(ch, colp.shape[2])]

    return pl.pallas_call(
        _mlstm_kernel,
        grid=(bsz // bpb, nl + nc),
        in_specs=side(cf) + side(cr),
        out_specs=[tok(cf, w), tok(cr, w)],
        out_shape=[jax.ShapeDtypeStruct((bsz, t, w), F32)] * 2,
        scratch_shapes=[pltpu.VMEM((bpb * ML_UNITS, ML_DH, 2 * ML_DH), F32)],
        compiler_params=_cparams(("arbitrary", "arbitrary")),
        name="mlstm",
    )(mq, mkt, mv, rowp, colp, mq, mkt, mv, rowp, colp)


def _attn_kernel(sink_ref, q_ref, k_ref, v_ref, o_ref, vext, *, seq, qbs):
    kv = pl.program_id(1)
    i = pl.program_id(2)
    blk = ATT_BLOCK
    t = k_ref.shape[0]
    ctx = t - seq
    nlb = seq // blk
    band = 3 * blk

    @pl.when(i == 0)
    def _():
        vext[:, 0:LANES] = v_ref[...]
        vext[:, LANES:2 * LANES] = jnp.ones((t, LANES), BF16)

    lane = lax.broadcasted_iota(jnp.int32, (blk, LANES), 1)
    low = lane < AT_DH
    zero = jnp.zeros((blk, LANES), BF16)
    rows = AT_G * blk
    gi = lax.broadcasted_iota(jnp.int32, (rows, 1), 0) // blk
    snk = jnp.zeros((rows, 1), F32)
    for g in range(AT_G):
        snk = jnp.where(gi == g, sink_ref[kv * AT_G + g] * LOG2E, snk)
    ti = lax.broadcasted_iota(jnp.int32, (blk, band), 0)
    ci = lax.broadcasted_iota(jnp.int32, (blk, band), 1)
    nt = (((1,), (1,)), ((), ()))
    k_ctx = k_ref[seq:t, :]
    v_ctx = vext[seq:t, :]

    for j in range(qbs):
        qb = i * qbs + j
        s0 = pl.multiple_of(jnp.clip((qb - 1) * blk, 0, t - band), blk)
        rel = ci - ti + (s0 - qb * blk + blk)
        lim = jnp.where(qb < nlb, seq - s0, 0)
        ok = jnp.logical_and(jnp.logical_and(rel >= 0, rel <= 2 * blk), ci < lim)
        bias = jnp.where(ok, 0.0, NEG)
        bias = jnp.concatenate([bias] * AT_G, axis=0)

        qs = []
        for h2 in range(AT_G // 2):
            qj = q_ref[j * blk:(j + 1) * blk, h2 * LANES:(h2 + 1) * LANES]
            qs += [jnp.where(low, qj, zero), jnp.where(low, zero, qj)]
        q_all = jnp.concatenate(qs, axis=0)
        s_band = lax.dot_general(q_all, k_ref[pl.ds(s0, band), :], nt,
                                 preferred_element_type=F32) + bias
        s_ctx = lax.dot_general(q_all, k_ctx, nt, preferred_element_type=F32)
        m = jnp.maximum(jnp.maximum(jnp.max(s_band, axis=1, keepdims=True),
                                    jnp.max(s_ctx, axis=1, keepdims=True)), snk)
        p_band = jnp.exp2(s_band - m).astype(BF16)
        p_ctx = jnp.exp2(s_ctx - m).astype(BF16)
        o2 = (jnp.dot(p_band, vext[pl.ds(s0, band), :], preferred_element_type=F32)
              + jnp.dot(p_ctx, v_ctx, preferred_element_type=F32))
        o2 = o2[:, :LANES] / (o2[:, LANES:] + jnp.exp2(snk - m))
        for h2 in range(AT_G // 2):
            a = o2[(2 * h2) * blk:(2 * h2 + 1) * blk, :]
            b = o2[(2 * h2 + 1) * blk:(2 * h2 + 2) * blk, :]
            o_ref[j * blk:(j + 1) * blk, h2 * LANES:(h2 + 1) * LANES] = (
                jnp.where(low, a, b).astype(o_ref.dtype))


def _attn_call(sink, aq, akd, avd, *, seq):
    bsz, t, _ = aq.shape
    blk = ATT_BLOCK
    qbs = 3
    assert t % (qbs * blk) == 0 and t >= 3 * blk
    gw = AT_G * AT_DH
    kern = functools.partial(_attn_kernel, seq=seq, qbs=qbs)
    whole = pl.BlockSpec((None, t, LANES), lambda b, kv, i: (b, 0, kv))
    return pl.pallas_call(
        kern,
        grid=(bsz, AT_KV, t // (qbs * blk)),
        in_specs=[
            pl.BlockSpec(memory_space=pltpu.SMEM),
            pl.BlockSpec((None, qbs * blk, gw), lambda b, kv, i: (b, i, kv)),
            whole, whole,
        ],
        out_specs=pl.BlockSpec((None, qbs * blk, gw), lambda b, kv, i: (b, i, kv)),
        out_shape=jax.ShapeDtypeStruct((bsz, t, AT_W), BF16),
        scratch_shapes=[pltpu.VMEM((t, 2 * LANES), BF16)],
        compiler_params=_cparams(("arbitrary", "arbitrary", "arbitrary")),
        name="attn",
    )(sink, aq, akd, avd)


def _merge_kernel(x_ref, ml_ref, mc_ref, rf_ref, rb_ref, gg_ref, hf_ref, hb_ref, mo_ref, yc_ref,
                  br_ref, ng_ref, wb_ref, wo_ref, g_ref, w1_hbm, w3_hbm, w2_hbm, o_ref, act_ref,
                  w1b, w3b, w2b, stg_c, stg_r, sem, *, lead, seq, tm, ck):
    @pl.when(jnp.logical_and(pl.program_id(0) == 0, pl.program_id(1) == 0))
    def _():
        _load_ffn_weights(w1_hbm, w3_hbm, w2_hbm, lead, w1b, w3b, w2b, stg_c, stg_r, sem, ck)

    t0 = pl.program_id(1) * tm
    rows = t0 + lax.broadcasted_iota(jnp.int32, (tm, 1), 0)
    is_ctx = rows >= seq
    d = x_ref.shape[1]
    rsum = (rf_ref[...] + rb_ref[...]).reshape(tm, D_RNN)
    ya = (rsum * gg_ref[...].astype(F32)).astype(BF16)
    hsum = hf_ref[...] + hb_ref[...]
    parts = []
    for hd in range(ML_HEADS):
        hh = hsum[:, hd * ML_DH:(hd + 1) * ML_DH]
        parts.append(hh * lax.rsqrt(jnp.mean(hh * hh, axis=-1, keepdims=True) + EPS))
    yb = (jnp.concatenate(parts, axis=1) * ng_ref[...] * mo_ref[...].astype(F32)).astype(BF16)
    m = (br_ref[:, 0:d].astype(F32) * jnp.dot(ya, wb_ref[0], preferred_element_type=F32)
         + br_ref[:, d:2 * d].astype(F32) * jnp.dot(yb, wb_ref[1], preferred_element_type=F32)
         + br_ref[:, 2 * d:3 * d].astype(F32) * jnp.dot(yc_ref[...], wb_ref[2],
                                                       preferred_element_type=F32))
    y = jnp.dot(m.astype(BF16), wo_ref[...], preferred_element_type=F32)
    x1 = x_ref[...] + _row_mod(ml_ref, mc_ref, is_ctx, 5) * y
    o_ref[...] = _ffn_body(x1, ml_ref, mc_ref, is_ctx, g_ref, w1b, w3b, w2b, act_ref,
                           base=6, ck=ck)


def _merge_call(x, ml, mc, rf, rb, gg, hf, hb, mo, yc, br, ng, wb, wo, g, w1, w3, w2,
                *, layer, seq, out_rows):
    bsz, t, d = x.shape
    dff = w1.shape[-1]
    tm = _pick_tile(t, (384, 128))
    ck = _pick_tile(dff, (256, 128))
    kern = functools.partial(_merge_kernel, lead=(layer, 1), seq=seq, tm=tm, ck=ck)
    hbm = pl.BlockSpec(memory_space=pl.ANY)

    def tok(wd):
        return pl.BlockSpec((None, tm, wd), lambda b, i: (b, i, 0))

    rspec = pl.BlockSpec((tm // SUBLANES, None, SUBLANES, D_RNN), lambda b, i: (i, b, 0, 0))

    return pl.pallas_call(
        kern,
        grid=(bsz, pl.cdiv(out_rows, tm)),
        in_specs=[
            tok(d),
            pl.BlockSpec((None, N_MOD, d), lambda b, i: (b, 0, 0)),
            _const_spec((N_MOD, d)),
            rspec, rspec, tok(D_RNN), tok(ML_W), tok(ML_W), tok(ML_W), tok(AT_W),
            tok(N_BRANCH * d),
            _const_spec((1, ML_W)),
            _const_spec((N_BRANCH, BRANCH_W, d), (layer,)),
            _const_spec((d, d), (layer,)),
            _const_spec((1, d)),
            hbm, hbm, hbm,
        ],
        out_specs=tok(d),
        out_shape=jax.ShapeDtypeStruct((bsz, out_rows, d), F32),
        scratch_shapes=[pltpu.VMEM((tm, dff), BF16)] + _ffn_weight_scratch(d, dff, ck),
        compiler_params=_cparams(("arbitrary", "arbitrary")),
        name="merge_ffn",
    )(x, ml, mc, rf, rb, gg, hf, hb, mo, yc, br, ng, wb, wo, g, w1, w3, w2)


def _block_diag(w):
    n, bi, bj = w.shape
    eye = jnp.eye(n, dtype=w.dtype)
    return (eye[:, None, :, None] * w[:, :, None, :]).reshape(n * bi, n * bj)


def _rope_tables(seq, t):
    rows = seq // GRID_W
    row = jnp.repeat(jnp.arange(rows), GRID_W).astype(F32)
    col = jnp.broadcast_to(jnp.arange(GRID_W), (rows, GRID_W)).reshape(-1).astype(F32)
    half = AT_DH // 2
    inv = ROPE_BASE ** (-jnp.arange(0, half, 2, dtype=F32) / half)
    ar = row[:, None] * inv
    ac = col[:, None] * inv
    ang = jnp.concatenate([ar, ar, ac, ac], axis=-1)
    cos = jnp.concatenate([jnp.cos(ang), jnp.ones((t - seq, AT_DH), F32)], axis=0)
    sin = jnp.concatenate([jnp.sin(ang), jnp.zeros((t - seq, AT_DH), F32)], axis=0)
    cos = jnp.tile(cos, (1, LANES // AT_DH))
    sin = jnp.tile(sin, (1, LANES // AT_DH))
    first_half = (jnp.arange(LANES) % 32) < 16
    sin_a = jnp.where(first_half, -sin, 0.0)
    sin_b = jnp.where(first_half, 0.0, sin)
    return cos, sin_a, sin_b


def kernel(x, c, ctx, c_ctx, ada_w, ada_b, norm_g, ffn_w1, ffn_w3, ffn_w2, w_in, rg_conv_w,
           rg_conv_b, rg_wa, rg_ba, rg_wi, rg_bi, rg_lam, ml_gate_b, ml_norm_g, at_qn_g,
           at_kn_g, at_sink, w_branch, w_out):
    bsz, seq, d = x.shape
    nctx = ctx.shape[1]
    t = seq + nctx
    depth = ada_w.shape[0]

    xs = jnp.concatenate([x, ctx], axis=1)
    mod_rows = 2 * SUBLANES
    cc = jnp.concatenate([c, c_ctx[None, :], jnp.zeros((mod_rows - bsz - 1, d), F32)], axis=0)
    mods = _ada_call(cc, ada_w, ada_b).reshape(depth, mod_rows, N_MOD, d)
    cos, sin_a, sin_b = _rope_tables(seq, t)
    gmat = _block_diag(jnp.full((AT_HEADS, AT_DH, AT_DH), 1.0 / AT_DH, F32)).astype(BF16)
    w1, w3, w2 = ffn_w1, ffn_w3, ffn_w2
    w_head, w_tail = w_in[..., :P_HEAD].astype(BF16), w_in[..., P_HEAD:].astype(BF16)
    w_br, w_o = w_branch.astype(BF16), w_out.astype(BF16)

    for l in range(depth):
        ml = mods[l, :bsz]
        mc = mods[l, bsz]
        last = l == depth - 1

        xs = _ffn_call(xs, ml, mc, norm_g[l, 0][None, :], w1, w3, w2,
                       lead=(l, 0), base=0, seq=seq, out_rows=t)

        qg = jnp.tile(at_qn_g[l], AT_HEADS)[None, :]
        kg = jnp.tile(at_kn_g[l], AT_KV)[None, :]
        gb = jnp.concatenate([ml_gate_b[l], jnp.zeros((LANES - N_GATE,), F32)])[None, :]
        (rgx, rgg, mq, mk, mv, mo, aq, akd, avd, br, mg) = _proj_call(
            xs, ml, mc, norm_g[l, 1][None, :], w_head, w_tail, gmat, qg, kg, gb,
            cos, sin_a, sin_b, layer=l, seq=seq)

        wcat = jnp.stack([jnp.concatenate([_block_diag(rg_wa[l, dr]), _block_diag(rg_wi[l, dr])],
                                          axis=1) for dr in range(2)])
        wcat = (0.5 * wcat).astype(BF16)
        bias = 0.5 * jnp.concatenate([rg_ba[l], rg_bi[l]], axis=1)[:, None, :]
        clam = (-0.5 * LOG2E * LRU_C * jax.nn.softplus(-rg_lam[l]))[:, None, :]
        rhf, rhb = _rglru_call(rgx, rg_conv_w[l], rg_conv_b[l][None, :], wcat, bias, clam,
                               seq=seq)

        gt = jnp.swapaxes(mg[:, :, :N_GATE], 1, 2)
        rowp, colsrc = _mlgate_call(gt, seq=seq)
        colp = jnp.swapaxes(colsrc, 1, 2)
        mkt = jnp.swapaxes(mk, 1, 2)
        mhf, mhb = _mlstm_call(mq, mkt, mv, rowp, colp, seq=seq)

        yc = _attn_call(at_sink[l], aq, akd, avd, seq=seq)

        xs = _merge_call(xs, ml, mc, rhf, rhb, rgg, mhf, mhb, mo, yc, br, ml_norm_g[l][None, :],
                         w_br, w_o, norm_g[l, 2][None, :], w1, w3, w2,
                         layer=l, seq=seq, out_rows=seq if last else t)
    return xs
```

```python
import functools

import jax
import jax.numpy as jnp
from jax import lax
from jax.experimental import pallas as pl
from jax.experimental.pallas import tpu as pltpu

F32 = jnp.float32
BF16 = jnp.bfloat16

EPS = 1e-6
NEG = -1e30
LOG2E = 1.4426950408889634
N_MOD = 9
GRID_W = 64
ROPE_BASE = 10000.0

D_RNN = 512
RNN_BLOCKS = 8
RNN_BLOCK = D_RNN // RNN_BLOCKS
CONV_W = 4
CONV_LEFT = 2
CONV_RIGHT = CONV_W - 1 - CONV_LEFT
LRU_C = 8.0

ML_HEADS = 4
ML_DH = 128
ML_W = ML_HEADS * ML_DH
ML_CHUNK = 128
ML_UNITS = 2 * ML_HEADS

AT_HEADS = 8
AT_KV = 2
AT_DH = 64
AT_G = AT_HEADS // AT_KV
AT_W = AT_HEADS * AT_DH
AT_KVW = AT_KV * AT_DH
ATT_BLOCK = 128

N_BRANCH = 3
BRANCH_W = 512

LANES = 128
SUBLANES = 8
VMEM_LIMIT = 56 * 1024 * 1024

P_RGX = 0
P_RGG = P_RGX + D_RNN
P_MQ = P_RGG + D_RNN
P_MK = P_MQ + ML_W
P_MV = P_MK + ML_W
P_MO = P_MV + ML_W
P_HEAD = P_MO + ML_W
N_GATE = 4 * ML_HEADS
T_AQ = 0
T_AK = T_AQ + AT_W
T_AV = T_AK + AT_KVW
T_BR = T_AV + AT_KVW


def _cparams(sem):
    return pltpu.CompilerParams(dimension_semantics=sem, vmem_limit_bytes=VMEM_LIMIT)


def _const_spec(shape, lead=()):
    nd = len(shape)
    idx = tuple(lead) + (0,) * nd
    return pl.BlockSpec((None,) * len(lead) + tuple(shape), lambda *_: idx,
                        pipeline_mode=pl.Buffered(1))


def _pick_tile(total, candidates):
    for c in candidates:
        if total % c == 0:
            return c
    raise ValueError(f"no tile for {total}")


def _sigmoid(x):
    return jax.nn.sigmoid(x)


def _ln_mod(x, g, shift, scale):
    ms = jnp.mean(x * x, axis=-1, keepdims=True)
    return x * lax.rsqrt(ms + EPS) * g * (1.0 + scale) + shift


def _row_mod(ml_ref, mc_ref, is_ctx, i):
    return jnp.where(is_ctx, mc_ref[i:i + 1, :], ml_ref[i:i + 1, :])


def _ada_kernel(cc_ref, w_ref, b_ref, o_ref):
    cc = cc_ref[...]
    s = cc * _sigmoid(cc)
    o_ref[...] = jnp.dot(s.astype(BF16), w_ref[...].astype(BF16),
                         preferred_element_type=F32) + b_ref[...]


def _ada_call(cc, ada_w, ada_b):
    depth, d, nout = ada_w.shape
    rows = cc.shape[0]
    tn = _pick_tile(nout, (1536, 1024, 512, 256, 128))
    return pl.pallas_call(
        _ada_kernel,
        grid=(depth, nout // tn),
        in_specs=[
            pl.BlockSpec((rows, d), lambda l, j: (0, 0)),
            pl.BlockSpec((None, d, tn), lambda l, j: (l, 0, j)),
            pl.BlockSpec((None, 1, tn), lambda l, j: (l, 0, j)),
        ],
        out_specs=pl.BlockSpec((None, rows, tn), lambda l, j: (l, 0, j)),
        out_shape=jax.ShapeDtypeStruct((depth, rows, nout), F32),
        compiler_params=_cparams(("arbitrary", "arbitrary")),
        name="ada_mod",
    )(cc, ada_w, ada_b.reshape(depth, 1, nout))


def _ffn_body(x, ml_ref, mc_ref, is_ctx, g_ref, w1_ref, w3_ref, w2_ref, act_ref, *, base, ck):
    h = _ln_mod(x, g_ref[...], _row_mod(ml_ref, mc_ref, is_ctx, base),
                _row_mod(ml_ref, mc_ref, is_ctx, base + 1)).astype(BF16)
    dff = w1_ref.shape[1]
    for j in range(dff // ck):
        a = jnp.dot(h, w1_ref[:, j * ck:(j + 1) * ck], preferred_element_type=F32)
        b = jnp.dot(h, w3_ref[:, j * ck:(j + 1) * ck], preferred_element_type=F32)
        act_ref[:, j * ck:(j + 1) * ck] = (a * _sigmoid(a) * b).astype(BF16)
    y = jnp.dot(act_ref[...], w2_ref[...], preferred_element_type=F32)
    return x + 0.5 * _row_mod(ml_ref, mc_ref, is_ctx, base + 2) * y


W13_ROWS = 128


def _ffn_weight_scratch(d, dff, ck):
    return [pltpu.VMEM((d, dff), BF16), pltpu.VMEM((d, dff), BF16), pltpu.VMEM((dff, d), BF16),
            pltpu.VMEM((2, W13_ROWS, dff), F32), pltpu.VMEM((2, ck, d), F32),
            pltpu.SemaphoreType.DMA((2,))]


def _load_ffn_weights(w1_hbm, w3_hbm, w2_hbm, lead, w1b, w3b, w2b, stg_c, stg_r, sem, ck):
    d, dff = w1b.shape
    jobs = []
    for src, dst in ((w1_hbm, w1b), (w3_hbm, w3b)):
        for j in range(d // W13_ROWS):
            rws = slice(j * W13_ROWS, (j + 1) * W13_ROWS)
            jobs.append((src.at[lead[0], lead[1], rws, :], stg_c, dst.at[rws, :]))
    for j in range(dff // ck):
        rws = slice(j * ck, (j + 1) * ck)
        jobs.append((w2_hbm.at[lead[0], lead[1], rws, :], stg_r, w2b.at[rws, :]))
    copies = [pltpu.make_async_copy(src, stg.at[k % 2], sem.at[k % 2])
              for k, (src, stg, _) in enumerate(jobs)]
    copies[0].start()
    for k, (_, stg, dst) in enumerate(jobs):
        if k + 1 < len(jobs):
            copies[k + 1].start()
        copies[k].wait()
        dst[...] = stg[k % 2].astype(BF16)


def _ffn_kernel(x_ref, ml_ref, mc_ref, g_ref, w1_hbm, w3_hbm, w2_hbm, o_ref, act_ref,
                w1b, w3b, w2b, stg_c, stg_r, sem, *, lead, base, seq, tm, ck):
    @pl.when(jnp.logical_and(pl.program_id(0) == 0, pl.program_id(1) == 0))
    def _():
        _load_ffn_weights(w1_hbm, w3_hbm, w2_hbm, lead, w1b, w3b, w2b, stg_c, stg_r, sem, ck)

    t0 = pl.program_id(1) * tm
    rows = t0 + lax.broadcasted_iota(jnp.int32, (tm, 1), 0)
    is_ctx = rows >= seq
    o_ref[...] = _ffn_body(x_ref[...], ml_ref, mc_ref, is_ctx, g_ref, w1b, w3b, w2b,
                           act_ref, base=base, ck=ck)


def _ffn_call(x, ml, mc, g, w1, w3, w2, *, lead, base, seq, out_rows):
    bsz, t, d = x.shape
    dff = w1.shape[-1]
    tm = _pick_tile(t, (768, 384, 128))
    ck = _pick_tile(dff, (256, 128))
    kern = functools.partial(_ffn_kernel, lead=lead, base=base, seq=seq, tm=tm, ck=ck)
    hbm = pl.BlockSpec(memory_space=pl.ANY)
    return pl.pallas_call(
        kern,
        grid=(bsz, pl.cdiv(out_rows, tm)),
        in_specs=[
            pl.BlockSpec((None, tm, d), lambda b, i: (b, i, 0)),
            pl.BlockSpec((None, N_MOD, d), lambda b, i: (b, 0, 0)),
            _const_spec((N_MOD, d)),
            _const_spec((1, d)),
            hbm, hbm, hbm,
        ],
        out_specs=pl.BlockSpec((None, tm, d), lambda b, i: (b, i, 0)),
        out_shape=jax.ShapeDtypeStruct((bsz, out_rows, d), F32),
        scratch_shapes=[pltpu.VMEM((tm, dff), BF16)] + _ffn_weight_scratch(d, dff, ck),
        compiler_params=_cparams(("arbitrary", "arbitrary")),
        name="ffn",
    )(x, ml, mc, g, w1, w3, w2)


def _gelu_tanh(x):
    return 0.5 * x * (1.0 + jnp.tanh(0.7978845608028654 * (x + 0.044715 * (x * x * x))))


def _head_norm(x, gmat, g):
    ms = jnp.dot((x * x).astype(BF16), gmat, preferred_element_type=F32)
    return x * lax.rsqrt(ms + EPS) * g


def _dup_heads(x):
    low = lax.broadcasted_iota(jnp.int32, x.shape, 1) < AT_DH
    sw = pltpu.roll(x, AT_DH, 1)
    return jnp.concatenate([jnp.where(low, x, sw), jnp.where(low, sw, x)], axis=1)


def _rope(x, cos, sin_a, sin_b):
    parts = []
    for j in range(x.shape[1] // LANES):
        xj = x[:, j * LANES:(j + 1) * LANES]
        parts.append(xj * cos + pltpu.roll(xj, LANES - 16, 1) * sin_a
                     + pltpu.roll(xj, 16, 1) * sin_b)
    return jnp.concatenate(parts, axis=1)


def _proj_kernel(x_ref, ml_ref, mc_ref, g_ref, wh_ref, wt_ref, gmat_ref, qg_ref, kg_ref, gb_ref,
                 cos_ref, sa_ref, sb_ref,
                 rgx_o, rgg_o, mq_o, mk_o, mv_o, mo_o, aq_o, ak_o, av_o, br_o, mg_o,
                 tail, *, seq, tm, d_model):
    @pl.when(jnp.logical_and(pl.program_id(0) == 0, pl.program_id(1) == 0))
    def _():
        for r in range(0, d_model, LANES):
            tail[r:r + LANES, :] = wt_ref[r:r + LANES, N_GATE:]

    t0 = pl.program_id(1) * tm
    rows = t0 + lax.broadcasted_iota(jnp.int32, (tm, 1), 0)
    is_ctx = rows >= seq
    h = _ln_mod(x_ref[...], g_ref[...], _row_mod(ml_ref, mc_ref, is_ctx, 3),
                _row_mod(ml_ref, mc_ref, is_ctx, 4)).astype(BF16)

    def seg(c0, w, ref=wh_ref):
        return jnp.dot(h, ref[:, c0:c0 + w], preferred_element_type=F32)

    rgx_o[...] = seg(P_RGX, D_RNN).reshape(rgx_o.shape)
    rgg_o[...] = _gelu_tanh(seg(P_RGG, D_RNN)).astype(BF16)
    mq_o[...] = seg(P_MQ, ML_W).astype(BF16)
    mk_o[...] = (seg(P_MK, ML_W) * (ML_DH ** -0.5)).astype(BF16)
    mv_o[...] = seg(P_MV, ML_W).astype(BF16)
    mo_o[...] = _sigmoid(seg(P_MO, ML_W)).astype(BF16)
    cos, sin_a, sin_b = cos_ref[...], sa_ref[...], sb_ref[...]
    gmat = gmat_ref[...]
    q = _head_norm(seg(T_AQ, AT_W, tail), gmat, qg_ref[...])
    aq_o[...] = (_rope(q, cos, sin_a, sin_b) * (AT_DH ** -0.5 * LOG2E)).astype(BF16)
    k = _head_norm(seg(T_AK, AT_KVW, tail), gmat[:AT_KVW, :AT_KVW], kg_ref[...])
    ak_o[...] = _dup_heads(_rope(k, cos, sin_a, sin_b)).astype(BF16)
    av_o[...] = _dup_heads(seg(T_AV, AT_KVW, tail)).astype(BF16)
    for c0 in range(0, N_BRANCH * d_model, 512):
        br_o[:, c0:c0 + 512] = _sigmoid(seg(T_BR + c0, 512, tail)).astype(BF16)
    mg_o[...] = seg(0, LANES, wt_ref) + gb_ref[...]


def _proj_call(x, ml, mc, g, wh, wt, gmat, qg, kg, gb, cos, sin_a, sin_b, *, layer, seq):
    bsz, t, d = x.shape
    ntail = wt.shape[-1] - N_GATE
    tm = _pick_tile(t, (384, 128))
    kern = functools.partial(_proj_kernel, seq=seq, tm=tm, d_model=d)
    widths = [(D_RNN, F32), (D_RNN, BF16), (ML_W, BF16), (ML_W, BF16), (ML_W, BF16),
              (ML_W, BF16), (AT_W, BF16), (2 * AT_KVW, BF16), (2 * AT_KVW, BF16),
              (N_BRANCH * d, BF16), (LANES, F32)]
    tab_spec = pl.BlockSpec((tm, LANES), lambda b, i: (i, 0))
    out_specs = [pl.BlockSpec((None, tm, wd), lambda b, i: (b, i, 0)) for wd, _ in widths]
    out_shape = [jax.ShapeDtypeStruct((bsz, t, wd), dt) for wd, dt in widths]
    out_specs[0] = pl.BlockSpec((tm // SUBLANES, None, SUBLANES, D_RNN), lambda b, i: (i, b, 0, 0))
    out_shape[0] = jax.ShapeDtypeStruct((t // SUBLANES, bsz, SUBLANES, D_RNN), F32)
    return pl.pallas_call(
        kern,
        grid=(bsz, t // tm),
        in_specs=[
            pl.BlockSpec((None, tm, d), lambda b, i: (b, i, 0)),
            pl.BlockSpec((None, N_MOD, d), lambda b, i: (b, 0, 0)),
            _const_spec((N_MOD, d)),
            _const_spec((1, d)),
            _const_spec((d, wh.shape[-1]), (layer,)),
            _const_spec((d, wt.shape[-1]), (layer,)),
            _const_spec((AT_W, AT_W)),
            _const_spec((1, AT_W)),
            _const_spec((1, AT_KVW)),
            _const_spec((1, LANES)),
            tab_spec, tab_spec, tab_spec,
        ],
        out_specs=out_specs,
        out_shape=out_shape,
        scratch_shapes=[pltpu.VMEM((d, ntail), BF16)],
        compiler_params=_cparams(("arbitrary", "arbitrary")),
        name="in_proj",
    )(x, ml, mc, g, wh, wt, gmat, qg, kg, gb, cos, sin_a, sin_b)


def _scan_chunk(s, rev, nl, nc):
    if rev:
        return jnp.where(s < nc, nl + nc - 1 - s, nl - 1 - (s - nc))
    return jnp.where(s < nc, nl + s, s - nc)


def _rglru_kernel(xf_ref, xfp_ref, xfn_ref, xb_ref, xbp_ref, xbn_ref, cw_ref, cb_ref, w_ref,
                  bias_ref, clam_ref, hf_ref, hb_ref, ext, a_s, b_s, h_s, hstate,
                  *, nl, nc, tc, sub):
    s = pl.program_id(0)
    bsz, c = hstate.shape[1], hstate.shape[2]
    ntb = tc // SUBLANES

    @pl.when(s == 0)
    def _():
        hstate[...] = jnp.zeros_like(hstate)

    cw = cw_ref[...]
    cb = cb_ref[...]
    for d, (x_ref, xp_ref, xn_ref) in enumerate(((xf_ref, xfp_ref, xfn_ref),
                                                (xb_ref, xbp_ref, xbn_ref))):
        chunk = _scan_chunk(s, d == 1, nl, nc)
        first = jnp.logical_or(chunk == 0, chunk == nl)
        last = jnp.logical_or(chunk == nl - 1, chunk == nl + nc - 1)
        prev = jnp.swapaxes(xp_ref[0], 0, 1)[SUBLANES - CONV_LEFT:]
        nxt = jnp.swapaxes(xn_ref[0], 0, 1)[:CONV_RIGHT]
        ext[d, 0:CONV_LEFT] = jnp.where(first, 0.0, prev)
        for i in range(ntb):
            o = CONV_LEFT + i * SUBLANES
            ext[d, o:o + SUBLANES] = jnp.swapaxes(x_ref[i], 0, 1)
        ext[d, CONV_LEFT + tc:CONV_LEFT + tc + CONV_RIGHT] = jnp.where(last, 0.0, nxt)

        def gates(i, carry, d=d):
            t0 = pl.multiple_of(i * sub, sub)
            u = cb
            for k in range(CONV_W):
                u = u + ext[d, pl.ds(t0 + k, sub)] * cw[k:k + 1, :]
            u2 = u.reshape(sub * bsz, c)
            z = jnp.dot(u2.astype(BF16), w_ref[d], preferred_element_type=F32) + bias_ref[d]
            a = jnp.exp2(clam_ref[d] * jnp.tanh(z[:, :c]) + clam_ref[d])
            gi = 0.5 * jnp.tanh(z[:, c:]) + 0.5
            s1 = 1.0 - a * a
            bb = (s1 * lax.rsqrt(jnp.maximum(s1, 1e-36))) * (gi * u2)
            a_s[d, pl.ds(t0, sub)] = a.reshape(sub, bsz, c)
            b_s[d, pl.ds(t0, sub)] = bb.reshape(sub, bsz, c)
            return carry

        lax.fori_loop(0, tc // sub, gates, 0)

    def step(k, hs):
        hf, hb = hs
        kb = tc - 1 - k
        hf = a_s[0, k] * hf + b_s[0, k]
        hb = a_s[1, kb] * hb + b_s[1, kb]
        h_s[0, k] = hf
        h_s[1, kb] = hb
        return hf, hb

    hf, hb = lax.fori_loop(0, tc, step, (hstate[0], hstate[1]), unroll=8)
    hstate[0] = hf
    hstate[1] = hb
    for d, o_ref in enumerate((hf_ref, hb_ref)):
        for i in range(ntb):
            o_ref[i] = jnp.swapaxes(h_s[d, i * SUBLANES:(i + 1) * SUBLANES], 0, 1)


def _rglru_call(rgx, cw, cb, w, bias, clam, *, seq):
    ntb_all, bsz, _, c = rgx.shape
    t = ntb_all * SUBLANES
    tc = 128
    sub = 32
    assert seq % tc == 0 and (t - seq) % tc == 0
    nl, nc = seq // tc, (t - seq) // tc
    ntb = tc // SUBLANES

    def specs(rev):
        chunk = functools.partial(_scan_chunk, rev=rev, nl=nl, nc=nc)
        return [
            pl.BlockSpec((ntb, bsz, SUBLANES, c), lambda s: (chunk(s), 0, 0, 0)),
            pl.BlockSpec((1, bsz, SUBLANES, c),
                         lambda s: (jnp.maximum(chunk(s) * ntb - 1, 0), 0, 0, 0)),
            pl.BlockSpec((1, bsz, SUBLANES, c),
                         lambda s: (jnp.minimum((chunk(s) + 1) * ntb, ntb_all - 1), 0, 0, 0)),
        ]

    kern = functools.partial(_rglru_kernel, nl=nl, nc=nc, tc=tc, sub=sub)
    fspec, bspec = specs(False), specs(True)
    return pl.pallas_call(
        kern,
        grid=(nl + nc,),
        in_specs=fspec + bspec + [
            _const_spec((CONV_W, c)),
            _const_spec((1, c)),
            _const_spec((2, c, 2 * c)),
            _const_spec((2, 1, 2 * c)),
            _const_spec((2, 1, c)),
        ],
        out_specs=[fspec[0], bspec[0]],
        out_shape=[jax.ShapeDtypeStruct(rgx.shape, F32)] * 2,
        scratch_shapes=[
            pltpu.VMEM((2, tc + CONV_W - 1, bsz, c), F32),
            pltpu.VMEM((2, tc, bsz, c), F32),
            pltpu.VMEM((2, tc, bsz, c), F32),
            pltpu.VMEM((2, tc, bsz, c), F32),
            pltpu.VMEM((2, bsz, c), F32),
        ],
        compiler_params=_cparams(("arbitrary",)),
        name="rglru",
    )(rgx, rgx, rgx, rgx, rgx, rgx, cw, cb, w, bias, clam)


def _log_sigmoid(x):
    return -(jnp.maximum(-x, 0.0) + jnp.log1p(jnp.exp(-jnp.abs(x))))


def _lane_scan(x, op, ident, rev):
    n = x.shape[1]
    lane = lax.broadcasted_iota(jnp.int32, x.shape, 1)
    sh = 1
    while sh < n:
        if rev:
            x = op(x, jnp.where(lane < n - sh, pltpu.roll(x, n - sh, 1), ident))
        else:
            x = op(x, jnp.where(lane >= sh, pltpu.roll(x, sh, 1), ident))
        sh *= 2
    return x


def _mlgate_kernel(g_ref, row_o, col_o, cb_s, gg_s, ml_s, tot_s, wm_s, mo_s, mn_s, *, nl, nc):
    L = ML_CHUNK
    H = ML_HEADS
    U = ML_UNITS
    is_bwd = lax.broadcasted_iota(jnp.int32, (U, L), 0) >= H
    for c in range(nl + nc):
        lanes = slice(c * L, (c + 1) * L)
        g16 = g_ref[:, lanes]
        li = jnp.concatenate([g16[0:H], g16[2 * H:3 * H]], axis=0)
        lf = _log_sigmoid(jnp.concatenate([g16[H:2 * H], g16[3 * H:4 * H]], axis=0))
        pre = _lane_scan(lf, jnp.add, 0.0, False)
        suf = _lane_scan(lf, jnp.add, 0.0, True)
        tot = pre + suf - lf
        cb = jnp.where(is_bwd, suf, pre)
        gg = li - cb
        pmax = _lane_scan(gg, jnp.maximum, NEG, False)
        smax = _lane_scan(gg, jnp.maximum, NEG, True)
        cb_s[:, lanes] = cb
        gg_s[:, lanes] = gg
        ml_s[:, lanes] = cb + jnp.where(is_bwd, smax, pmax)
        tot_s[:, lanes] = tot
        wm_s[:, lanes] = tot + jnp.maximum(pmax, smax)

    m = jnp.zeros((U, L), F32)
    for s in range(nl + nc):
        cf = nl + s if s < nc else s - nc
        cr = nl + nc - 1 - s if s < nc else nl - 1 - (s - nc)
        lf_, lr_ = slice(cf * L, (cf + 1) * L), slice(cr * L, (cr + 1) * L)
        tot = jnp.where(is_bwd, tot_s[:, lr_], tot_s[:, lf_])
        wm = jnp.where(is_bwd, wm_s[:, lr_], wm_s[:, lf_])
        m_new = jnp.maximum(tot + m, wm)
        mo_s[0:H, lf_] = m[0:H]
        mo_s[H:U, lr_] = m[H:U]
        mn_s[0:H, lf_] = m_new[0:H]
        mn_s[H:U, lr_] = m_new[H:U]
        m = m_new

    cb = cb_s[...]
    gg = gg_s[...]
    tot = tot_s[...]
    mo = mo_s[...]
    mn = mn_s[...]
    m_t = jnp.maximum(cb + mo, ml_s[...])
    row_o[0:U, :] = gg
    row_o[U:2 * U, :] = jnp.exp(tot + gg - mn)
    row_o[2 * U:3 * U, :] = jnp.exp(tot + mo - mn)
    row_o[3 * U:4 * U, :] = mo
    col_o[0:U, :] = cb - m_t
    col_o[U:2 * U, :] = jnp.exp(-m_t)


def _mlgate_call(gt, *, seq):
    bsz, ng, t = gt.shape
    nl, nc = seq // ML_CHUNK, (t - seq) // ML_CHUNK
    kern = functools.partial(_mlgate_kernel, nl=nl, nc=nc)
    rout = pl.BlockSpec((None, 4 * ML_UNITS, t), lambda b: (b, 0, 0))
    cout = pl.BlockSpec((None, 2 * ML_UNITS, t), lambda b: (b, 0, 0))
    return pl.pallas_call(
        kern,
        grid=(bsz,),
        in_specs=[pl.BlockSpec((None, ng, t), lambda b: (b, 0, 0))],
        out_specs=[rout, cout],
        out_shape=[jax.ShapeDtypeStruct((bsz, 4 * ML_UNITS, t), F32),
                   jax.ShapeDtypeStruct((bsz, 2 * ML_UNITS, t), F32)],
        scratch_shapes=[pltpu.VMEM((ML_UNITS, t), F32)] * 7,
        compiler_params=_cparams(("arbitrary",)),
        name="mlstm_gates",
    )(gt)


def _mlstm_kernel(qf_ref, ktf_ref, vf_ref, rf_ref, cf_ref, qb_ref, ktb_ref, vb_ref, rb_ref, cb_ref,
                  hf_ref, hb_ref, c_st):
    s = pl.program_id(1)

    @pl.when(s == 0)
    def _():
        c_st[...] = jnp.zeros_like(c_st)

    L, dh, U = ML_CHUNK, ML_DH, ML_UNITS
    ti = lax.broadcasted_iota(jnp.int32, (L, L), 0)
    si = lax.broadcasted_iota(jnp.int32, (L, L), 1)
    ones = jnp.ones((L, dh), BF16)
    for bb in range(qf_ref.shape[0]):
        for d, (q_ref, kt_ref, v_ref, r_ref, c_ref, o_ref) in enumerate(
                ((qf_ref, ktf_ref, vf_ref, rf_ref, cf_ref, hf_ref),
                 (qb_ref, ktb_ref, vb_ref, rb_ref, cb_ref, hb_ref))):
            tri = (si >= ti) if d == 1 else (si <= ti)
            rows = r_ref[bb]
            cols = c_ref[bb]
            for hd in range(ML_HEADS):
                u = d * ML_HEADS + hd
                sl = slice(hd * dh, (hd + 1) * dh)
                q = q_ref[bb, :, sl]
                kt = kt_ref[bb, sl, :]
                v_ext = jnp.concatenate([v_ref[bb, :, sl], ones], axis=1)
                g_row = rows[u:u + 1, :]
                w_row = rows[U + u:U + u + 1, :]
                a_row = rows[2 * U + u:2 * U + u + 1, :]
                m_row = rows[3 * U + u:3 * U + u + 1, :]
                xb = jnp.broadcast_to(cols[:, u:u + 1], (L, L))
                emt = cols[:, U + u:U + u + 1]
                c_old = c_st[bb * U + u]
                qkc = jnp.dot(q, jnp.concatenate([kt, c_old.astype(BF16)], axis=1),
                              preferred_element_type=F32)
                dm = jnp.exp(jnp.where(tri, xb + g_row, NEG))
                dec = jnp.exp(xb + m_row)
                sc = (qkc[:, :L] * dm).astype(BF16)
                ktw = (kt.astype(F32) * w_row).astype(BF16)
                sv = jnp.dot(jnp.concatenate([sc, ktw], axis=0), v_ext,
                             preferred_element_type=F32)
                ab = sv[:L] + jnp.concatenate([dec, dec], axis=1) * qkc[:, L:]
                o_ref[bb, :, sl] = ab[:, :dh] / jnp.maximum(jnp.abs(ab[:, dh:]), emt)
                c_st[bb * U + u] = jnp.concatenate([a_row, a_row], axis=1) * c_old + sv[L:]


def _mlstm_call(mq, mkt, mv, rowp, colp, *, seq):
    bsz, t, w = mq.shape
    nl, nc = seq // ML_CHUNK, (t - seq) // ML_CHUNK
    cf = functools.partial(_scan_chunk, rev=False, nl=nl, nc=nc)
    cr = functools.partial(_scan_chunk, rev=True, nl=nl, nc=nc)
    nr = rowp.shape[1]
    bpb = 8 if bsz % 8 == 0 else 1

    def tok(ch, wd):
        return pl.BlockSpec((bpb, ML_CHUNK, wd), lambda b, s: (b, ch(s), 0))

    def tr(ch, rows):
        return pl.BlockSpec((bpb, rows, ML_CHUNK), lambda b, s: (b, 0, ch(s)))

    def side(ch):
        return [tok(ch, w), tr(ch, w), tok(ch, w), tr(ch, nr), tok(ch, colp.shape[2])]

    return pl.pallas_call(
        _mlstm_kernel,
        grid=(bsz // bpb, nl + nc),
        in_specs=side(cf) + side(cr),
        out_specs=[tok(cf, w), tok(cr, w)],
        out_shape=[jax.ShapeDtypeStruct((bsz, t, w), F32)] * 2,
        scratch_shapes=[pltpu.VMEM((bpb * ML_UNITS, ML_DH, 2 * ML_DH), F32)],
        compiler_params=_cparams(("arbitrary", "arbitrary")),
        name="mlstm",
    )(mq, mkt, mv, rowp, colp, mq, mkt, mv, rowp, colp)


def _attn_kernel(sink_ref, q_ref, k_ref, v_ref, o_ref, vext, *, seq, qbs):
    kv = pl.program_id(1)
    i = pl.program_id(2)
    blk = ATT_BLOCK
    t = k_ref.shape[0]
    ctx = t - seq
    nlb = seq // blk
    band = 3 * blk

    @pl.when(i == 0)
    def _():
        vext[:, 0:LANES] = v_ref[...]
        vext[:, LANES:2 * LANES] = jnp.ones((t, LANES), BF16)

    lane = lax.broadcasted_iota(jnp.int32, (blk, LANES), 1)
    low = lane < AT_DH
    zero = jnp.zeros((blk, LANES), BF16)
    rows = AT_G * blk
    gi = lax.broadcasted_iota(jnp.int32, (rows, 1), 0) // blk
    snk = jnp.zeros((rows, 1), F32)
    for g in range(AT_G):
        snk = jnp.where(gi == g, sink_ref[kv * AT_G + g] * LOG2E, snk)
    ti = lax.broadcasted_iota(jnp.int32, (blk, band), 0)
    ci = lax.broadcasted_iota(jnp.int32, (blk, band), 1)
    nt = (((1,), (1,)), ((), ()))
    k_ctx = k_ref[seq:t, :]
    v_ctx = vext[seq:t, :]

    for j in range(qbs):
        qb = i * qbs + j
        s0 = pl.multiple_of(jnp.clip((qb - 1) * blk, 0, t - band), blk)
        rel = ci - ti + (s0 - qb * blk + blk)
        lim = jnp.where(qb < nlb, seq - s0, 0)
        ok = jnp.logical_and(jnp.logical_and(rel >= 0, rel <= 2 * blk), ci < lim)
        bias = jnp.where(ok, 0.0, NEG)
        bias = jnp.concatenate([bias] * AT_G, axis=0)

        qs = []
        for h2 in range(AT_G // 2):
            qj = q_ref[j * blk:(j + 1) * blk, h2 * LANES:(h2 + 1) * LANES]
            qs += [jnp.where(low, qj, zero), jnp.where(low, zero, qj)]
        q_all = jnp.concatenate(qs, axis=0)
        s_band = lax.dot_general(q_all, k_ref[pl.ds(s0, band), :], nt,
                                 preferred_element_type=F32) + bias
        s_ctx = lax.dot_general(q_all, k_ctx, nt, preferred_element_type=F32)
        m = jnp.maximum(jnp.maximum(jnp.max(s_band, axis=1, keepdims=True),
                                    jnp.max(s_ctx, axis=1, keepdims=True)), snk)
        p_band = jnp.exp2(s_band - m).astype(BF16)
        p_ctx = jnp.exp2(s_ctx - m).astype(BF16)
        o2 = (jnp.dot(p_band, vext[pl.ds(s0, band), :], preferred_element_type=F32)
              + jnp.dot(p_ctx, v_ctx, preferred_element_type=F32))
        o2 = o2[:, :LANES] / (o2[:, LANES:] + jnp.exp2(snk - m))
        for h2 in range(AT_G // 2):
            a = o2[(2 * h2) * blk:(2 * h2 + 1) * blk, :]
            b = o2[(2 * h2 + 1) * blk:(2 * h2 + 2) * blk, :]
            o_ref[j * blk:(j + 1) * blk, h2 * LANES:(h2 + 1) * LANES] = (
                jnp.where(low, a, b).astype(o_ref.dtype))


def _attn_call(sink, aq, akd, avd, *, seq):
    bsz, t, _ = aq.shape
    blk = ATT_BLOCK
    qbs = 3
    assert t % (qbs * blk) == 0 and t >= 3 * blk
    gw = AT_G * AT_DH
    kern = functools.partial(_attn_kernel, seq=seq, qbs=qbs)
    whole = pl.BlockSpec((None, t, LANES), lambda b, kv, i: (b, 0, kv))
    return pl.pallas_call(
        kern,
        grid=(bsz, AT_KV, t // (qbs * blk)),
        in_specs=[
            pl.BlockSpec(memory_space=pltpu.SMEM),
            pl.BlockSpec((None, qbs * blk, gw), lambda b, kv, i: (b, i, kv)),
            whole, whole,
        ],
        out_specs=pl.BlockSpec((None, qbs * blk, gw), lambda b, kv, i: (b, i, kv)),
        out_shape=jax.ShapeDtypeStruct((bsz, t, AT_W), BF16),
        scratch_shapes=[pltpu.VMEM((t, 2 * LANES), BF16)],
        compiler_params=_cparams(("arbitrary", "arbitrary", "arbitrary")),
        name="attn",
    )(sink, aq, akd, avd)


def _merge_kernel(x_ref, ml_ref, mc_ref, rf_ref, rb_ref, gg_ref, hf_ref, hb_ref, mo_ref, yc_ref,
                  br_ref, ng_ref, wb_ref, wo_ref, g_ref, w1_hbm, w3_hbm, w2_hbm, o_ref, act_ref,
                  w1b, w3b, w2b, stg_c, stg_r, sem, *, lead, seq, tm, ck):
    @pl.when(jnp.logical_and(pl.program_id(0) == 0, pl.program_id(1) == 0))
    def _():
        _load_ffn_weights(w1_hbm, w3_hbm, w2_hbm, lead, w1b, w3b, w2b, stg_c, stg_r, sem, ck)

    t0 = pl.program_id(1) * tm
    rows = t0 + lax.broadcasted_iota(jnp.int32, (tm, 1), 0)
    is_ctx = rows >= seq
    d = x_ref.shape[1]
    rsum = (rf_ref[...] + rb_ref[...]).reshape(tm, D_RNN)
    ya = (rsum * gg_ref[...].astype(F32)).astype(BF16)
    hsum = hf_ref[...] + hb_ref[...]
    parts = []
    for hd in range(ML_HEADS):
        hh = hsum[:, hd * ML_DH:(hd + 1) * ML_DH]
        parts.append(hh * lax.rsqrt(jnp.mean(hh * hh, axis=-1, keepdims=True) + EPS))
    yb = (jnp.concatenate(parts, axis=1) * ng_ref[...] * mo_ref[...].astype(F32)).astype(BF16)
    m = (br_ref[:, 0:d].astype(F32) * jnp.dot(ya, wb_ref[0], preferred_element_type=F32)
         + br_ref[:, d:2 * d].astype(F32) * jnp.dot(yb, wb_ref[1], preferred_element_type=F32)
         + br_ref[:, 2 * d:3 * d].astype(F32) * jnp.dot(yc_ref[...], wb_ref[2],
                                                       preferred_element_type=F32))
    y = jnp.dot(m.astype(BF16), wo_ref[...], preferred_element_type=F32)
    x1 = x_ref[...] + _row_mod(ml_ref, mc_ref, is_ctx, 5) * y
    o_ref[...] = _ffn_body(x1, ml_ref, mc_ref, is_ctx, g_ref, w1b, w3b, w2b, act_ref,
                           base=6, ck=ck)


def _merge_call(x, ml, mc, rf, rb, gg, hf, hb, mo, yc, br, ng, wb, wo, g, w1, w3, w2,
                *, layer, seq, out_rows):
    bsz, t, d = x.shape
    dff = w1.shape[-1]
    tm = _pick_tile(t, (384, 128))
    ck = _pick_tile(dff, (256, 128))
    kern = functools.partial(_merge_kernel, lead=(layer, 1), seq=seq, tm=tm, ck=ck)
    hbm = pl.BlockSpec(memory_space=pl.ANY)

    def tok(wd):
        return pl.BlockSpec((None, tm, wd), lambda b, i: (b, i, 0))

    rspec = pl.BlockSpec((tm // SUBLANES, None, SUBLANES, D_RNN), lambda b, i: (i, b, 0, 0))

    return pl.pallas_call(
        kern,
        grid=(bsz, pl.cdiv(out_rows, tm)),
        in_specs=[
            tok(d),
            pl.BlockSpec((None, N_MOD, d), lambda b, i: (b, 0, 0)),
            _const_spec((N_MOD, d)),
            rspec, rspec, tok(D_RNN), tok(ML_W), tok(ML_W), tok(ML_W), tok(AT_W),
            tok(N_BRANCH * d),
            _const_spec((1, ML_W)),
            _const_spec((N_BRANCH, BRANCH_W, d), (layer,)),
            _const_spec((d, d), (layer,)),
            _const_spec((1, d)),
            hbm, hbm, hbm,
        ],
        out_specs=tok(d),
        out_shape=jax.ShapeDtypeStruct((bsz, out_rows, d), F32),
        scratch_shapes=[pltpu.VMEM((tm, dff), BF16)] + _ffn_weight_scratch(d, dff, ck),
        compiler_params=_cparams(("arbitrary", "arbitrary")),
        name="merge_ffn",
    )(x, ml, mc, rf, rb, gg, hf, hb, mo, yc, br, ng, wb, wo, g, w1, w3, w2)


def _block_diag(w):
    n, bi, bj = w.shape
    eye = jnp.eye(n, dtype=w.dtype)
    return (eye[:, None, :, None] * w[:, :, None, :]).reshape(n * bi, n * bj)


def _rope_tables(seq, t):
    rows = seq // GRID_W
    row = jnp.repeat(jnp.arange(rows), GRID_W).astype(F32)
    col = jnp.broadcast_to(jnp.arange(GRID_W), (rows, GRID_W)).reshape(-1).astype(F32)
    half = AT_DH // 2
    inv = ROPE_BASE ** (-jnp.arange(0, half, 2, dtype=F32) / half)
    ar = row[:, None] * inv
    ac = col[:, None] * inv
    ang = jnp.concatenate([ar, ar, ac, ac], axis=-1)
    cos = jnp.concatenate([jnp.cos(ang), jnp.ones((t - seq, AT_DH), F32)], axis=0)
    sin = jnp.concatenate([jnp.sin(ang), jnp.zeros((t - seq, AT_DH), F32)], axis=0)
    cos = jnp.tile(cos, (1, LANES // AT_DH))
    sin = jnp.tile(sin, (1, LANES // AT_DH))
    first_half = (jnp.arange(LANES) % 32) < 16
    sin_a = jnp.where(first_half, -sin, 0.0)
    sin_b = jnp.where(first_half, 0.0, sin)
    return cos, sin_a, sin_b


def kernel(x, c, ctx, c_ctx, ada_w, ada_b, norm_g, ffn_w1, ffn_w3, ffn_w2, w_in, rg_conv_w,
           rg_conv_b, rg_wa, rg_ba, rg_wi, rg_bi, rg_lam, ml_gate_b, ml_norm_g, at_qn_g,
           at_kn_g, at_sink, w_branch, w_out):
    bsz, seq, d = x.shape
    nctx = ctx.shape[1]
    t = seq + nctx
    depth = ada_w.shape[0]

    xs = jnp.concatenate([x, ctx], axis=1)
    mod_rows = 2 * SUBLANES
    cc = jnp.concatenate([c, c_ctx[None, :], jnp.zeros((mod_rows - bsz - 1, d), F32)], axis=0)
    mods = _ada_call(cc, ada_w, ada_b).reshape(depth, mod_rows, N_MOD, d)
    cos, sin_a, sin_b = _rope_tables(seq, t)
    gmat = _block_diag(jnp.full((AT_HEADS, AT_DH, AT_DH), 1.0 / AT_DH, F32)).astype(BF16)
    w1, w3, w2 = ffn_w1, ffn_w3, ffn_w2
    w_head, w_tail = w_in[..., :P_HEAD].astype(BF16), w_in[..., P_HEAD:].astype(BF16)
    w_br, w_o = w_branch.astype(BF16), w_out.astype(BF16)

    for l in range(depth):
        ml = mods[l, :bsz]
        mc = mods[l, bsz]
        last = l == depth - 1

        xs = _ffn_call(xs, ml, mc, norm_g[l, 0][None, :], w1, w3, w2,
                       lead=(l, 0), base=0, seq=seq, out_rows=t)

        qg = jnp.tile(at_qn_g[l], AT_HEADS)[None, :]
        kg = jnp.tile(at_kn_g[l], AT_KV)[None, :]
        gb = jnp.concatenate([ml_gate_b[l], jnp.zeros((LANES - N_GATE,), F32)])[None, :]
        (rgx, rgg, mq, mk, mv, mo, aq, akd, avd, br, mg) = _proj_call(
            xs, ml, mc, norm_g[l, 1][None, :], w_head, w_tail, gmat, qg, kg, gb,
            cos, sin_a, sin_b, layer=l, seq=seq)

        wcat = jnp.stack([jnp.concatenate([_block_diag(rg_wa[l, dr]), _block_diag(rg_wi[l, dr])],
                                          axis=1) for dr in range(2)])
        wcat = (0.5 * wcat).astype(BF16)
        bias = 0.5 * jnp.concatenate([rg_ba[l], rg_bi[l]], axis=1)[:, None, :]
        clam = (-0.5 * LOG2E * LRU_C * jax.nn.softplus(-rg_lam[l]))[:, None, :]
        rhf, rhb = _rglru_call(rgx, rg_conv_w[l], rg_conv_b[l][None, :], wcat, bias, clam,
                               seq=seq)

        gt = jnp.swapaxes(mg[:, :, :N_GATE], 1, 2)
        rowp, colsrc = _mlgate_call(gt, seq=seq)
        colp = jnp.swapaxes(colsrc, 1, 2)
        mkt = jnp.swapaxes(mk, 1, 2)
        mhf, mhb = _mlstm_call(mq, mkt, mv, rowp, colp, seq=seq)

        yc = _attn_call(at_sink[l], aq, akd, avd, seq=seq)

        xs = _merge_call(xs, ml, mc, rhf, rhb, rgg, mhf, mhb, mo, yc, br, ml_norm_g[l][None, :],
                         w_br, w_o, norm_g[l, 2][None, :], w1, w3, w2,
                         layer=l, seq=seq, out_rows=seq if last else t)
    return xs
```

```python
import functools

import jax
import jax.numpy as jnp
from jax import lax
from jax.experimental import pallas as pl
from jax.experimental.pallas import tpu as pltpu

F32 = jnp.float32
BF16 = jnp.bfloat16

EPS = 1e-6
NEG = -1e30
LOG2E = 1.4426950408889634
N_MOD = 9
GRID_W = 64
ROPE_BASE = 10000.0

D_RNN = 512
RNN_BLOCKS = 8
RNN_BLOCK = D_RNN // RNN_BLOCKS
CONV_W = 4
CONV_LEFT = 2
CONV_RIGHT = CONV_W - 1 - CONV_LEFT
LRU_C = 8.0

ML_HEADS = 4
ML_DH = 128
ML_W = ML_HEADS * ML_DH
ML_CHUNK = 128
ML_UNITS = 2 * ML_HEADS

AT_HEADS = 8
AT_KV = 2
AT_DH = 64
AT_G = AT_HEADS // AT_KV
AT_W = AT_HEADS * AT_DH
AT_KVW = AT_KV * AT_DH
ATT_BLOCK = 128

N_BRANCH = 3
BRANCH_W = 512

LANES = 128
SUBLANES = 8
VMEM_LIMIT = 56 * 1024 * 1024

P_RGX = 0
P_RGG = P_RGX + D_RNN
P_MQ = P_RGG + D_RNN
P_MK = P_MQ + ML_W
P_MV = P_MK + ML_W
P_MO = P_MV + ML_W
P_HEAD = P_MO + ML_W
N_GATE = 4 * ML_HEADS
T_AQ = 0
T_AK = T_AQ + AT_W
T_AV = T_AK + AT_KVW
T_BR = T_AV + AT_KVW


def _cparams(sem):
    return pltpu.CompilerParams(dimension_semantics=sem, vmem_limit_bytes=VMEM_LIMIT)


def _const_spec(shape, lead=()):
    nd = len(shape)
    idx = tuple(lead) + (0,) * nd
    return pl.BlockSpec((None,) * len(lead) + tuple(shape), lambda *_: idx,
                        pipeline_mode=pl.Buffered(1))


def _pick_tile(total, candidates):
    for c in candidates:
        if total % c == 0:
            return c
    raise ValueError(f"no tile for {total}")


def _sigmoid(x):
    return jax.nn.sigmoid(x)


def _ln_mod(x, g, shift, scale):
    ms = jnp.mean(x * x, axis=-1, keepdims=True)
    return x * lax.rsqrt(ms + EPS) * g * (1.0 + scale) + shift


def _row_mod(ml_ref, mc_ref, is_ctx, i):
    return jnp.where(is_ctx, mc_ref[i:i + 1, :], ml_ref[i:i + 1, :])


def _ada_kernel(cc_ref, w_ref, b_ref, o_ref):
    cc = cc_ref[...]
    s = cc * _sigmoid(cc)
    o_ref[...] = jnp.dot(s.astype(BF16), w_ref[...].astype(BF16),
                         preferred_element_type=F32) + b_ref[...]


def _ada_call(cc, ada_w, ada_b):
    depth, d, nout = ada_w.shape
    rows = cc.shape[0]
    tn = _pick_tile(nout, (1536, 1024, 512, 256, 128))
    return pl.pallas_call(
        _ada_kernel,
        grid=(depth, nout // tn),
        in_specs=[
            pl.BlockSpec((rows, d), lambda l, j: (0, 0)),
            pl.BlockSpec((None, d, tn), lambda l, j: (l, 0, j)),
            pl.BlockSpec((None, 1, tn), lambda l, j: (l, 0, j)),
        ],
        out_specs=pl.BlockSpec((None, rows, tn), lambda l, j: (l, 0, j)),
        out_shape=jax.ShapeDtypeStruct((depth, rows, nout), F32),
        compiler_params=_cparams(("arbitrary", "arbitrary")),
        name="ada_mod",
    )(cc, ada_w, ada_b.reshape(depth, 1, nout))


def _ffn_body(x, ml_ref, mc_ref, is_ctx, g_ref, w1_ref, w3_ref, w2_ref, act_ref, *, base, ck,
              stream=None):
    h = _ln_mod(x, g_ref[...], _row_mod(ml_ref, mc_ref, is_ctx, base),
                _row_mod(ml_ref, mc_ref, is_ctx, base + 1)).astype(BF16)
    dff = w1_ref.shape[1]
    for j in range(dff // ck):
        if stream is not None:
            stream.finish(j)
        a = jnp.dot(h, w1_ref[:, j * ck:(j + 1) * ck], preferred_element_type=F32)
        b = jnp.dot(h, w3_ref[:, j * ck:(j + 1) * ck], preferred_element_type=F32)
        act_ref[:, j * ck:(j + 1) * ck] = (a * _sigmoid(a) * b).astype(BF16)
    y = jnp.dot(act_ref[...], w2_ref[...], preferred_element_type=F32)
    return x + 0.5 * _row_mod(ml_ref, mc_ref, is_ctx, base + 2) * y


def _ffn_weight_scratch(d, dff, ck):
    return [pltpu.VMEM((d, dff), BF16), pltpu.VMEM((d, dff), BF16), pltpu.VMEM((dff, d), BF16),
            pltpu.VMEM((2, d, ck), F32), pltpu.VMEM((ck, d), F32),
            pltpu.SemaphoreType.DMA((3,))]


class _FfnWeightStream:
    def __init__(self, hbm, lead, resident, stg_c, stg_r, sem, ck):
        self.hbm, self.lead, self.resident = hbm, lead, resident
        self.stages = (stg_c.at[0], stg_c.at[1], stg_r)
        self.sem, self.ck = sem, ck
        self.nck = resident[0].shape[1] // ck

    def _copies(self, j):
        l0, l1 = self.lead
        sl = slice(j * self.ck, (j + 1) * self.ck)
        srcs = (self.hbm[0].at[l0, l1, :, sl], self.hbm[1].at[l0, l1, :, sl],
                self.hbm[2].at[l0, l1, sl, :])
        return [pltpu.make_async_copy(src, stg, self.sem.at[i])
                for i, (src, stg) in enumerate(zip(srcs, self.stages))]

    def start(self, j):
        for cp in self._copies(j):
            cp.start()

    def finish(self, j):
        sl = slice(j * self.ck, (j + 1) * self.ck)
        for cp in self._copies(j):
            cp.wait()
        w1b, w3b, w2b = self.resident
        w1b[:, sl] = self.stages[0][...].astype(BF16)
        w3b[:, sl] = self.stages[1][...].astype(BF16)
        w2b[sl, :] = self.stages[2][...].astype(BF16)
        if j + 1 < self.nck:
            self.start(j + 1)


def _ffn_kernel(x_ref, ml_ref, mc_ref, g_ref, w1_hbm, w3_hbm, w2_hbm, o_ref, act_ref,
                w1b, w3b, w2b, stg_c, stg_r, sem, *, lead, base, seq, tm, ck):
    t0 = pl.program_id(1) * tm
    rows = t0 + lax.broadcasted_iota(jnp.int32, (tm, 1), 0)
    is_ctx = rows >= seq
    first = jnp.logical_and(pl.program_id(0) == 0, pl.program_id(1) == 0)

    @pl.when(first)
    def _():
        stream = _FfnWeightStream((w1_hbm, w3_hbm, w2_hbm), lead, (w1b, w3b, w2b),
                                  stg_c, stg_r, sem, ck)
        stream.start(0)
        o_ref[...] = _ffn_body(x_ref[...], ml_ref, mc_ref, is_ctx, g_ref, w1b, w3b, w2b,
                               act_ref, base=base, ck=ck, stream=stream)

    @pl.when(jnp.logical_not(first))
    def _():
        o_ref[...] = _ffn_body(x_ref[...], ml_ref, mc_ref, is_ctx, g_ref, w1b, w3b, w2b,
                               act_ref, base=base, ck=ck)


def _ffn_call(x, ml, mc, g, w1, w3, w2, *, lead, base, seq, out_rows):
    bsz, t, d = x.shape
    dff = w1.shape[-1]
    tm = _pick_tile(t, (768, 384, 128))
    ck = _pick_tile(dff, (256, 128))
    kern = functools.partial(_ffn_kernel, lead=lead, base=base, seq=seq, tm=tm, ck=ck)
    hbm = pl.BlockSpec(memory_space=pl.ANY)
    return pl.pallas_call(
        kern,
        grid=(bsz, pl.cdiv(out_rows, tm)),
        in_specs=[
            pl.BlockSpec((None, tm, d), lambda b, i: (b, i, 0)),
            pl.BlockSpec((None, N_MOD, d), lambda b, i: (b, 0, 0)),
            _const_spec((N_MOD, d)),
            _const_spec((1, d)),
            hbm, hbm, hbm,
        ],
        out_specs=pl.BlockSpec((None, tm, d), lambda b, i: (b, i, 0)),
        out_shape=jax.ShapeDtypeStruct((bsz, out_rows, d), F32),
        scratch_shapes=[pltpu.VMEM((tm, dff), BF16)] + _ffn_weight_scratch(d, dff, ck),
        compiler_params=_cparams(("arbitrary", "arbitrary")),
        name="ffn",
    )(x, ml, mc, g, w1, w3, w2)


def _gelu_tanh(x):
    return 0.5 * x * (1.0 + jnp.tanh(0.7978845608028654 * (x + 0.044715 * (x * x * x))))


def _head_norm(x, gmat, g):
    ms = jnp.dot((x * x).astype(BF16), gmat, preferred_element_type=F32)
    return x * lax.rsqrt(ms + EPS) * g


def _dup_heads(x):
    low = lax.broadcasted_iota(jnp.int32, x.shape, 1) < AT_DH
    sw = pltpu.roll(x, AT_DH, 1)
    return jnp.concatenate([jnp.where(low, x, sw), jnp.where(low, sw, x)], axis=1)


def _rope(x, cos, sin_a, sin_b):
    parts = []
    for j in range(x.shape[1] // LANES):
        xj = x[:, j * LANES:(j + 1) * LANES]
        parts.append(xj * cos + pltpu.roll(xj, LANES - 16, 1) * sin_a
                     + pltpu.roll(xj, 16, 1) * sin_b)
    return jnp.concatenate(parts, axis=1)


def _proj_kernel(x_ref, ml_ref, mc_ref, g_ref, wh_ref, wt_ref, gmat_ref, qg_ref, kg_ref, gb_ref,
                 cos_ref, sa_ref, sb_ref,
                 rgx_o, rgg_o, mq_o, mk_o, mv_o, mo_o, aq_o, ak_o, av_o, br_o, mg_o,
                 tail, *, seq, tm, d_model):
    @pl.when(jnp.logical_and(pl.program_id(0) == 0, pl.program_id(1) == 0))
    def _():
        for r in range(0, d_model, LANES):
            tail[r:r + LANES, :] = wt_ref[r:r + LANES, N_GATE:]

    t0 = pl.program_id(1) * tm
    rows = t0 + lax.broadcasted_iota(jnp.int32, (tm, 1), 0)
    is_ctx = rows >= seq
    h = _ln_mod(x_ref[...], g_ref[...], _row_mod(ml_ref, mc_ref, is_ctx, 3),
                _row_mod(ml_ref, mc_ref, is_ctx, 4)).astype(BF16)

    def seg(c0, w, ref=wh_ref):
        return jnp.dot(h, ref[:, c0:c0 + w], preferred_element_type=F32)

    rgx_o[...] = seg(P_RGX, D_RNN).reshape(rgx_o.shape)
    rgg_o[...] = _gelu_tanh(seg(P_RGG, D_RNN)).astype(BF16)
    mq_o[...] = seg(P_MQ, ML_W).astype(BF16)
    mk_o[...] = (seg(P_MK, ML_W) * (ML_DH ** -0.5)).astype(BF16)
    mv_o[...] = seg(P_MV, ML_W).astype(BF16)
    mo_o[...] = _sigmoid(seg(P_MO, ML_W)).astype(BF16)
    cos, sin_a, sin_b = cos_ref[...], sa_ref[...], sb_ref[...]
    gmat = gmat_ref[...]
    q = _head_norm(seg(T_AQ, AT_W, tail), gmat, qg_ref[...])
    aq_o[...] = (_rope(q, cos, sin_a, sin_b) * (AT_DH ** -0.5 * LOG2E)).astype(BF16)
    k = _head_norm(seg(T_AK, AT_KVW, tail), gmat[:AT_KVW, :AT_KVW], kg_ref[...])
    ak_o[...] = _dup_heads(_rope(k, cos, sin_a, sin_b)).astype(BF16)
    av_o[...] = _dup_heads(seg(T_AV, AT_KVW, tail)).astype(BF16)
    for c0 in range(0, N_BRANCH * d_model, 512):
        br_o[:, c0:c0 + 512] = _sigmoid(seg(T_BR + c0, 512, tail)).astype(BF16)
    mg_o[...] = seg(0, LANES, wt_ref) + gb_ref[...]


def _proj_call(x, ml, mc, g, wh, wt, gmat, qg, kg, gb, cos, sin_a, sin_b, *, layer, seq):
    bsz, t, d = x.shape
    ntail = wt.shape[-1] - N_GATE
    tm = _pick_tile(t, (384, 128))
    kern = functools.partial(_proj_kernel, seq=seq, tm=tm, d_model=d)
    widths = [(D_RNN, F32), (D_RNN, BF16), (ML_W, BF16), (ML_W, BF16), (ML_W, BF16),
              (ML_W, BF16), (AT_W, BF16), (2 * AT_KVW, BF16), (2 * AT_KVW, BF16),
              (N_BRANCH * d, BF16), (LANES, F32)]
    tab_spec = pl.BlockSpec((tm, LANES), lambda b, i: (i, 0))
    out_specs = [pl.BlockSpec((None, tm, wd), lambda b, i: (b, i, 0)) for wd, _ in widths]
    out_shape = [jax.ShapeDtypeStruct((bsz, t, wd), dt) for wd, dt in widths]
    out_specs[0] = pl.BlockSpec((tm // SUBLANES, None, SUBLANES, D_RNN), lambda b, i: (i, b, 0, 0))
    out_shape[0] = jax.ShapeDtypeStruct((t // SUBLANES, bsz, SUBLANES, D_RNN), F32)
    return pl.pallas_call(
        kern,
        grid=(bsz, t // tm),
        in_specs=[
            pl.BlockSpec((None, tm, d), lambda b, i: (b, i, 0)),
            pl.BlockSpec((None, N_MOD, d), lambda b, i: (b, 0, 0)),
            _const_spec((N_MOD, d)),
            _const_spec((1, d)),
            _const_spec((d, wh.shape[-1]), (layer,)),
            _const_spec((d, wt.shape[-1]), (layer,)),
            _const_spec((AT_W, AT_W)),
            _const_spec((1, AT_W)),
            _const_spec((1, AT_KVW)),
            _const_spec((1, LANES)),
            tab_spec, tab_spec, tab_spec,
        ],
        out_specs=out_specs,
        out_shape=out_shape,
        scratch_shapes=[pltpu.VMEM((d, ntail), BF16)],
        compiler_params=_cparams(("arbitrary", "arbitrary")),
        name="in_proj",
    )(x, ml, mc, g, wh, wt, gmat, qg, kg, gb, cos, sin_a, sin_b)


def _scan_chunk(s, rev, nl, nc):
    if rev:
        return jnp.where(s < nc, nl + nc - 1 - s, nl - 1 - (s - nc))
    return jnp.where(s < nc, nl + s, s - nc)


def _rglru_kernel(xf_ref, xfp_ref, xfn_ref, xb_ref, xbp_ref, xbn_ref, cw_ref, cb_ref, w_ref,
                  bias_ref, clam_ref, hf_ref, hb_ref, ext, a_s, b_s, h_s, hstate,
                  *, nl, nc, tc, sub):
    s = pl.program_id(0)
    bsz, c = hstate.shape[1], hstate.shape[2]
    ntb = tc // SUBLANES

    @pl.when(s == 0)
    def _():
        hstate[...] = jnp.zeros_like(hstate)

    cw = cw_ref[...]
    cb = cb_ref[...]
    for d, (x_ref, xp_ref, xn_ref) in enumerate(((xf_ref, xfp_ref, xfn_ref),
                                                (xb_ref, xbp_ref, xbn_ref))):
        chunk = _scan_chunk(s, d == 1, nl, nc)
        first = jnp.logical_or(chunk == 0, chunk == nl)
        last = jnp.logical_or(chunk == nl - 1, chunk == nl + nc - 1)
        prev = jnp.swapaxes(xp_ref[0], 0, 1)[SUBLANES - CONV_LEFT:]
        nxt = jnp.swapaxes(xn_ref[0], 0, 1)[:CONV_RIGHT]
        ext[d, 0:CONV_LEFT] = jnp.where(first, 0.0, prev)
        for i in range(ntb):
            o = CONV_LEFT + i * SUBLANES
            ext[d, o:o + SUBLANES] = jnp.swapaxes(x_ref[i], 0, 1)
        ext[d, CONV_LEFT + tc:CONV_LEFT + tc + CONV_RIGHT] = jnp.where(last, 0.0, nxt)

        def gates(i, carry, d=d):
            t0 = pl.multiple_of(i * sub, sub)
            u = cb
            for k in range(CONV_W):
                u = u + ext[d, pl.ds(t0 + k, sub)] * cw[k:k + 1, :]
            u2 = u.reshape(sub * bsz, c)
            z = jnp.dot(u2.astype(BF16), w_ref[d], preferred_element_type=F32) + bias_ref[d]
            a = jnp.exp2(clam_ref[d] * jnp.tanh(z[:, :c]) + clam_ref[d])
            gi = 0.5 * jnp.tanh(z[:, c:]) + 0.5
            s1 = 1.0 - a * a
            bb = (s1 * lax.rsqrt(jnp.maximum(s1, 1e-36))) * (gi * u2)
            a_s[d, pl.ds(t0, sub)] = a.reshape(sub, bsz, c)
            b_s[d, pl.ds(t0, sub)] = bb.reshape(sub, bsz, c)
            return carry

        lax.fori_loop(0, tc // sub, gates, 0)

    def step(k, hs):
        hf, hb = hs
        kb = tc - 1 - k
        hf = a_s[0, k] * hf + b_s[0, k]
        hb = a_s[1, kb] * hb + b_s[1, kb]
        h_s[0, k] = hf
        h_s[1, kb] = hb
        return hf, hb

    hf, hb = lax.fori_loop(0, tc, step, (hstate[0], hstate[1]), unroll=8)
    hstate[0] = hf
    hstate[1] = hb
    for d, o_ref in enumerate((hf_ref, hb_ref)):
        for i in range(ntb):
            o_ref[i] = jnp.swapaxes(h_s[d, i * SUBLANES:(i + 1) * SUBLANES], 0, 1)


def _rglru_call(rgx, cw, cb, w, bias, clam, *, seq):
    ntb_all, bsz, _, c = rgx.shape
    t = ntb_all * SUBLANES
    tc = 128
    sub = 32
    assert seq % tc == 0 and (t - seq) % tc == 0
    nl, nc = seq // tc, (t - seq) // tc
    ntb = tc // SUBLANES

    def specs(rev):
        chunk = functools.partial(_scan_chunk, rev=rev, nl=nl, nc=nc)
        return [
            pl.BlockSpec((ntb, bsz, SUBLANES, c), lambda s: (chunk(s), 0, 0, 0)),
            pl.BlockSpec((1, bsz, SUBLANES, c),
                         lambda s: (jnp.maximum(chunk(s) * ntb - 1, 0), 0, 0, 0)),
            pl.BlockSpec((1, bsz, SUBLANES, c),
                         lambda s: (jnp.minimum((chunk(s) + 1) * ntb, ntb_all - 1), 0, 0, 0)),
        ]

    kern = functools.partial(_rglru_kernel, nl=nl, nc=nc, tc=tc, sub=sub)
    fspec, bspec = specs(False), specs(True)
    return pl.pallas_call(
        kern,
        grid=(nl + nc,),
        in_specs=fspec + bspec + [
            _const_spec((CONV_W, c)),
            _const_spec((1, c)),
            _const_spec((2, c, 2 * c)),
            _const_spec((2, 1, 2 * c)),
            _const_spec((2, 1, c)),
        ],
        out_specs=[fspec[0], bspec[0]],
        out_shape=[jax.ShapeDtypeStruct(rgx.shape, F32)] * 2,
        scratch_shapes=[
            pltpu.VMEM((2, tc + CONV_W - 1, bsz, c), F32),
            pltpu.VMEM((2, tc, bsz, c), F32),
            pltpu.VMEM((2, tc, bsz, c), F32),
            pltpu.VMEM((2, tc, bsz, c), F32),
            pltpu.VMEM((2, bsz, c), F32),
        ],
        compiler_params=_cparams(("arbitrary",)),
        name="rglru",
    )(rgx, rgx, rgx, rgx, rgx, rgx, cw, cb, w, bias, clam)


def _log_sigmoid(x):
    return -(jnp.maximum(-x, 0.0) + jnp.log1p(jnp.exp(-jnp.abs(x))))


def _lane_scan(x, op, ident, rev):
    n = x.shape[1]
    lane = lax.broadcasted_iota(jnp.int32, x.shape, 1)
    sh = 1
    while sh < n:
        if rev:
            x = op(x, jnp.where(lane < n - sh, pltpu.roll(x, n - sh, 1), ident))
        else:
            x = op(x, jnp.where(lane >= sh, pltpu.roll(x, sh, 1), ident))
        sh *= 2
    return x


def _mlgate_kernel(g_ref, row_o, col_o, cb_s, gg_s, ml_s, tot_s, wm_s, mo_s, mn_s, *, nl, nc):
    L = ML_CHUNK
    H = ML_HEADS
    U = ML_UNITS
    is_bwd = lax.broadcasted_iota(jnp.int32, (U, L), 0) >= H
    for c in range(nl + nc):
        lanes = slice(c * L, (c + 1) * L)
        g16 = g_ref[:, lanes]
        li = jnp.concatenate([g16[0:H], g16[2 * H:3 * H]], axis=0)
        lf = _log_sigmoid(jnp.concatenate([g16[H:2 * H], g16[3 * H:4 * H]], axis=0))
        pre = _lane_scan(lf, jnp.add, 0.0, False)
        suf = _lane_scan(lf, jnp.add, 0.0, True)
        tot = pre + suf - lf
        cb = jnp.where(is_bwd, suf, pre)
        gg = li - cb
        pmax = _lane_scan(gg, jnp.maximum, NEG, False)
        smax = _lane_scan(gg, jnp.maximum, NEG, True)
        cb_s[:, lanes] = cb
        gg_s[:, lanes] = gg
        ml_s[:, lanes] = cb + jnp.where(is_bwd, smax, pmax)
        tot_s[:, lanes] = tot
        wm_s[:, lanes] = tot + jnp.maximum(pmax, smax)

    m = jnp.zeros((U, L), F32)
    for s in range(nl + nc):
        cf = nl + s if s < nc else s - nc
        cr = nl + nc - 1 - s if s < nc else nl - 1 - (s - nc)
        lf_, lr_ = slice(cf * L, (cf + 1) * L), slice(cr * L, (cr + 1) * L)
        tot = jnp.where(is_bwd, tot_s[:, lr_], tot_s[:, lf_])
        wm = jnp.where(is_bwd, wm_s[:, lr_], wm_s[:, lf_])
        m_new = jnp.maximum(tot + m, wm)
        mo_s[0:H, lf_] = m[0:H]
        mo_s[H:U, lr_] = m[H:U]
        mn_s[0:H, lf_] = m_new[0:H]
        mn_s[H:U, lr_] = m_new[H:U]
        m = m_new

    cb = cb_s[...]
    gg = gg_s[...]
    tot = tot_s[...]
    mo = mo_s[...]
    mn = mn_s[...]
    m_t = jnp.maximum(cb + mo, ml_s[...])
    row_o[0:U, :] = gg
    row_o[U:2 * U, :] = jnp.exp(tot + gg - mn)
    row_o[2 * U:3 * U, :] = jnp.exp(tot + mo - mn)
    row_o[3 * U:4 * U, :] = mo
    col_o[0:U, :] = cb - m_t
    col_o[U:2 * U, :] = jnp.exp(-m_t)


def _mlgate_call(gt, *, seq):
    bsz, ng, t = gt.shape
    nl, nc = seq // ML_CHUNK, (t - seq) // ML_CHUNK
    kern = functools.partial(_mlgate_kernel, nl=nl, nc=nc)
    rout = pl.BlockSpec((None, 4 * ML_UNITS, t), lambda b: (b, 0, 0))
    cout = pl.BlockSpec((None, 2 * ML_UNITS, t), lambda b: (b, 0, 0))
    return pl.pallas_call(
        kern,
        grid=(bsz,),
        in_specs=[pl.BlockSpec((None, ng, t), lambda b: (b, 0, 0))],
        out_specs=[rout, cout],
        out_shape=[jax.ShapeDtypeStruct((bsz, 4 * ML_UNITS, t), F32),
                   jax.ShapeDtypeStruct((bsz, 2 * ML_UNITS, t), F32)],
        scratch_shapes=[pltpu.VMEM((ML_UNITS, t), F32)] * 7,
        compiler_params=_cparams(("arbitrary",)),
        name="mlstm_gates",
    )(gt)


def _mlstm_kernel(qf_ref, ktf_ref, vf_ref, rf_ref, cf_ref, qb_ref, ktb_ref, vb_ref, rb_ref, cb_ref,
                  hf_ref, hb_ref, c_st):
    s = pl.program_id(1)

    @pl.when(s == 0)
    def _():
        c_st[...] = jnp.zeros_like(c_st)

    L, dh, U = ML_CHUNK, ML_DH, ML_UNITS
    ti = lax.broadcasted_iota(jnp.int32, (L, L), 0)
    si = lax.broadcasted_iota(jnp.int32, (L, L), 1)
    ones = jnp.ones((L, dh), BF16)
    for bb in range(qf_ref.shape[0]):
        for d, (q_ref, kt_ref, v_ref, r_ref, c_ref, o_ref) in enumerate(
                ((qf_ref, ktf_ref, vf_ref, rf_ref, cf_ref, hf_ref),
                 (qb_ref, ktb_ref, vb_ref, rb_ref, cb_ref, hb_ref))):
            tri = (si >= ti) if d == 1 else (si <= ti)
            rows = r_ref[bb]
            cols = c_ref[bb]
            for hd in range(ML_HEADS):
                u = d * ML_HEADS + hd
                sl = slice(hd * dh, (hd + 1) * dh)
                q = q_ref[bb, :, sl]
                kt = kt_ref[bb, sl, :]
                v_ext = jnp.concatenate([v_ref[bb, :, sl], ones], axis=1)
                g_row = rows[u:u + 1, :]
                w_row = rows[U + u:U + u + 1, :]
                a_row = rows[2 * U + u:2 * U + u + 1, :]
                m_row = rows[3 * U + u:3 * U + u + 1, :]
                xb = jnp.broadcast_to(cols[:, u:u + 1], (L, L))
                emt = cols[:, U + u:U + u + 1]
                c_old = c_st[bb * U + u]
                qkc = jnp.dot(q, jnp.concatenate([kt, c_old.astype(BF16)], axis=1),
                              preferred_element_type=F32)
                dm = jnp.exp(jnp.where(tri, xb + g_row, NEG))
                dec = jnp.exp(xb + m_row)
                sc = (qkc[:, :L] * dm).astype(BF16)
                ktw = (kt.astype(F32) * w_row).astype(BF16)
                sv = jnp.dot(jnp.concatenate([sc, ktw], axis=0), v_ext,
                             preferred_element_type=F32)
                ab = sv[:L] + jnp.concatenate([dec, dec], axis=1) * qkc[:, L:]
                o_ref[bb, :, sl] = ab[:, :dh] / jnp.maximum(jnp.abs(ab[:, dh:]), emt)
                c_st[bb * U + u] = jnp.concatenate([a_row, a_row], axis=1) * c_old + sv[L:]


def _mlstm_call(mq, mkt, mv, rowp, colp, *, seq):
    bsz, t, w = mq.shape
    nl, nc = seq // ML_CHUNK, (t - seq) // ML_CHUNK
    cf = functools.partial(_scan_chunk, rev=False, nl=nl, nc=nc)
    cr = functools.partial(_scan_chunk, rev=True, nl=nl, nc=nc)
    nr = rowp.shape[1]
    bpb = 8 if bsz % 8 == 0 else 1

    def tok(ch, wd):
        return pl.BlockSpec((bpb, ML_CHUNK, wd), lambda b, s: (b, ch(s), 0))

    def tr(ch, rows):
        return pl.BlockSpec((bpb, rows, ML_CHUNK), lambda b, s: (b, 0, ch(s)))

    def side(ch):
        return [tok(ch, w), tr(ch, w), tok(ch, w), tr(ch, nr), tok(ch, colp.shape[2])]

    return pl.pallas_call(
        _mlstm_kernel,
        grid=(bsz // bpb, nl + nc),
        in_specs=side(cf) + side(cr),
        out_specs=[tok(cf, w), tok(cr, w)],
        out_shape=[jax.ShapeDtypeStruct((bsz, t, w), F32)] * 2,
        scratch_shapes=[pltpu.VMEM((bpb * ML_UNITS, ML_DH, 2 * ML_DH), F32)],
        compiler_params=_cparams(("arbitrary", "arbitrary")),
        name="mlstm",
    )(mq, mkt, mv, rowp, colp, mq, mkt, mv, rowp, colp)


def _attn_kernel(sink_ref, q_ref, k_ref, v_ref, o_ref, vext, *, seq, qbs):
    kv = pl.program_id(1)
    i = pl.program_id(2)
    blk = ATT_BLOCK
    t = k_ref.shape[0]
    ctx = t - seq
    nlb = seq // blk
    band = 3 * blk

    @pl.when(i == 0)
    def _():
        vext[:, 0:LANES] = v_ref[...]
        vext[:, LANES:2 * LANES] = jnp.ones((t, LANES), BF16)

    lane = lax.broadcasted_iota(jnp.int32, (blk, LANES), 1)
    low = lane < AT_DH
    zero = jnp.zeros((blk, LANES), BF16)
    rows = AT_G * blk
    gi = lax.broadcasted_iota(jnp.int32, (rows, 1), 0) // blk
    snk = jnp.zeros((rows, 1), F32)
    for g in range(AT_G):
        snk = jnp.where(gi == g, sink_ref[kv * AT_G + g] * LOG2E, snk)
    ti = lax.broadcasted_iota(jnp.int32, (blk, band), 0)
    ci = lax.broadcasted_iota(jnp.int32, (blk, band), 1)
    nt = (((1,), (1,)), ((), ()))
    k_ctx = k_ref[seq:t, :]
    v_ctx = vext[seq:t, :]

    for j in range(qbs):
        qb = i * qbs + j
        s0 = pl.multiple_of(jnp.clip((qb - 1) * blk, 0, t - band), blk)
        rel = ci - ti + (s0 - qb * blk + blk)
        lim = jnp.where(qb < nlb, seq - s0, 0)
        ok = jnp.logical_and(jnp.logical_and(rel >= 0, rel <= 2 * blk), ci < lim)
        bias = jnp.where(ok, 0.0, NEG)
        bias = jnp.concatenate([bias] * AT_G, axis=0)

        qs = []
        for h2 in range(AT_G // 2):
            qj = q_ref[j * blk:(j + 1) * blk, h2 * LANES:(h2 + 1) * LANES]
            qs += [jnp.where(low, qj, zero), jnp.where(low, zero, qj)]
        q_all = jnp.concatenate(qs, axis=0)
        s_band = lax.dot_general(q_all, k_ref[pl.ds(s0, band), :], nt,
                                 preferred_element_type=F32) + bias
        s_ctx = lax.dot_general(q_all, k_ctx, nt, preferred_element_type=F32)
        m = jnp.maximum(jnp.maximum(jnp.max(s_band, axis=1, keepdims=True),
                                    jnp.max(s_ctx, axis=1, keepdims=True)), snk)
        p_band = jnp.exp2(s_band - m).astype(BF16)
        p_ctx = jnp.exp2(s_ctx - m).astype(BF16)
        o2 = (jnp.dot(p_band, vext[pl.ds(s0, band), :], preferred_element_type=F32)
              + jnp.dot(p_ctx, v_ctx, preferred_element_type=F32))
        o2 = o2[:, :LANES] / (o2[:, LANES:] + jnp.exp2(snk - m))
        for h2 in range(AT_G // 2):
            a = o2[(2 * h2) * blk:(2 * h2 + 1) * blk, :]
            b = o2[(2 * h2 + 1) * blk:(2 * h2 + 2) * blk, :]
            o_ref[j * blk:(j + 1) * blk, h2 * LANES:(h2 + 1) * LANES] = (
                jnp.where(low, a, b).astype(o_ref.dtype))


def _attn_call(sink, aq, akd, avd, *, seq):
    bsz, t, _ = aq.shape
    blk = ATT_BLOCK
    qbs = 3
    assert t % (qbs * blk) == 0 and t >= 3 * blk
    gw = AT_G * AT_DH
    kern = functools.partial(_attn_kernel, seq=seq, qbs=qbs)
    whole = pl.BlockSpec((None, t, LANES), lambda b, kv, i: (b, 0, kv))
    return pl.pallas_call(
        kern,
        grid=(bsz, AT_KV, t // (qbs * blk)),
        in_specs=[
            pl.BlockSpec(memory_space=pltpu.SMEM),
            pl.BlockSpec((None, qbs * blk, gw), lambda b, kv, i: (b, i, kv)),
            whole, whole,
        ],
        out_specs=pl.BlockSpec((None, qbs * blk, gw), lambda b, kv, i: (b, i, kv)),
        out_shape=jax.ShapeDtypeStruct((bsz, t, AT_W), BF16),
        scratch_shapes=[pltpu.VMEM((t, 2 * LANES), BF16)],
        compiler_params=_cparams(("arbitrary", "arbitrary", "arbitrary")),
        name="attn",
    )(sink, aq, akd, avd)


def _merge_kernel(x_ref, ml_ref, mc_ref, rf_ref, rb_ref, gg_ref, hf_ref, hb_ref, mo_ref, yc_ref,
                  br_ref, ng_ref, wb_ref, wo_ref, g_ref, w1_hbm, w3_hbm, w2_hbm, o_ref, act_ref,
                  w1b, w3b, w2b, stg_c, stg_r, sem, *, lead, seq, tm, ck):
    first = jnp.logical_and(pl.program_id(0) == 0, pl.program_id(1) == 0)
    stream = _FfnWeightStream((w1_hbm, w3_hbm, w2_hbm), lead, (w1b, w3b, w2b),
                              stg_c, stg_r, sem, ck)

    @pl.when(first)
    def _():
        stream.start(0)

    t0 = pl.program_id(1) * tm
    rows = t0 + lax.broadcasted_iota(jnp.int32, (tm, 1), 0)
    is_ctx = rows >= seq
    d = x_ref.shape[1]
    rsum = (rf_ref[...] + rb_ref[...]).reshape(tm, D_RNN)
    ya = (rsum * gg_ref[...].astype(F32)).astype(BF16)
    hsum = hf_ref[...] + hb_ref[...]
    parts = []
    for hd in range(ML_HEADS):
        hh = hsum[:, hd * ML_DH:(hd + 1) * ML_DH]
        parts.append(hh * lax.rsqrt(jnp.mean(hh * hh, axis=-1, keepdims=True) + EPS))
    yb = (jnp.concatenate(parts, axis=1) * ng_ref[...] * mo_ref[...].astype(F32)).astype(BF16)
    m = (br_ref[:, 0:d].astype(F32) * jnp.dot(ya, wb_ref[0], preferred_element_type=F32)
         + br_ref[:, d:2 * d].astype(F32) * jnp.dot(yb, wb_ref[1], preferred_element_type=F32)
         + br_ref[:, 2 * d:3 * d].astype(F32) * jnp.dot(yc_ref[...], wb_ref[2],
                                                       preferred_element_type=F32))
    y = jnp.dot(m.astype(BF16), wo_ref[...], preferred_element_type=F32)
    x1 = x_ref[...] + _row_mod(ml_ref, mc_ref, is_ctx, 5) * y
    @pl.when(first)
    def _():
        o_ref[...] = _ffn_body(x1, ml_ref, mc_ref, is_ctx, g_ref, w1b, w3b, w2b, act_ref,
                               base=6, ck=ck, stream=stream)

    @pl.when(jnp.logical_not(first))
    def _():
        o_ref[...] = _ffn_body(x1, ml_ref, mc_ref, is_ctx, g_ref, w1b, w3b, w2b, act_ref,
                               base=6, ck=ck)


def _merge_call(x, ml, mc, rf, rb, gg, hf, hb, mo, yc, br, ng, wb, wo, g, w1, w3, w2,
                *, layer, seq, out_rows):
    bsz, t, d = x.shape
    dff = w1.shape[-1]
    tm = _pick_tile(t, (384, 128))
    ck = _pick_tile(dff, (256, 128))
    kern = functools.partial(_merge_kernel, lead=(layer, 1), seq=seq, tm=tm, ck=ck)
    hbm = pl.BlockSpec(memory_space=pl.ANY)

    def tok(wd):
        return pl.BlockSpec((None, tm, wd), lambda b, i: (b, i, 0))

    rspec = pl.BlockSpec((tm // SUBLANES, None, SUBLANES, D_RNN), lambda b, i: (i, b, 0, 0))

    return pl.pallas_call(
        kern,
        grid=(bsz, pl.cdiv(out_rows, tm)),
        in_specs=[
            tok(d),
            pl.BlockSpec((None, N_MOD, d), lambda b, i: (b, 0, 0)),
            _const_spec((N_MOD, d)),
            rspec, rspec, tok(D_RNN), tok(ML_W), tok(ML_W), tok(ML_W), tok(AT_W),
            tok(N_BRANCH * d),
            _const_spec((1, ML_W)),
            _const_spec((N_BRANCH, BRANCH_W, d), (layer,)),
            _const_spec((d, d), (layer,)),
            _const_spec((1, d)),
            hbm, hbm, hbm,
        ],
        out_specs=tok(d),
        out_shape=jax.ShapeDtypeStruct((bsz, out_rows, d), F32),
        scratch_shapes=[pltpu.VMEM((tm, dff), BF16)] + _ffn_weight_scratch(d, dff, ck),
        compiler_params=_cparams(("arbitrary", "arbitrary")),
        name="merge_ffn",
    )(x, ml, mc, rf, rb, gg, hf, hb, mo, yc, br, ng, wb, wo, g, w1, w3, w2)


def _block_diag(w):
    n, bi, bj = w.shape
    eye = jnp.eye(n, dtype=w.dtype)
    return (eye[:, None, :, None] * w[:, :, None, :]).reshape(n * bi, n * bj)


def _rope_tables(seq, t):
    rows = seq // GRID_W
    row = jnp.repeat(jnp.arange(rows), GRID_W).astype(F32)
    col = jnp.broadcast_to(jnp.arange(GRID_W), (rows, GRID_W)).reshape(-1).astype(F32)
    half = AT_DH // 2
    inv = ROPE_BASE ** (-jnp.arange(0, half, 2, dtype=F32) / half)
    ar = row[:, None] * inv
    ac = col[:, None] * inv
    ang = jnp.concatenate([ar, ar, ac, ac], axis=-1)
    cos = jnp.concatenate([jnp.cos(ang), jnp.ones((t - seq, AT_DH), F32)], axis=0)
    sin = jnp.concatenate([jnp.sin(ang), jnp.zeros((t - seq, AT_DH), F32)], axis=0)
    cos = jnp.tile(cos, (1, LANES // AT_DH))
    sin = jnp.tile(sin, (1, LANES // AT_DH))
    first_half = (jnp.arange(LANES) % 32) < 16
    sin_a = jnp.where(first_half, -sin, 0.0)
    sin_b = jnp.where(first_half, 0.0, sin)
    return cos, sin_a, sin_b


def kernel(x, c, ctx, c_ctx, ada_w, ada_b, norm_g, ffn_w1, ffn_w3, ffn_w2, w_in, rg_conv_w,
           rg_conv_b, rg_wa, rg_ba, rg_wi, rg_bi, rg_lam, ml_gate_b, ml_norm_g, at_qn_g,
           at_kn_g, at_sink, w_branch, w_out):
    bsz, seq, d = x.shape
    nctx = ctx.shape[1]
    t = seq + nctx
    depth = ada_w.shape[0]

    xs = jnp.concatenate([x, ctx], axis=1)
    mod_rows = 2 * SUBLANES
    cc = jnp.concatenate([c, c_ctx[None, :], jnp.zeros((mod_rows - bsz - 1, d), F32)], axis=0)
    mods = _ada_call(cc, ada_w, ada_b).reshape(depth, mod_rows, N_MOD, d)
    cos, sin_a, sin_b = _rope_tables(seq, t)
    gmat = _block_diag(jnp.full((AT_HEADS, AT_DH, AT_DH), 1.0 / AT_DH, F32)).astype(BF16)
    w1, w3, w2 = ffn_w1, ffn_w3, ffn_w2
    w_head, w_tail = w_in[..., :P_HEAD].astype(BF16), w_in[..., P_HEAD:].astype(BF16)
    w_br, w_o = w_branch.astype(BF16), w_out.astype(BF16)

    for l in range(depth):
        ml = mods[l, :bsz]
        mc = mods[l, bsz]
        last = l == depth - 1

        xs = _ffn_call(xs, ml, mc, norm_g[l, 0][None, :], w1, w3, w2,
                       lead=(l, 0), base=0, seq=seq, out_rows=t)

        qg = jnp.tile(at_qn_g[l], AT_HEADS)[None, :]
        kg = jnp.tile(at_kn_g[l], AT_KV)[None, :]
        gb = jnp.concatenate([ml_gate_b[l], jnp.zeros((LANES - N_GATE,), F32)])[None, :]
        (rgx, rgg, mq, mk, mv, mo, aq, akd, avd, br, mg) = _proj_call(
            xs, ml, mc, norm_g[l, 1][None, :], w_head, w_tail, gmat, qg, kg, gb,
            cos, sin_a, sin_b, layer=l, seq=seq)

        wcat = jnp.stack([jnp.concatenate([_block_diag(rg_wa[l, dr]), _block_diag(rg_wi[l, dr])],
                                          axis=1) for dr in range(2)])
        wcat = (0.5 * wcat).astype(BF16)
        bias = 0.5 * jnp.concatenate([rg_ba[l], rg_bi[l]], axis=1)[:, None, :]
        clam = (-0.5 * LOG2E * LRU_C * jax.nn.softplus(-rg_lam[l]))[:, None, :]
        rhf, rhb = _rglru_call(rgx, rg_conv_w[l], rg_conv_b[l][None, :], wcat, bias, clam,
                               seq=seq)

        gt = jnp.swapaxes(mg[:, :, :N_GATE], 1, 2)
        rowp, colsrc = _mlgate_call(gt, seq=seq)
        colp = jnp.swapaxes(colsrc, 1, 2)
        mkt = jnp.swapaxes(mk, 1, 2)
        mhf, mhb = _mlstm_call(mq, mkt, mv, rowp, colp, seq=seq)

        yc = _attn_call(at_sink[l], aq, akd, avd, seq=seq)

        xs = _merge_call(xs, ml, mc, rhf, rhb, rgg, mhf, mhb, mo, yc, br, ml_norm_g[l][None, :],
                         w_br, w_o, norm_g[l, 2][None, :], w1, w3, w2,
                         layer=l, seq=seq, out_rows=seq if last else t)
    return xs
```

```python
import functools

import jax
import jax.numpy as jnp
from jax import lax
from jax.experimental import pallas as pl
from jax.experimental.pallas import tpu as pltpu

F32 = jnp.float32
BF16 = jnp.bfloat16

EPS = 1e-6
NEG = -1e30
LOG2E = 1.4426950408889634
N_MOD = 9
GRID_W = 64
ROPE_BASE = 10000.0

D_RNN = 512
RNN_BLOCKS = 8
RNN_BLOCK = D_RNN // RNN_BLOCKS
CONV_W = 4
CONV_LEFT = 2
CONV_RIGHT = CONV_W - 1 - CONV_LEFT
LRU_C = 8.0

ML_HEADS = 4
ML_DH = 128
ML_W = ML_HEADS * ML_DH
ML_CHUNK = 128
ML_UNITS = 2 * ML_HEADS

AT_HEADS = 8
AT_KV = 2
AT_DH = 64
AT_G = AT_HEADS // AT_KV
AT_W = AT_HEADS * AT_DH
AT_KVW = AT_KV * AT_DH
ATT_BLOCK = 128

N_BRANCH = 3
BRANCH_W = 512

LANES = 128
SUBLANES = 8
VMEM_LIMIT = 56 * 1024 * 1024

P_RGX = 0
P_RGG = P_RGX + D_RNN
P_MQ = P_RGG + D_RNN
P_MK = P_MQ + ML_W
P_MV = P_MK + ML_W
P_MO = P_MV + ML_W
P_HEAD = P_MO + ML_W
N_GATE = 4 * ML_HEADS
T_AQ = 0
T_AK = T_AQ + AT_W
T_AV = T_AK + AT_KVW
T_BR = T_AV + AT_KVW


def _cparams(sem):
    return pltpu.CompilerParams(dimension_semantics=sem, vmem_limit_bytes=VMEM_LIMIT)


def _const_spec(shape, lead=()):
    nd = len(shape)
    idx = tuple(lead) + (0,) * nd
    return pl.BlockSpec((None,) * len(lead) + tuple(shape), lambda *_: idx,
                        pipeline_mode=pl.Buffered(1))


def _pick_tile(total, candidates):
    for c in candidates:
        if total % c == 0:
            return c
    raise ValueError(f"no tile for {total}")


def _sigmoid(x):
    return jax.nn.sigmoid(x)


def _ln_mod(x, g, shift, scale):
    ms = jnp.mean(x * x, axis=-1, keepdims=True)
    return x * lax.rsqrt(ms + EPS) * g * (1.0 + scale) + shift


def _row_mod(ml_ref, mc_ref, is_ctx, i):
    return jnp.where(is_ctx, mc_ref[i:i + 1, :], ml_ref[i:i + 1, :])


def _ada_kernel(cc_ref, w_ref, b_ref, o_ref):
    cc = cc_ref[...]
    s = cc * _sigmoid(cc)
    o_ref[...] = jnp.dot(s.astype(BF16), w_ref[...].astype(BF16),
                         preferred_element_type=F32) + b_ref[...]


def _ada_call(cc, ada_w, ada_b):
    depth, d, nout = ada_w.shape
    rows = cc.shape[0]
    tn = _pick_tile(nout, (1536, 1024, 512, 256, 128))
    return pl.pallas_call(
        _ada_kernel,
        grid=(depth, nout // tn),
        in_specs=[
            pl.BlockSpec((rows, d), lambda l, j: (0, 0)),
            pl.BlockSpec((None, d, tn), lambda l, j: (l, 0, j)),
            pl.BlockSpec((None, 1, tn), lambda l, j: (l, 0, j)),
        ],
        out_specs=pl.BlockSpec((None, rows, tn), lambda l, j: (l, 0, j)),
        out_shape=jax.ShapeDtypeStruct((depth, rows, nout), F32),
        compiler_params=_cparams(("arbitrary", "arbitrary")),
        name="ada_mod",
    )(cc, ada_w, ada_b.reshape(depth, 1, nout))


def _ffn_body(x, ml_ref, mc_ref, is_ctx, g_ref, w1_ref, w3_ref, w2_ref, act_ref, *, base, ck,
              stream=None):
    h = _ln_mod(x, g_ref[...], _row_mod(ml_ref, mc_ref, is_ctx, base),
                _row_mod(ml_ref, mc_ref, is_ctx, base + 1)).astype(BF16)
    dff = w1_ref.shape[1]
    nck = dff // ck
    if stream is not None:
        stream.finish(0)
    for j in range(nck):
        if stream is not None and j + 1 < nck:
            stream.finish(j + 1)
        a = jnp.dot(h, w1_ref[:, j * ck:(j + 1) * ck], preferred_element_type=F32)
        b = jnp.dot(h, w3_ref[:, j * ck:(j + 1) * ck], preferred_element_type=F32)
        act_ref[:, j * ck:(j + 1) * ck] = (a * _sigmoid(a) * b).astype(BF16)
    y = jnp.dot(act_ref[...], w2_ref[...], preferred_element_type=F32)
    return x + 0.5 * _row_mod(ml_ref, mc_ref, is_ctx, base + 2) * y


def _ffn_weight_scratch(d, dff, ck):
    return [pltpu.VMEM((d, dff), BF16), pltpu.VMEM((d, dff), BF16), pltpu.VMEM((dff, d), BF16),
            pltpu.VMEM((2, 2, d, ck), F32), pltpu.VMEM((2, ck, d), F32),
            pltpu.SemaphoreType.DMA((2, 3))]


class _FfnWeightStream:
    def __init__(self, hbm, lead, resident, stg_c, stg_r, sem, ck):
        self.hbm, self.lead, self.resident = hbm, lead, resident
        self.stg_c, self.stg_r, self.sem, self.ck = stg_c, stg_r, sem, ck
        self.nck = resident[0].shape[1] // ck

    def _stages(self, j):
        s = j % 2
        return (self.stg_c.at[s, 0], self.stg_c.at[s, 1], self.stg_r.at[s])

    def _copies(self, j):
        l0, l1 = self.lead
        sl = slice(j * self.ck, (j + 1) * self.ck)
        srcs = (self.hbm[0].at[l0, l1, :, sl], self.hbm[1].at[l0, l1, :, sl],
                self.hbm[2].at[l0, l1, sl, :])
        return [pltpu.make_async_copy(src, stg, self.sem.at[j % 2, i])
                for i, (src, stg) in enumerate(zip(srcs, self._stages(j)))]

    def start(self, j):
        for cp in self._copies(j):
            cp.start()

    def prime(self):
        self.start(0)
        if self.nck > 1:
            self.start(1)

    def finish(self, j):
        sl = slice(j * self.ck, (j + 1) * self.ck)
        for cp in self._copies(j):
            cp.wait()
        w1b, w3b, w2b = self.resident
        s1, s3, s2 = self._stages(j)
        w1b[:, sl] = s1[...].astype(BF16)
        w3b[:, sl] = s3[...].astype(BF16)
        w2b[sl, :] = s2[...].astype(BF16)
        if j + 2 < self.nck:
            self.start(j + 2)


def _ffn_kernel(x_ref, ml_ref, mc_ref, g_ref, w1_hbm, w3_hbm, w2_hbm, o_ref, act_ref,
                w1b, w3b, w2b, stg_c, stg_r, sem, *, lead, base, seq, tm, ck):
    t0 = pl.program_id(1) * tm
    rows = t0 + lax.broadcasted_iota(jnp.int32, (tm, 1), 0)
    is_ctx = rows >= seq
    first = jnp.logical_and(pl.program_id(0) == 0, pl.program_id(1) == 0)

    @pl.when(first)
    def _():
        stream = _FfnWeightStream((w1_hbm, w3_hbm, w2_hbm), lead, (w1b, w3b, w2b),
                                  stg_c, stg_r, sem, ck)
        stream.prime()
        o_ref[...] = _ffn_body(x_ref[...], ml_ref, mc_ref, is_ctx, g_ref, w1b, w3b, w2b,
                               act_ref, base=base, ck=ck, stream=stream)

    @pl.when(jnp.logical_not(first))
    def _():
        o_ref[...] = _ffn_body(x_ref[...], ml_ref, mc_ref, is_ctx, g_ref, w1b, w3b, w2b,
                               act_ref, base=base, ck=ck)


def _ffn_call(x, ml, mc, g, w1, w3, w2, *, lead, base, seq, out_rows):
    bsz, t, d = x.shape
    dff = w1.shape[-1]
    tm = _pick_tile(t, (768, 384, 128))
    ck = _pick_tile(dff, (256, 128))
    kern = functools.partial(_ffn_kernel, lead=lead, base=base, seq=seq, tm=tm, ck=ck)
    hbm = pl.BlockSpec(memory_space=pl.ANY)
    return pl.pallas_call(
        kern,
        grid=(bsz, pl.cdiv(out_rows, tm)),
        in_specs=[
            pl.BlockSpec((None, tm, d), lambda b, i: (b, i, 0)),
            pl.BlockSpec((None, N_MOD, d), lambda b, i: (b, 0, 0)),
            _const_spec((N_MOD, d)),
            _const_spec((1, d)),
            hbm, hbm, hbm,
        ],
        out_specs=pl.BlockSpec((None, tm, d), lambda b, i: (b, i, 0)),
        out_shape=jax.ShapeDtypeStruct((bsz, out_rows, d), F32),
        scratch_shapes=[pltpu.VMEM((tm, dff), BF16)] + _ffn_weight_scratch(d, dff, ck),
        compiler_params=_cparams(("arbitrary", "arbitrary")),
        name="ffn",
    )(x, ml, mc, g, w1, w3, w2)


def _gelu_tanh(x):
    return 0.5 * x * (1.0 + jnp.tanh(0.7978845608028654 * (x + 0.044715 * (x * x * x))))


def _head_norm(x, gmat, g):
    ms = jnp.dot((x * x).astype(BF16), gmat, preferred_element_type=F32)
    return x * lax.rsqrt(ms + EPS) * g


def _dup_heads(x):
    low = lax.broadcasted_iota(jnp.int32, x.shape, 1) < AT_DH
    sw = pltpu.roll(x, AT_DH, 1)
    return jnp.concatenate([jnp.where(low, x, sw), jnp.where(low, sw, x)], axis=1)


def _rope(x, cos, sin_a, sin_b):
    parts = []
    for j in range(x.shape[1] // LANES):
        xj = x[:, j * LANES:(j + 1) * LANES]
        parts.append(xj * cos + pltpu.roll(xj, LANES - 16, 1) * sin_a
                     + pltpu.roll(xj, 16, 1) * sin_b)
    return jnp.concatenate(parts, axis=1)


def _proj_kernel(x_ref, ml_ref, mc_ref, g_ref, wh_ref, wt_ref, gmat_ref, qg_ref, kg_ref, gb_ref,
                 cos_ref, sa_ref, sb_ref,
                 rgx_o, rgg_o, mq_o, mk_o, mv_o, mo_o, aq_o, ak_o, av_o, br_o, mg_o,
                 tail, *, seq, tm, d_model):
    @pl.when(jnp.logical_and(pl.program_id(0) == 0, pl.program_id(1) == 0))
    def _():
        for r in range(0, d_model, LANES):
            tail[r:r + LANES, :] = wt_ref[r:r + LANES, N_GATE:]

    t0 = pl.program_id(1) * tm
    rows = t0 + lax.broadcasted_iota(jnp.int32, (tm, 1), 0)
    is_ctx = rows >= seq
    h = _ln_mod(x_ref[...], g_ref[...], _row_mod(ml_ref, mc_ref, is_ctx, 3),
                _row_mod(ml_ref, mc_ref, is_ctx, 4)).astype(BF16)

    def seg(c0, w, ref=wh_ref):
        return jnp.dot(h, ref[:, c0:c0 + w], preferred_element_type=F32)

    rgx_o[...] = seg(P_RGX, D_RNN).reshape(rgx_o.shape)
    rgg_o[...] = _gelu_tanh(seg(P_RGG, D_RNN)).astype(BF16)
    mq_o[...] = seg(P_MQ, ML_W).astype(BF16)
    mk_o[...] = (seg(P_MK, ML_W) * (ML_DH ** -0.5)).astype(BF16)
    mv_o[...] = seg(P_MV, ML_W).astype(BF16)
    mo_o[...] = _sigmoid(seg(P_MO, ML_W)).astype(BF16)
    cos, sin_a, sin_b = cos_ref[...], sa_ref[...], sb_ref[...]
    gmat = gmat_ref[...]
    q = _head_norm(seg(T_AQ, AT_W, tail), gmat, qg_ref[...])
    aq_o[...] = (_rope(q, cos, sin_a, sin_b) * (AT_DH ** -0.5 * LOG2E)).astype(BF16)
    k = _head_norm(seg(T_AK, AT_KVW, tail), gmat[:AT_KVW, :AT_KVW], kg_ref[...])
    ak_o[...] = _dup_heads(_rope(k, cos, sin_a, sin_b)).astype(BF16)
    av_o[...] = _dup_heads(seg(T_AV, AT_KVW, tail)).astype(BF16)
    for c0 in range(0, N_BRANCH * d_model, 512):
        br_o[:, c0:c0 + 512] = _sigmoid(seg(T_BR + c0, 512, tail)).astype(BF16)
    mg_o[...] = seg(0, LANES, wt_ref) + gb_ref[...]


def _proj_call(x, ml, mc, g, wh, wt, gmat, qg, kg, gb, cos, sin_a, sin_b, *, layer, seq):
    bsz, t, d = x.shape
    ntail = wt.shape[-1] - N_GATE
    tm = _pick_tile(t, (384, 128))
    kern = functools.partial(_proj_kernel, seq=seq, tm=tm, d_model=d)
    widths = [(D_RNN, F32), (D_RNN, BF16), (ML_W, BF16), (ML_W, BF16), (ML_W, BF16),
              (ML_W, BF16), (AT_W, BF16), (2 * AT_KVW, BF16), (2 * AT_KVW, BF16),
              (N_BRANCH * d, BF16), (LANES, F32)]
    tab_spec = pl.BlockSpec((tm, LANES), lambda b, i: (i, 0))
    out_specs = [pl.BlockSpec((None, tm, wd), lambda b, i: (b, i, 0)) for wd, _ in widths]
    out_shape = [jax.ShapeDtypeStruct((bsz, t, wd), dt) for wd, dt in widths]
    out_specs[0] = pl.BlockSpec((tm // SUBLANES, None, SUBLANES, D_RNN), lambda b, i: (i, b, 0, 0))
    out_shape[0] = jax.ShapeDtypeStruct((t // SUBLANES, bsz, SUBLANES, D_RNN), F32)
    return pl.pallas_call(
        kern,
        grid=(bsz, t // tm),
        in_specs=[
            pl.BlockSpec((None, tm, d), lambda b, i: (b, i, 0)),
            pl.BlockSpec((None, N_MOD, d), lambda b, i: (b, 0, 0)),
            _const_spec((N_MOD, d)),
            _const_spec((1, d)),
            _const_spec((d, wh.shape[-1]), (layer,)),
            _const_spec((d, wt.shape[-1]), (layer,)),
            _const_spec((AT_W, AT_W)),
            _const_spec((1, AT_W)),
            _const_spec((1, AT_KVW)),
            _const_spec((1, LANES)),
            tab_spec, tab_spec, tab_spec,
        ],
        out_specs=out_specs,
        out_shape=out_shape,
        scratch_shapes=[pltpu.VMEM((d, ntail), BF16)],
        compiler_params=_cparams(("arbitrary", "arbitrary")),
        name="in_proj",
    )(x, ml, mc, g, wh, wt, gmat, qg, kg, gb, cos, sin_a, sin_b)


def _scan_chunk(s, rev, nl, nc):
    if rev:
        return jnp.where(s < nc, nl + nc - 1 - s, nl - 1 - (s - nc))
    return jnp.where(s < nc, nl + s, s - nc)


def _rglru_kernel(xf_ref, xfp_ref, xfn_ref, xb_ref, xbp_ref, xbn_ref, cw_ref, cb_ref, w_ref,
                  bias_ref, clam_ref, hf_ref, hb_ref, ext, a_s, b_s, h_s, hstate,
                  *, nl, nc, tc, sub):
    s = pl.program_id(0)
    bsz, c = hstate.shape[1], hstate.shape[2]
    ntb = tc // SUBLANES

    @pl.when(s == 0)
    def _():
        hstate[...] = jnp.zeros_like(hstate)

    cw = cw_ref[...]
    cb = cb_ref[...]
    for d, (x_ref, xp_ref, xn_ref) in enumerate(((xf_ref, xfp_ref, xfn_ref),
                                                (xb_ref, xbp_ref, xbn_ref))):
        chunk = _scan_chunk(s, d == 1, nl, nc)
        first = jnp.logical_or(chunk == 0, chunk == nl)
        last = jnp.logical_or(chunk == nl - 1, chunk == nl + nc - 1)
        prev = jnp.swapaxes(xp_ref[0], 0, 1)[SUBLANES - CONV_LEFT:]
        nxt = jnp.swapaxes(xn_ref[0], 0, 1)[:CONV_RIGHT]
        ext[d, 0:CONV_LEFT] = jnp.where(first, 0.0, prev)
        for i in range(ntb):
            o = CONV_LEFT + i * SUBLANES
            ext[d, o:o + SUBLANES] = jnp.swapaxes(x_ref[i], 0, 1)
        ext[d, CONV_LEFT + tc:CONV_LEFT + tc + CONV_RIGHT] = jnp.where(last, 0.0, nxt)

        def gates(i, carry, d=d):
            t0 = pl.multiple_of(i * sub, sub)
            u = cb
            for k in range(CONV_W):
                u = u + ext[d, pl.ds(t0 + k, sub)] * cw[k:k + 1, :]
            u2 = u.reshape(sub * bsz, c)
            z = jnp.dot(u2.astype(BF16), w_ref[d], preferred_element_type=F32) + bias_ref[d]
            a = jnp.exp2(clam_ref[d] * jnp.tanh(z[:, :c]) + clam_ref[d])
            gi = 0.5 * jnp.tanh(z[:, c:]) + 0.5
            s1 = 1.0 - a * a
            bb = (s1 * lax.rsqrt(jnp.maximum(s1, 1e-36))) * (gi * u2)
            a_s[d, pl.ds(t0, sub)] = a.reshape(sub, bsz, c)
            b_s[d, pl.ds(t0, sub)] = bb.reshape(sub, bsz, c)
            return carry

        lax.fori_loop(0, tc // sub, gates, 0)

    def step(k, hs):
        hf, hb = hs
        kb = tc - 1 - k
        hf = a_s[0, k] * hf + b_s[0, k]
        hb = a_s[1, kb] * hb + b_s[1, kb]
        h_s[0, k] = hf
        h_s[1, kb] = hb
        return hf, hb

    hf, hb = lax.fori_loop(0, tc, step, (hstate[0], hstate[1]), unroll=8)
    hstate[0] = hf
    hstate[1] = hb
    for d, o_ref in enumerate((hf_ref, hb_ref)):
        for i in range(ntb):
            o_ref[i] = jnp.swapaxes(h_s[d, i * SUBLANES:(i + 1) * SUBLANES], 0, 1)


def _rglru_call(rgx, cw, cb, w, bias, clam, *, seq):
    ntb_all, bsz, _, c = rgx.shape
    t = ntb_all * SUBLANES
    tc = 128
    sub = 32
    assert seq % tc == 0 and (t - seq) % tc == 0
    nl, nc = seq // tc, (t - seq) // tc
    ntb = tc // SUBLANES

    def specs(rev):
        chunk = functools.partial(_scan_chunk, rev=rev, nl=nl, nc=nc)
        return [
            pl.BlockSpec((ntb, bsz, SUBLANES, c), lambda s: (chunk(s), 0, 0, 0)),
            pl.BlockSpec((1, bsz, SUBLANES, c),
                         lambda s: (jnp.maximum(chunk(s) * ntb - 1, 0), 0, 0, 0)),
            pl.BlockSpec((1, bsz, SUBLANES, c),
                         lambda s: (jnp.minimum((chunk(s) + 1) * ntb, ntb_all - 1), 0, 0, 0)),
        ]

    kern = functools.partial(_rglru_kernel, nl=nl, nc=nc, tc=tc, sub=sub)
    fspec, bspec = specs(False), specs(True)
    return pl.pallas_call(
        kern,
        grid=(nl + nc,),
        in_specs=fspec + bspec + [
            _const_spec((CONV_W, c)),
            _const_spec((1, c)),
            _const_spec((2, c, 2 * c)),
            _const_spec((2, 1, 2 * c)),
            _const_spec((2, 1, c)),
        ],
        out_specs=[fspec[0], bspec[0]],
        out_shape=[jax.ShapeDtypeStruct(rgx.shape, F32)] * 2,
        scratch_shapes=[
            pltpu.VMEM((2, tc + CONV_W - 1, bsz, c), F32),
            pltpu.VMEM((2, tc, bsz, c), F32),
            pltpu.VMEM((2, tc, bsz, c), F32),
            pltpu.VMEM((2, tc, bsz, c), F32),
            pltpu.VMEM((2, bsz, c), F32),
        ],
        compiler_params=_cparams(("arbitrary",)),
        name="rglru",
    )(rgx, rgx, rgx, rgx, rgx, rgx, cw, cb, w, bias, clam)


def _log_sigmoid(x):
    return -(jnp.maximum(-x, 0.0) + jnp.log1p(jnp.exp(-jnp.abs(x))))


def _lane_scan(x, op, ident, rev):
    n = x.shape[1]
    lane = lax.broadcasted_iota(jnp.int32, x.shape, 1)
    sh = 1
    while sh < n:
        if rev:
            x = op(x, jnp.where(lane < n - sh, pltpu.roll(x, n - sh, 1), ident))
        else:
            x = op(x, jnp.where(lane >= sh, pltpu.roll(x, sh, 1), ident))
        sh *= 2
    return x


def _mlgate_kernel(g_ref, row_o, col_o, cb_s, gg_s, ml_s, tot_s, wm_s, mo_s, mn_s, *, nl, nc):
    L = ML_CHUNK
    H = ML_HEADS
    U = ML_UNITS
    is_bwd = lax.broadcasted_iota(jnp.int32, (U, L), 0) >= H
    for c in range(nl + nc):
        lanes = slice(c * L, (c + 1) * L)
        g16 = g_ref[:, lanes]
        li = jnp.concatenate([g16[0:H], g16[2 * H:3 * H]], axis=0)
        lf = _log_sigmoid(jnp.concatenate([g16[H:2 * H], g16[3 * H:4 * H]], axis=0))
        pre = _lane_scan(lf, jnp.add, 0.0, False)
        suf = _lane_scan(lf, jnp.add, 0.0, True)
        tot = pre + suf - lf
        cb = jnp.where(is_bwd, suf, pre)
        gg = li - cb
        pmax = _lane_scan(gg, jnp.maximum, NEG, False)
        smax = _lane_scan(gg, jnp.maximum, NEG, True)
        cb_s[:, lanes] = cb
        gg_s[:, lanes] = gg
        ml_s[:, lanes] = cb + jnp.where(is_bwd, smax, pmax)
        tot_s[:, lanes] = tot
        wm_s[:, lanes] = tot + jnp.maximum(pmax, smax)

    m = jnp.zeros((U, L), F32)
    for s in range(nl + nc):
        cf = nl + s if s < nc else s - nc
        cr = nl + nc - 1 - s if s < nc else nl - 1 - (s - nc)
        lf_, lr_ = slice(cf * L, (cf + 1) * L), slice(cr * L, (cr + 1) * L)
        tot = jnp.where(is_bwd, tot_s[:, lr_], tot_s[:, lf_])
        wm = jnp.where(is_bwd, wm_s[:, lr_], wm_s[:, lf_])
        m_new = jnp.maximum(tot + m, wm)
        mo_s[0:H, lf_] = m[0:H]
        mo_s[H:U, lr_] = m[H:U]
        mn_s[0:H, lf_] = m_new[0:H]
        mn_s[H:U, lr_] = m_new[H:U]
        m = m_new

    cb = cb_s[...]
    gg = gg_s[...]
    tot = tot_s[...]
    mo = mo_s[...]
    mn = mn_s[...]
    m_t = jnp.maximum(cb + mo, ml_s[...])
    row_o[0:U, :] = gg
    row_o[U:2 * U, :] = jnp.exp(tot + gg - mn)
    row_o[2 * U:3 * U, :] = jnp.exp(tot + mo - mn)
    row_o[3 * U:4 * U, :] = mo
    col_o[0:U, :] = cb - m_t
    col_o[U:2 * U, :] = jnp.exp(-m_t)


def _mlgate_call(gt, *, seq):
    bsz, ng, t = gt.shape
    nl, nc = seq // ML_CHUNK, (t - seq) // ML_CHUNK
    kern = functools.partial(_mlgate_kernel, nl=nl, nc=nc)
    rout = pl.BlockSpec((None, 4 * ML_UNITS, t), lambda b: (b, 0, 0))
    cout = pl.BlockSpec((None, 2 * ML_UNITS, t), lambda b: (b, 0, 0))
    return pl.pallas_call(
        kern,
        grid=(bsz,),
        in_specs=[pl.BlockSpec((None, ng, t), lambda b: (b, 0, 0))],
        out_specs=[rout, cout],
        out_shape=[jax.ShapeDtypeStruct((bsz, 4 * ML_UNITS, t), F32),
                   jax.ShapeDtypeStruct((bsz, 2 * ML_UNITS, t), F32)],
        scratch_shapes=[pltpu.VMEM((ML_UNITS, t), F32)] * 7,
        compiler_params=_cparams(("arbitrary",)),
        name="mlstm_gates",
    )(gt)


def _mlstm_kernel(qf_ref, ktf_ref, vf_ref, rf_ref, cf_ref, qb_ref, ktb_ref, vb_ref, rb_ref, cb_ref,
                  hf_ref, hb_ref, c_st):
    s = pl.program_id(1)

    @pl.when(s == 0)
    def _():
        c_st[...] = jnp.zeros_like(c_st)

    L, dh, U = ML_CHUNK, ML_DH, ML_UNITS
    ti = lax.broadcasted_iota(jnp.int32, (L, L), 0)
    si = lax.broadcasted_iota(jnp.int32, (L, L), 1)
    ones = jnp.ones((L, dh), BF16)
    for bb in range(qf_ref.shape[0]):
        for d, (q_ref, kt_ref, v_ref, r_ref, c_ref, o_ref) in enumerate(
                ((qf_ref, ktf_ref, vf_ref, rf_ref, cf_ref, hf_ref),
                 (qb_ref, ktb_ref, vb_ref, rb_ref, cb_ref, hb_ref))):
            tri = (si >= ti) if d == 1 else (si <= ti)
            rows = r_ref[bb]
            cols = c_ref[bb]
            for hd in range(ML_HEADS):
                u = d * ML_HEADS + hd
                sl = slice(hd * dh, (hd + 1) * dh)
                q = q_ref[bb, :, sl]
                kt = kt_ref[bb, sl, :]
                v_ext = jnp.concatenate([v_ref[bb, :, sl], ones], axis=1)
                g_row = rows[u:u + 1, :]
                w_row = rows[U + u:U + u + 1, :]
                a_row = rows[2 * U + u:2 * U + u + 1, :]
                m_row = rows[3 * U + u:3 * U + u + 1, :]
                xb = jnp.broadcast_to(cols[:, u:u + 1], (L, L))
                emt = cols[:, U + u:U + u + 1]
                c_old = c_st[bb * U + u]
                qkc = jnp.dot(q, jnp.concatenate([kt, c_old.astype(BF16)], axis=1),
                              preferred_element_type=F32)
                dm = jnp.exp(jnp.where(tri, xb + g_row, NEG))
                dec = jnp.exp(xb + m_row)
                sc = (qkc[:, :L] * dm).astype(BF16)
                ktw = (kt.astype(F32) * w_row).astype(BF16)
                sv = jnp.dot(jnp.concatenate([sc, ktw], axis=0), v_ext,
                             preferred_element_type=F32)
                ab = sv[:L] + jnp.concatenate([dec, dec], axis=1) * qkc[:, L:]
                o_ref[bb, :, sl] = ab[:, :dh] / jnp.maximum(jnp.abs(ab[:, dh:]), emt)
                c_st[bb * U + u] = jnp.concatenate([a_row, a_row], axis=1) * c_old + sv[L:]


def _mlstm_call(mq, mkt, mv, rowp, colp, *, seq):
    bsz, t, w = mq.shape
    nl, nc = seq // ML_CHUNK, (t - seq) // ML_CHUNK
    cf = functools.partial(_scan_chunk, rev=False, nl=nl, nc=nc)
    cr = functools.partial(_scan_chunk, rev=True, nl=nl, nc=nc)
    nr = rowp.shape[1]
    bpb = 8 if bsz % 8 == 0 else 1

    def tok(ch, wd):
        return pl.BlockSpec((bpb, ML_CHUNK, wd), lambda b, s: (b, ch(s), 0))

    def tr(ch, rows):
        return pl.BlockSpec((bpb, rows, ML_CHUNK), lambda b, s: (b, 0, ch(s)))

    def side(ch):
        return [tok(ch, w), tr(ch, w), tok(ch, w), tr(ch, nr), tok(ch, colp.shape[2])]

    return pl.pallas_call(
        _mlstm_kernel,
        grid=(bsz // bpb, nl + nc),
        in_specs=side(cf) + side(cr),
        out_specs=[tok(cf, w), tok(cr, w)],
        out_shape=[jax.ShapeDtypeStruct((bsz, t, w), F32)] * 2,
        scratch_shapes=[pltpu.VMEM((bpb * ML_UNITS, ML_DH, 2 * ML_DH), F32)],
        compiler_params=_cparams(("arbitrary", "arbitrary")),
        name="mlstm",
    )(mq, mkt, mv, rowp, colp, mq, mkt, mv, rowp, colp)


def _attn_kernel(sink_ref, q_ref, k_ref, v_ref, o_ref, vext, *, seq, qbs):
    kv = pl.program_id(1)
    i = pl.program_id(2)
    blk = ATT_BLOCK
    t = k_ref.shape[0]
    ctx = t - seq
    nlb = seq // blk
    band = 3 * blk

    @pl.when(i == 0)
    def _():
        vext[:, 0:LANES] = v_ref[...]
        vext[:, LANES:2 * LANES] = jnp.ones((t, LANES), BF16)

    lane = lax.broadcasted_iota(jnp.int32, (blk, LANES), 1)
    low = lane < AT_DH
    zero = jnp.zeros((blk, LANES), BF16)
    rows = AT_G * blk
    gi = lax.broadcasted_iota(jnp.int32, (rows, 1), 0) // blk
    snk = jnp.zeros((rows, 1), F32)
    for g in range(AT_G):
        snk = jnp.where(gi == g, sink_ref[kv * AT_G + g] * LOG2E, snk)
    ti = lax.broadcasted_iota(jnp.int32, (blk, band), 0)
    ci = lax.broadcasted_iota(jnp.int32, (blk, band), 1)
    nt = (((1,), (1,)), ((), ()))
    k_ctx = k_ref[seq:t, :]
    v_ctx = vext[seq:t, :]

    for j in range(qbs):
        qb = i * qbs + j
        s0 = pl.multiple_of(jnp.clip((qb - 1) * blk, 0, t - band), blk)
        rel = ci - ti + (s0 - qb * blk + blk)
        lim = jnp.where(qb < nlb, seq - s0, 0)
        ok = jnp.logical_and(jnp.logical_and(rel >= 0, rel <= 2 * blk), ci < lim)
        bias = jnp.where(ok, 0.0, NEG)
        bias = jnp.concatenate([bias] * AT_G, axis=0)

        qs = []
        for h2 in range(AT_G // 2):
            qj = q_ref[j * blk:(j + 1) * blk, h2 * LANES:(h2 + 1) * LANES]
            qs += [jnp.where(low, qj, zero), jnp.where(low, zero, qj)]
        q_all = jnp.concatenate(qs, axis=0)
        s_band = lax.dot_general(q_all, k_ref[pl.ds(s0, band), :], nt,
                                 preferred_element_type=F32) + bias
        s_ctx = lax.dot_general(q_all, k_ctx, nt, preferred_element_type=F32)
        m = jnp.maximum(jnp.maximum(jnp.max(s_band, axis=1, keepdims=True),
                                    jnp.max(s_ctx, axis=1, keepdims=True)), snk)
        p_band = jnp.exp2(s_band - m).astype(BF16)
        p_ctx = jnp.exp2(s_ctx - m).astype(BF16)
        o2 = (jnp.dot(p_band, vext[pl.ds(s0, band), :], preferred_element_type=F32)
              + jnp.dot(p_ctx, v_ctx, preferred_element_type=F32))
        o2 = o2[:, :LANES] / (o2[:, LANES:] + jnp.exp2(snk - m))
        for h2 in range(AT_G // 2):
            a = o2[(2 * h2) * blk:(2 * h2 + 1) * blk, :]
            b = o2[(2 * h2 + 1) * blk:(2 * h2 + 2) * blk, :]
            o_ref[j * blk:(j + 1) * blk, h2 * LANES:(h2 + 1) * LANES] = (
                jnp.where(low, a, b).astype(o_ref.dtype))


def _attn_call(sink, aq, akd, avd, *, seq):
    bsz, t, _ = aq.shape
    blk = ATT_BLOCK
    qbs = 3
    assert t % (qbs * blk) == 0 and t >= 3 * blk
    gw = AT_G * AT_DH
    kern = functools.partial(_attn_kernel, seq=seq, qbs=qbs)
    whole = pl.BlockSpec((None, t, LANES), lambda b, kv, i: (b, 0, kv))
    return pl.pallas_call(
        kern,
        grid=(bsz, AT_KV, t // (qbs * blk)),
        in_specs=[
            pl.BlockSpec(memory_space=pltpu.SMEM),
            pl.BlockSpec((None, qbs * blk, gw), lambda b, kv, i: (b, i, kv)),
            whole, whole,
        ],
        out_specs=pl.BlockSpec((None, qbs * blk, gw), lambda b, kv, i: (b, i, kv)),
        out_shape=jax.ShapeDtypeStruct((bsz, t, AT_W), BF16),
        scratch_shapes=[pltpu.VMEM((t, 2 * LANES), BF16)],
        compiler_params=_cparams(("arbitrary", "arbitrary", "arbitrary")),
        name="attn",
    )(sink, aq, akd, avd)


def _merge_kernel(x_ref, ml_ref, mc_ref, rf_ref, rb_ref, gg_ref, hf_ref, hb_ref, mo_ref, yc_ref,
                  br_ref, ng_ref, wb_ref, wo_ref, g_ref, w1_hbm, w3_hbm, w2_hbm, o_ref, act_ref,
                  w1b, w3b, w2b, stg_c, stg_r, sem, *, lead, seq, tm, ck):
    first = jnp.logical_and(pl.program_id(0) == 0, pl.program_id(1) == 0)
    stream = _FfnWeightStream((w1_hbm, w3_hbm, w2_hbm), lead, (w1b, w3b, w2b),
                              stg_c, stg_r, sem, ck)

    @pl.when(first)
    def _():
        stream.prime()

    t0 = pl.program_id(1) * tm
    rows = t0 + lax.broadcasted_iota(jnp.int32, (tm, 1), 0)
    is_ctx = rows >= seq
    d = x_ref.shape[1]
    rsum = (rf_ref[...] + rb_ref[...]).reshape(tm, D_RNN)
    ya = (rsum * gg_ref[...].astype(F32)).astype(BF16)
    hsum = hf_ref[...] + hb_ref[...]
    parts = []
    for hd in range(ML_HEADS):
        hh = hsum[:, hd * ML_DH:(hd + 1) * ML_DH]
        parts.append(hh * lax.rsqrt(jnp.mean(hh * hh, axis=-1, keepdims=True) + EPS))
    yb = (jnp.concatenate(parts, axis=1) * ng_ref[...] * mo_ref[...].astype(F32)).astype(BF16)
    m = (br_ref[:, 0:d].astype(F32) * jnp.dot(ya, wb_ref[0], preferred_element_type=F32)
         + br_ref[:, d:2 * d].astype(F32) * jnp.dot(yb, wb_ref[1], preferred_element_type=F32)
         + br_ref[:, 2 * d:3 * d].astype(F32) * jnp.dot(yc_ref[...], wb_ref[2],
                                                       preferred_element_type=F32))
    y = jnp.dot(m.astype(BF16), wo_ref[...], preferred_element_type=F32)
    x1 = x_ref[...] + _row_mod(ml_ref, mc_ref, is_ctx, 5) * y
    @pl.when(first)
    def _():
        o_ref[...] = _ffn_body(x1, ml_ref, mc_ref, is_ctx, g_ref, w1b, w3b, w2b, act_ref,
                               base=6, ck=ck, stream=stream)

    @pl.when(jnp.logical_not(first))
    def _():
        o_ref[...] = _ffn_body(x1, ml_ref, mc_ref, is_ctx, g_ref, w1b, w3b, w2b, act_ref,
                               base=6, ck=ck)


def _merge_call(x, ml, mc, rf, rb, gg, hf, hb, mo, yc, br, ng, wb, wo, g, w1, w3, w2,
                *, layer, seq, out_rows):
    bsz, t, d = x.shape
    dff = w1.shape[-1]
    tm = _pick_tile(t, (384, 128))
    ck = _pick_tile(dff, (256, 128))
    kern = functools.partial(_merge_kernel, lead=(layer, 1), seq=seq, tm=tm, ck=ck)
    hbm = pl.BlockSpec(memory_space=pl.ANY)

    def tok(wd):
        return pl.BlockSpec((None, tm, wd), lambda b, i: (b, i, 0))

    rspec = pl.BlockSpec((tm // SUBLANES, None, SUBLANES, D_RNN), lambda b, i: (i, b, 0, 0))

    return pl.pallas_call(
        kern,
        grid=(bsz, pl.cdiv(out_rows, tm)),
        in_specs=[
            tok(d),
            pl.BlockSpec((None, N_MOD, d), lambda b, i: (b, 0, 0)),
            _const_spec((N_MOD, d)),
            rspec, rspec, tok(D_RNN), tok(ML_W), tok(ML_W), tok(ML_W), tok(AT_W),
            tok(N_BRANCH * d),
            _const_spec((1, ML_W)),
            _const_spec((N_BRANCH, BRANCH_W, d), (layer,)),
            _const_spec((d, d), (layer,)),
            _const_spec((1, d)),
            hbm, hbm, hbm,
        ],
        out_specs=tok(d),
        out_shape=jax.ShapeDtypeStruct((bsz, out_rows, d), F32),
        scratch_shapes=[pltpu.VMEM((tm, dff), BF16)] + _ffn_weight_scratch(d, dff, ck),
        compiler_params=_cparams(("arbitrary", "arbitrary")),
        name="merge_ffn",
    )(x, ml, mc, rf, rb, gg, hf, hb, mo, yc, br, ng, wb, wo, g, w1, w3, w2)


def _block_diag(w):
    n, bi, bj = w.shape
    eye = jnp.eye(n, dtype=w.dtype)
    return (eye[:, None, :, None] * w[:, :, None, :]).reshape(n * bi, n * bj)


def _rope_tables(seq, t):
    rows = seq // GRID_W
    row = jnp.repeat(jnp.arange(rows), GRID_W).astype(F32)
    col = jnp.broadcast_to(jnp.arange(GRID_W), (rows, GRID_W)).reshape(-1).astype(F32)
    half = AT_DH // 2
    inv = ROPE_BASE ** (-jnp.arange(0, half, 2, dtype=F32) / half)
    ar = row[:, None] * inv
    ac = col[:, None] * inv
    ang = jnp.concatenate([ar, ar, ac, ac], axis=-1)
    cos = jnp.concatenate([jnp.cos(ang), jnp.ones((t - seq, AT_DH), F32)], axis=0)
    sin = jnp.concatenate([jnp.sin(ang), jnp.zeros((t - seq, AT_DH), F32)], axis=0)
    cos = jnp.tile(cos, (1, LANES // AT_DH))
    sin = jnp.tile(sin, (1, LANES // AT_DH))
    first_half = (jnp.arange(LANES) % 32) < 16
    sin_a = jnp.where(first_half, -sin, 0.0)
    sin_b = jnp.where(first_half, 0.0, sin)
    return cos, sin_a, sin_b


def kernel(x, c, ctx, c_ctx, ada_w, ada_b, norm_g, ffn_w1, ffn_w3, ffn_w2, w_in, rg_conv_w,
           rg_conv_b, rg_wa, rg_ba, rg_wi, rg_bi, rg_lam, ml_gate_b, ml_norm_g, at_qn_g,
           at_kn_g, at_sink, w_branch, w_out):
    bsz, seq, d = x.shape
    nctx = ctx.shape[1]
    t = seq + nctx
    depth = ada_w.shape[0]

    xs = jnp.concatenate([x, ctx], axis=1)
    mod_rows = 2 * SUBLANES
    cc = jnp.concatenate([c, c_ctx[None, :], jnp.zeros((mod_rows - bsz - 1, d), F32)], axis=0)
    mods = _ada_call(cc, ada_w, ada_b).reshape(depth, mod_rows, N_MOD, d)
    cos, sin_a, sin_b = _rope_tables(seq, t)
    gmat = _block_diag(jnp.full((AT_HEADS, AT_DH, AT_DH), 1.0 / AT_DH, F32)).astype(BF16)
    w1, w3, w2 = ffn_w1, ffn_w3, ffn_w2
    w_head, w_tail = w_in[..., :P_HEAD].astype(BF16), w_in[..., P_HEAD:].astype(BF16)
    w_br, w_o = w_branch.astype(BF16), w_out.astype(BF16)

    for l in range(depth):
        ml = mods[l, :bsz]
        mc = mods[l, bsz]
        last = l == depth - 1

        xs = _ffn_call(xs, ml, mc, norm_g[l, 0][None, :], w1, w3, w2,
                       lead=(l, 0), base=0, seq=seq, out_rows=t)

        qg = jnp.tile(at_qn_g[l], AT_HEADS)[None, :]
        kg = jnp.tile(at_kn_g[l], AT_KV)[None, :]
        gb = jnp.concatenate([ml_gate_b[l], jnp.zeros((LANES - N_GATE,), F32)])[None, :]
        (rgx, rgg, mq, mk, mv, mo, aq, akd, avd, br, mg) = _proj_call(
            xs, ml, mc, norm_g[l, 1][None, :], w_head, w_tail, gmat, qg, kg, gb,
            cos, sin_a, sin_b, layer=l, seq=seq)

        wcat = jnp.stack([jnp.concatenate([_block_diag(rg_wa[l, dr]), _block_diag(rg_wi[l, dr])],
                                          axis=1) for dr in range(2)])
        wcat = (0.5 * wcat).astype(BF16)
        bias = 0.5 * jnp.concatenate([rg_ba[l], rg_bi[l]], axis=1)[:, None, :]
        clam = (-0.5 * LOG2E * LRU_C * jax.nn.softplus(-rg_lam[l]))[:, None, :]
        rhf, rhb = _rglru_call(rgx, rg_conv_w[l], rg_conv_b[l][None, :], wcat, bias, clam,
                               seq=seq)

        gt = jnp.swapaxes(mg[:, :, :N_GATE], 1, 2)
        rowp, colsrc = _mlgate_call(gt, seq=seq)
        colp = jnp.swapaxes(colsrc, 1, 2)
        mkt = jnp.swapaxes(mk, 1, 2)
        mhf, mhb = _mlstm_call(mq, mkt, mv, rowp, colp, seq=seq)

        yc = _attn_call(at_sink[l], aq, akd, avd, seq=seq)

        xs = _merge_call(xs, ml, mc, rhf, rhb, rgg, mhf, mhb, mo, yc, br, ml_norm_g[l][None, :],
                         w_br, w_o, norm_g[l, 2][None, :], w1, w3, w2,
                         layer=l, seq=seq, out_rows=seq if last else t)
    return xs
```

```python
import functools

import jax
import jax.numpy as jnp
from jax import lax
from jax.experimental import pallas as pl
from jax.experimental.pallas import tpu as pltpu

F32 = jnp.float32
BF16 = jnp.bfloat16

EPS = 1e-6
NEG = -1e30
LOG2E = 1.4426950408889634
N_MOD = 9
GRID_W = 64
ROPE_BASE = 10000.0

D_RNN = 512
RNN_BLOCKS = 8
RNN_BLOCK = D_RNN // RNN_BLOCKS
CONV_W = 4
CONV_LEFT = 2
CONV_RIGHT = CONV_W - 1 - CONV_LEFT
LRU_C = 8.0

ML_HEADS = 4
ML_DH = 128
ML_W = ML_HEADS * ML_DH
ML_CHUNK = 128
ML_UNITS = 2 * ML_HEADS

AT_HEADS = 8
AT_KV = 2
AT_DH = 64
AT_G = AT_HEADS // AT_KV
AT_W = AT_HEADS * AT_DH
AT_KVW = AT_KV * AT_DH
ATT_BLOCK = 128

N_BRANCH = 3
BRANCH_W = 512

LANES = 128
SUBLANES = 8
VMEM_LIMIT = 56 * 1024 * 1024

P_RGX = 0
P_RGG = P_RGX + D_RNN
P_MQ = P_RGG + D_RNN
P_MK = P_MQ + ML_W
P_MV = P_MK + ML_W
P_MO = P_MV + ML_W
P_HEAD = P_MO + ML_W
N_GATE = 4 * ML_HEADS
T_AQ = 0
T_AK = T_AQ + AT_W
T_AV = T_AK + AT_KVW
T_BR = T_AV + AT_KVW


def _cparams(sem):
    return pltpu.CompilerParams(dimension_semantics=sem, vmem_limit_bytes=VMEM_LIMIT)


def _const_spec(shape, lead=()):
    nd = len(shape)
    idx = tuple(lead) + (0,) * nd
    return pl.BlockSpec((None,) * len(lead) + tuple(shape), lambda *_: idx,
                        pipeline_mode=pl.Buffered(1))


def _pick_tile(total, candidates):
    for c in candidates:
        if total % c == 0:
            return c
    raise ValueError(f"no tile for {total}")


def _sigmoid(x):
    return jax.nn.sigmoid(x)


def _ln_mod(x, g, shift, scale):
    ms = jnp.mean(x * x, axis=-1, keepdims=True)
    return x * lax.rsqrt(ms + EPS) * g * (1.0 + scale) + shift


def _row_mod(ml_ref, mc_ref, is_ctx, i):
    return jnp.where(is_ctx, mc_ref[i:i + 1, :], ml_ref[i:i + 1, :])


def _ada_kernel(cc_ref, w_ref, b_ref, o_ref):
    cc = cc_ref[...]
    s = cc * _sigmoid(cc)
    o_ref[...] = jnp.dot(s.astype(BF16), w_ref[...].astype(BF16),
                         preferred_element_type=F32) + b_ref[...]


def _ada_call(cc, ada_w, ada_b):
    depth, d, nout = ada_w.shape
    rows = cc.shape[0]
    tn = _pick_tile(nout, (1536, 1024, 512, 256, 128))
    return pl.pallas_call(
        _ada_kernel,
        grid=(depth, nout // tn),
        in_specs=[
            pl.BlockSpec((rows, d), lambda l, j: (0, 0)),
            pl.BlockSpec((None, d, tn), lambda l, j: (l, 0, j)),
            pl.BlockSpec((None, 1, tn), lambda l, j: (l, 0, j)),
        ],
        out_specs=pl.BlockSpec((None, rows, tn), lambda l, j: (l, 0, j)),
        out_shape=jax.ShapeDtypeStruct((depth, rows, nout), F32),
        compiler_params=_cparams(("arbitrary", "arbitrary")),
        name="ada_mod",
    )(cc, ada_w, ada_b.reshape(depth, 1, nout))


def _ffn_body(x, ml_ref, mc_ref, is_ctx, g_ref, w1_ref, w3_ref, w2_ref, act_ref, *, base, ck,
              stream=None):
    h = _ln_mod(x, g_ref[...], _row_mod(ml_ref, mc_ref, is_ctx, base),
                _row_mod(ml_ref, mc_ref, is_ctx, base + 1)).astype(BF16)
    dff = w1_ref.shape[1]
    nck = dff // ck
    if stream is not None:
        stream.land(0)
        stream.refill(0)
    for j in range(nck):
        if stream is not None and j + 1 < nck:
            stream.land(j + 1)
        a = jnp.dot(h, w1_ref[:, j * ck:(j + 1) * ck], preferred_element_type=F32)
        b = jnp.dot(h, w3_ref[:, j * ck:(j + 1) * ck], preferred_element_type=F32)
        act_ref[:, j * ck:(j + 1) * ck] = (a * _sigmoid(a) * b).astype(BF16)
        if stream is not None and j + 1 < nck:
            stream.refill(j + 1)
    y = jnp.dot(act_ref[...], w2_ref[...], preferred_element_type=F32)
    return x + 0.5 * _row_mod(ml_ref, mc_ref, is_ctx, base + 2) * y


def _ffn_weight_scratch(d, dff, ck):
    return [pltpu.VMEM((d, dff), BF16), pltpu.VMEM((d, dff), BF16), pltpu.VMEM((dff, d), BF16),
            pltpu.VMEM((2, 2, d, ck), F32), pltpu.VMEM((2, ck, d), F32),
            pltpu.SemaphoreType.DMA((2, 3))]


class _FfnWeightStream:
    def __init__(self, hbm, lead, resident, stg_c, stg_r, sem, ck):
        self.hbm, self.lead, self.resident = hbm, lead, resident
        self.stg_c, self.stg_r, self.sem, self.ck = stg_c, stg_r, sem, ck
        self.nck = resident[0].shape[1] // ck

    def _stages(self, j):
        s = j % 2
        return (self.stg_c.at[s, 0], self.stg_c.at[s, 1], self.stg_r.at[s])

    def _copies(self, j):
        l0, l1 = self.lead
        sl = slice(j * self.ck, (j + 1) * self.ck)
        srcs = (self.hbm[0].at[l0, l1, :, sl], self.hbm[1].at[l0, l1, :, sl],
                self.hbm[2].at[l0, l1, sl, :])
        return [pltpu.make_async_copy(src, stg, self.sem.at[j % 2, i])
                for i, (src, stg) in enumerate(zip(srcs, self._stages(j)))]

    def start(self, j):
        for cp in self._copies(j):
            cp.start()

    def prime(self):
        self.start(0)
        if self.nck > 1:
            self.start(1)

    def land(self, j):
        sl = slice(j * self.ck, (j + 1) * self.ck)
        for cp in self._copies(j):
            cp.wait()
        w1b, w3b, w2b = self.resident
        s1, s3, s2 = self._stages(j)
        w1b[:, sl] = s1[...].astype(BF16)
        w3b[:, sl] = s3[...].astype(BF16)
        w2b[sl, :] = s2[...].astype(BF16)

    def refill(self, j):
        if j + 2 < self.nck:
            self.start(j + 2)


def _ffn_kernel(x_ref, ml_ref, mc_ref, g_ref, w1_hbm, w3_hbm, w2_hbm, o_ref, act_ref,
                w1b, w3b, w2b, stg_c, stg_r, sem, *, lead, base, seq, tm, ck):
    t0 = pl.program_id(1) * tm
    rows = t0 + lax.broadcasted_iota(jnp.int32, (tm, 1), 0)
    is_ctx = rows >= seq
    first = jnp.logical_and(pl.program_id(0) == 0, pl.program_id(1) == 0)

    @pl.when(first)
    def _():
        stream = _FfnWeightStream((w1_hbm, w3_hbm, w2_hbm), lead, (w1b, w3b, w2b),
                                  stg_c, stg_r, sem, ck)
        stream.prime()
        o_ref[...] = _ffn_body(x_ref[...], ml_ref, mc_ref, is_ctx, g_ref, w1b, w3b, w2b,
                               act_ref, base=base, ck=ck, stream=stream)

    @pl.when(jnp.logical_not(first))
    def _():
        o_ref[...] = _ffn_body(x_ref[...], ml_ref, mc_ref, is_ctx, g_ref, w1b, w3b, w2b,
                               act_ref, base=base, ck=ck)


def _ffn_call(x, ml, mc, g, w1, w3, w2, *, lead, base, seq, out_rows):
    bsz, t, d = x.shape
    dff = w1.shape[-1]
    tm = _pick_tile(t, (768, 384, 128))
    ck = _pick_tile(dff, (256, 128))
    kern = functools.partial(_ffn_kernel, lead=lead, base=base, seq=seq, tm=tm, ck=ck)
    hbm = pl.BlockSpec(memory_space=pl.ANY)
    return pl.pallas_call(
        kern,
        grid=(bsz, pl.cdiv(out_rows, tm)),
        in_specs=[
            pl.BlockSpec((None, tm, d), lambda b, i: (b, i, 0)),
            pl.BlockSpec((None, N_MOD, d), lambda b, i: (b, 0, 0)),
            _const_spec((N_MOD, d)),
            _const_spec((1, d)),
            hbm, hbm, hbm,
        ],
        out_specs=pl.BlockSpec((None, tm, d), lambda b, i: (b, i, 0)),
        out_shape=jax.ShapeDtypeStruct((bsz, out_rows, d), F32),
        scratch_shapes=[pltpu.VMEM((tm, dff), BF16)] + _ffn_weight_scratch(d, dff, ck),
        compiler_params=_cparams(("arbitrary", "arbitrary")),
        name="ffn",
    )(x, ml, mc, g, w1, w3, w2)


def _gelu_tanh(x):
    return 0.5 * x * (1.0 + jnp.tanh(0.7978845608028654 * (x + 0.044715 * (x * x * x))))


def _head_norm(x, gmat, g):
    ms = jnp.dot((x * x).astype(BF16), gmat, preferred_element_type=F32)
    return x * lax.rsqrt(ms + EPS) * g


def _dup_heads(x):
    low = lax.broadcasted_iota(jnp.int32, x.shape, 1) < AT_DH
    sw = pltpu.roll(x, AT_DH, 1)
    return jnp.concatenate([jnp.where(low, x, sw), jnp.where(low, sw, x)], axis=1)


def _rope(x, cos, sin_a, sin_b):
    parts = []
    for j in range(x.shape[1] // LANES):
        xj = x[:, j * LANES:(j + 1) * LANES]
        parts.append(xj * cos + pltpu.roll(xj, LANES - 16, 1) * sin_a
                     + pltpu.roll(xj, 16, 1) * sin_b)
    return jnp.concatenate(parts, axis=1)


def _proj_kernel(x_ref, ml_ref, mc_ref, g_ref, wh_ref, wt_ref, gmat_ref, qg_ref, kg_ref, gb_ref,
                 cos_ref, sa_ref, sb_ref,
                 rgx_o, rgg_o, mq_o, mk_o, mv_o, mo_o, aq_o, ak_o, av_o, br_o, mg_o,
                 tail, *, seq, tm, d_model):
    @pl.when(jnp.logical_and(pl.program_id(0) == 0, pl.program_id(1) == 0))
    def _():
        for r in range(0, d_model, LANES):
            tail[r:r + LANES, :] = wt_ref[r:r + LANES, N_GATE:]

    t0 = pl.program_id(1) * tm
    rows = t0 + lax.broadcasted_iota(jnp.int32, (tm, 1), 0)
    is_ctx = rows >= seq
    h = _ln_mod(x_ref[...], g_ref[...], _row_mod(ml_ref, mc_ref, is_ctx, 3),
                _row_mod(ml_ref, mc_ref, is_ctx, 4)).astype(BF16)

    def seg(c0, w, ref=wh_ref):
        return jnp.dot(h, ref[:, c0:c0 + w], preferred_element_type=F32)

    rgx_o[...] = seg(P_RGX, D_RNN).reshape(rgx_o.shape)
    rgg_o[...] = _gelu_tanh(seg(P_RGG, D_RNN)).astype(BF16)
    mq_o[...] = seg(P_MQ, ML_W).astype(BF16)
    mk_o[...] = (seg(P_MK, ML_W) * (ML_DH ** -0.5)).astype(BF16)
    mv_o[...] = seg(P_MV, ML_W).astype(BF16)
    mo_o[...] = _sigmoid(seg(P_MO, ML_W)).astype(BF16)
    cos, sin_a, sin_b = cos_ref[...], sa_ref[...], sb_ref[...]
    gmat = gmat_ref[...]
    q = _head_norm(seg(T_AQ, AT_W, tail), gmat, qg_ref[...])
    aq_o[...] = (_rope(q, cos, sin_a, sin_b) * (AT_DH ** -0.5 * LOG2E)).astype(BF16)
    k = _head_norm(seg(T_AK, AT_KVW, tail), gmat[:AT_KVW, :AT_KVW], kg_ref[...])
    ak_o[...] = _dup_heads(_rope(k, cos, sin_a, sin_b)).astype(BF16)
    av_o[...] = _dup_heads(seg(T_AV, AT_KVW, tail)).astype(BF16)
    for c0 in range(0, N_BRANCH * d_model, 512):
        br_o[:, c0:c0 + 512] = _sigmoid(seg(T_BR + c0, 512, tail)).astype(BF16)
    mg_o[...] = seg(0, LANES, wt_ref) + gb_ref[...]


def _proj_call(x, ml, mc, g, wh, wt, gmat, qg, kg, gb, cos, sin_a, sin_b, *, layer, seq):
    bsz, t, d = x.shape
    ntail = wt.shape[-1] - N_GATE
    tm = _pick_tile(t, (384, 128))
    kern = functools.partial(_proj_kernel, seq=seq, tm=tm, d_model=d)
    widths = [(D_RNN, F32), (D_RNN, BF16), (ML_W, BF16), (ML_W, BF16), (ML_W, BF16),
              (ML_W, BF16), (AT_W, BF16), (2 * AT_KVW, BF16), (2 * AT_KVW, BF16),
              (N_BRANCH * d, BF16), (LANES, F32)]
    tab_spec = pl.BlockSpec((tm, LANES), lambda b, i: (i, 0))
    out_specs = [pl.BlockSpec((None, tm, wd), lambda b, i: (b, i, 0)) for wd, _ in widths]
    out_shape = [jax.ShapeDtypeStruct((bsz, t, wd), dt) for wd, dt in widths]
    out_specs[0] = pl.BlockSpec((tm // SUBLANES, None, SUBLANES, D_RNN), lambda b, i: (i, b, 0, 0))
    out_shape[0] = jax.ShapeDtypeStruct((t // SUBLANES, bsz, SUBLANES, D_RNN), F32)
    return pl.pallas_call(
        kern,
        grid=(bsz, t // tm),
        in_specs=[
            pl.BlockSpec((None, tm, d), lambda b, i: (b, i, 0)),
            pl.BlockSpec((None, N_MOD, d), lambda b, i: (b, 0, 0)),
            _const_spec((N_MOD, d)),
            _const_spec((1, d)),
            _const_spec((d, wh.shape[-1]), (layer,)),
            _const_spec((d, wt.shape[-1]), (layer,)),
            _const_spec((AT_W, AT_W)),
            _const_spec((1, AT_W)),
            _const_spec((1, AT_KVW)),
            _const_spec((1, LANES)),
            tab_spec, tab_spec, tab_spec,
        ],
        out_specs=out_specs,
        out_shape=out_shape,
        scratch_shapes=[pltpu.VMEM((d, ntail), BF16)],
        compiler_params=_cparams(("arbitrary", "arbitrary")),
        name="in_proj",
    )(x, ml, mc, g, wh, wt, gmat, qg, kg, gb, cos, sin_a, sin_b)


def _scan_chunk(s, rev, nl, nc):
    if rev:
        return jnp.where(s < nc, nl + nc - 1 - s, nl - 1 - (s - nc))
    return jnp.where(s < nc, nl + s, s - nc)


def _rglru_kernel(xf_ref, xfp_ref, xfn_ref, xb_ref, xbp_ref, xbn_ref, cw_ref, cb_ref, w_ref,
                  bias_ref, clam_ref, hf_ref, hb_ref, ext, a_s, b_s, h_s, hstate,
                  *, nl, nc, tc, sub):
    s = pl.program_id(0)
    bsz, c = hstate.shape[1], hstate.shape[2]
    ntb = tc // SUBLANES

    @pl.when(s == 0)
    def _():
        hstate[...] = jnp.zeros_like(hstate)

    cw = cw_ref[...]
    cb = cb_ref[...]
    for d, (x_ref, xp_ref, xn_ref) in enumerate(((xf_ref, xfp_ref, xfn_ref),
                                                (xb_ref, xbp_ref, xbn_ref))):
        chunk = _scan_chunk(s, d == 1, nl, nc)
        first = jnp.logical_or(chunk == 0, chunk == nl)
        last = jnp.logical_or(chunk == nl - 1, chunk == nl + nc - 1)
        prev = jnp.swapaxes(xp_ref[0], 0, 1)[SUBLANES - CONV_LEFT:]
        nxt = jnp.swapaxes(xn_ref[0], 0, 1)[:CONV_RIGHT]
        ext[d, 0:CONV_LEFT] = jnp.where(first, 0.0, prev)
        for i in range(ntb):
            o = CONV_LEFT + i * SUBLANES
            ext[d, o:o + SUBLANES] = jnp.swapaxes(x_ref[i], 0, 1)
        ext[d, CONV_LEFT + tc:CONV_LEFT + tc + CONV_RIGHT] = jnp.where(last, 0.0, nxt)

        def gates(i, carry, d=d):
            t0 = pl.multiple_of(i * sub, sub)
            u = cb
            for k in range(CONV_W):
                u = u + ext[d, pl.ds(t0 + k, sub)] * cw[k:k + 1, :]
            u2 = u.reshape(sub * bsz, c)
            z = jnp.dot(u2.astype(BF16), w_ref[d], preferred_element_type=F32) + bias_ref[d]
            a = jnp.exp2(clam_ref[d] * jnp.tanh(z[:, :c]) + clam_ref[d])
            gi = 0.5 * jnp.tanh(z[:, c:]) + 0.5
            s1 = 1.0 - a * a
            bb = (s1 * lax.rsqrt(jnp.maximum(s1, 1e-36))) * (gi * u2)
            a_s[d, pl.ds(t0, sub)] = a.reshape(sub, bsz, c)
            b_s[d, pl.ds(t0, sub)] = bb.reshape(sub, bsz, c)
            return carry

        lax.fori_loop(0, tc // sub, gates, 0)

    def step(k, hs):
        hf, hb = hs
        kb = tc - 1 - k
        hf = a_s[0, k] * hf + b_s[0, k]
        hb = a_s[1, kb] * hb + b_s[1, kb]
        h_s[0, k] = hf
        h_s[1, kb] = hb
        return hf, hb

    hf, hb = lax.fori_loop(0, tc, step, (hstate[0], hstate[1]), unroll=8)
    hstate[0] = hf
    hstate[1] = hb
    for d, o_ref in enumerate((hf_ref, hb_ref)):
        for i in range(ntb):
            o_ref[i] = jnp.swapaxes(h_s[d, i * SUBLANES:(i + 1) * SUBLANES], 0, 1)


def _rglru_call(rgx, cw, cb, w, bias, clam, *, seq):
    ntb_all, bsz, _, c = rgx.shape
    t = ntb_all * SUBLANES
    tc = 128
    sub = 32
    assert seq % tc == 0 and (t - seq) % tc == 0
    nl, nc = seq // tc, (t - seq) // tc
    ntb = tc // SUBLANES

    def specs(rev):
        chunk = functools.partial(_scan_chunk, rev=rev, nl=nl, nc=nc)
        return [
            pl.BlockSpec((ntb, bsz, SUBLANES, c), lambda s: (chunk(s), 0, 0, 0)),
            pl.BlockSpec((1, bsz, SUBLANES, c),
                         lambda s: (jnp.maximum(chunk(s) * ntb - 1, 0), 0, 0, 0)),
            pl.BlockSpec((1, bsz, SUBLANES, c),
                         lambda s: (jnp.minimum((chunk(s) + 1) * ntb, ntb_all - 1), 0, 0, 0)),
        ]

    kern = functools.partial(_rglru_kernel, nl=nl, nc=nc, tc=tc, sub=sub)
    fspec, bspec = specs(False), specs(True)
    return pl.pallas_call(
        kern,
        grid=(nl + nc,),
        in_specs=fspec + bspec + [
            _const_spec((CONV_W, c)),
            _const_spec((1, c)),
            _const_spec((2, c, 2 * c)),
            _const_spec((2, 1, 2 * c)),
            _const_spec((2, 1, c)),
        ],
        out_specs=[fspec[0], bspec[0]],
        out_shape=[jax.ShapeDtypeStruct(rgx.shape, F32)] * 2,
        scratch_shapes=[
            pltpu.VMEM((2, tc + CONV_W - 1, bsz, c), F32),
            pltpu.VMEM((2, tc, bsz, c), F32),
            pltpu.VMEM((2, tc, bsz, c), F32),
            pltpu.VMEM((2, tc, bsz, c), F32),
            pltpu.VMEM((2, bsz, c), F32),
        ],
        compiler_params=_cparams(("arbitrary",)),
        name="rglru",
    )(rgx, rgx, rgx, rgx, rgx, rgx, cw, cb, w, bias, clam)


def _log_sigmoid(x):
    return -(jnp.maximum(-x, 0.0) + jnp.log1p(jnp.exp(-jnp.abs(x))))


def _lane_scan(x, op, ident, rev):
    n = x.shape[1]
    lane = lax.broadcasted_iota(jnp.int32, x.shape, 1)
    sh = 1
    while sh < n:
        if rev:
            x = op(x, jnp.where(lane < n - sh, pltpu.roll(x, n - sh, 1), ident))
        else:
            x = op(x, jnp.where(lane >= sh, pltpu.roll(x, sh, 1), ident))
        sh *= 2
    return x


def _mlgate_kernel(g_ref, row_o, col_o, cb_s, gg_s, ml_s, tot_s, wm_s, mo_s, mn_s, *, nl, nc):
    L = ML_CHUNK
    H = ML_HEADS
    U = ML_UNITS
    is_bwd = lax.broadcasted_iota(jnp.int32, (U, L), 0) >= H
    for c in range(nl + nc):
        lanes = slice(c * L, (c + 1) * L)
        g16 = g_ref[:, lanes]
        li = jnp.concatenate([g16[0:H], g16[2 * H:3 * H]], axis=0)
        lf = _log_sigmoid(jnp.concatenate([g16[H:2 * H], g16[3 * H:4 * H]], axis=0))
        pre = _lane_scan(lf, jnp.add, 0.0, False)
        suf = _lane_scan(lf, jnp.add, 0.0, True)
        tot = pre + suf - lf
        cb = jnp.where(is_bwd, suf, pre)
        gg = li - cb
        pmax = _lane_scan(gg, jnp.maximum, NEG, False)
        smax = _lane_scan(gg, jnp.maximum, NEG, True)
        cb_s[:, lanes] = cb
        gg_s[:, lanes] = gg
        ml_s[:, lanes] = cb + jnp.where(is_bwd, smax, pmax)
        tot_s[:, lanes] = tot
        wm_s[:, lanes] = tot + jnp.maximum(pmax, smax)

    m = jnp.zeros((U, L), F32)
    for s in range(nl + nc):
        cf = nl + s if s < nc else s - nc
        cr = nl + nc - 1 - s if s < nc else nl - 1 - (s - nc)
        lf_, lr_ = slice(cf * L, (cf + 1) * L), slice(cr * L, (cr + 1) * L)
        tot = jnp.where(is_bwd, tot_s[:, lr_], tot_s[:, lf_])
        wm = jnp.where(is_bwd, wm_s[:, lr_], wm_s[:, lf_])
        m_new = jnp.maximum(tot + m, wm)
        mo_s[0:H, lf_] = m[0:H]
        mo_s[H:U, lr_] = m[H:U]
        mn_s[0:H, lf_] = m_new[0:H]
        mn_s[H:U, lr_] = m_new[H:U]
        m = m_new

    cb = cb_s[...]
    gg = gg_s[...]
    tot = tot_s[...]
    mo = mo_s[...]
    mn = mn_s[...]
    m_t = jnp.maximum(cb + mo, ml_s[...])
    row_o[0:U, :] = gg
    row_o[U:2 * U, :] = jnp.exp(tot + gg - mn)
    row_o[2 * U:3 * U, :] = jnp.exp(tot + mo - mn)
    row_o[3 * U:4 * U, :] = mo
    col_o[0:U, :] = cb - m_t
    col_o[U:2 * U, :] = jnp.exp(-m_t)


def _mlgate_call(gt, *, seq):
    bsz, ng, t = gt.shape
    nl, nc = seq // ML_CHUNK, (t - seq) // ML_CHUNK
    kern = functools.partial(_mlgate_kernel, nl=nl, nc=nc)
    rout = pl.BlockSpec((None, 4 * ML_UNITS, t), lambda b: (b, 0, 0))
    cout = pl.BlockSpec((None, 2 * ML_UNITS, t), lambda b: (b, 0, 0))
    return pl.pallas_call(
        kern,
        grid=(bsz,),
        in_specs=[pl.BlockSpec((None, ng, t), lambda b: (b, 0, 0))],
        out_specs=[rout, cout],
        out_shape=[jax.ShapeDtypeStruct((bsz, 4 * ML_UNITS, t), F32),
                   jax.ShapeDtypeStruct((bsz, 2 * ML_UNITS, t), F32)],
        scratch_shapes=[pltpu.VMEM((ML_UNITS, t), F32)] * 7,
        compiler_params=_cparams(("arbitrary",)),
        name="mlstm_gates",
    )(gt)


def _mlstm_kernel(qf_ref, ktf_ref, vf_ref, rf_ref, cf_ref, qb_ref, ktb_ref, vb_ref, rb_ref, cb_ref,
                  hf_ref, hb_ref, c_st):
    s = pl.program_id(1)

    @pl.when(s == 0)
    def _():
        c_st[...] = jnp.zeros_like(c_st)

    L, dh, U = ML_CHUNK, ML_DH, ML_UNITS
    ti = lax.broadcasted_iota(jnp.int32, (L, L), 0)
    si = lax.broadcasted_iota(jnp.int32, (L, L), 1)
    ones = jnp.ones((L, dh), BF16)
    for bb in range(qf_ref.shape[0]):
        for d, (q_ref, kt_ref, v_ref, r_ref, c_ref, o_ref) in enumerate(
                ((qf_ref, ktf_ref, vf_ref, rf_ref, cf_ref, hf_ref),
                 (qb_ref, ktb_ref, vb_ref, rb_ref, cb_ref, hb_ref))):
            tri = (si >= ti) if d == 1 else (si <= ti)
            rows = r_ref[bb]
            cols = c_ref[bb]
            for hd in range(ML_HEADS):
                u = d * ML_HEADS + hd
                sl = slice(hd * dh, (hd + 1) * dh)
                q = q_ref[bb, :, sl]
                kt = kt_ref[bb, sl, :]
                v_ext = jnp.concatenate([v_ref[bb, :, sl], ones], axis=1)
                g_row = rows[u:u + 1, :]
                w_row = rows[U + u:U + u + 1, :]
                a_row = rows[2 * U + u:2 * U + u + 1, :]
                m_row = rows[3 * U + u:3 * U + u + 1, :]
                xb = jnp.broadcast_to(cols[:, u:u + 1], (L, L))
                emt = cols[:, U + u:U + u + 1]
                c_old = c_st[bb * U + u]
                qkc = jnp.dot(q, jnp.concatenate([kt, c_old.astype(BF16)], axis=1),
                              preferred_element_type=F32)
                dm = jnp.exp(jnp.where(tri, xb + g_row, NEG))
                dec = jnp.exp(xb + m_row)
                sc = (qkc[:, :L] * dm).astype(BF16)
                ktw = (kt.astype(F32) * w_row).astype(BF16)
                sv = jnp.dot(jnp.concatenate([sc, ktw], axis=0), v_ext,
                             preferred_element_type=F32)
                ab = sv[:L] + jnp.concatenate([dec, dec], axis=1) * qkc[:, L:]
                o_ref[bb, :, sl] = ab[:, :dh] / jnp.maximum(jnp.abs(ab[:, dh:]), emt)
                c_st[bb * U + u] = jnp.concatenate([a_row, a_row], axis=1) * c_old + sv[L:]


def _mlstm_call(mq, mkt, mv, rowp, colp, *, seq):
    bsz, t, w = mq.shape
    nl, nc = seq // ML_CHUNK, (t - seq) // ML_CHUNK
    cf = functools.partial(_scan_chunk, rev=False, nl=nl, nc=nc)
    cr = functools.partial(_scan_chunk, rev=True, nl=nl, nc=nc)
    nr = rowp.shape[1]
    bpb = 8 if bsz % 8 == 0 else 1

    def tok(ch, wd):
        return pl.BlockSpec((bpb, ML_CHUNK, wd), lambda b, s: (b, ch(s), 0))

    def tr(ch, rows):
        return pl.BlockSpec((bpb, rows, ML_CHUNK), lambda b, s: (b, 0, ch(s)))

    def side(ch):
        return [tok(ch, w), tr(ch, w), tok(ch, w), tr(ch, nr), tok(ch, colp.shape[2])]

    return pl.pallas_call(
        _mlstm_kernel,
        grid=(bsz // bpb, nl + nc),
        in_specs=side(cf) + side(cr),
        out_specs=[tok(cf, w), tok(cr, w)],
        out_shape=[jax.ShapeDtypeStruct((bsz, t, w), F32)] * 2,
        scratch_shapes=[pltpu.VMEM((bpb * ML_UNITS, ML_DH, 2 * ML_DH), F32)],
        compiler_params=_cparams(("arbitrary", "arbitrary")),
        name="mlstm",
    )(mq, mkt, mv, rowp, colp, mq, mkt, mv, rowp, colp)


def _attn_kernel(sink_ref, q_ref, k_ref, v_ref, o_ref, vext, *, seq, qbs):
    kv = pl.program_id(1)
    i = pl.program_id(2)
    blk = ATT_BLOCK
    t = k_ref.shape[0]
    ctx = t - seq
    nlb = seq // blk
    band = 3 * blk

    @pl.when(i == 0)
    def _():
        vext[:, 0:LANES] = v_ref[...]
        vext[:, LANES:2 * LANES] = jnp.ones((t, LANES), BF16)

    lane = lax.broadcasted_iota(jnp.int32, (blk, LANES), 1)
    low = lane < AT_DH
    zero = jnp.zeros((blk, LANES), BF16)
    rows = AT_G * blk
    gi = lax.broadcasted_iota(jnp.int32, (rows, 1), 0) // blk
    snk = jnp.zeros((rows, 1), F32)
    for g in range(AT_G):
        snk = jnp.where(gi == g, sink_ref[kv * AT_G + g] * LOG2E, snk)
    ti = lax.broadcasted_iota(jnp.int32, (blk, band), 0)
    ci = lax.broadcasted_iota(jnp.int32, (blk, band), 1)
    nt = (((1,), (1,)), ((), ()))
    k_ctx = k_ref[seq:t, :]
    v_ctx = vext[seq:t, :]

    for j in range(qbs):
        qb = i * qbs + j
        s0 = pl.multiple_of(jnp.clip((qb - 1) * blk, 0, t - band), blk)
        rel = ci - ti + (s0 - qb * blk + blk)
        lim = jnp.where(qb < nlb, seq - s0, 0)
        ok = jnp.logical_and(jnp.logical_and(rel >= 0, rel <= 2 * blk), ci < lim)
        bias = jnp.where(ok, 0.0, NEG)
        bias = jnp.concatenate([bias] * AT_G, axis=0)

        qs = []
        for h2 in range(AT_G // 2):
            qj = q_ref[j * blk:(j + 1) * blk, h2 * LANES:(h2 + 1) * LANES]
            qs += [jnp.where(low, qj, zero), jnp.where(low, zero, qj)]
        q_all = jnp.concatenate(qs, axis=0)
        s_band = lax.dot_general(q_all, k_ref[pl.ds(s0, band), :], nt,
                                 preferred_element_type=F32) + bias
        s_ctx = lax.dot_general(q_all, k_ctx, nt, preferred_element_type=F32)
        m = jnp.maximum(jnp.maximum(jnp.max(s_band, axis=1, keepdims=True),
                                    jnp.max(s_ctx, axis=1, keepdims=True)), snk)
        p_band = jnp.exp2(s_band - m).astype(BF16)
        p_ctx = jnp.exp2(s_ctx - m).astype(BF16)
        o2 = (jnp.dot(p_band, vext[pl.ds(s0, band), :], preferred_element_type=F32)
              + jnp.dot(p_ctx, v_ctx, preferred_element_type=F32))
        o2 = o2[:, :LANES] / (o2[:, LANES:] + jnp.exp2(snk - m))
        for h2 in range(AT_G // 2):
            a = o2[(2 * h2) * blk:(2 * h2 + 1) * blk, :]
            b = o2[(2 * h2 + 1) * blk:(2 * h2 + 2) * blk, :]
            o_ref[j * blk:(j + 1) * blk, h2 * LANES:(h2 + 1) * LANES] = (
                jnp.where(low, a, b).astype(o_ref.dtype))


def _attn_call(sink, aq, akd, avd, *, seq):
    bsz, t, _ = aq.shape
    blk = ATT_BLOCK
    qbs = 3
    assert t % (qbs * blk) == 0 and t >= 3 * blk
    gw = AT_G * AT_DH
    kern = functools.partial(_attn_kernel, seq=seq, qbs=qbs)
    whole = pl.BlockSpec((None, t, LANES), lambda b, kv, i: (b, 0, kv))
    return pl.pallas_call(
        kern,
        grid=(bsz, AT_KV, t // (qbs * blk)),
        in_specs=[
            pl.BlockSpec(memory_space=pltpu.SMEM),
            pl.BlockSpec((None, qbs * blk, gw), lambda b, kv, i: (b, i, kv)),
            whole, whole,
        ],
        out_specs=pl.BlockSpec((None, qbs * blk, gw), lambda b, kv, i: (b, i, kv)),
        out_shape=jax.ShapeDtypeStruct((bsz, t, AT_W), BF16),
        scratch_shapes=[pltpu.VMEM((t, 2 * LANES), BF16)],
        compiler_params=_cparams(("arbitrary", "arbitrary", "arbitrary")),
        name="attn",
    )(sink, aq, akd, avd)


def _merge_kernel(x_ref, ml_ref, mc_ref, rf_ref, rb_ref, gg_ref, hf_ref, hb_ref, mo_ref, yc_ref,
                  br_ref, ng_ref, wb_ref, wo_ref, g_ref, w1_hbm, w3_hbm, w2_hbm, o_ref, act_ref,
                  w1b, w3b, w2b, stg_c, stg_r, sem, *, lead, seq, tm, ck):
    first = jnp.logical_and(pl.program_id(0) == 0, pl.program_id(1) == 0)
    stream = _FfnWeightStream((w1_hbm, w3_hbm, w2_hbm), lead, (w1b, w3b, w2b),
                              stg_c, stg_r, sem, ck)

    @pl.when(first)
    def _():
        stream.prime()

    t0 = pl.program_id(1) * tm
    rows = t0 + lax.broadcasted_iota(jnp.int32, (tm, 1), 0)
    is_ctx = rows >= seq
    d = x_ref.shape[1]
    rsum = (rf_ref[...] + rb_ref[...]).reshape(tm, D_RNN)
    ya = (rsum * gg_ref[...].astype(F32)).astype(BF16)
    hsum = hf_ref[...] + hb_ref[...]
    parts = []
    for hd in range(ML_HEADS):
        hh = hsum[:, hd * ML_DH:(hd + 1) * ML_DH]
        parts.append(hh * lax.rsqrt(jnp.mean(hh * hh, axis=-1, keepdims=True) + EPS))
    yb = (jnp.concatenate(parts, axis=1) * ng_ref[...] * mo_ref[...].astype(F32)).astype(BF16)
    m = (br_ref[:, 0:d].astype(F32) * jnp.dot(ya, wb_ref[0], preferred_element_type=F32)
         + br_ref[:, d:2 * d].astype(F32) * jnp.dot(yb, wb_ref[1], preferred_element_type=F32)
         + br_ref[:, 2 * d:3 * d].astype(F32) * jnp.dot(yc_ref[...], wb_ref[2],
                                                       preferred_element_type=F32))
    y = jnp.dot(m.astype(BF16), wo_ref[...], preferred_element_type=F32)
    x1 = x_ref[...] + _row_mod(ml_ref, mc_ref, is_ctx, 5) * y
    @pl.when(first)
    def _():
        o_ref[...] = _ffn_body(x1, ml_ref, mc_ref, is_ctx, g_ref, w1b, w3b, w2b, act_ref,
                               base=6, ck=ck, stream=stream)

    @pl.when(jnp.logical_not(first))
    def _():
        o_ref[...] = _ffn_body(x1, ml_ref, mc_ref, is_ctx, g_ref, w1b, w3b, w2b, act_ref,
                               base=6, ck=ck)


def _merge_call(x, ml, mc, rf, rb, gg, hf, hb, mo, yc, br, ng, wb, wo, g, w1, w3, w2,
                *, layer, seq, out_rows):
    bsz, t, d = x.shape
    dff = w1.shape[-1]
    tm = _pick_tile(t, (384, 128))
    ck = _pick_tile(dff, (256, 128))
    kern = functools.partial(_merge_kernel, lead=(layer, 1), seq=seq, tm=tm, ck=ck)
    hbm = pl.BlockSpec(memory_space=pl.ANY)

    def tok(wd):
        return pl.BlockSpec((None, tm, wd), lambda b, i: (b, i, 0))

    rspec = pl.BlockSpec((tm // SUBLANES, None, SUBLANES, D_RNN), lambda b, i: (i, b, 0, 0))

    return pl.pallas_call(
        kern,
        grid=(bsz, pl.cdiv(out_rows, tm)),
        in_specs=[
            tok(d),
            pl.BlockSpec((None, N_MOD, d), lambda b, i: (b, 0, 0)),
            _const_spec((N_MOD, d)),
            rspec, rspec, tok(D_RNN), tok(ML_W), tok(ML_W), tok(ML_W), tok(AT_W),
            tok(N_BRANCH * d),
            _const_spec((1, ML_W)),
            _const_spec((N_BRANCH, BRANCH_W, d), (layer,)),
            _const_spec((d, d), (layer,)),
            _const_spec((1, d)),
            hbm, hbm, hbm,
        ],
        out_specs=tok(d),
        out_shape=jax.ShapeDtypeStruct((bsz, out_rows, d), F32),
        scratch_shapes=[pltpu.VMEM((tm, dff), BF16)] + _ffn_weight_scratch(d, dff, ck),
        compiler_params=_cparams(("arbitrary", "arbitrary")),
        name="merge_ffn",
    )(x, ml, mc, rf, rb, gg, hf, hb, mo, yc, br, ng, wb, wo, g, w1, w3, w2)


def _block_diag(w):
    n, bi, bj = w.shape
    eye = jnp.eye(n, dtype=w.dtype)
    return (eye[:, None, :, None] * w[:, :, None, :]).reshape(n * bi, n * bj)


def _rope_tables(seq, t):
    rows = seq // GRID_W
    row = jnp.repeat(jnp.arange(rows), GRID_W).astype(F32)
    col = jnp.broadcast_to(jnp.arange(GRID_W), (rows, GRID_W)).reshape(-1).astype(F32)
    half = AT_DH // 2
    inv = ROPE_BASE ** (-jnp.arange(0, half, 2, dtype=F32) / half)
    ar = row[:, None] * inv
    ac = col[:, None] * inv
    ang = jnp.concatenate([ar, ar, ac, ac], axis=-1)
    cos = jnp.concatenate([jnp.cos(ang), jnp.ones((t - seq, AT_DH), F32)], axis=0)
    sin = jnp.concatenate([jnp.sin(ang), jnp.zeros((t - seq, AT_DH), F32)], axis=0)
    cos = jnp.tile(cos, (1, LANES // AT_DH))
    sin = jnp.tile(sin, (1, LANES // AT_DH))
    first_half = (jnp.arange(LANES) % 32) < 16
    sin_a = jnp.where(first_half, -sin, 0.0)
    sin_b = jnp.where(first_half, 0.0, sin)
    return cos, sin_a, sin_b


def kernel(x, c, ctx, c_ctx, ada_w, ada_b, norm_g, ffn_w1, ffn_w3, ffn_w2, w_in, rg_conv_w,
           rg_conv_b, rg_wa, rg_ba, rg_wi, rg_bi, rg_lam, ml_gate_b, ml_norm_g, at_qn_g,
           at_kn_g, at_sink, w_branch, w_out):
    bsz, seq, d = x.shape
    nctx = ctx.shape[1]
    t = seq + nctx
    depth = ada_w.shape[0]

    xs = jnp.concatenate([x, ctx], axis=1)
    mod_rows = 2 * SUBLANES
    cc = jnp.concatenate([c, c_ctx[None, :], jnp.zeros((mod_rows - bsz - 1, d), F32)], axis=0)
    mods = _ada_call(cc, ada_w, ada_b).reshape(depth, mod_rows, N_MOD, d)
    cos, sin_a, sin_b = _rope_tables(seq, t)
    gmat = _block_diag(jnp.full((AT_HEADS, AT_DH, AT_DH), 1.0 / AT_DH, F32)).astype(BF16)
    w1, w3, w2 = ffn_w1, ffn_w3, ffn_w2
    w_head, w_tail = w_in[..., :P_HEAD].astype(BF16), w_in[..., P_HEAD:].astype(BF16)
    w_br, w_o = w_branch.astype(BF16), w_out.astype(BF16)

    for l in range(depth):
        ml = mods[l, :bsz]
        mc = mods[l, bsz]
        last = l == depth - 1

        xs = _ffn_call(xs, ml, mc, norm_g[l, 0][None, :], w1, w3, w2,
                       lead=(l, 0), base=0, seq=seq, out_rows=t)

        qg = jnp.tile(at_qn_g[l], AT_HEADS)[None, :]
        kg = jnp.tile(at_kn_g[l], AT_KV)[None, :]
        gb = jnp.concatenate([ml_gate_b[l], jnp.zeros((LANES - N_GATE,), F32)])[None, :]
        (rgx, rgg, mq, mk, mv, mo, aq, akd, avd, br, mg) = _proj_call(
            xs, ml, mc, norm_g[l, 1][None, :], w_head, w_tail, gmat, qg, kg, gb,
            cos, sin_a, sin_b, layer=l, seq=seq)

        wcat = jnp.stack([jnp.concatenate([_block_diag(rg_wa[l, dr]), _block_diag(rg_wi[l, dr])],
                                          axis=1) for dr in range(2)])
        wcat = (0.5 * wcat).astype(BF16)
        bias = 0.5 * jnp.concatenate([rg_ba[l], rg_bi[l]], axis=1)[:, None, :]
        clam = (-0.5 * LOG2E * LRU_C * jax.nn.softplus(-rg_lam[l]))[:, None, :]
        rhf, rhb = _rglru_call(rgx, rg_conv_w[l], rg_conv_b[l][None, :], wcat, bias, clam,
                               seq=seq)

        gt = jnp.swapaxes(mg[:, :, :N_GATE], 1, 2)
        rowp, colsrc = _mlgate_call(gt, seq=seq)
        colp = jnp.swapaxes(colsrc, 1, 2)
        mkt = jnp.swapaxes(mk, 1, 2)
        mhf, mhb = _mlstm_call(mq, mkt, mv, rowp, colp, seq=seq)

        yc = _attn_call(at_sink[l], aq, akd, avd, seq=seq)

        xs = _merge_call(xs, ml, mc, rhf, rhb, rgg, mhf, mhb, mo, yc, br, ml_norm_g[l][None, :],
                         w_br, w_o, norm_g[l, 2][None, :], w1, w3, w2,
                         layer=l, seq=seq, out_rows=seq if last else t)
    return xs
```

```python
import functools

import jax
import jax.numpy as jnp
from jax import lax
from jax.experimental import pallas as pl
from jax.experimental.pallas import tpu as pltpu

F32 = jnp.float32
BF16 = jnp.bfloat16

EPS = 1e-6
NEG = -1e30
LOG2E = 1.4426950408889634
N_MOD = 9
GRID_W = 64
ROPE_BASE = 10000.0

D_RNN = 512
RNN_BLOCKS = 8
RNN_BLOCK = D_RNN // RNN_BLOCKS
CONV_W = 4
CONV_LEFT = 2
CONV_RIGHT = CONV_W - 1 - CONV_LEFT
LRU_C = 8.0

ML_HEADS = 4
ML_DH = 128
ML_W = ML_HEADS * ML_DH
ML_CHUNK = 128
ML_UNITS = 2 * ML_HEADS

AT_HEADS = 8
AT_KV = 2
AT_DH = 64
AT_G = AT_HEADS // AT_KV
AT_W = AT_HEADS * AT_DH
AT_KVW = AT_KV * AT_DH
ATT_BLOCK = 128

N_BRANCH = 3
BRANCH_W = 512

LANES = 128
SUBLANES = 8
VMEM_LIMIT = 56 * 1024 * 1024

P_RGX = 0
P_RGG = P_RGX + D_RNN
P_MQ = P_RGG + D_RNN
P_MK = P_MQ + ML_W
P_MV = P_MK + ML_W
P_MO = P_MV + ML_W
P_HEAD = P_MO + ML_W
N_GATE = 4 * ML_HEADS
T_AQ = 0
T_AK = T_AQ + AT_W
T_AV = T_AK + AT_KVW
T_BR = T_AV + AT_KVW


def _cparams(sem):
    return pltpu.CompilerParams(dimension_semantics=sem, vmem_limit_bytes=VMEM_LIMIT)


def _const_spec(shape, lead=()):
    nd = len(shape)
    idx = tuple(lead) + (0,) * nd
    return pl.BlockSpec((None,) * len(lead) + tuple(shape), lambda *_: idx,
                        pipeline_mode=pl.Buffered(1))


def _pick_tile(total, candidates):
    for c in candidates:
        if total % c == 0:
            return c
    raise ValueError(f"no tile for {total}")


def _sigmoid(x):
    return jax.nn.sigmoid(x)


def _ln_mod(x, g, shift, scale):
    ms = jnp.mean(x * x, axis=-1, keepdims=True)
    return x * lax.rsqrt(ms + EPS) * g * (1.0 + scale) + shift


def _row_mod(ml_ref, mc_ref, is_ctx, i):
    return jnp.where(is_ctx, mc_ref[i:i + 1, :], ml_ref[i:i + 1, :])


def _ada_kernel(cc_ref, w_ref, b_ref, o_ref):
    cc = cc_ref[...]
    s = cc * _sigmoid(cc)
    o_ref[...] = jnp.dot(s.astype(BF16), w_ref[...].astype(BF16),
                         preferred_element_type=F32) + b_ref[...]


def _ada_call(cc, ada_w, ada_b):
    depth, d, nout = ada_w.shape
    rows = cc.shape[0]
    tn = _pick_tile(nout, (1536, 1024, 512, 256, 128))
    return pl.pallas_call(
        _ada_kernel,
        grid=(depth, nout // tn),
        in_specs=[
            pl.BlockSpec((rows, d), lambda l, j: (0, 0)),
            pl.BlockSpec((None, d, tn), lambda l, j: (l, 0, j)),
            pl.BlockSpec((None, 1, tn), lambda l, j: (l, 0, j)),
        ],
        out_specs=pl.BlockSpec((None, rows, tn), lambda l, j: (l, 0, j)),
        out_shape=jax.ShapeDtypeStruct((depth, rows, nout), F32),
        compiler_params=_cparams(("arbitrary", "arbitrary")),
        name="ada_mod",
    )(cc, ada_w, ada_b.reshape(depth, 1, nout))


def _ffn_body(x, ml_ref, mc_ref, is_ctx, g_ref, w1_ref, w3_ref, w2_ref, act_ref, *, base, ck,
              stream=None):
    h = _ln_mod(x, g_ref[...], _row_mod(ml_ref, mc_ref, is_ctx, base),
                _row_mod(ml_ref, mc_ref, is_ctx, base + 1)).astype(BF16)
    dff = w1_ref.shape[1]
    nck = dff // ck
    if stream is not None:
        stream.land(0)
        stream.refill(0)
    for j in range(nck):
        if stream is not None and j + 1 < nck:
            stream.land(j + 1)
        a = jnp.dot(h, w1_ref[:, j * ck:(j + 1) * ck], preferred_element_type=F32)
        b = jnp.dot(h, w3_ref[:, j * ck:(j + 1) * ck], preferred_element_type=F32)
        act_ref[:, j * ck:(j + 1) * ck] = (a * _sigmoid(a) * b).astype(BF16)
        if stream is not None and j + 1 < nck:
            stream.refill(j + 1)
    y = jnp.dot(act_ref[...], w2_ref[...], preferred_element_type=F32)
    return x + 0.5 * _row_mod(ml_ref, mc_ref, is_ctx, base + 2) * y


def _ffn_weight_scratch(d, dff, ck):
    return [pltpu.VMEM((d, dff), BF16), pltpu.VMEM((d, dff), BF16), pltpu.VMEM((dff, d), BF16),
            pltpu.VMEM((2, 2, d, ck), F32), pltpu.VMEM((2, ck, d), F32),
            pltpu.SemaphoreType.DMA((2, 3))]


class _FfnWeightStream:
    def __init__(self, hbm, lead, resident, stg_c, stg_r, sem, ck):
        self.hbm, self.lead, self.resident = hbm, lead, resident
        self.stg_c, self.stg_r, self.sem, self.ck = stg_c, stg_r, sem, ck
        self.nck = resident[0].shape[1] // ck

    def _stages(self, j):
        s = j % 2
        return (self.stg_c.at[s, 0], self.stg_c.at[s, 1], self.stg_r.at[s])

    def _copies(self, j):
        l0, l1 = self.lead
        sl = slice(j * self.ck, (j + 1) * self.ck)
        srcs = (self.hbm[0].at[l0, l1, :, sl], self.hbm[1].at[l0, l1, :, sl],
                self.hbm[2].at[l0, l1, sl, :])
        return [pltpu.make_async_copy(src, stg, self.sem.at[j % 2, i])
                for i, (src, stg) in enumerate(zip(srcs, self._stages(j)))]

    def start(self, j):
        for cp in self._copies(j):
            cp.start()

    def prime(self):
        self.start(0)
        if self.nck > 1:
            self.start(1)

    def land(self, j):
        sl = slice(j * self.ck, (j + 1) * self.ck)
        for cp in self._copies(j):
            cp.wait()
        w1b, w3b, w2b = self.resident
        s1, s3, s2 = self._stages(j)
        w1b[:, sl] = s1[...].astype(BF16)
        w3b[:, sl] = s3[...].astype(BF16)
        w2b[sl, :] = s2[...].astype(BF16)

    def refill(self, j):
        if j + 2 < self.nck:
            self.start(j + 2)


def _ffn_kernel(x_ref, ml_ref, mc_ref, g_ref, w1_hbm, w3_hbm, w2_hbm, o_ref, act_ref,
                w1b, w3b, w2b, stg_c, stg_r, sem, *, lead, base, seq, tm, ck):
    t0 = pl.program_id(1) * tm
    rows = t0 + lax.broadcasted_iota(jnp.int32, (tm, 1), 0)
    is_ctx = rows >= seq
    first = jnp.logical_and(pl.program_id(0) == 0, pl.program_id(1) == 0)

    @pl.when(first)
    def _():
        stream = _FfnWeightStream((w1_hbm, w3_hbm, w2_hbm), lead, (w1b, w3b, w2b),
                                  stg_c, stg_r, sem, ck)
        stream.prime()
        o_ref[...] = _ffn_body(x_ref[...], ml_ref, mc_ref, is_ctx, g_ref, w1b, w3b, w2b,
                               act_ref, base=base, ck=ck, stream=stream)

    @pl.when(jnp.logical_not(first))
    def _():
        o_ref[...] = _ffn_body(x_ref[...], ml_ref, mc_ref, is_ctx, g_ref, w1b, w3b, w2b,
                               act_ref, base=base, ck=ck)


def _ffn_call(x, ml, mc, g, w1, w3, w2, *, lead, base, seq, out_rows):
    bsz, t, d = x.shape
    dff = w1.shape[-1]
    tm = _pick_tile(t, (768, 384, 128))
    ck = _pick_tile(dff, (256, 128))
    kern = functools.partial(_ffn_kernel, lead=lead, base=base, seq=seq, tm=tm, ck=ck)
    hbm = pl.BlockSpec(memory_space=pl.ANY)
    return pl.pallas_call(
        kern,
        grid=(bsz, pl.cdiv(out_rows, tm)),
        in_specs=[
            pl.BlockSpec((None, tm, d), lambda b, i: (b, i, 0)),
            pl.BlockSpec((None, N_MOD, d), lambda b, i: (b, 0, 0)),
            _const_spec((N_MOD, d)),
            _const_spec((1, d)),
            hbm, hbm, hbm,
        ],
        out_specs=pl.BlockSpec((None, tm, d), lambda b, i: (b, i, 0)),
        out_shape=jax.ShapeDtypeStruct((bsz, out_rows, d), F32),
        scratch_shapes=[pltpu.VMEM((tm, dff), BF16)] + _ffn_weight_scratch(d, dff, ck),
        compiler_params=_cparams(("arbitrary", "arbitrary")),
        name="ffn",
    )(x, ml, mc, g, w1, w3, w2)


def _gelu_tanh(x):
    return 0.5 * x * (1.0 + jnp.tanh(0.7978845608028654 * (x + 0.044715 * (x * x * x))))


def _head_norm(x, gmat, g):
    ms = jnp.dot((x * x).astype(BF16), gmat, preferred_element_type=F32)
    return x * lax.rsqrt(ms + EPS) * g


def _dup_heads(x):
    low = lax.broadcasted_iota(jnp.int32, x.shape, 1) < AT_DH
    sw = pltpu.roll(x, AT_DH, 1)
    return jnp.concatenate([jnp.where(low, x, sw), jnp.where(low, sw, x)], axis=1)


def _rope(x, cos, sin_a, sin_b):
    parts = []
    for j in range(x.shape[1] // LANES):
        xj = x[:, j * LANES:(j + 1) * LANES]
        parts.append(xj * cos + pltpu.roll(xj, LANES - 16, 1) * sin_a
                     + pltpu.roll(xj, 16, 1) * sin_b)
    return jnp.concatenate(parts, axis=1)


W_IN_CHUNK = 256


def _stream_cast(jobs, stage, sem):
    copies = [pltpu.make_async_copy(src, stage.at[k % 2], sem.at[k % 2])
              for k, (src, _) in enumerate(jobs)]
    for cp in copies[:2]:
        cp.start()
    for k, (_, dst) in enumerate(jobs):
        copies[k].wait()
        dst[...] = stage[k % 2].astype(BF16)
        if k + 2 < len(jobs):
            copies[k + 2].start()


def _proj_kernel(x_ref, ml_ref, mc_ref, g_ref, win_hbm, wlast_ref, gmat_ref, qg_ref, kg_ref, gb_ref,
                 cos_ref, sa_ref, sb_ref,
                 rgx_o, rgg_o, mq_o, mk_o, mv_o, mo_o, aq_o, ak_o, av_o, br_o, mg_o,
                 wh_ref, wt_ref, tail, stage, sem, *, layer, seq, tm, d_model):
    @pl.when(jnp.logical_and(pl.program_id(0) == 0, pl.program_id(1) == 0))
    def _():
        sc = W_IN_CHUNK
        nfull = (wt_ref.shape[1] - LANES) // sc
        jobs = [(win_hbm.at[layer, :, c * sc:(c + 1) * sc], wh_ref.at[:, c * sc:(c + 1) * sc])
                for c in range(P_HEAD // sc)]
        jobs += [(win_hbm.at[layer, :, P_HEAD + c * sc:P_HEAD + (c + 1) * sc],
                  wt_ref.at[:, c * sc:(c + 1) * sc]) for c in range(nfull)]
        _stream_cast(jobs, stage, sem)
        wt_ref[:, nfull * sc:] = wlast_ref[...]
        for r in range(0, d_model, LANES):
            tail[r:r + LANES, :] = wt_ref[r:r + LANES, N_GATE:N_GATE + tail.shape[1]]

    t0 = pl.program_id(1) * tm
    rows = t0 + lax.broadcasted_iota(jnp.int32, (tm, 1), 0)
    is_ctx = rows >= seq
    h = _ln_mod(x_ref[...], g_ref[...], _row_mod(ml_ref, mc_ref, is_ctx, 3),
                _row_mod(ml_ref, mc_ref, is_ctx, 4)).astype(BF16)

    def seg(c0, w, ref=wh_ref):
        return jnp.dot(h, ref[:, c0:c0 + w], preferred_element_type=F32)

    rgx_o[...] = seg(P_RGX, D_RNN).reshape(rgx_o.shape)
    rgg_o[...] = _gelu_tanh(seg(P_RGG, D_RNN)).astype(BF16)
    mq_o[...] = seg(P_MQ, ML_W).astype(BF16)
    mk_o[...] = (seg(P_MK, ML_W) * (ML_DH ** -0.5)).astype(BF16)
    mv_o[...] = seg(P_MV, ML_W).astype(BF16)
    mo_o[...] = _sigmoid(seg(P_MO, ML_W)).astype(BF16)
    cos, sin_a, sin_b = cos_ref[...], sa_ref[...], sb_ref[...]
    gmat = gmat_ref[...]
    q = _head_norm(seg(T_AQ, AT_W, tail), gmat, qg_ref[...])
    aq_o[...] = (_rope(q, cos, sin_a, sin_b) * (AT_DH ** -0.5 * LOG2E)).astype(BF16)
    k = _head_norm(seg(T_AK, AT_KVW, tail), gmat[:AT_KVW, :AT_KVW], kg_ref[...])
    ak_o[...] = _dup_heads(_rope(k, cos, sin_a, sin_b)).astype(BF16)
    av_o[...] = _dup_heads(seg(T_AV, AT_KVW, tail)).astype(BF16)
    for c0 in range(0, N_BRANCH * d_model, 512):
        br_o[:, c0:c0 + 512] = _sigmoid(seg(T_BR + c0, 512, tail)).astype(BF16)
    mg_o[...] = seg(0, LANES, wt_ref) + gb_ref[...]


def _proj_call(x, ml, mc, g, w_in, w_last, gmat, qg, kg, gb, cos, sin_a, sin_b, *, layer, seq):
    bsz, t, d = x.shape
    ncol = w_in.shape[-1]
    ntail = ncol - P_HEAD - N_GATE
    assert P_HEAD % W_IN_CHUNK == 0
    nfull = (ncol - P_HEAD) // W_IN_CHUNK
    assert 0 < ncol - P_HEAD - nfull * W_IN_CHUNK <= LANES
    wt_cols = nfull * W_IN_CHUNK + LANES
    tm = _pick_tile(t, (384, 128))
    kern = functools.partial(_proj_kernel, layer=layer, seq=seq, tm=tm, d_model=d)
    widths = [(D_RNN, F32), (D_RNN, BF16), (ML_W, BF16), (ML_W, BF16), (ML_W, BF16),
              (ML_W, BF16), (AT_W, BF16), (2 * AT_KVW, BF16), (2 * AT_KVW, BF16),
              (N_BRANCH * d, BF16), (LANES, F32)]
    tab_spec = pl.BlockSpec((tm, LANES), lambda b, i: (i, 0))
    out_specs = [pl.BlockSpec((None, tm, wd), lambda b, i: (b, i, 0)) for wd, _ in widths]
    out_shape = [jax.ShapeDtypeStruct((bsz, t, wd), dt) for wd, dt in widths]
    out_specs[0] = pl.BlockSpec((tm // SUBLANES, None, SUBLANES, D_RNN), lambda b, i: (i, b, 0, 0))
    out_shape[0] = jax.ShapeDtypeStruct((t // SUBLANES, bsz, SUBLANES, D_RNN), F32)
    return pl.pallas_call(
        kern,
        grid=(bsz, t // tm),
        in_specs=[
            pl.BlockSpec((None, tm, d), lambda b, i: (b, i, 0)),
            pl.BlockSpec((None, N_MOD, d), lambda b, i: (b, 0, 0)),
            _const_spec((N_MOD, d)),
            _const_spec((1, d)),
            pl.BlockSpec(memory_space=pl.ANY),
            _const_spec((d, LANES), (layer,)),
            _const_spec((AT_W, AT_W)),
            _const_spec((1, AT_W)),
            _const_spec((1, AT_KVW)),
            _const_spec((1, LANES)),
            tab_spec, tab_spec, tab_spec,
        ],
        out_specs=out_specs,
        out_shape=out_shape,
        scratch_shapes=[pltpu.VMEM((d, P_HEAD), BF16), pltpu.VMEM((d, wt_cols), BF16),
                        pltpu.VMEM((d, ntail), BF16), pltpu.VMEM((2, d, W_IN_CHUNK), F32),
                        pltpu.SemaphoreType.DMA((2,))],
        compiler_params=_cparams(("arbitrary", "arbitrary")),
        name="in_proj",
    )(x, ml, mc, g, w_in, w_last, gmat, qg, kg, gb, cos, sin_a, sin_b)


def _scan_chunk(s, rev, nl, nc):
    if rev:
        return jnp.where(s < nc, nl + nc - 1 - s, nl - 1 - (s - nc))
    return jnp.where(s < nc, nl + s, s - nc)


def _rglru_kernel(xf_ref, xfp_ref, xfn_ref, xb_ref, xbp_ref, xbn_ref, cw_ref, cb_ref, w_ref,
                  bias_ref, clam_ref, hf_ref, hb_ref, ext, a_s, b_s, h_s, hstate,
                  *, nl, nc, tc, sub):
    s = pl.program_id(0)
    bsz, c = hstate.shape[1], hstate.shape[2]
    ntb = tc // SUBLANES

    @pl.when(s == 0)
    def _():
        hstate[...] = jnp.zeros_like(hstate)

    cw = cw_ref[...]
    cb = cb_ref[...]
    for d, (x_ref, xp_ref, xn_ref) in enumerate(((xf_ref, xfp_ref, xfn_ref),
                                                (xb_ref, xbp_ref, xbn_ref))):
        chunk = _scan_chunk(s, d == 1, nl, nc)
        first = jnp.logical_or(chunk == 0, chunk == nl)
        last = jnp.logical_or(chunk == nl - 1, chunk == nl + nc - 1)
        prev = jnp.swapaxes(xp_ref[0], 0, 1)[SUBLANES - CONV_LEFT:]
        nxt = jnp.swapaxes(xn_ref[0], 0, 1)[:CONV_RIGHT]
        ext[d, 0:CONV_LEFT] = jnp.where(first, 0.0, prev)
        for i in range(ntb):
            o = CONV_LEFT + i * SUBLANES
            ext[d, o:o + SUBLANES] = jnp.swapaxes(x_ref[i], 0, 1)
        ext[d, CONV_LEFT + tc:CONV_LEFT + tc + CONV_RIGHT] = jnp.where(last, 0.0, nxt)

        def gates(i, carry, d=d):
            t0 = pl.multiple_of(i * sub, sub)
            u = cb
            for k in range(CONV_W):
                u = u + ext[d, pl.ds(t0 + k, sub)] * cw[k:k + 1, :]
            u2 = u.reshape(sub * bsz, c)
            z = jnp.dot(u2.astype(BF16), w_ref[d], preferred_element_type=F32) + bias_ref[d]
            a = jnp.exp2(clam_ref[d] * jnp.tanh(z[:, :c]) + clam_ref[d])
            gi = 0.5 * jnp.tanh(z[:, c:]) + 0.5
            s1 = 1.0 - a * a
            bb = (s1 * lax.rsqrt(jnp.maximum(s1, 1e-36))) * (gi * u2)
            a_s[d, pl.ds(t0, sub)] = a.reshape(sub, bsz, c)
            b_s[d, pl.ds(t0, sub)] = bb.reshape(sub, bsz, c)
            return carry

        lax.fori_loop(0, tc // sub, gates, 0)

    def step(k, hs):
        hf, hb = hs
        kb = tc - 1 - k
        hf = a_s[0, k] * hf + b_s[0, k]
        hb = a_s[1, kb] * hb + b_s[1, kb]
        h_s[0, k] = hf
        h_s[1, kb] = hb
        return hf, hb

    hf, hb = lax.fori_loop(0, tc, step, (hstate[0], hstate[1]), unroll=8)
    hstate[0] = hf
    hstate[1] = hb
    for d, o_ref in enumerate((hf_ref, hb_ref)):
        for i in range(ntb):
            o_ref[i] = jnp.swapaxes(h_s[d, i * SUBLANES:(i + 1) * SUBLANES], 0, 1)


def _rglru_call(rgx, cw, cb, w, bias, clam, *, seq):
    ntb_all, bsz, _, c = rgx.shape
    t = ntb_all * SUBLANES
    tc = 128
    sub = 32
    assert seq % tc == 0 and (t - seq) % tc == 0
    nl, nc = seq // tc, (t - seq) // tc
    ntb = tc // SUBLANES

    def specs(rev):
        chunk = functools.partial(_scan_chunk, rev=rev, nl=nl, nc=nc)
        return [
            pl.BlockSpec((ntb, bsz, SUBLANES, c), lambda s: (chunk(s), 0, 0, 0)),
            pl.BlockSpec((1, bsz, SUBLANES, c),
                         lambda s: (jnp.maximum(chunk(s) * ntb - 1, 0), 0, 0, 0)),
            pl.BlockSpec((1, bsz, SUBLANES, c),
                         lambda s: (jnp.minimum((chunk(s) + 1) * ntb, ntb_all - 1), 0, 0, 0)),
        ]

    kern = functools.partial(_rglru_kernel, nl=nl, nc=nc, tc=tc, sub=sub)
    fspec, bspec = specs(False), specs(True)
    return pl.pallas_call(
        kern,
        grid=(nl + nc,),
        in_specs=fspec + bspec + [
            _const_spec((CONV_W, c)),
            _const_spec((1, c)),
            _const_spec((2, c, 2 * c)),
            _const_spec((2, 1, 2 * c)),
            _const_spec((2, 1, c)),
        ],
        out_specs=[fspec[0], bspec[0]],
        out_shape=[jax.ShapeDtypeStruct(rgx.shape, F32)] * 2,
        scratch_shapes=[
            pltpu.VMEM((2, tc + CONV_W - 1, bsz, c), F32),
            pltpu.VMEM((2, tc, bsz, c), F32),
            pltpu.VMEM((2, tc, bsz, c), F32),
            pltpu.VMEM((2, tc, bsz, c), F32),
            pltpu.VMEM((2, bsz, c), F32),
        ],
        compiler_params=_cparams(("arbitrary",)),
        name="rglru",
    )(rgx, rgx, rgx, rgx, rgx, rgx, cw, cb, w, bias, clam)


def _log_sigmoid(x):
    return -(jnp.maximum(-x, 0.0) + jnp.log1p(jnp.exp(-jnp.abs(x))))


def _lane_scan(x, op, ident, rev):
    n = x.shape[1]
    lane = lax.broadcasted_iota(jnp.int32, x.shape, 1)
    sh = 1
    while sh < n:
        if rev:
            x = op(x, jnp.where(lane < n - sh, pltpu.roll(x, n - sh, 1), ident))
        else:
            x = op(x, jnp.where(lane >= sh, pltpu.roll(x, sh, 1), ident))
        sh *= 2
    return x


def _mlgate_kernel(g_ref, row_o, col_o, cb_s, gg_s, ml_s, tot_s, wm_s, mo_s, mn_s, *, nl, nc):
    L = ML_CHUNK
    H = ML_HEADS
    U = ML_UNITS
    is_bwd = lax.broadcasted_iota(jnp.int32, (U, L), 0) >= H
    for c in range(nl + nc):
        lanes = slice(c * L, (c + 1) * L)
        g16 = g_ref[:, lanes]
        li = jnp.concatenate([g16[0:H], g16[2 * H:3 * H]], axis=0)
        lf = _log_sigmoid(jnp.concatenate([g16[H:2 * H], g16[3 * H:4 * H]], axis=0))
        pre = _lane_scan(lf, jnp.add, 0.0, False)
        suf = _lane_scan(lf, jnp.add, 0.0, True)
        tot = pre + suf - lf
        cb = jnp.where(is_bwd, suf, pre)
        gg = li - cb
        pmax = _lane_scan(gg, jnp.maximum, NEG, False)
        smax = _lane_scan(gg, jnp.maximum, NEG, True)
        cb_s[:, lanes] = cb
        gg_s[:, lanes] = gg
        ml_s[:, lanes] = cb + jnp.where(is_bwd, smax, pmax)
        tot_s[:, lanes] = tot
        wm_s[:, lanes] = tot + jnp.maximum(pmax, smax)

    m = jnp.zeros((U, L), F32)
    for s in range(nl + nc):
        cf = nl + s if s < nc else s - nc
        cr = nl + nc - 1 - s if s < nc else nl - 1 - (s - nc)
        lf_, lr_ = slice(cf * L, (cf + 1) * L), slice(cr * L, (cr + 1) * L)
        tot = jnp.where(is_bwd, tot_s[:, lr_], tot_s[:, lf_])
        wm = jnp.where(is_bwd, wm_s[:, lr_], wm_s[:, lf_])
        m_new = jnp.maximum(tot + m, wm)
        mo_s[0:H, lf_] = m[0:H]
        mo_s[H:U, lr_] = m[H:U]
        mn_s[0:H, lf_] = m_new[0:H]
        mn_s[H:U, lr_] = m_new[H:U]
        m = m_new

    cb = cb_s[...]
    gg = gg_s[...]
    tot = tot_s[...]
    mo = mo_s[...]
    mn = mn_s[...]
    m_t = jnp.maximum(cb + mo, ml_s[...])
    row_o[0:U, :] = gg
    row_o[U:2 * U, :] = jnp.exp(tot + gg - mn)
    row_o[2 * U:3 * U, :] = jnp.exp(tot + mo - mn)
    row_o[3 * U:4 * U, :] = mo
    col_o[0:U, :] = cb - m_t
    col_o[U:2 * U, :] = jnp.exp(-m_t)


def _mlgate_call(gt, *, seq):
    bsz, ng, t = gt.shape
    nl, nc = seq // ML_CHUNK, (t - seq) // ML_CHUNK
    kern = functools.partial(_mlgate_kernel, nl=nl, nc=nc)
    rout = pl.BlockSpec((None, 4 * ML_UNITS, t), lambda b: (b, 0, 0))
    cout = pl.BlockSpec((None, 2 * ML_UNITS, t), lambda b: (b, 0, 0))
    return pl.pallas_call(
        kern,
        grid=(bsz,),
        in_specs=[pl.BlockSpec((None, ng, t), lambda b: (b, 0, 0))],
        out_specs=[rout, cout],
        out_shape=[jax.ShapeDtypeStruct((bsz, 4 * ML_UNITS, t), F32),
                   jax.ShapeDtypeStruct((bsz, 2 * ML_UNITS, t), F32)],
        scratch_shapes=[pltpu.VMEM((ML_UNITS, t), F32)] * 7,
        compiler_params=_cparams(("arbitrary",)),
        name="mlstm_gates",
    )(gt)


def _mlstm_kernel(qf_ref, ktf_ref, vf_ref, rf_ref, cf_ref, qb_ref, ktb_ref, vb_ref, rb_ref, cb_ref,
                  hf_ref, hb_ref, c_st):
    s = pl.program_id(1)

    @pl.when(s == 0)
    def _():
        c_st[...] = jnp.zeros_like(c_st)

    L, dh, U = ML_CHUNK, ML_DH, ML_UNITS
    ti = lax.broadcasted_iota(jnp.int32, (L, L), 0)
    si = lax.broadcasted_iota(jnp.int32, (L, L), 1)
    ones = jnp.ones((L, dh), BF16)
    for bb in range(qf_ref.shape[0]):
        for d, (q_ref, kt_ref, v_ref, r_ref, c_ref, o_ref) in enumerate(
                ((qf_ref, ktf_ref, vf_ref, rf_ref, cf_ref, hf_ref),
                 (qb_ref, ktb_ref, vb_ref, rb_ref, cb_ref, hb_ref))):
            tri = (si >= ti) if d == 1 else (si <= ti)
            rows = r_ref[bb]
            cols = c_ref[bb]
            for hd in range(ML_HEADS):
                u = d * ML_HEADS + hd
                sl = slice(hd * dh, (hd + 1) * dh)
                q = q_ref[bb, :, sl]
                kt = kt_ref[bb, sl, :]
                v_ext = jnp.concatenate([v_ref[bb, :, sl], ones], axis=1)
                g_row = rows[u:u + 1, :]
                w_row = rows[U + u:U + u + 1, :]
                a_row = rows[2 * U + u:2 * U + u + 1, :]
                m_row = rows[3 * U + u:3 * U + u + 1, :]
                xb = jnp.broadcast_to(cols[:, u:u + 1], (L, L))
                emt = cols[:, U + u:U + u + 1]
                c_old = c_st[bb * U + u]
                qkc = jnp.dot(q, jnp.concatenate([kt, c_old.astype(BF16)], axis=1),
                              preferred_element_type=F32)
                dm = jnp.exp(jnp.where(tri, xb + g_row, NEG))
                dec = jnp.exp(xb + m_row)
                sc = (qkc[:, :L] * dm).astype(BF16)
                ktw = (kt.astype(F32) * w_row).astype(BF16)
                sv = jnp.dot(jnp.concatenate([sc, ktw], axis=0), v_ext,
                             preferred_element_type=F32)
                ab = sv[:L] + jnp.concatenate([dec, dec], axis=1) * qkc[:, L:]
                o_ref[bb, :, sl] = ab[:, :dh] / jnp.maximum(jnp.abs(ab[:, dh:]), emt)
                c_st[bb * U + u] = jnp.concatenate([a_row, a_row], axis=1) * c_old + sv[L:]


def _mlstm_call(mq, mkt, mv, rowp, colp, *, seq):
    bsz, t, w = mq.shape
    nl, nc = seq // ML_CHUNK, (t - seq) // ML_CHUNK
    cf = functools.partial(_scan_chunk, rev=False, nl=nl, nc=nc)
    cr = functools.partial(_scan_chunk, rev=True, nl=nl, nc=nc)
    nr = rowp.shape[1]
    bpb = 8 if bsz % 8 == 0 else 1

    def tok(ch, wd):
        return pl.BlockSpec((bpb, ML_CHUNK, wd), lambda b, s: (b, ch(s), 0))

    def tr(ch, rows):
        return pl.BlockSpec((bpb, rows, ML_CHUNK), lambda b, s: (b, 0, ch(s)))

    def side(ch):
        return [tok(ch, w), tr(ch, w), tok(ch, w), tr(ch, nr), tok(ch, colp.shape[2])]

    return pl.pallas_call(
        _mlstm_kernel,
        grid=(bsz // bpb, nl + nc),
        in_specs=side(cf) + side(cr),
        out_specs=[tok(cf, w), tok(cr, w)],
        out_shape=[jax.ShapeDtypeStruct((bsz, t, w), F32)] * 2,
        scratch_shapes=[pltpu.VMEM((bpb * ML_UNITS, ML_DH, 2 * ML_DH), F32)],
        compiler_params=_cparams(("arbitrary", "arbitrary")),
        name="mlstm",
    )(mq, mkt, mv, rowp, colp, mq, mkt, mv, rowp, colp)


def _attn_kernel(sink_ref, q_ref, k_ref, v_ref, o_ref, vext, *, seq, qbs):
    kv = pl.program_id(1)
    i = pl.program_id(2)
    blk = ATT_BLOCK
    t = k_ref.shape[0]
    ctx = t - seq
    nlb = seq // blk
    band = 3 * blk

    @pl.when(i == 0)
    def _():
        vext[:, 0:LANES] = v_ref[...]
        vext[:, LANES:2 * LANES] = jnp.ones((t, LANES), BF16)

    lane = lax.broadcasted_iota(jnp.int32, (blk, LANES), 1)
    low = lane < AT_DH
    zero = jnp.zeros((blk, LANES), BF16)
    rows = AT_G * blk
    gi = lax.broadcasted_iota(jnp.int32, (rows, 1), 0) // blk
    snk = jnp.zeros((rows, 1), F32)
    for g in range(AT_G):
        snk = jnp.where(gi == g, sink_ref[kv * AT_G + g] * LOG2E, snk)
    ti = lax.broadcasted_iota(jnp.int32, (blk, band), 0)
    ci = lax.broadcasted_iota(jnp.int32, (blk, band), 1)
    nt = (((1,), (1,)), ((), ()))
    k_ctx = k_ref[seq:t, :]
    v_ctx = vext[seq:t, :]

    for j in range(qbs):
        qb = i * qbs + j
        s0 = pl.multiple_of(jnp.clip((qb - 1) * blk, 0, t - band), blk)
        rel = ci - ti + (s0 - qb * blk + blk)
        lim = jnp.where(qb < nlb, seq - s0, 0)
        ok = jnp.logical_and(jnp.logical_and(rel >= 0, rel <= 2 * blk), ci < lim)
        bias = jnp.where(ok, 0.0, NEG)
        bias = jnp.concatenate([bias] * AT_G, axis=0)

        qs = []
        for h2 in range(AT_G // 2):
            qj = q_ref[j * blk:(j + 1) * blk, h2 * LANES:(h2 + 1) * LANES]
            qs += [jnp.where(low, qj, zero), jnp.where(low, zero, qj)]
        q_all = jnp.concatenate(qs, axis=0)
        s_band = lax.dot_general(q_all, k_ref[pl.ds(s0, band), :], nt,
                                 preferred_element_type=F32) + bias
        s_ctx = lax.dot_general(q_all, k_ctx, nt, preferred_element_type=F32)
        m = jnp.maximum(jnp.maximum(jnp.max(s_band, axis=1, keepdims=True),
                                    jnp.max(s_ctx, axis=1, keepdims=True)), snk)
        p_band = jnp.exp2(s_band - m).astype(BF16)
        p_ctx = jnp.exp2(s_ctx - m).astype(BF16)
        o2 = (jnp.dot(p_band, vext[pl.ds(s0, band), :], preferred_element_type=F32)
              + jnp.dot(p_ctx, v_ctx, preferred_element_type=F32))
        o2 = o2[:, :LANES] / (o2[:, LANES:] + jnp.exp2(snk - m))
        for h2 in range(AT_G // 2):
            a = o2[(2 * h2) * blk:(2 * h2 + 1) * blk, :]
            b = o2[(2 * h2 + 1) * blk:(2 * h2 + 2) * blk, :]
            o_ref[j * blk:(j + 1) * blk, h2 * LANES:(h2 + 1) * LANES] = (
                jnp.where(low, a, b).astype(o_ref.dtype))


def _attn_call(sink, aq, akd, avd, *, seq):
    bsz, t, _ = aq.shape
    blk = ATT_BLOCK
    qbs = 3
    assert t % (qbs * blk) == 0 and t >= 3 * blk
    gw = AT_G * AT_DH
    kern = functools.partial(_attn_kernel, seq=seq, qbs=qbs)
    whole = pl.BlockSpec((None, t, LANES), lambda b, kv, i: (b, 0, kv))
    return pl.pallas_call(
        kern,
        grid=(bsz, AT_KV, t // (qbs * blk)),
        in_specs=[
            pl.BlockSpec(memory_space=pltpu.SMEM),
            pl.BlockSpec((None, qbs * blk, gw), lambda b, kv, i: (b, i, kv)),
            whole, whole,
        ],
        out_specs=pl.BlockSpec((None, qbs * blk, gw), lambda b, kv, i: (b, i, kv)),
        out_shape=jax.ShapeDtypeStruct((bsz, t, AT_W), BF16),
        scratch_shapes=[pltpu.VMEM((t, 2 * LANES), BF16)],
        compiler_params=_cparams(("arbitrary", "arbitrary", "arbitrary")),
        name="attn",
    )(sink, aq, akd, avd)


def _merge_kernel(x_ref, ml_ref, mc_ref, rf_ref, rb_ref, gg_ref, hf_ref, hb_ref, mo_ref, yc_ref,
                  br_ref, ng_ref, wb_ref, wo_ref, g_ref, w1_hbm, w3_hbm, w2_hbm, o_ref, act_ref,
                  w1b, w3b, w2b, stg_c, stg_r, sem, *, lead, seq, tm, ck):
    first = jnp.logical_and(pl.program_id(0) == 0, pl.program_id(1) == 0)
    stream = _FfnWeightStream((w1_hbm, w3_hbm, w2_hbm), lead, (w1b, w3b, w2b),
                              stg_c, stg_r, sem, ck)

    @pl.when(first)
    def _():
        stream.prime()

    t0 = pl.program_id(1) * tm
    rows = t0 + lax.broadcasted_iota(jnp.int32, (tm, 1), 0)
    is_ctx = rows >= seq
    d = x_ref.shape[1]
    rsum = (rf_ref[...] + rb_ref[...]).reshape(tm, D_RNN)
    ya = (rsum * gg_ref[...].astype(F32)).astype(BF16)
    hsum = hf_ref[...] + hb_ref[...]
    parts = []
    for hd in range(ML_HEADS):
        hh = hsum[:, hd * ML_DH:(hd + 1) * ML_DH]
        parts.append(hh * lax.rsqrt(jnp.mean(hh * hh, axis=-1, keepdims=True) + EPS))
    yb = (jnp.concatenate(parts, axis=1) * ng_ref[...] * mo_ref[...].astype(F32)).astype(BF16)
    m = (br_ref[:, 0:d].astype(F32) * jnp.dot(ya, wb_ref[0], preferred_element_type=F32)
         + br_ref[:, d:2 * d].astype(F32) * jnp.dot(yb, wb_ref[1], preferred_element_type=F32)
         + br_ref[:, 2 * d:3 * d].astype(F32) * jnp.dot(yc_ref[...], wb_ref[2],
                                                       preferred_element_type=F32))
    y = jnp.dot(m.astype(BF16), wo_ref[...], preferred_element_type=F32)
    x1 = x_ref[...] + _row_mod(ml_ref, mc_ref, is_ctx, 5) * y
    @pl.when(first)
    def _():
        o_ref[...] = _ffn_body(x1, ml_ref, mc_ref, is_ctx, g_ref, w1b, w3b, w2b, act_ref,
                               base=6, ck=ck, stream=stream)

    @pl.when(jnp.logical_not(first))
    def _():
        o_ref[...] = _ffn_body(x1, ml_ref, mc_ref, is_ctx, g_ref, w1b, w3b, w2b, act_ref,
                               base=6, ck=ck)


def _merge_call(x, ml, mc, rf, rb, gg, hf, hb, mo, yc, br, ng, wb, wo, g, w1, w3, w2,
                *, layer, seq, out_rows):
    bsz, t, d = x.shape
    dff = w1.shape[-1]
    tm = _pick_tile(t, (384, 128))
    ck = _pick_tile(dff, (256, 128))
    kern = functools.partial(_merge_kernel, lead=(layer, 1), seq=seq, tm=tm, ck=ck)
    hbm = pl.BlockSpec(memory_space=pl.ANY)

    def tok(wd):
        return pl.BlockSpec((None, tm, wd), lambda b, i: (b, i, 0))

    rspec = pl.BlockSpec((tm // SUBLANES, None, SUBLANES, D_RNN), lambda b, i: (i, b, 0, 0))

    return pl.pallas_call(
        kern,
        grid=(bsz, pl.cdiv(out_rows, tm)),
        in_specs=[
            tok(d),
            pl.BlockSpec((None, N_MOD, d), lambda b, i: (b, 0, 0)),
            _const_spec((N_MOD, d)),
            rspec, rspec, tok(D_RNN), tok(ML_W), tok(ML_W), tok(ML_W), tok(AT_W),
            tok(N_BRANCH * d),
            _const_spec((1, ML_W)),
            _const_spec((N_BRANCH, BRANCH_W, d), (layer,)),
            _const_spec((d, d), (layer,)),
            _const_spec((1, d)),
            hbm, hbm, hbm,
        ],
        out_specs=tok(d),
        out_shape=jax.ShapeDtypeStruct((bsz, out_rows, d), F32),
        scratch_shapes=[pltpu.VMEM((tm, dff), BF16)] + _ffn_weight_scratch(d, dff, ck),
        compiler_params=_cparams(("arbitrary", "arbitrary")),
        name="merge_ffn",
    )(x, ml, mc, rf, rb, gg, hf, hb, mo, yc, br, ng, wb, wo, g, w1, w3, w2)


def _block_diag(w):
    n, bi, bj = w.shape
    eye = jnp.eye(n, dtype=w.dtype)
    return (eye[:, None, :, None] * w[:, :, None, :]).reshape(n * bi, n * bj)


def _rope_tables(seq, t):
    rows = seq // GRID_W
    row = jnp.repeat(jnp.arange(rows), GRID_W).astype(F32)
    col = jnp.broadcast_to(jnp.arange(GRID_W), (rows, GRID_W)).reshape(-1).astype(F32)
    half = AT_DH // 2
    inv = ROPE_BASE ** (-jnp.arange(0, half, 2, dtype=F32) / half)
    ar = row[:, None] * inv
    ac = col[:, None] * inv
    ang = jnp.concatenate([ar, ar, ac, ac], axis=-1)
    cos = jnp.concatenate([jnp.cos(ang), jnp.ones((t - seq, AT_DH), F32)], axis=0)
    sin = jnp.concatenate([jnp.sin(ang), jnp.zeros((t - seq, AT_DH), F32)], axis=0)
    cos = jnp.tile(cos, (1, LANES // AT_DH))
    sin = jnp.tile(sin, (1, LANES // AT_DH))
    first_half = (jnp.arange(LANES) % 32) < 16
    sin_a = jnp.where(first_half, -sin, 0.0)
    sin_b = jnp.where(first_half, 0.0, sin)
    return cos, sin_a, sin_b


def kernel(x, c, ctx, c_ctx, ada_w, ada_b, norm_g, ffn_w1, ffn_w3, ffn_w2, w_in, rg_conv_w,
           rg_conv_b, rg_wa, rg_ba, rg_wi, rg_bi, rg_lam, ml_gate_b, ml_norm_g, at_qn_g,
           at_kn_g, at_sink, w_branch, w_out):
    bsz, seq, d = x.shape
    nctx = ctx.shape[1]
    t = seq + nctx
    depth = ada_w.shape[0]

    xs = jnp.concatenate([x, ctx], axis=1)
    mod_rows = 2 * SUBLANES
    cc = jnp.concatenate([c, c_ctx[None, :], jnp.zeros((mod_rows - bsz - 1, d), F32)], axis=0)
    mods = _ada_call(cc, ada_w, ada_b).reshape(depth, mod_rows, N_MOD, d)
    cos, sin_a, sin_b = _rope_tables(seq, t)
    gmat = _block_diag(jnp.full((AT_HEADS, AT_DH, AT_DH), 1.0 / AT_DH, F32)).astype(BF16)
    w1, w3, w2 = ffn_w1, ffn_w3, ffn_w2
    n_whole = (w_in.shape[-1] - P_HEAD) // W_IN_CHUNK * W_IN_CHUNK + P_HEAD
    w_last = w_in[..., n_whole:]
    w_last = jnp.pad(w_last, ((0, 0), (0, 0), (0, LANES - w_last.shape[-1]))).astype(BF16)
    w_br, w_o = w_branch.astype(BF16), w_out.astype(BF16)

    for l in range(depth):
        ml = mods[l, :bsz]
        mc = mods[l, bsz]
        last = l == depth - 1

        xs = _ffn_call(xs, ml, mc, norm_g[l, 0][None, :], w1, w3, w2,
                       lead=(l, 0), base=0, seq=seq, out_rows=t)

        qg = jnp.tile(at_qn_g[l], AT_HEADS)[None, :]
        kg = jnp.tile(at_kn_g[l], AT_KV)[None, :]
        gb = jnp.concatenate([ml_gate_b[l], jnp.zeros((LANES - N_GATE,), F32)])[None, :]
        (rgx, rgg, mq, mk, mv, mo, aq, akd, avd, br, mg) = _proj_call(
            xs, ml, mc, norm_g[l, 1][None, :], w_in, w_last, gmat, qg, kg, gb,
            cos, sin_a, sin_b, layer=l, seq=seq)

        wcat = jnp.stack([jnp.concatenate([_block_diag(rg_wa[l, dr]), _block_diag(rg_wi[l, dr])],
                                          axis=1) for dr in range(2)])
        wcat = (0.5 * wcat).astype(BF16)
        bias = 0.5 * jnp.concatenate([rg_ba[l], rg_bi[l]], axis=1)[:, None, :]
        clam = (-0.5 * LOG2E * LRU_C * jax.nn.softplus(-rg_lam[l]))[:, None, :]
        rhf, rhb = _rglru_call(rgx, rg_conv_w[l], rg_conv_b[l][None, :], wcat, bias, clam,
                               seq=seq)

        gt = jnp.swapaxes(mg[:, :, :N_GATE], 1, 2)
        rowp, colsrc = _mlgate_call(gt, seq=seq)
        colp = jnp.swapaxes(colsrc, 1, 2)
        mkt = jnp.swapaxes(mk, 1, 2)
        mhf, mhb = _mlstm_call(mq, mkt, mv, rowp, colp, seq=seq)

        yc = _attn_call(at_sink[l], aq, akd, avd, seq=seq)

        xs = _merge_call(xs, ml, mc, rhf, rhb, rgg, mhf, mhb, mo, yc, br, ml_norm_g[l][None, :],
                         w_br, w_o, norm_g[l, 2][None, :], w1, w3, w2,
                         layer=l, seq=seq, out_rows=seq if last else t)
    return xs
```

```python
import functools

import jax
import jax.numpy as jnp
from jax import lax
from jax.experimental import pallas as pl
from jax.experimental.pallas import tpu as pltpu

F32 = jnp.float32
BF16 = jnp.bfloat16

EPS = 1e-6
NEG = -1e30
TINY = 1e-36
LOG2E = 1.4426950408889634
N_MOD = 9
GRID_W = 64
ROPE_BASE = 10000.0

D_RNN = 512
RNN_BLOCKS = 8
RNN_BLOCK = D_RNN // RNN_BLOCKS
CONV_W = 4
CONV_LEFT = 2
CONV_RIGHT = CONV_W - 1 - CONV_LEFT
LRU_C = 8.0

ML_HEADS = 4
ML_DH = 128
ML_W = ML_HEADS * ML_DH
ML_CHUNK = 128
ML_UNITS = 2 * ML_HEADS

AT_HEADS = 8
AT_KV = 2
AT_DH = 64
AT_G = AT_HEADS // AT_KV
AT_W = AT_HEADS * AT_DH
AT_KVW = AT_KV * AT_DH
ATT_BLOCK = 128
ROPE_Q = AT_DH // 4

N_BRANCH = 3
BRANCH_W = 512

LANES = 128
SUBLANES = 8
VMEM_LIMIT = 56 * 1024 * 1024

P_RGX = 0
P_RGG = P_RGX + D_RNN
P_MQ = P_RGG + D_RNN
P_MK = P_MQ + ML_W
P_MV = P_MK + ML_W
P_MO = P_MV + ML_W
P_HEAD = P_MO + ML_W
N_GATE = 4 * ML_HEADS
T_AQ = 0
T_AK = T_AQ + AT_W
T_AV = T_AK + AT_KVW
T_BR = T_AV + AT_KVW
GATE_SEG = 512


def _cparams(sem):
    return pltpu.CompilerParams(dimension_semantics=sem, vmem_limit_bytes=VMEM_LIMIT)


def _const_spec(shape, lead=()):
    nd = len(shape)
    idx = tuple(lead) + (0,) * nd
    return pl.BlockSpec((None,) * len(lead) + tuple(shape), lambda *_: idx,
                        pipeline_mode=pl.Buffered(1))


def _pick_tile(total, candidates):
    for c in candidates:
        if total % c == 0:
            return c
    raise ValueError(f"no tile for {total}")


def _sigmoid(x):
    return jax.nn.sigmoid(x)


def _ln_mod(x, g, shift, scale):
    ms = jnp.mean(x * x, axis=-1, keepdims=True)
    return x * lax.rsqrt(ms + EPS) * g * (1.0 + scale) + shift


def _row_mod(ml_ref, mc_ref, is_ctx, i):
    return jnp.where(is_ctx, mc_ref[i:i + 1, :], ml_ref[i:i + 1, :])


def _ada_kernel(cc_ref, w_ref, b_ref, o_ref):
    cc = cc_ref[...]
    s = cc * _sigmoid(cc)
    o_ref[...] = jnp.dot(s.astype(BF16), w_ref[...].astype(BF16),
                         preferred_element_type=F32) + b_ref[...]


def _ada_call(cc, ada_w, ada_b):
    depth, d, nout = ada_w.shape
    rows = cc.shape[0]
    tn = _pick_tile(nout, (1536, 1024, 512, 256, 128))
    return pl.pallas_call(
        _ada_kernel,
        grid=(depth, nout // tn),
        in_specs=[
            pl.BlockSpec((rows, d), lambda l, j: (0, 0)),
            pl.BlockSpec((None, d, tn), lambda l, j: (l, 0, j)),
            pl.BlockSpec((None, 1, tn), lambda l, j: (l, 0, j)),
        ],
        out_specs=pl.BlockSpec((None, rows, tn), lambda l, j: (l, 0, j)),
        out_shape=jax.ShapeDtypeStruct((depth, rows, nout), F32),
        compiler_params=_cparams(("arbitrary", "arbitrary")),
        name="ada_mod",
    )(cc, ada_w, ada_b.reshape(depth, 1, nout))


def _ffn_body(x, ml_ref, mc_ref, is_ctx, g_ref, w1_ref, w3_ref, w2_ref, act_ref, *, base, ck,
              stream=None):
    h = _ln_mod(x, g_ref[...], _row_mod(ml_ref, mc_ref, is_ctx, base),
                _row_mod(ml_ref, mc_ref, is_ctx, base + 1)).astype(BF16)
    dff = w1_ref.shape[1]
    nck = dff // ck
    if stream is not None:
        stream.land(0)
        stream.refill(0)
    for j in range(nck):
        if stream is not None and j + 1 < nck:
            stream.land(j + 1)
        a = jnp.dot(h, w1_ref[:, j * ck:(j + 1) * ck], preferred_element_type=F32)
        b = jnp.dot(h, w3_ref[:, j * ck:(j + 1) * ck], preferred_element_type=F32)
        act_ref[:, j * ck:(j + 1) * ck] = (a * _sigmoid(a) * b).astype(BF16)
        if stream is not None and j + 1 < nck:
            stream.refill(j + 1)
    y = jnp.dot(act_ref[...], w2_ref[...], preferred_element_type=F32)
    return x + 0.5 * _row_mod(ml_ref, mc_ref, is_ctx, base + 2) * y


def _ffn_weight_scratch(d, dff, ck):
    return [pltpu.VMEM((d, dff), BF16), pltpu.VMEM((d, dff), BF16), pltpu.VMEM((dff, d), BF16),
            pltpu.VMEM((2, 2, d, ck), F32), pltpu.VMEM((2, ck, d), F32),
            pltpu.SemaphoreType.DMA((2, 3))]


class _FfnWeightStream:
    def __init__(self, hbm, lead, resident, stg_c, stg_r, sem, ck):
        self.hbm, self.lead, self.resident = hbm, lead, resident
        self.stg_c, self.stg_r, self.sem, self.ck = stg_c, stg_r, sem, ck
        self.nck = resident[0].shape[1] // ck

    def _stages(self, j):
        s = j % 2
        return (self.stg_c.at[s, 0], self.stg_c.at[s, 1], self.stg_r.at[s])

    def _copies(self, j):
        l0, l1 = self.lead
        sl = slice(j * self.ck, (j + 1) * self.ck)
        srcs = (self.hbm[0].at[l0, l1, :, sl], self.hbm[1].at[l0, l1, :, sl],
                self.hbm[2].at[l0, l1, sl, :])
        return [pltpu.make_async_copy(src, stg, self.sem.at[j % 2, i])
                for i, (src, stg) in enumerate(zip(srcs, self._stages(j)))]

    def start(self, j):
        for cp in self._copies(j):
            cp.start()

    def prime(self):
        self.start(0)
        if self.nck > 1:
            self.start(1)

    def land(self, j):
        sl = slice(j * self.ck, (j + 1) * self.ck)
        for cp in self._copies(j):
            cp.wait()
        w1b, w3b, w2b = self.resident
        s1, s3, s2 = self._stages(j)
        w1b[:, sl] = s1[...].astype(BF16)
        w3b[:, sl] = s3[...].astype(BF16)
        w2b[sl, :] = s2[...].astype(BF16)

    def refill(self, j):
        if j + 2 < self.nck:
            self.start(j + 2)


def _ffn_kernel(x_ref, ml_ref, mc_ref, g_ref, w1_hbm, w3_hbm, w2_hbm, o_ref, act_ref,
                w1b, w3b, w2b, stg_c, stg_r, sem, *, lead, base, seq, tm, ck):
    t0 = pl.program_id(1) * tm
    rows = t0 + lax.broadcasted_iota(jnp.int32, (tm, 1), 0)
    is_ctx = rows >= seq
    first = jnp.logical_and(pl.program_id(0) == 0, pl.program_id(1) == 0)

    @pl.when(first)
    def _():
        stream = _FfnWeightStream((w1_hbm, w3_hbm, w2_hbm), lead, (w1b, w3b, w2b),
                                  stg_c, stg_r, sem, ck)
        stream.prime()
        o_ref[...] = _ffn_body(x_ref[...], ml_ref, mc_ref, is_ctx, g_ref, w1b, w3b, w2b,
                               act_ref, base=base, ck=ck, stream=stream)

    @pl.when(jnp.logical_not(first))
    def _():
        o_ref[...] = _ffn_body(x_ref[...], ml_ref, mc_ref, is_ctx, g_ref, w1b, w3b, w2b,
                               act_ref, base=base, ck=ck)


def _ffn_call(x, ml, mc, g, w1, w3, w2, *, lead, base, seq, out_rows):
    bsz, t, d = x.shape
    dff = w1.shape[-1]
    tm = _pick_tile(t, (768, 384, 128))
    ck = _pick_tile(dff, (256, 128))
    kern = functools.partial(_ffn_kernel, lead=lead, base=base, seq=seq, tm=tm, ck=ck)
    hbm = pl.BlockSpec(memory_space=pl.ANY)
    return pl.pallas_call(
        kern,
        grid=(bsz, pl.cdiv(out_rows, tm)),
        in_specs=[
            pl.BlockSpec((None, tm, d), lambda b, i: (b, i, 0)),
            pl.BlockSpec((None, N_MOD, d), lambda b, i: (b, 0, 0)),
            _const_spec((N_MOD, d)),
            _const_spec((1, d)),
            hbm, hbm, hbm,
        ],
        out_specs=pl.BlockSpec((None, tm, d), lambda b, i: (b, i, 0)),
        out_shape=jax.ShapeDtypeStruct((bsz, out_rows, d), F32),
        scratch_shapes=[pltpu.VMEM((tm, dff), BF16)] + _ffn_weight_scratch(d, dff, ck),
        compiler_params=_cparams(("arbitrary", "arbitrary")),
        name="ffn",
    )(x, ml, mc, g, w1, w3, w2)


def _gelu_tanh(x):
    return 0.5 * x * (1.0 + jnp.tanh(0.7978845608028654 * (x + 0.044715 * (x * x * x))))


def _head_norm(x, gmat, g):
    ms = jnp.dot((x * x).astype(BF16), gmat, preferred_element_type=F32)
    return x * lax.rsqrt(ms + EPS) * g


def _dup_heads(x):
    low = lax.broadcasted_iota(jnp.int32, x.shape, 1) < AT_DH
    sw = pltpu.roll(x, AT_DH, 1)
    return jnp.concatenate([jnp.where(low, x, sw), jnp.where(low, sw, x)], axis=1)


def _rope(x, cos, sin_a, sin_b):
    parts = []
    for j in range(x.shape[1] // LANES):
        xj = x[:, j * LANES:(j + 1) * LANES]
        parts.append(xj * cos + pltpu.roll(xj, LANES - ROPE_Q, 1) * sin_a
                     + pltpu.roll(xj, ROPE_Q, 1) * sin_b)
    return jnp.concatenate(parts, axis=1)


def _proj_kernel(x_ref, ml_ref, mc_ref, g_ref, wh_ref, wt_ref, gmat_ref, qg_ref, kg_ref, gb_ref,
                 cos_ref, sa_ref, sb_ref,
                 rgx_o, rgg_o, mq_o, mk_o, mv_o, mo_o, aq_o, ak_o, av_o, br_o, mg_o,
                 tail, *, seq, tm, d_model):
    @pl.when(jnp.logical_and(pl.program_id(0) == 0, pl.program_id(1) == 0))
    def _():
        for r in range(0, d_model, LANES):
            tail[r:r + LANES, :] = wt_ref[r:r + LANES, N_GATE:]

    t0 = pl.program_id(1) * tm
    rows = t0 + lax.broadcasted_iota(jnp.int32, (tm, 1), 0)
    is_ctx = rows >= seq
    h = _ln_mod(x_ref[...], g_ref[...], _row_mod(ml_ref, mc_ref, is_ctx, 3),
                _row_mod(ml_ref, mc_ref, is_ctx, 4)).astype(BF16)

    def seg(c0, w, ref=wh_ref):
        return jnp.dot(h, ref[:, c0:c0 + w], preferred_element_type=F32)

    rgx_o[...] = seg(P_RGX, D_RNN).reshape(rgx_o.shape)
    rgg_o[...] = _gelu_tanh(seg(P_RGG, D_RNN)).astype(BF16)
    mq_o[...] = seg(P_MQ, ML_W).astype(BF16)
    mk_o[...] = (seg(P_MK, ML_W) * (ML_DH ** -0.5)).astype(BF16)
    mv_o[...] = seg(P_MV, ML_W).astype(BF16)
    mo_o[...] = _sigmoid(seg(P_MO, ML_W)).astype(BF16)
    cos, sin_a, sin_b = cos_ref[...], sa_ref[...], sb_ref[...]
    gmat = gmat_ref[...]
    q = _head_norm(seg(T_AQ, AT_W, tail), gmat, qg_ref[...])
    aq_o[...] = (_rope(q, cos, sin_a, sin_b) * (AT_DH ** -0.5 * LOG2E)).astype(BF16)
    k = _head_norm(seg(T_AK, AT_KVW, tail), gmat[:AT_KVW, :AT_KVW], kg_ref[...])
    ak_o[...] = _dup_heads(_rope(k, cos, sin_a, sin_b)).astype(BF16)
    av_o[...] = _dup_heads(seg(T_AV, AT_KVW, tail)).astype(BF16)
    for c0 in range(0, N_BRANCH * d_model, GATE_SEG):
        br_o[:, c0:c0 + GATE_SEG] = _sigmoid(seg(T_BR + c0, GATE_SEG, tail)).astype(BF16)
    mg_o[...] = seg(0, LANES, wt_ref) + gb_ref[...]


def _proj_call(x, ml, mc, g, wh, wt, gmat, qg, kg, gb, cos, sin_a, sin_b, *, layer, seq):
    bsz, t, d = x.shape
    ntail = wt.shape[-1] - N_GATE
    tm = _pick_tile(t, (384, 128))
    kern = functools.partial(_proj_kernel, seq=seq, tm=tm, d_model=d)
    widths = [(D_RNN, F32), (D_RNN, BF16), (ML_W, BF16), (ML_W, BF16), (ML_W, BF16),
              (ML_W, BF16), (AT_W, BF16), (2 * AT_KVW, BF16), (2 * AT_KVW, BF16),
              (N_BRANCH * d, BF16), (LANES, F32)]
    tab_spec = pl.BlockSpec((tm, LANES), lambda b, i: (i, 0))
    out_specs = [pl.BlockSpec((None, tm, wd), lambda b, i: (b, i, 0)) for wd, _ in widths]
    out_shape = [jax.ShapeDtypeStruct((bsz, t, wd), dt) for wd, dt in widths]
    out_specs[0] = pl.BlockSpec((tm // SUBLANES, None, SUBLANES, D_RNN), lambda b, i: (i, b, 0, 0))
    out_shape[0] = jax.ShapeDtypeStruct((t // SUBLANES, bsz, SUBLANES, D_RNN), F32)
    return pl.pallas_call(
        kern,
        grid=(bsz, t // tm),
        in_specs=[
            pl.BlockSpec((None, tm, d), lambda b, i: (b, i, 0)),
            pl.BlockSpec((None, N_MOD, d), lambda b, i: (b, 0, 0)),
            _const_spec((N_MOD, d)),
            _const_spec((1, d)),
            _const_spec((d, wh.shape[-1]), (layer,)),
            _const_spec((d, wt.shape[-1]), (layer,)),
            _const_spec((AT_W, AT_W)),
            _const_spec((1, AT_W)),
            _const_spec((1, AT_KVW)),
            _const_spec((1, LANES)),
            tab_spec, tab_spec, tab_spec,
        ],
        out_specs=out_specs,
        out_shape=out_shape,
        scratch_shapes=[pltpu.VMEM((d, ntail), BF16)],
        compiler_params=_cparams(("arbitrary", "arbitrary")),
        name="in_proj",
    )(x, ml, mc, g, wh, wt, gmat, qg, kg, gb, cos, sin_a, sin_b)


def _scan_chunk(s, rev, nl, nc):
    if rev:
        return jnp.where(s < nc, nl + nc - 1 - s, nl - 1 - (s - nc))
    return jnp.where(s < nc, nl + s, s - nc)


def _rglru_kernel(xf_ref, xfp_ref, xfn_ref, xb_ref, xbp_ref, xbn_ref, cw_ref, cb_ref, w_ref,
                  bias_ref, clam_ref, hf_ref, hb_ref, ext, a_s, b_s, h_s, hstate,
                  *, nl, nc, tc, sub):
    s = pl.program_id(0)
    bsz, c = hstate.shape[1], hstate.shape[2]
    ntb = tc // SUBLANES

    @pl.when(s == 0)
    def _():
        hstate[...] = jnp.zeros_like(hstate)

    cw = cw_ref[...]
    cb = cb_ref[...]
    for d, (x_ref, xp_ref, xn_ref) in enumerate(((xf_ref, xfp_ref, xfn_ref),
                                                (xb_ref, xbp_ref, xbn_ref))):
        chunk = _scan_chunk(s, d == 1, nl, nc)
        first = jnp.logical_or(chunk == 0, chunk == nl)
        last = jnp.logical_or(chunk == nl - 1, chunk == nl + nc - 1)
        prev = jnp.swapaxes(xp_ref[0], 0, 1)[SUBLANES - CONV_LEFT:]
        nxt = jnp.swapaxes(xn_ref[0], 0, 1)[:CONV_RIGHT]
        ext[d, 0:CONV_LEFT] = jnp.where(first, 0.0, prev)
        for i in range(ntb):
            o = CONV_LEFT + i * SUBLANES
            ext[d, o:o + SUBLANES] = jnp.swapaxes(x_ref[i], 0, 1)
        ext[d, CONV_LEFT + tc:CONV_LEFT + tc + CONV_RIGHT] = jnp.where(last, 0.0, nxt)

        def gates(i, carry, d=d):
            t0 = pl.multiple_of(i * sub, sub)
            u = cb
            for k in range(CONV_W):
                u = u + ext[d, pl.ds(t0 + k, sub)] * cw[k:k + 1, :]
            u2 = u.reshape(sub * bsz, c)
            z = jnp.dot(u2.astype(BF16), w_ref[d], preferred_element_type=F32) + bias_ref[d]
            a = jnp.exp2(clam_ref[d] * jnp.tanh(z[:, :c]) + clam_ref[d])
            gi = 0.5 * jnp.tanh(z[:, c:]) + 0.5
            s1 = 1.0 - a * a
            bb = (s1 * lax.rsqrt(jnp.maximum(s1, TINY))) * (gi * u2)
            a_s[d, pl.ds(t0, sub)] = a.reshape(sub, bsz, c)
            b_s[d, pl.ds(t0, sub)] = bb.reshape(sub, bsz, c)
            return carry

        lax.fori_loop(0, tc // sub, gates, 0)

    def step(k, hs):
        hf, hb = hs
        kb = tc - 1 - k
        hf = a_s[0, k] * hf + b_s[0, k]
        hb = a_s[1, kb] * hb + b_s[1, kb]
        h_s[0, k] = hf
        h_s[1, kb] = hb
        return hf, hb

    hf, hb = lax.fori_loop(0, tc, step, (hstate[0], hstate[1]), unroll=8)
    hstate[0] = hf
    hstate[1] = hb
    for d, o_ref in enumerate((hf_ref, hb_ref)):
        for i in range(ntb):
            o_ref[i] = jnp.swapaxes(h_s[d, i * SUBLANES:(i + 1) * SUBLANES], 0, 1)


def _rglru_call(rgx, cw, cb, w, bias, clam, *, seq):
    ntb_all, bsz, _, c = rgx.shape
    t = ntb_all * SUBLANES
    tc = 128
    sub = 32
    assert seq % tc == 0 and (t - seq) % tc == 0
    nl, nc = seq // tc, (t - seq) // tc
    ntb = tc // SUBLANES

    def specs(rev):
        chunk = functools.partial(_scan_chunk, rev=rev, nl=nl, nc=nc)
        return [
            pl.BlockSpec((ntb, bsz, SUBLANES, c), lambda s: (chunk(s), 0, 0, 0)),
            pl.BlockSpec((1, bsz, SUBLANES, c),
                         lambda s: (jnp.maximum(chunk(s) * ntb - 1, 0), 0, 0, 0)),
            pl.BlockSpec((1, bsz, SUBLANES, c),
                         lambda s: (jnp.minimum((chunk(s) + 1) * ntb, ntb_all - 1), 0, 0, 0)),
        ]

    kern = functools.partial(_rglru_kernel, nl=nl, nc=nc, tc=tc, sub=sub)
    fspec, bspec = specs(False), specs(True)
    return pl.pallas_call(
        kern,
        grid=(nl + nc,),
        in_specs=fspec + bspec + [
            _const_spec((CONV_W, c)),
            _const_spec((1, c)),
            _const_spec((2, c, 2 * c)),
            _const_spec((2, 1, 2 * c)),
            _const_spec((2, 1, c)),
        ],
        out_specs=[fspec[0], bspec[0]],
        out_shape=[jax.ShapeDtypeStruct(rgx.shape, F32)] * 2,
        scratch_shapes=[
            pltpu.VMEM((2, tc + CONV_W - 1, bsz, c), F32),
            pltpu.VMEM((2, tc, bsz, c), F32),
            pltpu.VMEM((2, tc, bsz, c), F32),
            pltpu.VMEM((2, tc, bsz, c), F32),
            pltpu.VMEM((2, bsz, c), F32),
        ],
        compiler_params=_cparams(("arbitrary",)),
        name="rglru",
    )(rgx, rgx, rgx, rgx, rgx, rgx, cw, cb, w, bias, clam)


def _log_sigmoid(x):
    return -(jnp.maximum(-x, 0.0) + jnp.log1p(jnp.exp(-jnp.abs(x))))


def _lane_scan(x, op, ident, rev):
    n = x.shape[1]
    lane = lax.broadcasted_iota(jnp.int32, x.shape, 1)
    sh = 1
    while sh < n:
        if rev:
            x = op(x, jnp.where(lane < n - sh, pltpu.roll(x, n - sh, 1), ident))
        else:
            x = op(x, jnp.where(lane >= sh, pltpu.roll(x, sh, 1), ident))
        sh *= 2
    return x


def _mlgate_kernel(g_ref, row_o, col_o, cb_s, gg_s, ml_s, tot_s, wm_s, mo_s, mn_s, *, nl, nc):
    L = ML_CHUNK
    H = ML_HEADS
    U = ML_UNITS
    is_bwd = lax.broadcasted_iota(jnp.int32, (U, L), 0) >= H
    for c in range(nl + nc):
        lanes = slice(c * L, (c + 1) * L)
        g16 = g_ref[:, lanes]
        li = jnp.concatenate([g16[0:H], g16[2 * H:3 * H]], axis=0)
        lf = _log_sigmoid(jnp.concatenate([g16[H:2 * H], g16[3 * H:4 * H]], axis=0))
        pre = _lane_scan(lf, jnp.add, 0.0, False)
        suf = _lane_scan(lf, jnp.add, 0.0, True)
        tot = pre + suf - lf
        cb = jnp.where(is_bwd, suf, pre)
        gg = li - cb
        pmax = _lane_scan(gg, jnp.maximum, NEG, False)
        smax = _lane_scan(gg, jnp.maximum, NEG, True)
        cb_s[:, lanes] = cb
        gg_s[:, lanes] = gg
        ml_s[:, lanes] = cb + jnp.where(is_bwd, smax, pmax)
        tot_s[:, lanes] = tot
        wm_s[:, lanes] = tot + jnp.maximum(pmax, smax)

    m = jnp.zeros((U, L), F32)
    for s in range(nl + nc):
        cf = nl + s if s < nc else s - nc
        cr = nl + nc - 1 - s if s < nc else nl - 1 - (s - nc)
        lf_, lr_ = slice(cf * L, (cf + 1) * L), slice(cr * L, (cr + 1) * L)
        tot = jnp.where(is_bwd, tot_s[:, lr_], tot_s[:, lf_])
        wm = jnp.where(is_bwd, wm_s[:, lr_], wm_s[:, lf_])
        m_new = jnp.maximum(tot + m, wm)
        mo_s[0:H, lf_] = m[0:H]
        mo_s[H:U, lr_] = m[H:U]
        mn_s[0:H, lf_] = m_new[0:H]
        mn_s[H:U, lr_] = m_new[H:U]
        m = m_new

    cb = cb_s[...]
    gg = gg_s[...]
    tot = tot_s[...]
    mo = mo_s[...]
    mn = mn_s[...]
    m_t = jnp.maximum(cb + mo, ml_s[...])
    row_o[0:U, :] = gg
    row_o[U:2 * U, :] = jnp.exp(tot + gg - mn)
    row_o[2 * U:3 * U, :] = jnp.exp(tot + mo - mn)
    row_o[3 * U:4 * U, :] = mo
    col_o[0:U, :] = cb - m_t
    col_o[U:2 * U, :] = jnp.exp(-m_t)


def _mlgate_call(gt, *, seq):
    bsz, ng, t = gt.shape
    nl, nc = seq // ML_CHUNK, (t - seq) // ML_CHUNK
    kern = functools.partial(_mlgate_kernel, nl=nl, nc=nc)
    rout = pl.BlockSpec((None, 4 * ML_UNITS, t), lambda b: (b, 0, 0))
    cout = pl.BlockSpec((None, 2 * ML_UNITS, t), lambda b: (b, 0, 0))
    return pl.pallas_call(
        kern,
        grid=(bsz,),
        in_specs=[pl.BlockSpec((None, ng, t), lambda b: (b, 0, 0))],
        out_specs=[rout, cout],
        out_shape=[jax.ShapeDtypeStruct((bsz, 4 * ML_UNITS, t), F32),
                   jax.ShapeDtypeStruct((bsz, 2 * ML_UNITS, t), F32)],
        scratch_shapes=[pltpu.VMEM((ML_UNITS, t), F32)] * 7,
        compiler_params=_cparams(("arbitrary",)),
        name="mlstm_gates",
    )(gt)


def _mlstm_kernel(qf_ref, ktf_ref, vf_ref, rf_ref, cf_ref, qb_ref, ktb_ref, vb_ref, rb_ref, cb_ref,
                  hf_ref, hb_ref, c_st):
    s = pl.program_id(1)

    @pl.when(s == 0)
    def _():
        c_st[...] = jnp.zeros_like(c_st)

    L, dh, U = ML_CHUNK, ML_DH, ML_UNITS
    ti = lax.broadcasted_iota(jnp.int32, (L, L), 0)
    si = lax.broadcasted_iota(jnp.int32, (L, L), 1)
    ones = jnp.ones((L, dh), BF16)
    for bb in range(qf_ref.shape[0]):
        for d, (q_ref, kt_ref, v_ref, r_ref, c_ref, o_ref) in enumerate(
                ((qf_ref, ktf_ref, vf_ref, rf_ref, cf_ref, hf_ref),
                 (qb_ref, ktb_ref, vb_ref, rb_ref, cb_ref, hb_ref))):
            tri = (si >= ti) if d == 1 else (si <= ti)
            rows = r_ref[bb]
            cols = c_ref[bb]
            for hd in range(ML_HEADS):
                u = d * ML_HEADS + hd
                sl = slice(hd * dh, (hd + 1) * dh)
                q = q_ref[bb, :, sl]
                kt = kt_ref[bb, sl, :]
                v_ext = jnp.concatenate([v_ref[bb, :, sl], ones], axis=1)
                g_row = rows[u:u + 1, :]
                w_row = rows[U + u:U + u + 1, :]
                a_row = rows[2 * U + u:2 * U + u + 1, :]
                m_row = rows[3 * U + u:3 * U + u + 1, :]
                xb = jnp.broadcast_to(cols[:, u:u + 1], (L, L))
                emt = cols[:, U + u:U + u + 1]
                c_old = c_st[bb * U + u]
                qkc = jnp.dot(q, jnp.concatenate([kt, c_old.astype(BF16)], axis=1),
                              preferred_element_type=F32)
                dm = jnp.exp(jnp.where(tri, xb + g_row, NEG))
                dec = jnp.exp(xb + m_row)
                sc = (qkc[:, :L] * dm).astype(BF16)
                ktw = (kt.astype(F32) * w_row).astype(BF16)
                sv = jnp.dot(jnp.concatenate([sc, ktw], axis=0), v_ext,
                             preferred_element_type=F32)
                ab = sv[:L] + jnp.concatenate([dec, dec], axis=1) * qkc[:, L:]
                o_ref[bb, :, sl] = (ab[:, :dh] / jnp.maximum(jnp.abs(ab[:, dh:]), emt)
                                    ).astype(o_ref.dtype)
                c_st[bb * U + u] = jnp.concatenate([a_row, a_row], axis=1) * c_old + sv[L:]


def _mlstm_call(mq, mkt, mv, rowp, colp, *, seq):
    bsz, t, w = mq.shape
    nl, nc = seq // ML_CHUNK, (t - seq) // ML_CHUNK
    cf = functools.partial(_scan_chunk, rev=False, nl=nl, nc=nc)
    cr = functools.partial(_scan_chunk, rev=True, nl=nl, nc=nc)
    nr = rowp.shape[1]
    bpb = 8 if bsz % 8 == 0 else 1

    def tok(ch, wd):
        return pl.BlockSpec((bpb, ML_CHUNK, wd), lambda b, s: (b, ch(s), 0))

    def tr(ch, rows):
        return pl.BlockSpec((bpb, rows, ML_CHUNK), lambda b, s: (b, 0, ch(s)))

    def side(ch):
        return [tok(ch, w), tr(ch, w), tok(ch, w), tr(ch, nr), tok(ch, colp.shape[2])]

    return pl.pallas_call(
        _mlstm_kernel,
        grid=(bsz // bpb, nl + nc),
        in_specs=side(cf) + side(cr),
        out_specs=[tok(cf, w), tok(cr, w)],
        out_shape=[jax.ShapeDtypeStruct((bsz, t, w), BF16)] * 2,
        scratch_shapes=[pltpu.VMEM((bpb * ML_UNITS, ML_DH, 2 * ML_DH), F32)],
        compiler_params=_cparams(("arbitrary", "arbitrary")),
        name="mlstm",
    )(mq, mkt, mv, rowp, colp, mq, mkt, mv, rowp, colp)


def _attn_kernel(sink_ref, q_ref, k_ref, v_ref, o_ref, vext, *, seq, qbs):
    kv = pl.program_id(1)
    i = pl.program_id(2)
    blk = ATT_BLOCK
    t = k_ref.shape[0]
    ctx = t - seq
    nlb = seq // blk
    band = 3 * blk

    @pl.when(i == 0)
    def _():
        vext[:, 0:LANES] = v_ref[...]
        vext[:, LANES:2 * LANES] = jnp.ones((t, LANES), BF16)

    lane = lax.broadcasted_iota(jnp.int32, (blk, LANES), 1)
    low = lane < AT_DH
    zero = jnp.zeros((blk, LANES), BF16)
    rows = AT_G * blk
    gi = lax.broadcasted_iota(jnp.int32, (rows, 1), 0) // blk
    snk = jnp.zeros((rows, 1), F32)
    for g in range(AT_G):
        snk = jnp.where(gi == g, sink_ref[kv * AT_G + g] * LOG2E, snk)
    ti = lax.broadcasted_iota(jnp.int32, (blk, band), 0)
    ci = lax.broadcasted_iota(jnp.int32, (blk, band), 1)
    nt = (((1,), (1,)), ((), ()))
    k_ctx = k_ref[seq:t, :]
    v_ctx = vext[seq:t, :]

    for j in range(qbs):
        qb = i * qbs + j
        s0 = pl.multiple_of(jnp.clip((qb - 1) * blk, 0, t - band), blk)
        rel = ci - ti + (s0 - qb * blk + blk)
        lim = jnp.where(qb < nlb, seq - s0, 0)
        ok = jnp.logical_and(jnp.logical_and(rel >= 0, rel <= 2 * blk), ci < lim)
        bias = jnp.where(ok, 0.0, NEG)
        bias = jnp.concatenate([bias] * AT_G, axis=0)

        qs = []
        for h2 in range(AT_G // 2):
            qj = q_ref[j * blk:(j + 1) * blk, h2 * LANES:(h2 + 1) * LANES]
            qs += [jnp.where(low, qj, zero), jnp.where(low, zero, qj)]
        q_all = jnp.concatenate(qs, axis=0)
        s_band = lax.dot_general(q_all, k_ref[pl.ds(s0, band), :], nt,
                                 preferred_element_type=F32) + bias
        s_ctx = lax.dot_general(q_all, k_ctx, nt, preferred_element_type=F32)
        m = jnp.maximum(jnp.maximum(jnp.max(s_band, axis=1, keepdims=True),
                                    jnp.max(s_ctx, axis=1, keepdims=True)), snk)
        p_band = jnp.exp2(s_band - m).astype(BF16)
        p_ctx = jnp.exp2(s_ctx - m).astype(BF16)
        o2 = (jnp.dot(p_band, vext[pl.ds(s0, band), :], preferred_element_type=F32)
              + jnp.dot(p_ctx, v_ctx, preferred_element_type=F32))
        o2 = o2[:, :LANES] / (o2[:, LANES:] + jnp.exp2(snk - m))
        for h2 in range(AT_G // 2):
            a = o2[(2 * h2) * blk:(2 * h2 + 1) * blk, :]
            b = o2[(2 * h2 + 1) * blk:(2 * h2 + 2) * blk, :]
            o_ref[j * blk:(j + 1) * blk, h2 * LANES:(h2 + 1) * LANES] = (
                jnp.where(low, a, b).astype(o_ref.dtype))


def _attn_call(sink, aq, akd, avd, *, seq):
    bsz, t, _ = aq.shape
    blk = ATT_BLOCK
    qbs = 3
    assert t % (qbs * blk) == 0 and t >= 3 * blk
    gw = AT_G * AT_DH
    kern = functools.partial(_attn_kernel, seq=seq, qbs=qbs)
    whole = pl.BlockSpec((None, t, LANES), lambda b, kv, i: (b, 0, kv))
    return pl.pallas_call(
        kern,
        grid=(bsz, AT_KV, t // (qbs * blk)),
        in_specs=[
            pl.BlockSpec(memory_space=pltpu.SMEM),
            pl.BlockSpec((None, qbs * blk, gw), lambda b, kv, i: (b, i, kv)),
            whole, whole,
        ],
        out_specs=pl.BlockSpec((None, qbs * blk, gw), lambda b, kv, i: (b, i, kv)),
        out_shape=jax.ShapeDtypeStruct((bsz, t, AT_W), BF16),
        scratch_shapes=[pltpu.VMEM((t, 2 * LANES), BF16)],
        compiler_params=_cparams(("arbitrary", "arbitrary", "arbitrary")),
        name="attn",
    )(sink, aq, akd, avd)


def _merge_kernel(x_ref, ml_ref, mc_ref, rf_ref, rb_ref, gg_ref, hf_ref, hb_ref, mo_ref, yc_ref,
                  br_ref, ng_ref, wb_ref, wo_ref, g_ref, w1_hbm, w3_hbm, w2_hbm, o_ref, act_ref,
                  w1b, w3b, w2b, stg_c, stg_r, sem, *, lead, seq, tm, ck):
    first = jnp.logical_and(pl.program_id(0) == 0, pl.program_id(1) == 0)
    stream = _FfnWeightStream((w1_hbm, w3_hbm, w2_hbm), lead, (w1b, w3b, w2b),
                              stg_c, stg_r, sem, ck)

    @pl.when(first)
    def _():
        stream.prime()

    t0 = pl.program_id(1) * tm
    rows = t0 + lax.broadcasted_iota(jnp.int32, (tm, 1), 0)
    is_ctx = rows >= seq
    d = x_ref.shape[1]
    rsum = (rf_ref[...] + rb_ref[...]).reshape(tm, D_RNN)
    ya = (rsum * gg_ref[...].astype(F32)).astype(BF16)
    hsum = hf_ref[...].astype(F32) + hb_ref[...].astype(F32)
    parts = []
    for hd in range(ML_HEADS):
        hh = hsum[:, hd * ML_DH:(hd + 1) * ML_DH]
        parts.append(hh * lax.rsqrt(jnp.mean(hh * hh, axis=-1, keepdims=True) + EPS))
    yb = (jnp.concatenate(parts, axis=1) * ng_ref[...] * mo_ref[...].astype(F32)).astype(BF16)
    m = (br_ref[:, 0:d].astype(F32) * jnp.dot(ya, wb_ref[0], preferred_element_type=F32)
         + br_ref[:, d:2 * d].astype(F32) * jnp.dot(yb, wb_ref[1], preferred_element_type=F32)
         + br_ref[:, 2 * d:3 * d].astype(F32) * jnp.dot(yc_ref[...], wb_ref[2],
                                                       preferred_element_type=F32))
    y = jnp.dot(m.astype(BF16), wo_ref[...], preferred_element_type=F32)
    x1 = x_ref[...] + _row_mod(ml_ref, mc_ref, is_ctx, 5) * y
    @pl.when(first)
    def _():
        o_ref[...] = _ffn_body(x1, ml_ref, mc_ref, is_ctx, g_ref, w1b, w3b, w2b, act_ref,
                               base=6, ck=ck, stream=stream)

    @pl.when(jnp.logical_not(first))
    def _():
        o_ref[...] = _ffn_body(x1, ml_ref, mc_ref, is_ctx, g_ref, w1b, w3b, w2b, act_ref,
                               base=6, ck=ck)


def _merge_call(x, ml, mc, rf, rb, gg, hf, hb, mo, yc, br, ng, wb, wo, g, w1, w3, w2,
                *, layer, seq, out_rows):
    bsz, t, d = x.shape
    dff = w1.shape[-1]
    tm = _pick_tile(t, (384, 128))
    ck = _pick_tile(dff, (256, 128))
    kern = functools.partial(_merge_kernel, lead=(layer, 1), seq=seq, tm=tm, ck=ck)
    hbm = pl.BlockSpec(memory_space=pl.ANY)

    def tok(wd):
        return pl.BlockSpec((None, tm, wd), lambda b, i: (b, i, 0))

    rspec = pl.BlockSpec((tm // SUBLANES, None, SUBLANES, D_RNN), lambda b, i: (i, b, 0, 0))

    return pl.pallas_call(
        kern,
        grid=(bsz, pl.cdiv(out_rows, tm)),
        in_specs=[
            tok(d),
            pl.BlockSpec((None, N_MOD, d), lambda b, i: (b, 0, 0)),
            _const_spec((N_MOD, d)),
            rspec, rspec, tok(D_RNN), tok(ML_W), tok(ML_W), tok(ML_W), tok(AT_W),
            tok(N_BRANCH * d),
            _const_spec((1, ML_W)),
            _const_spec((N_BRANCH, BRANCH_W, d), (layer,)),
            _const_spec((d, d), (layer,)),
            _const_spec((1, d)),
            hbm, hbm, hbm,
        ],
        out_specs=tok(d),
        out_shape=jax.ShapeDtypeStruct((bsz, out_rows, d), F32),
        scratch_shapes=[pltpu.VMEM((tm, dff), BF16)] + _ffn_weight_scratch(d, dff, ck),
        compiler_params=_cparams(("arbitrary", "arbitrary")),
        name="merge_ffn",
    )(x, ml, mc, rf, rb, gg, hf, hb, mo, yc, br, ng, wb, wo, g, w1, w3, w2)


def _block_diag(w):
    n, bi, bj = w.shape
    eye = jnp.eye(n, dtype=w.dtype)
    return (eye[:, None, :, None] * w[:, :, None, :]).reshape(n * bi, n * bj)


def _rope_tables(seq, t):
    rows = seq // GRID_W
    row = jnp.repeat(jnp.arange(rows), GRID_W).astype(F32)
    col = jnp.broadcast_to(jnp.arange(GRID_W), (rows, GRID_W)).reshape(-1).astype(F32)
    half = AT_DH // 2
    inv = ROPE_BASE ** (-jnp.arange(0, half, 2, dtype=F32) / half)
    ar = row[:, None] * inv
    ac = col[:, None] * inv
    ang = jnp.concatenate([ar, ar, ac, ac], axis=-1)
    cos = jnp.concatenate([jnp.cos(ang), jnp.ones((t - seq, AT_DH), F32)], axis=0)
    sin = jnp.concatenate([jnp.sin(ang), jnp.zeros((t - seq, AT_DH), F32)], axis=0)
    cos = jnp.tile(cos, (1, LANES // AT_DH))
    sin = jnp.tile(sin, (1, LANES // AT_DH))
    first_half = (jnp.arange(LANES) % (2 * ROPE_Q)) < ROPE_Q
    sin_a = jnp.where(first_half, -sin, 0.0)
    sin_b = jnp.where(first_half, 0.0, sin)
    return cos, sin_a, sin_b


def kernel(x, c, ctx, c_ctx, ada_w, ada_b, norm_g, ffn_w1, ffn_w3, ffn_w2, w_in, rg_conv_w,
           rg_conv_b, rg_wa, rg_ba, rg_wi, rg_bi, rg_lam, ml_gate_b, ml_norm_g, at_qn_g,
           at_kn_g, at_sink, w_branch, w_out):
    bsz, seq, d = x.shape
    nctx = ctx.shape[1]
    t = seq + nctx
    depth = ada_w.shape[0]

    xs = jnp.concatenate([x, ctx], axis=1)
    mod_rows = 2 * SUBLANES
    cc = jnp.concatenate([c, c_ctx[None, :], jnp.zeros((mod_rows - bsz - 1, d), F32)], axis=0)
    mods = _ada_call(cc, ada_w, ada_b).reshape(depth, mod_rows, N_MOD, d)
    cos, sin_a, sin_b = _rope_tables(seq, t)
    gmat = _block_diag(jnp.full((AT_HEADS, AT_DH, AT_DH), 1.0 / AT_DH, F32)).astype(BF16)
    w1, w3, w2 = ffn_w1, ffn_w3, ffn_w2
    w_head, w_tail = w_in[..., :P_HEAD].astype(BF16), w_in[..., P_HEAD:].astype(BF16)
    w_br, w_o = w_branch.astype(BF16), w_out.astype(BF16)

    for l in range(depth):
        ml = mods[l, :bsz]
        mc = mods[l, bsz]
        last = l == depth - 1

        xs = _ffn_call(xs, ml, mc, norm_g[l, 0][None, :], w1, w3, w2,
                       lead=(l, 0), base=0, seq=seq, out_rows=t)

        qg = jnp.tile(at_qn_g[l], AT_HEADS)[None, :]
        kg = jnp.tile(at_kn_g[l], AT_KV)[None, :]
        gb = jnp.concatenate([ml_gate_b[l], jnp.zeros((LANES - N_GATE,), F32)])[None, :]
        (rgx, rgg, mq, mk, mv, mo, aq, akd, avd, br, mg) = _proj_call(
            xs, ml, mc, norm_g[l, 1][None, :], w_head, w_tail, gmat, qg, kg, gb,
            cos, sin_a, sin_b, layer=l, seq=seq)

        wcat = jnp.stack([jnp.concatenate([_block_diag(rg_wa[l, dr]), _block_diag(rg_wi[l, dr])],
                                          axis=1) for dr in range(2)])
        wcat = (0.5 * wcat).astype(BF16)
        bias = 0.5 * jnp.concatenate([rg_ba[l], rg_bi[l]], axis=1)[:, None, :]
        clam = (-0.5 * LOG2E * LRU_C * jax.nn.softplus(-rg_lam[l]))[:, None, :]
        rhf, rhb = _rglru_call(rgx, rg_conv_w[l], rg_conv_b[l][None, :], wcat, bias, clam,
                               seq=seq)

        gt = jnp.swapaxes(mg[:, :, :N_GATE], 1, 2)
        rowp, colsrc = _mlgate_call(gt, seq=seq)
        colp = jnp.swapaxes(colsrc, 1, 2)
        mkt = jnp.swapaxes(mk, 1, 2)
        mhf, mhb = _mlstm_call(mq, mkt, mv, rowp, colp, seq=seq)

        yc = _attn_call(at_sink[l], aq, akd, avd, seq=seq)

        xs = _merge_call(xs, ml, mc, rhf, rhb, rgg, mhf, mhb, mo, yc, br, ml_norm_g[l][None, :],
                         w_br, w_o, norm_g[l, 2][None, :], w1, w3, w2,
                         layer=l, seq=seq, out_rows=seq if last else t)
    return xs
```

```python
import functools

import jax
import jax.numpy as jnp
from jax import lax
from jax.experimental import pallas as pl
from jax.experimental.pallas import tpu as pltpu

F32 = jnp.float32
BF16 = jnp.bfloat16

EPS = 1e-6
NEG = -1e30
TINY = 1e-36
LOG2E = 1.4426950408889634
N_MOD = 9
GRID_W = 64
ROPE_BASE = 10000.0

D_RNN = 512
RNN_BLOCKS = 8
RNN_BLOCK = D_RNN // RNN_BLOCKS
CONV_W = 4
CONV_LEFT = 2
CONV_RIGHT = CONV_W - 1 - CONV_LEFT
LRU_C = 8.0

ML_HEADS = 4
ML_DH = 128
ML_W = ML_HEADS * ML_DH
ML_CHUNK = 128
ML_UNITS = 2 * ML_HEADS

AT_HEADS = 8
AT_KV = 2
AT_DH = 64
AT_G = AT_HEADS // AT_KV
AT_W = AT_HEADS * AT_DH
AT_KVW = AT_KV * AT_DH
ATT_BLOCK = 128
ROPE_Q = AT_DH // 4

N_BRANCH = 3
BRANCH_W = 512

LANES = 128
SUBLANES = 8
VMEM_LIMIT = 56 * 1024 * 1024

P_RGX = 0
P_RGG = P_RGX + D_RNN
P_MQ = P_RGG + D_RNN
P_MK = P_MQ + ML_W
P_MV = P_MK + ML_W
P_MO = P_MV + ML_W
P_HEAD = P_MO + ML_W
N_GATE = 4 * ML_HEADS
T_AQ = 0
T_AK = T_AQ + AT_W
T_AV = T_AK + AT_KVW
T_BR = T_AV + AT_KVW
GATE_SEG = 512


def _cparams(sem):
    return pltpu.CompilerParams(dimension_semantics=sem, vmem_limit_bytes=VMEM_LIMIT)


def _const_spec(shape, lead=()):
    nd = len(shape)
    idx = tuple(lead) + (0,) * nd
    return pl.BlockSpec((None,) * len(lead) + tuple(shape), lambda *_: idx,
                        pipeline_mode=pl.Buffered(1))


def _pick_tile(total, candidates):
    for c in candidates:
        if total % c == 0:
            return c
    raise ValueError(f"no tile for {total}")


def _sigmoid(x):
    return jax.nn.sigmoid(x)


def _ln_mod(x, g, shift, scale):
    ms = jnp.mean(x * x, axis=-1, keepdims=True)
    return x * lax.rsqrt(ms + EPS) * g * (1.0 + scale) + shift


def _row_mod(ml_ref, mc_ref, is_ctx, i):
    return jnp.where(is_ctx, mc_ref[i:i + 1, :], ml_ref[i:i + 1, :])


def _ada_kernel(cc_ref, w_ref, b_ref, o_ref):
    cc = cc_ref[...]
    s = cc * _sigmoid(cc)
    o_ref[...] = jnp.dot(s.astype(BF16), w_ref[...].astype(BF16),
                         preferred_element_type=F32) + b_ref[...]


def _ada_call(cc, ada_w, ada_b):
    depth, d, nout = ada_w.shape
    rows = cc.shape[0]
    tn = _pick_tile(nout, (1536, 1024, 512, 256, 128))
    return pl.pallas_call(
        _ada_kernel,
        grid=(depth, nout // tn),
        in_specs=[
            pl.BlockSpec((rows, d), lambda l, j: (0, 0)),
            pl.BlockSpec((None, d, tn), lambda l, j: (l, 0, j)),
            pl.BlockSpec((None, 1, tn), lambda l, j: (l, 0, j)),
        ],
        out_specs=pl.BlockSpec((None, rows, tn), lambda l, j: (l, 0, j)),
        out_shape=jax.ShapeDtypeStruct((depth, rows, nout), F32),
        compiler_params=_cparams(("arbitrary", "arbitrary")),
        name="ada_mod",
    )(cc, ada_w, ada_b.reshape(depth, 1, nout))


def _ffn_body(x, ml_ref, mc_ref, is_ctx, g_ref, w1_ref, w3_ref, w2_ref, act_ref, *, base, ck,
              stream=None):
    h = _ln_mod(x, g_ref[...], _row_mod(ml_ref, mc_ref, is_ctx, base),
                _row_mod(ml_ref, mc_ref, is_ctx, base + 1)).astype(BF16)
    dff = w1_ref.shape[1]
    nck = dff // ck
    if stream is not None:
        stream.land(0)
        stream.refill(0)
    for j in range(nck):
        if stream is not None and j + 1 < nck:
            stream.land(j + 1)
        a = jnp.dot(h, w1_ref[:, j * ck:(j + 1) * ck], preferred_element_type=F32)
        b = jnp.dot(h, w3_ref[:, j * ck:(j + 1) * ck], preferred_element_type=F32)
        act_ref[:, j * ck:(j + 1) * ck] = (a * _sigmoid(a) * b).astype(BF16)
        if stream is not None and j + 1 < nck:
            stream.refill(j + 1)
    y = jnp.dot(act_ref[...], w2_ref[...], preferred_element_type=F32)
    return x + 0.5 * _row_mod(ml_ref, mc_ref, is_ctx, base + 2) * y


def _ffn_weight_scratch(d, dff, ck):
    return [pltpu.VMEM((d, dff), BF16), pltpu.VMEM((d, dff), BF16), pltpu.VMEM((dff, d), BF16),
            pltpu.VMEM((2, 2, d, ck), F32), pltpu.VMEM((2, ck, d), F32),
            pltpu.SemaphoreType.DMA((2, 3))]


class _FfnWeightStream:
    def __init__(self, hbm, lead, resident, stg_c, stg_r, sem, ck):
        self.hbm, self.lead, self.resident = hbm, lead, resident
        self.stg_c, self.stg_r, self.sem, self.ck = stg_c, stg_r, sem, ck
        self.nck = resident[0].shape[1] // ck

    def _stages(self, j):
        s = j % 2
        return (self.stg_c.at[s, 0], self.stg_c.at[s, 1], self.stg_r.at[s])

    def _copies(self, j):
        l0, l1 = self.lead
        sl = slice(j * self.ck, (j + 1) * self.ck)
        srcs = (self.hbm[0].at[l0, l1, :, sl], self.hbm[1].at[l0, l1, :, sl],
                self.hbm[2].at[l0, l1, sl, :])
        return [pltpu.make_async_copy(src, stg, self.sem.at[j % 2, i])
                for i, (src, stg) in enumerate(zip(srcs, self._stages(j)))]

    def start(self, j):
        for cp in self._copies(j):
            cp.start()

    def prime(self):
        self.start(0)
        if self.nck > 1:
            self.start(1)

    def land(self, j):
        sl = slice(j * self.ck, (j + 1) * self.ck)
        for cp in self._copies(j):
            cp.wait()
        w1b, w3b, w2b = self.resident
        s1, s3, s2 = self._stages(j)
        w1b[:, sl] = s1[...].astype(BF16)
        w3b[:, sl] = s3[...].astype(BF16)
        w2b[sl, :] = s2[...].astype(BF16)

    def refill(self, j):
        if j + 2 < self.nck:
            self.start(j + 2)


def _ffn_kernel(x_ref, ml_ref, mc_ref, g_ref, w1_hbm, w3_hbm, w2_hbm, o_ref, act_ref,
                w1b, w3b, w2b, stg_c, stg_r, sem, *, lead, base, seq, tm, ck):
    t0 = pl.program_id(1) * tm
    rows = t0 + lax.broadcasted_iota(jnp.int32, (tm, 1), 0)
    is_ctx = rows >= seq
    first = jnp.logical_and(pl.program_id(0) == 0, pl.program_id(1) == 0)

    @pl.when(first)
    def _():
        stream = _FfnWeightStream((w1_hbm, w3_hbm, w2_hbm), lead, (w1b, w3b, w2b),
                                  stg_c, stg_r, sem, ck)
        stream.prime()
        o_ref[...] = _ffn_body(x_ref[...], ml_ref, mc_ref, is_ctx, g_ref, w1b, w3b, w2b,
                               act_ref, base=base, ck=ck, stream=stream)

    @pl.when(jnp.logical_not(first))
    def _():
        o_ref[...] = _ffn_body(x_ref[...], ml_ref, mc_ref, is_ctx, g_ref, w1b, w3b, w2b,
                               act_ref, base=base, ck=ck)


def _ffn_call(x, ml, mc, g, w1, w3, w2, *, lead, base, seq, out_rows):
    bsz, t, d = x.shape
    dff = w1.shape[-1]
    tm = _pick_tile(t, (768, 384, 128))
    ck = _pick_tile(dff, (256, 128))
    kern = functools.partial(_ffn_kernel, lead=lead, base=base, seq=seq, tm=tm, ck=ck)
    hbm = pl.BlockSpec(memory_space=pl.ANY)
    return pl.pallas_call(
        kern,
        grid=(bsz, pl.cdiv(out_rows, tm)),
        in_specs=[
            pl.BlockSpec((None, tm, d), lambda b, i: (b, i, 0)),
            pl.BlockSpec((None, N_MOD, d), lambda b, i: (b, 0, 0)),
            _const_spec((N_MOD, d)),
            _const_spec((1, d)),
            hbm, hbm, hbm,
        ],
        out_specs=pl.BlockSpec((None, tm, d), lambda b, i: (b, i, 0)),
        out_shape=jax.ShapeDtypeStruct((bsz, out_rows, d), F32),
        scratch_shapes=[pltpu.VMEM((tm, dff), BF16)] + _ffn_weight_scratch(d, dff, ck),
        compiler_params=_cparams(("arbitrary", "arbitrary")),
        name="ffn",
    )(x, ml, mc, g, w1, w3, w2)


def _gelu_tanh(x):
    return 0.5 * x * (1.0 + jnp.tanh(0.7978845608028654 * (x + 0.044715 * (x * x * x))))


def _head_norm(x, gmat, g):
    ms = jnp.dot((x * x).astype(BF16), gmat, preferred_element_type=F32)
    return x * lax.rsqrt(ms + EPS) * g


def _dup_heads(x):
    low = lax.broadcasted_iota(jnp.int32, x.shape, 1) < AT_DH
    sw = pltpu.roll(x, AT_DH, 1)
    return jnp.concatenate([jnp.where(low, x, sw), jnp.where(low, sw, x)], axis=1)


def _rope(x, cos, sin_a, sin_b):
    parts = []
    for j in range(x.shape[1] // LANES):
        xj = x[:, j * LANES:(j + 1) * LANES]
        parts.append(xj * cos + pltpu.roll(xj, LANES - ROPE_Q, 1) * sin_a
                     + pltpu.roll(xj, ROPE_Q, 1) * sin_b)
    return jnp.concatenate(parts, axis=1)


def _proj_kernel(x_ref, ml_ref, mc_ref, g_ref, wh_ref, wt_ref, gmat_ref, qg_ref, kg_ref, gb_ref,
                 cos_ref, sa_ref, sb_ref,
                 rgx_o, rgg_o, mq_o, mk_o, mv_o, mo_o, aq_o, ak_o, av_o, br_o, mg_o,
                 tail, *, seq, tm, d_model):
    @pl.when(jnp.logical_and(pl.program_id(0) == 0, pl.program_id(1) == 0))
    def _():
        for r in range(0, d_model, LANES):
            tail[r:r + LANES, :] = wt_ref[r:r + LANES, N_GATE:]

    t0 = pl.program_id(1) * tm
    rows = t0 + lax.broadcasted_iota(jnp.int32, (tm, 1), 0)
    is_ctx = rows >= seq
    h = _ln_mod(x_ref[...], g_ref[...], _row_mod(ml_ref, mc_ref, is_ctx, 3),
                _row_mod(ml_ref, mc_ref, is_ctx, 4)).astype(BF16)

    def seg(c0, w, ref=wh_ref):
        return jnp.dot(h, ref[:, c0:c0 + w], preferred_element_type=F32)

    rgx_o[...] = seg(P_RGX, D_RNN).reshape(rgx_o.shape)
    rgg_o[...] = _gelu_tanh(seg(P_RGG, D_RNN)).astype(BF16)
    mq_o[...] = seg(P_MQ, ML_W).astype(BF16)
    mk_o[...] = (seg(P_MK, ML_W) * (ML_DH ** -0.5)).astype(BF16)
    mv_o[...] = seg(P_MV, ML_W).astype(BF16)
    mo_o[...] = _sigmoid(seg(P_MO, ML_W)).astype(BF16)
    cos, sin_a, sin_b = cos_ref[...], sa_ref[...], sb_ref[...]
    gmat = gmat_ref[...]
    q = _head_norm(seg(T_AQ, AT_W, tail), gmat, qg_ref[...])
    aq_o[...] = (_rope(q, cos, sin_a, sin_b) * (AT_DH ** -0.5 * LOG2E)).astype(BF16)
    k = _head_norm(seg(T_AK, AT_KVW, tail), gmat[:AT_KVW, :AT_KVW], kg_ref[...])
    ak_o[...] = _dup_heads(_rope(k, cos, sin_a, sin_b)).astype(BF16)
    av_o[...] = _dup_heads(seg(T_AV, AT_KVW, tail)).astype(BF16)
    for c0 in range(0, N_BRANCH * d_model, GATE_SEG):
        br_o[:, c0:c0 + GATE_SEG] = _sigmoid(seg(T_BR + c0, GATE_SEG, tail)).astype(BF16)
    mg_o[...] = seg(0, LANES, wt_ref) + gb_ref[...]


def _proj_call(x, ml, mc, g, wh, wt, gmat, qg, kg, gb, cos, sin_a, sin_b, *, layer, seq):
    bsz, t, d = x.shape
    ntail = wt.shape[-1] - N_GATE
    tm = _pick_tile(t, (384, 128))
    kern = functools.partial(_proj_kernel, seq=seq, tm=tm, d_model=d)
    widths = [(D_RNN, F32), (D_RNN, BF16), (ML_W, BF16), (ML_W, BF16), (ML_W, BF16),
              (ML_W, BF16), (AT_W, BF16), (2 * AT_KVW, BF16), (2 * AT_KVW, BF16),
              (N_BRANCH * d, BF16), (LANES, F32)]
    tab_spec = pl.BlockSpec((tm, LANES), lambda b, i: (i, 0))
    out_specs = [pl.BlockSpec((None, tm, wd), lambda b, i: (b, i, 0)) for wd, _ in widths]
    out_shape = [jax.ShapeDtypeStruct((bsz, t, wd), dt) for wd, dt in widths]
    out_specs[0] = pl.BlockSpec((tm // SUBLANES, None, SUBLANES, D_RNN), lambda b, i: (i, b, 0, 0))
    out_shape[0] = jax.ShapeDtypeStruct((t // SUBLANES, bsz, SUBLANES, D_RNN), F32)
    return pl.pallas_call(
        kern,
        grid=(bsz, t // tm),
        in_specs=[
            pl.BlockSpec((None, tm, d), lambda b, i: (b, i, 0)),
            pl.BlockSpec((None, N_MOD, d), lambda b, i: (b, 0, 0)),
            _const_spec((N_MOD, d)),
            _const_spec((1, d)),
            _const_spec((d, wh.shape[-1]), (layer,)),
            _const_spec((d, wt.shape[-1]), (layer,)),
            _const_spec((AT_W, AT_W)),
            _const_spec((1, AT_W)),
            _const_spec((1, AT_KVW)),
            _const_spec((1, LANES)),
            tab_spec, tab_spec, tab_spec,
        ],
        out_specs=out_specs,
        out_shape=out_shape,
        scratch_shapes=[pltpu.VMEM((d, ntail), BF16)],
        compiler_params=_cparams(("arbitrary", "arbitrary")),
        name="in_proj",
    )(x, ml, mc, g, wh, wt, gmat, qg, kg, gb, cos, sin_a, sin_b)


def _scan_chunk(s, rev, nl, nc):
    if rev:
        return jnp.where(s < nc, nl + nc - 1 - s, nl - 1 - (s - nc))
    return jnp.where(s < nc, nl + s, s - nc)


def _rglru_kernel(xf_ref, xfp_ref, xfn_ref, xb_ref, xbp_ref, xbn_ref, cw_ref, cb_ref, w_ref,
                  bias_ref, clam_ref, hf_ref, hb_ref, ext, a_s, b_s, h_s, hstate,
                  *, nl, nc, tc, sub):
    s = pl.program_id(0)
    bsz, c = hstate.shape[1], hstate.shape[2]
    ntb = tc // SUBLANES

    @pl.when(s == 0)
    def _():
        hstate[...] = jnp.zeros_like(hstate)

    cw = cw_ref[...]
    cb = cb_ref[...]
    for d, (x_ref, xp_ref, xn_ref) in enumerate(((xf_ref, xfp_ref, xfn_ref),
                                                (xb_ref, xbp_ref, xbn_ref))):
        chunk = _scan_chunk(s, d == 1, nl, nc)
        first = jnp.logical_or(chunk == 0, chunk == nl)
        last = jnp.logical_or(chunk == nl - 1, chunk == nl + nc - 1)
        prev = jnp.swapaxes(xp_ref[0], 0, 1)[SUBLANES - CONV_LEFT:]
        nxt = jnp.swapaxes(xn_ref[0], 0, 1)[:CONV_RIGHT]
        ext[d, 0:CONV_LEFT] = jnp.where(first, 0.0, prev)
        for i in range(ntb):
            o = CONV_LEFT + i * SUBLANES
            ext[d, o:o + SUBLANES] = jnp.swapaxes(x_ref[i], 0, 1)
        ext[d, CONV_LEFT + tc:CONV_LEFT + tc + CONV_RIGHT] = jnp.where(last, 0.0, nxt)

        def gates(i, carry, d=d):
            t0 = pl.multiple_of(i * sub, sub)
            u = cb
            for k in range(CONV_W):
                u = u + ext[d, pl.ds(t0 + k, sub)] * cw[k:k + 1, :]
            u2 = u.reshape(sub * bsz, c)
            z = jnp.dot(u2.astype(BF16), w_ref[d], preferred_element_type=F32) + bias_ref[d]
            a = jnp.exp2(clam_ref[d] * jnp.tanh(z[:, :c]) + clam_ref[d])
            gi = 0.5 * jnp.tanh(z[:, c:]) + 0.5
            s1 = 1.0 - a * a
            bb = (s1 * lax.rsqrt(jnp.maximum(s1, TINY))) * (gi * u2)
            a_s[d, pl.ds(t0, sub)] = a.reshape(sub, bsz, c)
            b_s[d, pl.ds(t0, sub)] = bb.reshape(sub, bsz, c)
            return carry

        lax.fori_loop(0, tc // sub, gates, 0)

    def step(k, hs):
        hf, hb = hs
        kb = tc - 1 - k
        hf = a_s[0, k] * hf + b_s[0, k]
        hb = a_s[1, kb] * hb + b_s[1, kb]
        h_s[0, k] = hf
        h_s[1, kb] = hb
        return hf, hb

    hf, hb = lax.fori_loop(0, tc, step, (hstate[0], hstate[1]), unroll=8)
    hstate[0] = hf
    hstate[1] = hb
    for d, o_ref in enumerate((hf_ref, hb_ref)):
        for i in range(ntb):
            o_ref[i] = jnp.swapaxes(h_s[d, i * SUBLANES:(i + 1) * SUBLANES], 0, 1)


def _rglru_call(rgx, cw, cb, w, bias, clam, *, seq):
    ntb_all, bsz, _, c = rgx.shape
    t = ntb_all * SUBLANES
    tc = 128
    sub = 32
    assert seq % tc == 0 and (t - seq) % tc == 0
    nl, nc = seq // tc, (t - seq) // tc
    ntb = tc // SUBLANES

    def specs(rev):
        chunk = functools.partial(_scan_chunk, rev=rev, nl=nl, nc=nc)
        return [
            pl.BlockSpec((ntb, bsz, SUBLANES, c), lambda s: (chunk(s), 0, 0, 0)),
            pl.BlockSpec((1, bsz, SUBLANES, c),
                         lambda s: (jnp.maximum(chunk(s) * ntb - 1, 0), 0, 0, 0)),
            pl.BlockSpec((1, bsz, SUBLANES, c),
                         lambda s: (jnp.minimum((chunk(s) + 1) * ntb, ntb_all - 1), 0, 0, 0)),
        ]

    kern = functools.partial(_rglru_kernel, nl=nl, nc=nc, tc=tc, sub=sub)
    fspec, bspec = specs(False), specs(True)
    return pl.pallas_call(
        kern,
        grid=(nl + nc,),
        in_specs=fspec + bspec + [
            _const_spec((CONV_W, c)),
            _const_spec((1, c)),
            _const_spec((2, c, 2 * c)),
            _const_spec((2, 1, 2 * c)),
            _const_spec((2, 1, c)),
        ],
        out_specs=[fspec[0], bspec[0]],
        out_shape=[jax.ShapeDtypeStruct(rgx.shape, F32)] * 2,
        scratch_shapes=[
            pltpu.VMEM((2, tc + CONV_W - 1, bsz, c), F32),
            pltpu.VMEM((2, tc, bsz, c), F32),
            pltpu.VMEM((2, tc, bsz, c), F32),
            pltpu.VMEM((2, tc, bsz, c), F32),
            pltpu.VMEM((2, bsz, c), F32),
        ],
        compiler_params=_cparams(("arbitrary",)),
        name="rglru",
    )(rgx, rgx, rgx, rgx, rgx, rgx, cw, cb, w, bias, clam)


def _log_sigmoid(x):
    return -(jnp.maximum(-x, 0.0) + jnp.log1p(jnp.exp(-jnp.abs(x))))


def _lane_scan(x, op, ident, rev):
    n = x.shape[1]
    lane = lax.broadcasted_iota(jnp.int32, x.shape, 1)
    sh = 1
    while sh < n:
        if rev:
            x = op(x, jnp.where(lane < n - sh, pltpu.roll(x, n - sh, 1), ident))
        else:
            x = op(x, jnp.where(lane >= sh, pltpu.roll(x, sh, 1), ident))
        sh *= 2
    return x


def _mlgate_kernel(g_ref, row_o, col_o, cb_s, gg_s, ml_s, tot_s, wm_s, mo_s, mn_s, *, nl, nc):
    L = ML_CHUNK
    H = ML_HEADS
    U = ML_UNITS
    is_bwd = lax.broadcasted_iota(jnp.int32, (U, L), 0) >= H
    for c in range(nl + nc):
        lanes = slice(c * L, (c + 1) * L)
        g16 = g_ref[:, lanes]
        li = jnp.concatenate([g16[0:H], g16[2 * H:3 * H]], axis=0)
        lf = _log_sigmoid(jnp.concatenate([g16[H:2 * H], g16[3 * H:4 * H]], axis=0))
        pre = _lane_scan(lf, jnp.add, 0.0, False)
        suf = _lane_scan(lf, jnp.add, 0.0, True)
        tot = pre + suf - lf
        cb = jnp.where(is_bwd, suf, pre)
        gg = li - cb
        pmax = _lane_scan(gg, jnp.maximum, NEG, False)
        smax = _lane_scan(gg, jnp.maximum, NEG, True)
        cb_s[:, lanes] = cb
        gg_s[:, lanes] = gg
        ml_s[:, lanes] = cb + jnp.where(is_bwd, smax, pmax)
        tot_s[:, lanes] = tot
        wm_s[:, lanes] = tot + jnp.maximum(pmax, smax)

    m = jnp.zeros((U, L), F32)
    for s in range(nl + nc):
        cf = nl + s if s < nc else s - nc
        cr = nl + nc - 1 - s if s < nc else nl - 1 - (s - nc)
        lf_, lr_ = slice(cf * L, (cf + 1) * L), slice(cr * L, (cr + 1) * L)
        tot = jnp.where(is_bwd, tot_s[:, lr_], tot_s[:, lf_])
        wm = jnp.where(is_bwd, wm_s[:, lr_], wm_s[:, lf_])
        m_new = jnp.maximum(tot + m, wm)
        mo_s[0:H, lf_] = m[0:H]
        mo_s[H:U, lr_] = m[H:U]
        mn_s[0:H, lf_] = m_new[0:H]
        mn_s[H:U, lr_] = m_new[H:U]
        m = m_new

    cb = cb_s[...]
    gg = gg_s[...]
    tot = tot_s[...]
    mo = mo_s[...]
    mn = mn_s[...]
    m_t = jnp.maximum(cb + mo, ml_s[...])
    row_o[0:U, :] = gg
    row_o[U:2 * U, :] = jnp.exp(tot + gg - mn)
    row_o[2 * U:3 * U, :] = jnp.exp(tot + mo - mn)
    row_o[3 * U:4 * U, :] = mo
    col_o[0:U, :] = cb - m_t
    col_o[U:2 * U, :] = jnp.exp(-m_t)


def _mlgate_call(gt, *, seq):
    bsz, ng, t = gt.shape
    nl, nc = seq // ML_CHUNK, (t - seq) // ML_CHUNK
    kern = functools.partial(_mlgate_kernel, nl=nl, nc=nc)
    rout = pl.BlockSpec((None, 4 * ML_UNITS, t), lambda b: (b, 0, 0))
    cout = pl.BlockSpec((None, 2 * ML_UNITS, t), lambda b: (b, 0, 0))
    return pl.pallas_call(
        kern,
        grid=(bsz,),
        in_specs=[pl.BlockSpec((None, ng, t), lambda b: (b, 0, 0))],
        out_specs=[rout, cout],
        out_shape=[jax.ShapeDtypeStruct((bsz, 4 * ML_UNITS, t), F32),
                   jax.ShapeDtypeStruct((bsz, 2 * ML_UNITS, t), F32)],
        scratch_shapes=[pltpu.VMEM((ML_UNITS, t), F32)] * 7,
        compiler_params=_cparams(("arbitrary",)),
        name="mlstm_gates",
    )(gt)


def _mlstm_kernel(qf_ref, ktf_ref, vf_ref, rf_ref, cf_ref, qb_ref, ktb_ref, vb_ref, rb_ref, cb_ref,
                  hf_ref, hb_ref, c_st):
    s = pl.program_id(1)

    @pl.when(s == 0)
    def _():
        c_st[...] = jnp.zeros_like(c_st)

    L, dh, U = ML_CHUNK, ML_DH, ML_UNITS
    ti = lax.broadcasted_iota(jnp.int32, (L, L), 0)
    si = lax.broadcasted_iota(jnp.int32, (L, L), 1)
    ones = jnp.ones((L, dh), BF16)
    for bb in range(qf_ref.shape[0]):
        for d, (q_ref, kt_ref, v_ref, r_ref, c_ref, o_ref) in enumerate(
                ((qf_ref, ktf_ref, vf_ref, rf_ref, cf_ref, hf_ref),
                 (qb_ref, ktb_ref, vb_ref, rb_ref, cb_ref, hb_ref))):
            tri = (si >= ti) if d == 1 else (si <= ti)
            rows = r_ref[bb]
            cols = c_ref[bb]
            for hd in range(ML_HEADS):
                u = d * ML_HEADS + hd
                sl = slice(hd * dh, (hd + 1) * dh)
                q = q_ref[bb, :, sl]
                kt = kt_ref[bb, sl, :]
                v_ext = jnp.concatenate([v_ref[bb, :, sl], ones], axis=1)
                g_row = rows[u:u + 1, :]
                w_row = rows[U + u:U + u + 1, :]
                a_row = rows[2 * U + u:2 * U + u + 1, :]
                m_row = rows[3 * U + u:3 * U + u + 1, :]
                xb = jnp.broadcast_to(cols[:, u:u + 1], (L, L))
                emt = cols[:, U + u:U + u + 1]
                c_old = c_st[bb * U + u]
                qkc = jnp.dot(q, jnp.concatenate([kt, c_old.astype(BF16)], axis=1),
                              preferred_element_type=F32)
                dm = jnp.exp(jnp.where(tri, xb + g_row, NEG))
                dec = jnp.exp(xb + m_row)
                sc = (qkc[:, :L] * dm).astype(BF16)
                ktw = (kt.astype(F32) * w_row).astype(BF16)
                sv = jnp.dot(jnp.concatenate([sc, ktw], axis=0), v_ext,
                             preferred_element_type=F32)
                ab = sv[:L] + jnp.concatenate([dec, dec], axis=1) * qkc[:, L:]
                o_ref[bb, :, sl] = ab[:, :dh] / jnp.maximum(jnp.abs(ab[:, dh:]), emt)
                c_st[bb * U + u] = jnp.concatenate([a_row, a_row], axis=1) * c_old + sv[L:]


def _mlstm_call(mq, mkt, mv, rowp, colp, *, seq):
    bsz, t, w = mq.shape
    nl, nc = seq // ML_CHUNK, (t - seq) // ML_CHUNK
    cf = functools.partial(_scan_chunk, rev=False, nl=nl, nc=nc)
    cr = functools.partial(_scan_chunk, rev=True, nl=nl, nc=nc)
    nr = rowp.shape[1]
    bpb = 8 if bsz % 8 == 0 else 1

    def tok(ch, wd):
        return pl.BlockSpec((bpb, ML_CHUNK, wd), lambda b, s: (b, ch(s), 0))

    def tr(ch, rows):
        return pl.BlockSpec((bpb, rows, ML_CHUNK), lambda b, s: (b, 0, ch(s)))

    def side(ch):
        return [tok(ch, w), tr(ch, w), tok(ch, w), tr(ch, nr), tok(ch, colp.shape[2])]

    return pl.pallas_call(
        _mlstm_kernel,
        grid=(bsz // bpb, nl + nc),
        in_specs=side(cf) + side(cr),
        out_specs=[tok(cf, w), tok(cr, w)],
        out_shape=[jax.ShapeDtypeStruct((bsz, t, w), F32)] * 2,
        scratch_shapes=[pltpu.VMEM((bpb * ML_UNITS, ML_DH, 2 * ML_DH), F32)],
        compiler_params=_cparams(("arbitrary", "arbitrary")),
        name="mlstm",
    )(mq, mkt, mv, rowp, colp, mq, mkt, mv, rowp, colp)


def _attn_kernel(sink_ref, q_ref, k_ref, v_ref, o_ref, vext, *, seq, qbs):
    kv = pl.program_id(1)
    i = pl.program_id(2)
    blk = ATT_BLOCK
    t = k_ref.shape[0]
    ctx = t - seq
    nlb = seq // blk
    band = 3 * blk

    @pl.when(i == 0)
    def _():
        vext[:, 0:LANES] = v_ref[...]
        vext[:, LANES:2 * LANES] = jnp.ones((t, LANES), BF16)

    lane = lax.broadcasted_iota(jnp.int32, (blk, LANES), 1)
    low = lane < AT_DH
    zero = jnp.zeros((blk, LANES), BF16)
    rows = AT_G * blk
    gi = lax.broadcasted_iota(jnp.int32, (rows, 1), 0) // blk
    snk = jnp.zeros((rows, 1), F32)
    for g in range(AT_G):
        snk = jnp.where(gi == g, sink_ref[kv * AT_G + g] * LOG2E, snk)
    ti = lax.broadcasted_iota(jnp.int32, (blk, band), 0)
    ci = lax.broadcasted_iota(jnp.int32, (blk, band), 1)
    nt = (((1,), (1,)), ((), ()))
    k_ctx = k_ref[seq:t, :]
    v_ctx = vext[seq:t, :]

    for j in range(qbs):
        qb = i * qbs + j
        s0 = pl.multiple_of(jnp.clip((qb - 1) * blk, 0, t - band), blk)
        rel = ci - ti + (s0 - qb * blk + blk)
        lim = jnp.where(qb < nlb, seq - s0, 0)
        ok = jnp.logical_and(jnp.logical_and(rel >= 0, rel <= 2 * blk), ci < lim)
        bias = jnp.where(ok, 0.0, NEG)
        bias = jnp.concatenate([bias] * AT_G, axis=0)

        qs = []
        for h2 in range(AT_G // 2):
            qj = q_ref[j * blk:(j + 1) * blk, h2 * LANES:(h2 + 1) * LANES]
            qs += [jnp.where(low, qj, zero), jnp.where(low, zero, qj)]
        q_all = jnp.concatenate(qs, axis=0)
        s_band = lax.dot_general(q_all, k_ref[pl.ds(s0, band), :], nt,
                                 preferred_element_type=F32) + bias
        s_ctx = lax.dot_general(q_all, k_ctx, nt, preferred_element_type=F32)
        m = jnp.maximum(jnp.maximum(jnp.max(s_band, axis=1, keepdims=True),
                                    jnp.max(s_ctx, axis=1, keepdims=True)), snk)
        p_band = jnp.exp2(s_band - m).astype(BF16)
        p_ctx = jnp.exp2(s_ctx - m).astype(BF16)
        o2 = (jnp.dot(p_band, vext[pl.ds(s0, band), :], preferred_element_type=F32)
              + jnp.dot(p_ctx, v_ctx, preferred_element_type=F32))
        o2 = o2[:, :LANES] / (o2[:, LANES:] + jnp.exp2(snk - m))
        for h2 in range(AT_G // 2):
            a = o2[(2 * h2) * blk:(2 * h2 + 1) * blk, :]
            b = o2[(2 * h2 + 1) * blk:(2 * h2 + 2) * blk, :]
            o_ref[j * blk:(j + 1) * blk, h2 * LANES:(h2 + 1) * LANES] = (
                jnp.where(low, a, b).astype(o_ref.dtype))


def _attn_call(sink, aq, akd, avd, *, seq):
    bsz, t, _ = aq.shape
    blk = ATT_BLOCK
    qbs = 3
    assert t % (qbs * blk) == 0 and t >= 3 * blk
    gw = AT_G * AT_DH
    kern = functools.partial(_attn_kernel, seq=seq, qbs=qbs)
    whole = pl.BlockSpec((None, t, LANES), lambda b, kv, i: (b, 0, kv))
    return pl.pallas_call(
        kern,
        grid=(bsz, AT_KV, t // (qbs * blk)),
        in_specs=[
            pl.BlockSpec(memory_space=pltpu.SMEM),
            pl.BlockSpec((None, qbs * blk, gw), lambda b, kv, i: (b, i, kv)),
            whole, whole,
        ],
        out_specs=pl.BlockSpec((None, qbs * blk, gw), lambda b, kv, i: (b, i, kv)),
        out_shape=jax.ShapeDtypeStruct((bsz, t, AT_W), BF16),
        scratch_shapes=[pltpu.VMEM((t, 2 * LANES), BF16)],
        compiler_params=_cparams(("arbitrary", "arbitrary", "arbitrary")),
        name="attn",
    )(sink, aq, akd, avd)


def _merge_kernel(x_ref, ml_ref, mc_ref, rf_ref, rb_ref, gg_ref, hf_ref, hb_ref, mo_ref, yc_ref,
                  br_ref, ng_ref, wb_ref, wo_ref, g_ref, w1_hbm, w3_hbm, w2_hbm, o_ref, act_ref,
                  w1b, w3b, w2b, stg_c, stg_r, sem, *, lead, seq, tm, ck):
    first = jnp.logical_and(pl.program_id(0) == 0, pl.program_id(1) == 0)
    stream = _FfnWeightStream((w1_hbm, w3_hbm, w2_hbm), lead, (w1b, w3b, w2b),
                              stg_c, stg_r, sem, ck)

    @pl.when(first)
    def _():
        stream.prime()

    t0 = pl.program_id(1) * tm
    rows = t0 + lax.broadcasted_iota(jnp.int32, (tm, 1), 0)
    is_ctx = rows >= seq
    d = x_ref.shape[1]
    rsum = (rf_ref[...] + rb_ref[...]).reshape(tm, D_RNN)
    ya = (rsum * gg_ref[...].astype(F32)).astype(BF16)
    hsum = hf_ref[...] + hb_ref[...]
    parts = []
    for hd in range(ML_HEADS):
        hh = hsum[:, hd * ML_DH:(hd + 1) * ML_DH]
        parts.append(hh * lax.rsqrt(jnp.mean(hh * hh, axis=-1, keepdims=True) + EPS))
    yb = (jnp.concatenate(parts, axis=1) * ng_ref[...] * mo_ref[...].astype(F32)).astype(BF16)
    m = (br_ref[:, 0:d].astype(F32) * jnp.dot(ya, wb_ref[0], preferred_element_type=F32)
         + br_ref[:, d:2 * d].astype(F32) * jnp.dot(yb, wb_ref[1], preferred_element_type=F32)
         + br_ref[:, 2 * d:3 * d].astype(F32) * jnp.dot(yc_ref[...], wb_ref[2],
                                                       preferred_element_type=F32))
    y = jnp.dot(m.astype(BF16), wo_ref[...], preferred_element_type=F32)
    x1 = x_ref[...] + _row_mod(ml_ref, mc_ref, is_ctx, 5) * y
    @pl.when(first)
    def _():
        o_ref[...] = _ffn_body(x1, ml_ref, mc_ref, is_ctx, g_ref, w1b, w3b, w2b, act_ref,
                               base=6, ck=ck, stream=stream)

    @pl.when(jnp.logical_not(first))
    def _():
        o_ref[...] = _ffn_body(x1, ml_ref, mc_ref, is_ctx, g_ref, w1b, w3b, w2b, act_ref,
                               base=6, ck=ck)


def _merge_call(x, ml, mc, rf, rb, gg, hf, hb, mo, yc, br, ng, wb, wo, g, w1, w3, w2,
                *, layer, seq, out_rows):
    bsz, t, d = x.shape
    dff = w1.shape[-1]
    tm = _pick_tile(t, (384, 128))
    ck = _pick_tile(dff, (256, 128))
    kern = functools.partial(_merge_kernel, lead=(layer, 1), seq=seq, tm=tm, ck=ck)
    hbm = pl.BlockSpec(memory_space=pl.ANY)

    def tok(wd):
        return pl.BlockSpec((None, tm, wd), lambda b, i: (b, i, 0))

    rspec = pl.BlockSpec((tm // SUBLANES, None, SUBLANES, D_RNN), lambda b, i: (i, b, 0, 0))

    return pl.pallas_call(
        kern,
        grid=(bsz, pl.cdiv(out_rows, tm)),
        in_specs=[
            tok(d),
            pl.BlockSpec((None, N_MOD, d), lambda b, i: (b, 0, 0)),
            _const_spec((N_MOD, d)),
            rspec, rspec, tok(D_RNN), tok(ML_W), tok(ML_W), tok(ML_W), tok(AT_W),
            tok(N_BRANCH * d),
            _const_spec((1, ML_W)),
            _const_spec((N_BRANCH, BRANCH_W, d), (layer,)),
            _const_spec((d, d), (layer,)),
            _const_spec((1, d)),
            hbm, hbm, hbm,
        ],
        out_specs=tok(d),
        out_shape=jax.ShapeDtypeStruct((bsz, out_rows, d), F32),
        scratch_shapes=[pltpu.VMEM((tm, dff), BF16)] + _ffn_weight_scratch(d, dff, ck),
        compiler_params=_cparams(("arbitrary", "arbitrary")),
        name="merge_ffn",
    )(x, ml, mc, rf, rb, gg, hf, hb, mo, yc, br, ng, wb, wo, g, w1, w3, w2)


def _block_diag(w):
    n, bi, bj = w.shape
    eye = jnp.eye(n, dtype=w.dtype)
    return (eye[:, None, :, None] * w[:, :, None, :]).reshape(n * bi, n * bj)


def _rope_tables(seq, t):
    rows = seq // GRID_W
    row = jnp.repeat(jnp.arange(rows), GRID_W).astype(F32)
    col = jnp.broadcast_to(jnp.arange(GRID_W), (rows, GRID_W)).reshape(-1).astype(F32)
    half = AT_DH // 2
    inv = ROPE_BASE ** (-jnp.arange(0, half, 2, dtype=F32) / half)
    ar = row[:, None] * inv
    ac = col[:, None] * inv
    ang = jnp.concatenate([ar, ar, ac, ac], axis=-1)
    cos = jnp.concatenate([jnp.cos(ang), jnp.ones((t - seq, AT_DH), F32)], axis=0)
    sin = jnp.concatenate([jnp.sin(ang), jnp.zeros((t - seq, AT_DH), F32)], axis=0)
    cos = jnp.tile(cos, (1, LANES // AT_DH))
    sin = jnp.tile(sin, (1, LANES // AT_DH))
    first_half = (jnp.arange(LANES) % (2 * ROPE_Q)) < ROPE_Q
    sin_a = jnp.where(first_half, -sin, 0.0)
    sin_b = jnp.where(first_half, 0.0, sin)
    return cos, sin_a, sin_b


def kernel(x, c, ctx, c_ctx, ada_w, ada_b, norm_g, ffn_w1, ffn_w3, ffn_w2, w_in, rg_conv_w,
           rg_conv_b, rg_wa, rg_ba, rg_wi, rg_bi, rg_lam, ml_gate_b, ml_norm_g, at_qn_g,
           at_kn_g, at_sink, w_branch, w_out):
    bsz, seq, d = x.shape
    nctx = ctx.shape[1]
    t = seq + nctx
    depth = ada_w.shape[0]

    xs = jnp.concatenate([x, ctx], axis=1)
    mod_rows = 2 * SUBLANES
    cc = jnp.concatenate([c, c_ctx[None, :], jnp.zeros((mod_rows - bsz - 1, d), F32)], axis=0)
    mods = _ada_call(cc, ada_w, ada_b).reshape(depth, mod_rows, N_MOD, d)
    cos, sin_a, sin_b = _rope_tables(seq, t)
    gmat = _block_diag(jnp.full((AT_HEADS, AT_DH, AT_DH), 1.0 / AT_DH, F32)).astype(BF16)
    w1, w3, w2 = ffn_w1, ffn_w3, ffn_w2
    w_head, w_tail = w_in[..., :P_HEAD].astype(BF16), w_in[..., P_HEAD:].astype(BF16)
    w_br, w_o = w_branch.astype(BF16), w_out.astype(BF16)

    for l in range(depth):
        ml = mods[l, :bsz]
        mc = mods[l, bsz]
        last = l == depth - 1

        xs = _ffn_call(xs, ml, mc, norm_g[l, 0][None, :], w1, w3, w2,
                       lead=(l, 0), base=0, seq=seq, out_rows=t)

        qg = jnp.tile(at_qn_g[l], AT_HEADS)[None, :]
        kg = jnp.tile(at_kn_g[l], AT_KV)[None, :]
        gb = jnp.concatenate([ml_gate_b[l], jnp.zeros((LANES - N_GATE,), F32)])[None, :]
        (rgx, rgg, mq, mk, mv, mo, aq, akd, avd, br, mg) = _proj_call(
            xs, ml, mc, norm_g[l, 1][None, :], w_head, w_tail, gmat, qg, kg, gb,
            cos, sin_a, sin_b, layer=l, seq=seq)

        wcat = jnp.stack([jnp.concatenate([_block_diag(rg_wa[l, dr]), _block_diag(rg_wi[l, dr])],
                                          axis=1) for dr in range(2)])
        wcat = (0.5 * wcat).astype(BF16)
        bias = 0.5 * jnp.concatenate([rg_ba[l], rg_bi[l]], axis=1)[:, None, :]
        clam = (-0.5 * LOG2E * LRU_C * jax.nn.softplus(-rg_lam[l]))[:, None, :]
        rhf, rhb = _rglru_call(rgx, rg_conv_w[l], rg_conv_b[l][None, :], wcat, bias, clam,
                               seq=seq)

        gt = jnp.swapaxes(mg[:, :, :N_GATE], 1, 2)
        rowp, colsrc = _mlgate_call(gt, seq=seq)
        colp = jnp.swapaxes(colsrc, 1, 2)
        mkt = jnp.swapaxes(mk, 1, 2)
        mhf, mhb = _mlstm_call(mq, mkt, mv, rowp, colp, seq=seq)

        yc = _attn_call(at_sink[l], aq, akd, avd, seq=seq)

        xs = _merge_call(xs, ml, mc, rhf, rhb, rgg, mhf, mhb, mo, yc, br, ml_norm_g[l][None, :],
                         w_br, w_o, norm_g[l, 2][None, :], w1, w3, w2,
                         layer=l, seq=seq, out_rows=seq if last else t)
    return xs
```

```python
import functools

import jax
import jax.numpy as jnp
from jax import lax
from jax.experimental import pallas as pl
from jax.experimental.pallas import tpu as pltpu

F32 = jnp.float32
BF16 = jnp.bfloat16

EPS = 1e-6
NEG = -1e30
TINY = 1e-36
LOG2E = 1.4426950408889634
N_MOD = 9
GRID_W = 64
ROPE_BASE = 10000.0

D_RNN = 512
RNN_BLOCKS = 8
RNN_BLOCK = D_RNN // RNN_BLOCKS
CONV_W = 4
CONV_LEFT = 2
CONV_RIGHT = CONV_W - 1 - CONV_LEFT
LRU_C = 8.0

ML_HEADS = 4
ML_DH = 128
ML_W = ML_HEADS * ML_DH
ML_CHUNK = 128
ML_UNITS = 2 * ML_HEADS

AT_HEADS = 8
AT_KV = 2
AT_DH = 64
AT_G = AT_HEADS // AT_KV
AT_W = AT_HEADS * AT_DH
AT_KVW = AT_KV * AT_DH
ATT_BLOCK = 128
ROPE_Q = AT_DH // 4

N_BRANCH = 3
BRANCH_W = 512

LANES = 128
SUBLANES = 8
VMEM_LIMIT = 56 * 1024 * 1024

P_RGX = 0
P_RGG = P_RGX + D_RNN
P_MQ = P_RGG + D_RNN
P_MK = P_MQ + ML_W
P_MV = P_MK + ML_W
P_MO = P_MV + ML_W
P_HEAD = P_MO + ML_W
N_GATE = 4 * ML_HEADS
T_AQ = 0
T_AK = T_AQ + AT_W
T_AV = T_AK + AT_KVW
T_BR = T_AV + AT_KVW
GATE_SEG = 512


def _cparams(sem):
    return pltpu.CompilerParams(dimension_semantics=sem, vmem_limit_bytes=VMEM_LIMIT)


def _const_spec(shape, lead=()):
    nd = len(shape)
    idx = tuple(lead) + (0,) * nd
    return pl.BlockSpec((None,) * len(lead) + tuple(shape), lambda *_: idx,
                        pipeline_mode=pl.Buffered(1))


def _pick_tile(total, candidates):
    for c in candidates:
        if total % c == 0:
            return c
    raise ValueError(f"no tile for {total}")


def _sigmoid(x):
    return jax.nn.sigmoid(x)


def _ln_mod(x, g, shift, scale):
    ms = jnp.mean(x * x, axis=-1, keepdims=True)
    return x * lax.rsqrt(ms + EPS) * g * (1.0 + scale) + shift


def _row_mod(ml_ref, mc_ref, is_ctx, i):
    return jnp.where(is_ctx, mc_ref[i:i + 1, :], ml_ref[i:i + 1, :])


def _ada_kernel(cc_ref, w_ref, b_ref, o_ref):
    cc = cc_ref[...]
    s = cc * _sigmoid(cc)
    o_ref[...] = jnp.dot(s.astype(BF16), w_ref[...].astype(BF16),
                         preferred_element_type=F32) + b_ref[...]


def _ada_call(cc, ada_w, ada_b):
    depth, d, nout = ada_w.shape
    rows = cc.shape[0]
    tn = _pick_tile(nout, (1536, 1024, 512, 256, 128))
    return pl.pallas_call(
        _ada_kernel,
        grid=(depth, nout // tn),
        in_specs=[
            pl.BlockSpec((rows, d), lambda l, j: (0, 0)),
            pl.BlockSpec((None, d, tn), lambda l, j: (l, 0, j)),
            pl.BlockSpec((None, 1, tn), lambda l, j: (l, 0, j)),
        ],
        out_specs=pl.BlockSpec((None, rows, tn), lambda l, j: (l, 0, j)),
        out_shape=jax.ShapeDtypeStruct((depth, rows, nout), F32),
        compiler_params=_cparams(("arbitrary", "arbitrary")),
        name="ada_mod",
    )(cc, ada_w, ada_b.reshape(depth, 1, nout))


def _ffn_body(x, ml_ref, mc_ref, is_ctx, g_ref, w1_ref, w3_ref, w2_ref, act_ref, *, base, ck,
              stream=None):
    h = _ln_mod(x, g_ref[...], _row_mod(ml_ref, mc_ref, is_ctx, base),
                _row_mod(ml_ref, mc_ref, is_ctx, base + 1)).astype(BF16)
    dff = w1_ref.shape[1]
    nck = dff // ck
    if stream is not None:
        stream.land(0)
        stream.refill(0)
    for j in range(nck):
        if stream is not None and j + 1 < nck:
            stream.land(j + 1)
        a = jnp.dot(h, w1_ref[:, j * ck:(j + 1) * ck], preferred_element_type=F32)
        b = jnp.dot(h, w3_ref[:, j * ck:(j + 1) * ck], preferred_element_type=F32)
        act_ref[:, j * ck:(j + 1) * ck] = (a * _sigmoid(a) * b).astype(BF16)
        if stream is not None and j + 1 < nck:
            stream.refill(j + 1)
    y = jnp.dot(act_ref[...], w2_ref[...], preferred_element_type=F32)
    return x + 0.5 * _row_mod(ml_ref, mc_ref, is_ctx, base + 2) * y


def _ffn_weight_scratch(d, dff, ck):
    return [pltpu.VMEM((d, dff), BF16), pltpu.VMEM((d, dff), BF16), pltpu.VMEM((dff, d), BF16),
            pltpu.VMEM((2, 2, d, ck), F32), pltpu.VMEM((2, ck, d), F32),
            pltpu.SemaphoreType.DMA((2, 3))]


class _FfnWeightStream:
    def __init__(self, hbm, lead, resident, stg_c, stg_r, sem, ck):
        self.hbm, self.lead, self.resident = hbm, lead, resident
        self.stg_c, self.stg_r, self.sem, self.ck = stg_c, stg_r, sem, ck
        self.nck = resident[0].shape[1] // ck

    def _stages(self, j):
        s = j % 2
        return (self.stg_c.at[s, 0], self.stg_c.at[s, 1], self.stg_r.at[s])

    def _copies(self, j):
        l0, l1 = self.lead
        sl = slice(j * self.ck, (j + 1) * self.ck)
        srcs = (self.hbm[0].at[l0, l1, :, sl], self.hbm[1].at[l0, l1, :, sl],
                self.hbm[2].at[l0, l1, sl, :])
        return [pltpu.make_async_copy(src, stg, self.sem.at[j % 2, i])
                for i, (src, stg) in enumerate(zip(srcs, self._stages(j)))]

    def start(self, j):
        for cp in self._copies(j):
            cp.start()

    def prime(self):
        self.start(0)
        if self.nck > 1:
            self.start(1)

    def land(self, j):
        sl = slice(j * self.ck, (j + 1) * self.ck)
        for cp in self._copies(j):
            cp.wait()
        w1b, w3b, w2b = self.resident
        s1, s3, s2 = self._stages(j)
        w1b[:, sl] = s1[...].astype(BF16)
        w3b[:, sl] = s3[...].astype(BF16)
        w2b[sl, :] = s2[...].astype(BF16)

    def refill(self, j):
        if j + 2 < self.nck:
            self.start(j + 2)


def _ffn_kernel(x_ref, ml_ref, mc_ref, g_ref, w1_hbm, w3_hbm, w2_hbm, o_ref, act_ref,
                w1b, w3b, w2b, stg_c, stg_r, sem, *, lead, base, seq, tm, ck):
    t0 = pl.program_id(1) * tm
    rows = t0 + lax.broadcasted_iota(jnp.int32, (tm, 1), 0)
    is_ctx = rows >= seq
    first = jnp.logical_and(pl.program_id(0) == 0, pl.program_id(1) == 0)

    @pl.when(first)
    def _():
        stream = _FfnWeightStream((w1_hbm, w3_hbm, w2_hbm), lead, (w1b, w3b, w2b),
                                  stg_c, stg_r, sem, ck)
        stream.prime()
        o_ref[...] = _ffn_body(x_ref[...], ml_ref, mc_ref, is_ctx, g_ref, w1b, w3b, w2b,
                               act_ref, base=base, ck=ck, stream=stream)

    @pl.when(jnp.logical_not(first))
    def _():
        o_ref[...] = _ffn_body(x_ref[...], ml_ref, mc_ref, is_ctx, g_ref, w1b, w3b, w2b,
                               act_ref, base=base, ck=ck)


def _ffn_call(x, ml, mc, g, w1, w3, w2, *, lead, base, seq, out_rows):
    bsz, t, d = x.shape
    dff = w1.shape[-1]
    tm = _pick_tile(t, (768, 384, 128))
    ck = _pick_tile(dff, (256, 128))
    kern = functools.partial(_ffn_kernel, lead=lead, base=base, seq=seq, tm=tm, ck=ck)
    hbm = pl.BlockSpec(memory_space=pl.ANY)
    return pl.pallas_call(
        kern,
        grid=(bsz, pl.cdiv(out_rows, tm)),
        in_specs=[
            pl.BlockSpec((None, tm, d), lambda b, i: (b, i, 0)),
            pl.BlockSpec((None, N_MOD, d), lambda b, i: (b, 0, 0)),
            _const_spec((N_MOD, d)),
            _const_spec((1, d)),
            hbm, hbm, hbm,
        ],
        out_specs=pl.BlockSpec((None, tm, d), lambda b, i: (b, i, 0)),
        out_shape=jax.ShapeDtypeStruct((bsz, out_rows, d), F32),
        scratch_shapes=[pltpu.VMEM((tm, dff), BF16)] + _ffn_weight_scratch(d, dff, ck),
        compiler_params=_cparams(("arbitrary", "arbitrary")),
        name="ffn",
    )(x, ml, mc, g, w1, w3, w2)


def _gelu_tanh(x):
    return 0.5 * x * (1.0 + jnp.tanh(0.7978845608028654 * (x + 0.044715 * (x * x * x))))


def _head_norm(x, gmat, g):
    ms = jnp.dot((x * x).astype(BF16), gmat, preferred_element_type=F32)
    return x * lax.rsqrt(ms + EPS) * g


def _dup_heads(x):
    low = lax.broadcasted_iota(jnp.int32, x.shape, 1) < AT_DH
    sw = pltpu.roll(x, AT_DH, 1)
    return jnp.concatenate([jnp.where(low, x, sw), jnp.where(low, sw, x)], axis=1)


def _rope(x, cos, sin_a, sin_b):
    parts = []
    for j in range(x.shape[1] // LANES):
        xj = x[:, j * LANES:(j + 1) * LANES]
        parts.append(xj * cos + pltpu.roll(xj, LANES - ROPE_Q, 1) * sin_a
                     + pltpu.roll(xj, ROPE_Q, 1) * sin_b)
    return jnp.concatenate(parts, axis=1)


def _proj_kernel(x_ref, ml_ref, mc_ref, g_ref, wh_ref, wt_ref, gmat_ref, qg_ref, kg_ref, gb_ref,
                 cos_ref, sa_ref, sb_ref,
                 rgx_o, rgg_o, mq_o, mk_o, mv_o, mo_o, aq_o, ak_o, av_o, br_o, mg_o,
                 tail, *, seq, tm, d_model):
    @pl.when(jnp.logical_and(pl.program_id(0) == 0, pl.program_id(1) == 0))
    def _():
        for r in range(0, d_model, LANES):
            tail[r:r + LANES, :] = wt_ref[r:r + LANES, N_GATE:]

    t0 = pl.program_id(1) * tm
    rows = t0 + lax.broadcasted_iota(jnp.int32, (tm, 1), 0)
    is_ctx = rows >= seq
    h = _ln_mod(x_ref[...], g_ref[...], _row_mod(ml_ref, mc_ref, is_ctx, 3),
                _row_mod(ml_ref, mc_ref, is_ctx, 4)).astype(BF16)

    def seg(c0, w, ref=wh_ref):
        return jnp.dot(h, ref[:, c0:c0 + w], preferred_element_type=F32)

    rgx_o[...] = seg(P_RGX, D_RNN).reshape(rgx_o.shape)
    rgg_o[...] = _gelu_tanh(seg(P_RGG, D_RNN)).astype(BF16)
    mq_o[...] = seg(P_MQ, ML_W).astype(BF16)
    mk_o[...] = (seg(P_MK, ML_W) * (ML_DH ** -0.5)).astype(BF16)
    mv_o[...] = seg(P_MV, ML_W).astype(BF16)
    mo_o[...] = _sigmoid(seg(P_MO, ML_W)).astype(BF16)
    cos, sin_a, sin_b = cos_ref[...], sa_ref[...], sb_ref[...]
    gmat = gmat_ref[...]
    q = _head_norm(seg(T_AQ, AT_W, tail), gmat, qg_ref[...])
    aq_o[...] = (_rope(q, cos, sin_a, sin_b) * (AT_DH ** -0.5 * LOG2E)).astype(BF16)
    k = _head_norm(seg(T_AK, AT_KVW, tail), gmat[:AT_KVW, :AT_KVW], kg_ref[...])
    ak_o[...] = _dup_heads(_rope(k, cos, sin_a, sin_b)).astype(BF16)
    av_o[...] = _dup_heads(seg(T_AV, AT_KVW, tail)).astype(BF16)
    for c0 in range(0, N_BRANCH * d_model, GATE_SEG):
        br_o[:, c0:c0 + GATE_SEG] = _sigmoid(seg(T_BR + c0, GATE_SEG, tail)).astype(BF16)
    mg_o[...] = seg(0, LANES, wt_ref) + gb_ref[...]


def _proj_call(x, ml, mc, g, wh, wt, gmat, qg, kg, gb, cos, sin_a, sin_b, *, layer, seq):
    bsz, t, d = x.shape
    ntail = wt.shape[-1] - N_GATE
    tm = _pick_tile(t, (384, 128))
    kern = functools.partial(_proj_kernel, seq=seq, tm=tm, d_model=d)
    widths = [(D_RNN, F32), (D_RNN, BF16), (ML_W, BF16), (ML_W, BF16), (ML_W, BF16),
              (ML_W, BF16), (AT_W, BF16), (2 * AT_KVW, BF16), (2 * AT_KVW, BF16),
              (N_BRANCH * d, BF16), (LANES, F32)]
    tab_spec = pl.BlockSpec((tm, LANES), lambda b, i: (i, 0))
    out_specs = [pl.BlockSpec((None, tm, wd), lambda b, i: (b, i, 0)) for wd, _ in widths]
    out_shape = [jax.ShapeDtypeStruct((bsz, t, wd), dt) for wd, dt in widths]
    out_specs[0] = pl.BlockSpec((tm // SUBLANES, None, SUBLANES, D_RNN), lambda b, i: (i, b, 0, 0))
    out_shape[0] = jax.ShapeDtypeStruct((t // SUBLANES, bsz, SUBLANES, D_RNN), F32)
    return pl.pallas_call(
        kern,
        grid=(bsz, t // tm),
        in_specs=[
            pl.BlockSpec((None, tm, d), lambda b, i: (b, i, 0)),
            pl.BlockSpec((None, N_MOD, d), lambda b, i: (b, 0, 0)),
            _const_spec((N_MOD, d)),
            _const_spec((1, d)),
            _const_spec((d, wh.shape[-1]), (layer,)),
            _const_spec((d, wt.shape[-1]), (layer,)),
            _const_spec((AT_W, AT_W)),
            _const_spec((1, AT_W)),
            _const_spec((1, AT_KVW)),
            _const_spec((1, LANES)),
            tab_spec, tab_spec, tab_spec,
        ],
        out_specs=out_specs,
        out_shape=out_shape,
        scratch_shapes=[pltpu.VMEM((d, ntail), BF16)],
        compiler_params=_cparams(("arbitrary", "arbitrary")),
        name="in_proj",
    )(x, ml, mc, g, wh, wt, gmat, qg, kg, gb, cos, sin_a, sin_b)


def _scan_chunk(s, rev, nl, nc):
    if rev:
        return jnp.where(s < nc, nl + nc - 1 - s, nl - 1 - (s - nc))
    return jnp.where(s < nc, nl + s, s - nc)


def _rglru_kernel(xf_ref, xfp_ref, xfn_ref, xb_ref, xbp_ref, xbn_ref, cw_ref, cb_ref, w_ref,
                  bias_ref, clam_ref, hf_ref, hb_ref, ext, a_s, b_s, h_s, hstate,
                  *, nl, nc, tc, sub):
    s = pl.program_id(0)
    bsz, c = hstate.shape[1], hstate.shape[2]
    ntb = tc // SUBLANES

    @pl.when(s == 0)
    def _():
        hstate[...] = jnp.zeros_like(hstate)

    cw = cw_ref[...]
    cb = cb_ref[...]
    for d, (x_ref, xp_ref, xn_ref) in enumerate(((xf_ref, xfp_ref, xfn_ref),
                                                (xb_ref, xbp_ref, xbn_ref))):
        chunk = _scan_chunk(s, d == 1, nl, nc)
        first = jnp.logical_or(chunk == 0, chunk == nl)
        last = jnp.logical_or(chunk == nl - 1, chunk == nl + nc - 1)
        prev = jnp.swapaxes(xp_ref[0], 0, 1)[SUBLANES - CONV_LEFT:]
        nxt = jnp.swapaxes(xn_ref[0], 0, 1)[:CONV_RIGHT]
        ext[d, 0:CONV_LEFT] = jnp.where(first, 0.0, prev)
        for i in range(ntb):
            o = CONV_LEFT + i * SUBLANES
            ext[d, o:o + SUBLANES] = jnp.swapaxes(x_ref[i], 0, 1)
        ext[d, CONV_LEFT + tc:CONV_LEFT + tc + CONV_RIGHT] = jnp.where(last, 0.0, nxt)

        def gates(i, carry, d=d):
            t0 = pl.multiple_of(i * sub, sub)
            u = cb
            for k in range(CONV_W):
                u = u + ext[d, pl.ds(t0 + k, sub)] * cw[k:k + 1, :]
            u2 = u.reshape(sub * bsz, c)
            z = jnp.dot(u2.astype(BF16), w_ref[d], preferred_element_type=F32) + bias_ref[d]
            a = jnp.exp2(clam_ref[d] * jnp.tanh(z[:, :c]) + clam_ref[d])
            gi = 0.5 * jnp.tanh(z[:, c:]) + 0.5
            s1 = 1.0 - a * a
            bb = (s1 * lax.rsqrt(jnp.maximum(s1, TINY))) * (gi * u2)
            a_s[d, pl.ds(t0, sub)] = a.reshape(sub, bsz, c)
            b_s[d, pl.ds(t0, sub)] = bb.reshape(sub, bsz, c)
            return carry

        lax.fori_loop(0, tc // sub, gates, 0)

    def step(k, hs):
        hf, hb = hs
        kb = tc - 1 - k
        hf = a_s[0, k] * hf + b_s[0, k]
        hb = a_s[1, kb] * hb + b_s[1, kb]
        h_s[0, k] = hf
        h_s[1, kb] = hb
        return hf, hb

    hf, hb = lax.fori_loop(0, tc, step, (hstate[0], hstate[1]), unroll=8)
    hstate[0] = hf
    hstate[1] = hb
    for d, o_ref in enumerate((hf_ref, hb_ref)):
        for i in range(ntb):
            o_ref[i] = jnp.swapaxes(h_s[d, i * SUBLANES:(i + 1) * SUBLANES], 0, 1)


def _rglru_call(rgx, cw, cb, w, bias, clam, *, seq):
    ntb_all, bsz, _, c = rgx.shape
    t = ntb_all * SUBLANES
    tc = 128
    sub = tc
    assert seq % tc == 0 and (t - seq) % tc == 0
    nl, nc = seq // tc, (t - seq) // tc
    ntb = tc // SUBLANES

    def specs(rev):
        chunk = functools.partial(_scan_chunk, rev=rev, nl=nl, nc=nc)
        return [
            pl.BlockSpec((ntb, bsz, SUBLANES, c), lambda s: (chunk(s), 0, 0, 0)),
            pl.BlockSpec((1, bsz, SUBLANES, c),
                         lambda s: (jnp.maximum(chunk(s) * ntb - 1, 0), 0, 0, 0)),
            pl.BlockSpec((1, bsz, SUBLANES, c),
                         lambda s: (jnp.minimum((chunk(s) + 1) * ntb, ntb_all - 1), 0, 0, 0)),
        ]

    kern = functools.partial(_rglru_kernel, nl=nl, nc=nc, tc=tc, sub=sub)
    fspec, bspec = specs(False), specs(True)
    return pl.pallas_call(
        kern,
        grid=(nl + nc,),
        in_specs=fspec + bspec + [
            _const_spec((CONV_W, c)),
            _const_spec((1, c)),
            _const_spec((2, c, 2 * c)),
            _const_spec((2, 1, 2 * c)),
            _const_spec((2, 1, c)),
        ],
        out_specs=[fspec[0], bspec[0]],
        out_shape=[jax.ShapeDtypeStruct(rgx.shape, F32)] * 2,
        scratch_shapes=[
            pltpu.VMEM((2, tc + CONV_W - 1, bsz, c), F32),
            pltpu.VMEM((2, tc, bsz, c), F32),
            pltpu.VMEM((2, tc, bsz, c), F32),
            pltpu.VMEM((2, tc, bsz, c), F32),
            pltpu.VMEM((2, bsz, c), F32),
        ],
        compiler_params=_cparams(("arbitrary",)),
        name="rglru",
    )(rgx, rgx, rgx, rgx, rgx, rgx, cw, cb, w, bias, clam)


def _log_sigmoid(x):
    return -(jnp.maximum(-x, 0.0) + jnp.log1p(jnp.exp(-jnp.abs(x))))


def _lane_scan(x, op, ident, rev):
    n = x.shape[1]
    lane = lax.broadcasted_iota(jnp.int32, x.shape, 1)
    sh = 1
    while sh < n:
        if rev:
            x = op(x, jnp.where(lane < n - sh, pltpu.roll(x, n - sh, 1), ident))
        else:
            x = op(x, jnp.where(lane >= sh, pltpu.roll(x, sh, 1), ident))
        sh *= 2
    return x


def _mlgate_kernel(g_ref, row_o, col_o, cb_s, gg_s, ml_s, tot_s, wm_s, mo_s, mn_s, *, nl, nc):
    L = ML_CHUNK
    H = ML_HEADS
    U = ML_UNITS
    is_bwd = lax.broadcasted_iota(jnp.int32, (U, L), 0) >= H
    for c in range(nl + nc):
        lanes = slice(c * L, (c + 1) * L)
        g16 = g_ref[:, lanes]
        li = jnp.concatenate([g16[0:H], g16[2 * H:3 * H]], axis=0)
        lf = _log_sigmoid(jnp.concatenate([g16[H:2 * H], g16[3 * H:4 * H]], axis=0))
        pre = _lane_scan(lf, jnp.add, 0.0, False)
        suf = _lane_scan(lf, jnp.add, 0.0, True)
        tot = pre + suf - lf
        cb = jnp.where(is_bwd, suf, pre)
        gg = li - cb
        pmax = _lane_scan(gg, jnp.maximum, NEG, False)
        smax = _lane_scan(gg, jnp.maximum, NEG, True)
        cb_s[:, lanes] = cb
        gg_s[:, lanes] = gg
        ml_s[:, lanes] = cb + jnp.where(is_bwd, smax, pmax)
        tot_s[:, lanes] = tot
        wm_s[:, lanes] = tot + jnp.maximum(pmax, smax)

    m = jnp.zeros((U, L), F32)
    for s in range(nl + nc):
        cf = nl + s if s < nc else s - nc
        cr = nl + nc - 1 - s if s < nc else nl - 1 - (s - nc)
        lf_, lr_ = slice(cf * L, (cf + 1) * L), slice(cr * L, (cr + 1) * L)
        tot = jnp.where(is_bwd, tot_s[:, lr_], tot_s[:, lf_])
        wm = jnp.where(is_bwd, wm_s[:, lr_], wm_s[:, lf_])
        m_new = jnp.maximum(tot + m, wm)
        mo_s[0:H, lf_] = m[0:H]
        mo_s[H:U, lr_] = m[H:U]
        mn_s[0:H, lf_] = m_new[0:H]
        mn_s[H:U, lr_] = m_new[H:U]
        m = m_new

    cb = cb_s[...]
    gg = gg_s[...]
    tot = tot_s[...]
    mo = mo_s[...]
    mn = mn_s[...]
    m_t = jnp.maximum(cb + mo, ml_s[...])
    row_o[0:U, :] = gg
    row_o[U:2 * U, :] = jnp.exp(tot + gg - mn)
    row_o[2 * U:3 * U, :] = jnp.exp(tot + mo - mn)
    row_o[3 * U:4 * U, :] = mo
    col_o[0:U, :] = cb - m_t
    col_o[U:2 * U, :] = jnp.exp(-m_t)


def _mlgate_call(gt, *, seq):
    bsz, ng, t = gt.shape
    nl, nc = seq // ML_CHUNK, (t - seq) // ML_CHUNK
    kern = functools.partial(_mlgate_kernel, nl=nl, nc=nc)
    rout = pl.BlockSpec((None, 4 * ML_UNITS, t), lambda b: (b, 0, 0))
    cout = pl.BlockSpec((None, 2 * ML_UNITS, t), lambda b: (b, 0, 0))
    return pl.pallas_call(
        kern,
        grid=(bsz,),
        in_specs=[pl.BlockSpec((None, ng, t), lambda b: (b, 0, 0))],
        out_specs=[rout, cout],
        out_shape=[jax.ShapeDtypeStruct((bsz, 4 * ML_UNITS, t), F32),
                   jax.ShapeDtypeStruct((bsz, 2 * ML_UNITS, t), F32)],
        scratch_shapes=[pltpu.VMEM((ML_UNITS, t), F32)] * 7,
        compiler_params=_cparams(("arbitrary",)),
        name="mlstm_gates",
    )(gt)


def _mlstm_kernel(qf_ref, ktf_ref, vf_ref, rf_ref, cf_ref, qb_ref, ktb_ref, vb_ref, rb_ref, cb_ref,
                  hf_ref, hb_ref, c_st):
    s = pl.program_id(1)

    @pl.when(s == 0)
    def _():
        c_st[...] = jnp.zeros_like(c_st)

    L, dh, U = ML_CHUNK, ML_DH, ML_UNITS
    ti = lax.broadcasted_iota(jnp.int32, (L, L), 0)
    si = lax.broadcasted_iota(jnp.int32, (L, L), 1)
    ones = jnp.ones((L, dh), BF16)
    for bb in range(qf_ref.shape[0]):
        for d, (q_ref, kt_ref, v_ref, r_ref, c_ref, o_ref) in enumerate(
                ((qf_ref, ktf_ref, vf_ref, rf_ref, cf_ref, hf_ref),
                 (qb_ref, ktb_ref, vb_ref, rb_ref, cb_ref, hb_ref))):
            tri = (si >= ti) if d == 1 else (si <= ti)
            rows = r_ref[bb]
            cols = c_ref[bb]
            for hd in range(ML_HEADS):
                u = d * ML_HEADS + hd
                sl = slice(hd * dh, (hd + 1) * dh)
                q = q_ref[bb, :, sl]
                kt = kt_ref[bb, sl, :]
                v_ext = jnp.concatenate([v_ref[bb, :, sl], ones], axis=1)
                g_row = rows[u:u + 1, :]
                w_row = rows[U + u:U + u + 1, :]
                a_row = rows[2 * U + u:2 * U + u + 1, :]
                m_row = rows[3 * U + u:3 * U + u + 1, :]
                xb = jnp.broadcast_to(cols[:, u:u + 1], (L, L))
                emt = cols[:, U + u:U + u + 1]
                c_old = c_st[bb * U + u]
                qkc = jnp.dot(q, jnp.concatenate([kt, c_old.astype(BF16)], axis=1),
                              preferred_element_type=F32)
                dm = jnp.exp(jnp.where(tri, xb + g_row, NEG))
                dec = jnp.exp(xb + m_row)
                sc = (qkc[:, :L] * dm).astype(BF16)
                ktw = (kt.astype(F32) * w_row).astype(BF16)
                sv = jnp.dot(jnp.concatenate([sc, ktw], axis=0), v_ext,
                             preferred_element_type=F32)
                ab = sv[:L] + jnp.concatenate([dec, dec], axis=1) * qkc[:, L:]
                o_ref[bb, :, sl] = ab[:, :dh] / jnp.maximum(jnp.abs(ab[:, dh:]), emt)
                c_st[bb * U + u] = jnp.concatenate([a_row, a_row], axis=1) * c_old + sv[L:]


def _mlstm_call(mq, mkt, mv, rowp, colp, *, seq):
    bsz, t, w = mq.shape
    nl, nc = seq // ML_CHUNK, (t - seq) // ML_CHUNK
    cf = functools.partial(_scan_chunk, rev=False, nl=nl, nc=nc)
    cr = functools.partial(_scan_chunk, rev=True, nl=nl, nc=nc)
    nr = rowp.shape[1]
    bpb = 8 if bsz % 8 == 0 else 1

    def tok(ch, wd):
        return pl.BlockSpec((bpb, ML_CHUNK, wd), lambda b, s: (b, ch(s), 0))

    def tr(ch, rows):
        return pl.BlockSpec((bpb, rows, ML_CHUNK), lambda b, s: (b, 0, ch(s)))

    def side(ch):
        return [tok(ch, w), tr(ch, w), tok(ch, w), tr(ch, nr), tok(ch, colp.shape[2])]

    return pl.pallas_call(
        _mlstm_kernel,
        grid=(bsz // bpb, nl + nc),
        in_specs=side(cf) + side(cr),
        out_specs=[tok(cf, w), tok(cr, w)],
        out_shape=[jax.ShapeDtypeStruct((bsz, t, w), F32)] * 2,
        scratch_shapes=[pltpu.VMEM((bpb * ML_UNITS, ML_DH, 2 * ML_DH), F32)],
        compiler_params=_cparams(("arbitrary", "arbitrary")),
        name="mlstm",
    )(mq, mkt, mv, rowp, colp, mq, mkt, mv, rowp, colp)


def _attn_kernel(sink_ref, q_ref, k_ref, v_ref, o_ref, vext, *, seq, qbs):
    kv = pl.program_id(1)
    i = pl.program_id(2)
    blk = ATT_BLOCK
    t = k_ref.shape[0]
    ctx = t - seq
    nlb = seq // blk
    band = 3 * blk

    @pl.when(i == 0)
    def _():
        vext[:, 0:LANES] = v_ref[...]
        vext[:, LANES:2 * LANES] = jnp.ones((t, LANES), BF16)

    lane = lax.broadcasted_iota(jnp.int32, (blk, LANES), 1)
    low = lane < AT_DH
    zero = jnp.zeros((blk, LANES), BF16)
    rows = AT_G * blk
    gi = lax.broadcasted_iota(jnp.int32, (rows, 1), 0) // blk
    snk = jnp.zeros((rows, 1), F32)
    for g in range(AT_G):
        snk = jnp.where(gi == g, sink_ref[kv * AT_G + g] * LOG2E, snk)
    ti = lax.broadcasted_iota(jnp.int32, (blk, band), 0)
    ci = lax.broadcasted_iota(jnp.int32, (blk, band), 1)
    nt = (((1,), (1,)), ((), ()))
    k_ctx = k_ref[seq:t, :]
    v_ctx = vext[seq:t, :]

    for j in range(qbs):
        qb = i * qbs + j
        s0 = pl.multiple_of(jnp.clip((qb - 1) * blk, 0, t - band), blk)
        rel = ci - ti + (s0 - qb * blk + blk)
        lim = jnp.where(qb < nlb, seq - s0, 0)
        ok = jnp.logical_and(jnp.logical_and(rel >= 0, rel <= 2 * blk), ci < lim)
        bias = jnp.where(ok, 0.0, NEG)
        bias = jnp.concatenate([bias] * AT_G, axis=0)

        qs = []
        for h2 in range(AT_G // 2):
            qj = q_ref[j * blk:(j + 1) * blk, h2 * LANES:(h2 + 1) * LANES]
            qs += [jnp.where(low, qj, zero), jnp.where(low, zero, qj)]
        q_all = jnp.concatenate(qs, axis=0)
        s_band = lax.dot_general(q_all, k_ref[pl.ds(s0, band), :], nt,
                                 preferred_element_type=F32) + bias
        s_ctx = lax.dot_general(q_all, k_ctx, nt, preferred_element_type=F32)
        m = jnp.maximum(jnp.maximum(jnp.max(s_band, axis=1, keepdims=True),
                                    jnp.max(s_ctx, axis=1, keepdims=True)), snk)
        p_band = jnp.exp2(s_band - m).astype(BF16)
        p_ctx = jnp.exp2(s_ctx - m).astype(BF16)
        o2 = (jnp.dot(p_band, vext[pl.ds(s0, band), :], preferred_element_type=F32)
              + jnp.dot(p_ctx, v_ctx, preferred_element_type=F32))
        o2 = o2[:, :LANES] / (o2[:, LANES:] + jnp.exp2(snk - m))
        for h2 in range(AT_G // 2):
            a = o2[(2 * h2) * blk:(2 * h2 + 1) * blk, :]
            b = o2[(2 * h2 + 1) * blk:(2 * h2 + 2) * blk, :]
            o_ref[j * blk:(j + 1) * blk, h2 * LANES:(h2 + 1) * LANES] = (
                jnp.where(low, a, b).astype(o_ref.dtype))


def _attn_call(sink, aq, akd, avd, *, seq):
    bsz, t, _ = aq.shape
    blk = ATT_BLOCK
    qbs = 3
    assert t % (qbs * blk) == 0 and t >= 3 * blk
    gw = AT_G * AT_DH
    kern = functools.partial(_attn_kernel, seq=seq, qbs=qbs)
    whole = pl.BlockSpec((None, t, LANES), lambda b, kv, i: (b, 0, kv))
    return pl.pallas_call(
        kern,
        grid=(bsz, AT_KV, t // (qbs * blk)),
        in_specs=[
            pl.BlockSpec(memory_space=pltpu.SMEM),
            pl.BlockSpec((None, qbs * blk, gw), lambda b, kv, i: (b, i, kv)),
            whole, whole,
        ],
        out_specs=pl.BlockSpec((None, qbs * blk, gw), lambda b, kv, i: (b, i, kv)),
        out_shape=jax.ShapeDtypeStruct((bsz, t, AT_W), BF16),
        scratch_shapes=[pltpu.VMEM((t, 2 * LANES), BF16)],
        compiler_params=_cparams(("arbitrary", "arbitrary", "arbitrary")),
        name="attn",
    )(sink, aq, akd, avd)


def _merge_kernel(x_ref, ml_ref, mc_ref, rf_ref, rb_ref, gg_ref, hf_ref, hb_ref, mo_ref, yc_ref,
                  br_ref, ng_ref, wb_ref, wo_ref, g_ref, w1_hbm, w3_hbm, w2_hbm, o_ref, act_ref,
                  w1b, w3b, w2b, stg_c, stg_r, sem, *, lead, seq, tm, ck):
    first = jnp.logical_and(pl.program_id(0) == 0, pl.program_id(1) == 0)
    stream = _FfnWeightStream((w1_hbm, w3_hbm, w2_hbm), lead, (w1b, w3b, w2b),
                              stg_c, stg_r, sem, ck)

    @pl.when(first)
    def _():
        stream.prime()

    t0 = pl.program_id(1) * tm
    rows = t0 + lax.broadcasted_iota(jnp.int32, (tm, 1), 0)
    is_ctx = rows >= seq
    d = x_ref.shape[1]
    rsum = (rf_ref[...] + rb_ref[...]).reshape(tm, D_RNN)
    ya = (rsum * gg_ref[...].astype(F32)).astype(BF16)
    hsum = hf_ref[...] + hb_ref[...]
    parts = []
    for hd in range(ML_HEADS):
        hh = hsum[:, hd * ML_DH:(hd + 1) * ML_DH]
        parts.append(hh * lax.rsqrt(jnp.mean(hh * hh, axis=-1, keepdims=True) + EPS))
    yb = (jnp.concatenate(parts, axis=1) * ng_ref[...] * mo_ref[...].astype(F32)).astype(BF16)
    m = (br_ref[:, 0:d].astype(F32) * jnp.dot(ya, wb_ref[0], preferred_element_type=F32)
         + br_ref[:, d:2 * d].astype(F32) * jnp.dot(yb, wb_ref[1], preferred_element_type=F32)
         + br_ref[:, 2 * d:3 * d].astype(F32) * jnp.dot(yc_ref[...], wb_ref[2],
                                                       preferred_element_type=F32))
    y = jnp.dot(m.astype(BF16), wo_ref[...], preferred_element_type=F32)
    x1 = x_ref[...] + _row_mod(ml_ref, mc_ref, is_ctx, 5) * y
    @pl.when(first)
    def _():
        o_ref[...] = _ffn_body(x1, ml_ref, mc_ref, is_ctx, g_ref, w1b, w3b, w2b, act_ref,
                               base=6, ck=ck, stream=stream)

    @pl.when(jnp.logical_not(first))
    def _():
        o_ref[...] = _ffn_body(x1, ml_ref, mc_ref, is_ctx, g_ref, w1b, w3b, w2b, act_ref,
                               base=6, ck=ck)


def _merge_call(x, ml, mc, rf, rb, gg, hf, hb, mo, yc, br, ng, wb, wo, g, w1, w3, w2,
                *, layer, seq, out_rows):
    bsz, t, d = x.shape
    dff = w1.shape[-1]
    tm = _pick_tile(t, (384, 128))
    ck = _pick_tile(dff, (256, 128))
    kern = functools.partial(_merge_kernel, lead=(layer, 1), seq=seq, tm=tm, ck=ck)
    hbm = pl.BlockSpec(memory_space=pl.ANY)

    def tok(wd):
        return pl.BlockSpec((None, tm, wd), lambda b, i: (b, i, 0))

    rspec = pl.BlockSpec((tm // SUBLANES, None, SUBLANES, D_RNN), lambda b, i: (i, b, 0, 0))

    return pl.pallas_call(
        kern,
        grid=(bsz, pl.cdiv(out_rows, tm)),
        in_specs=[
            tok(d),
            pl.BlockSpec((None, N_MOD, d), lambda b, i: (b, 0, 0)),
            _const_spec((N_MOD, d)),
            rspec, rspec, tok(D_RNN), tok(ML_W), tok(ML_W), tok(ML_W), tok(AT_W),
            tok(N_BRANCH * d),
            _const_spec((1, ML_W)),
            _const_spec((N_BRANCH, BRANCH_W, d), (layer,)),
            _const_spec((d, d), (layer,)),
            _const_spec((1, d)),
            hbm, hbm, hbm,
        ],
        out_specs=tok(d),
        out_shape=jax.ShapeDtypeStruct((bsz, out_rows, d), F32),
        scratch_shapes=[pltpu.VMEM((tm, dff), BF16)] + _ffn_weight_scratch(d, dff, ck),
        compiler_params=_cparams(("arbitrary", "arbitrary")),
        name="merge_ffn",
    )(x, ml, mc, rf, rb, gg, hf, hb, mo, yc, br, ng, wb, wo, g, w1, w3, w2)


def _block_diag(w):
    n, bi, bj = w.shape
    eye = jnp.eye(n, dtype=w.dtype)
    return (eye[:, None, :, None] * w[:, :, None, :]).reshape(n * bi, n * bj)


def _rope_tables(seq, t):
    rows = seq // GRID_W
    row = jnp.repeat(jnp.arange(rows), GRID_W).astype(F32)
    col = jnp.broadcast_to(jnp.arange(GRID_W), (rows, GRID_W)).reshape(-1).astype(F32)
    half = AT_DH // 2
    inv = ROPE_BASE ** (-jnp.arange(0, half, 2, dtype=F32) / half)
    ar = row[:, None] * inv
    ac = col[:, None] * inv
    ang = jnp.concatenate([ar, ar, ac, ac], axis=-1)
    cos = jnp.concatenate([jnp.cos(ang), jnp.ones((t - seq, AT_DH), F32)], axis=0)
    sin = jnp.concatenate([jnp.sin(ang), jnp.zeros((t - seq, AT_DH), F32)], axis=0)
    cos = jnp.tile(cos, (1, LANES // AT_DH))
    sin = jnp.tile(sin, (1, LANES // AT_DH))
    first_half = (jnp.arange(LANES) % (2 * ROPE_Q)) < ROPE_Q
    sin_a = jnp.where(first_half, -sin, 0.0)
    sin_b = jnp.where(first_half, 0.0, sin)
    return cos, sin_a, sin_b


def kernel(x, c, ctx, c_ctx, ada_w, ada_b, norm_g, ffn_w1, ffn_w3, ffn_w2, w_in, rg_conv_w,
           rg_conv_b, rg_wa, rg_ba, rg_wi, rg_bi, rg_lam, ml_gate_b, ml_norm_g, at_qn_g,
           at_kn_g, at_sink, w_branch, w_out):
    bsz, seq, d = x.shape
    nctx = ctx.shape[1]
    t = seq + nctx
    depth = ada_w.shape[0]

    xs = jnp.concatenate([x, ctx], axis=1)
    mod_rows = 2 * SUBLANES
    cc = jnp.concatenate([c, c_ctx[None, :], jnp.zeros((mod_rows - bsz - 1, d), F32)], axis=0)
    mods = _ada_call(cc, ada_w, ada_b).reshape(depth, mod_rows, N_MOD, d)
    cos, sin_a, sin_b = _rope_tables(seq, t)
    gmat = _block_diag(jnp.full((AT_HEADS, AT_DH, AT_DH), 1.0 / AT_DH, F32)).astype(BF16)
    w1, w3, w2 = ffn_w1, ffn_w3, ffn_w2
    w_head, w_tail = w_in[..., :P_HEAD].astype(BF16), w_in[..., P_HEAD:].astype(BF16)
    w_br, w_o = w_branch.astype(BF16), w_out.astype(BF16)

    for l in range(depth):
        ml = mods[l, :bsz]
        mc = mods[l, bsz]
        last = l == depth - 1

        xs = _ffn_call(xs, ml, mc, norm_g[l, 0][None, :], w1, w3, w2,
                       lead=(l, 0), base=0, seq=seq, out_rows=t)

        qg = jnp.tile(at_qn_g[l], AT_HEADS)[None, :]
        kg = jnp.tile(at_kn_g[l], AT_KV)[None, :]
        gb = jnp.concatenate([ml_gate_b[l], jnp.zeros((LANES - N_GATE,), F32)])[None, :]
        (rgx, rgg, mq, mk, mv, mo, aq, akd, avd, br, mg) = _proj_call(
            xs, ml, mc, norm_g[l, 1][None, :], w_head, w_tail, gmat, qg, kg, gb,
            cos, sin_a, sin_b, layer=l, seq=seq)

        wcat = jnp.stack([jnp.concatenate([_block_diag(rg_wa[l, dr]), _block_diag(rg_wi[l, dr])],
                                          axis=1) for dr in range(2)])
        wcat = (0.5 * wcat).astype(BF16)
        bias = 0.5 * jnp.concatenate([rg_ba[l], rg_bi[l]], axis=1)[:, None, :]
        clam = (-0.5 * LOG2E * LRU_C * jax.nn.softplus(-rg_lam[l]))[:, None, :]
        rhf, rhb = _rglru_call(rgx, rg_conv_w[l], rg_conv_b[l][None, :], wcat, bias, clam,
                               seq=seq)

        gt = jnp.swapaxes(mg[:, :, :N_GATE], 1, 2)
        rowp, colsrc = _mlgate_call(gt, seq=seq)
        colp = jnp.swapaxes(colsrc, 1, 2)
        mkt = jnp.swapaxes(mk, 1, 2)
        mhf, mhb = _mlstm_call(mq, mkt, mv, rowp, colp, seq=seq)

        yc = _attn_call(at_sink[l], aq, akd, avd, seq=seq)

        xs = _merge_call(xs, ml, mc, rhf, rhb, rgg, mhf, mhb, mo, yc, br, ml_norm_g[l][None, :],
                         w_br, w_o, norm_g[l, 2][None, :], w1, w3, w2,
                         layer=l, seq=seq, out_rows=seq if last else t)
    return xs
```

```python
import functools

import jax
import jax.numpy as jnp
from jax import lax
from jax.experimental import pallas as pl
from jax.experimental.pallas import tpu as pltpu

F32 = jnp.float32
BF16 = jnp.bfloat16

EPS = 1e-6
NEG = -1e30
TINY = 1e-36
LOG2E = 1.4426950408889634
N_MOD = 9
GRID_W = 64
ROPE_BASE = 10000.0

D_RNN = 512
RNN_BLOCKS = 8
RNN_BLOCK = D_RNN // RNN_BLOCKS
CONV_W = 4
CONV_LEFT = 2
CONV_RIGHT = CONV_W - 1 - CONV_LEFT
LRU_C = 8.0

ML_HEADS = 4
ML_DH = 128
ML_W = ML_HEADS * ML_DH
ML_CHUNK = 128
ML_UNITS = 2 * ML_HEADS

AT_HEADS = 8
AT_KV = 2
AT_DH = 64
AT_G = AT_HEADS // AT_KV
AT_W = AT_HEADS * AT_DH
AT_KVW = AT_KV * AT_DH
ATT_BLOCK = 128
ROPE_Q = AT_DH // 4

N_BRANCH = 3
BRANCH_W = 512

LANES = 128
SUBLANES = 8
VMEM_LIMIT = 56 * 1024 * 1024

P_RGX = 0
P_RGG = P_RGX + D_RNN
P_MQ = P_RGG + D_RNN
P_MK = P_MQ + ML_W
P_MV = P_MK + ML_W
P_MO = P_MV + ML_W
P_HEAD = P_MO + ML_W
N_GATE = 4 * ML_HEADS
T_AQ = 0
T_AK = T_AQ + AT_W
T_AV = T_AK + AT_KVW
T_BR = T_AV + AT_KVW
GATE_SEG = 512


def _cparams(sem):
    return pltpu.CompilerParams(dimension_semantics=sem, vmem_limit_bytes=VMEM_LIMIT)


def _const_spec(shape, lead=()):
    nd = len(shape)
    idx = tuple(lead) + (0,) * nd
    return pl.BlockSpec((None,) * len(lead) + tuple(shape), lambda *_: idx,
                        pipeline_mode=pl.Buffered(1))


def _pick_tile(total, candidates):
    for c in candidates:
        if total % c == 0:
            return c
    raise ValueError(f"no tile for {total}")


def _sigmoid(x):
    return jax.nn.sigmoid(x)


def _ln_mod(x, g, shift, scale):
    ms = jnp.mean(x * x, axis=-1, keepdims=True)
    return x * lax.rsqrt(ms + EPS) * g * (1.0 + scale) + shift


def _row_mod(ml_ref, mc_ref, is_ctx, i):
    return jnp.where(is_ctx, mc_ref[i:i + 1, :], ml_ref[i:i + 1, :])


def _ada_kernel(cc_ref, w_ref, b_ref, o_ref):
    cc = cc_ref[...]
    s = cc * _sigmoid(cc)
    o_ref[...] = jnp.dot(s.astype(BF16), w_ref[...].astype(BF16),
                         preferred_element_type=F32) + b_ref[...]


def _ada_call(cc, ada_w, ada_b):
    depth, d, nout = ada_w.shape
    rows = cc.shape[0]
    tn = _pick_tile(nout, (1536, 1024, 512, 256, 128))
    return pl.pallas_call(
        _ada_kernel,
        grid=(depth, nout // tn),
        in_specs=[
            pl.BlockSpec((rows, d), lambda l, j: (0, 0)),
            pl.BlockSpec((None, d, tn), lambda l, j: (l, 0, j)),
            pl.BlockSpec((None, 1, tn), lambda l, j: (l, 0, j)),
        ],
        out_specs=pl.BlockSpec((None, rows, tn), lambda l, j: (l, 0, j)),
        out_shape=jax.ShapeDtypeStruct((depth, rows, nout), F32),
        compiler_params=_cparams(("arbitrary", "arbitrary")),
        name="ada_mod",
    )(cc, ada_w, ada_b.reshape(depth, 1, nout))


def _ffn_body(x, ml_ref, mc_ref, is_ctx, g_ref, w1_ref, w3_ref, w2_ref, act_ref, *, base, ck,
              stream=None):
    h = _ln_mod(x, g_ref[...], _row_mod(ml_ref, mc_ref, is_ctx, base),
                _row_mod(ml_ref, mc_ref, is_ctx, base + 1)).astype(BF16)
    dff = w1_ref.shape[1]
    nck = dff // ck
    if stream is not None:
        stream.land(0)
        stream.refill(0)
    for j in range(nck):
        if stream is not None and j + 1 < nck:
            stream.land(j + 1)
        a = jnp.dot(h, w1_ref[:, j * ck:(j + 1) * ck], preferred_element_type=F32)
        b = jnp.dot(h, w3_ref[:, j * ck:(j + 1) * ck], preferred_element_type=F32)
        act_ref[:, j * ck:(j + 1) * ck] = (a * _sigmoid(a) * b).astype(BF16)
        if stream is not None and j + 1 < nck:
            stream.refill(j + 1)
    y = jnp.dot(act_ref[...], w2_ref[...], preferred_element_type=F32)
    return x + 0.5 * _row_mod(ml_ref, mc_ref, is_ctx, base + 2) * y


def _ffn_weight_scratch(d, dff, ck):
    return [pltpu.VMEM((d, dff), BF16), pltpu.VMEM((d, dff), BF16), pltpu.VMEM((dff, d), BF16),
            pltpu.VMEM((2, 2, d, ck), F32), pltpu.VMEM((2, ck, d), F32),
            pltpu.SemaphoreType.DMA((2, 3))]


class _FfnWeightStream:
    def __init__(self, hbm, lead, resident, stg_c, stg_r, sem, ck):
        self.hbm, self.lead, self.resident = hbm, lead, resident
        self.stg_c, self.stg_r, self.sem, self.ck = stg_c, stg_r, sem, ck
        self.nck = resident[0].shape[1] // ck

    def _stages(self, j):
        s = j % 2
        return (self.stg_c.at[s, 0], self.stg_c.at[s, 1], self.stg_r.at[s])

    def _copies(self, j):
        l0, l1 = self.lead
        sl = slice(j * self.ck, (j + 1) * self.ck)
        srcs = (self.hbm[0].at[l0, l1, :, sl], self.hbm[1].at[l0, l1, :, sl],
                self.hbm[2].at[l0, l1, sl, :])
        return [pltpu.make_async_copy(src, stg, self.sem.at[j % 2, i])
                for i, (src, stg) in enumerate(zip(srcs, self._stages(j)))]

    def start(self, j):
        for cp in self._copies(j):
            cp.start()

    def prime(self):
        self.start(0)
        if self.nck > 1:
            self.start(1)

    def land(self, j):
        sl = slice(j * self.ck, (j + 1) * self.ck)
        for cp in self._copies(j):
            cp.wait()
        w1b, w3b, w2b = self.resident
        s1, s3, s2 = self._stages(j)
        w1b[:, sl] = s1[...].astype(BF16)
        w3b[:, sl] = s3[...].astype(BF16)
        w2b[sl, :] = s2[...].astype(BF16)

    def refill(self, j):
        if j + 2 < self.nck:
            self.start(j + 2)


def _ffn_kernel(x_ref, ml_ref, mc_ref, g_ref, w1_hbm, w3_hbm, w2_hbm, o_ref, act_ref,
                w1b, w3b, w2b, stg_c, stg_r, sem, *, lead, base, seq, tm, ck):
    t0 = pl.program_id(1) * tm
    rows = t0 + lax.broadcasted_iota(jnp.int32, (tm, 1), 0)
    is_ctx = rows >= seq
    first = jnp.logical_and(pl.program_id(0) == 0, pl.program_id(1) == 0)

    @pl.when(first)
    def _():
        stream = _FfnWeightStream((w1_hbm, w3_hbm, w2_hbm), lead, (w1b, w3b, w2b),
                                  stg_c, stg_r, sem, ck)
        stream.prime()
        o_ref[...] = _ffn_body(x_ref[...], ml_ref, mc_ref, is_ctx, g_ref, w1b, w3b, w2b,
                               act_ref, base=base, ck=ck, stream=stream)

    @pl.when(jnp.logical_not(first))
    def _():
        o_ref[...] = _ffn_body(x_ref[...], ml_ref, mc_ref, is_ctx, g_ref, w1b, w3b, w2b,
                               act_ref, base=base, ck=ck)


def _ffn_call(x, ml, mc, g, w1, w3, w2, *, lead, base, seq, out_rows):
    bsz, t, d = x.shape
    dff = w1.shape[-1]
    tm = _pick_tile(t, (768, 384, 128))
    ck = _pick_tile(dff, (256, 128))
    kern = functools.partial(_ffn_kernel, lead=lead, base=base, seq=seq, tm=tm, ck=ck)
    hbm = pl.BlockSpec(memory_space=pl.ANY)
    return pl.pallas_call(
        kern,
        grid=(bsz, pl.cdiv(out_rows, tm)),
        in_specs=[
            pl.BlockSpec((None, tm, d), lambda b, i: (b, i, 0)),
            pl.BlockSpec((None, N_MOD, d), lambda b, i: (b, 0, 0)),
            _const_spec((N_MOD, d)),
            _const_spec((1, d)),
            hbm, hbm, hbm,
        ],
        out_specs=pl.BlockSpec((None, tm, d), lambda b, i: (b, i, 0)),
        out_shape=jax.ShapeDtypeStruct((bsz, out_rows, d), F32),
        scratch_shapes=[pltpu.VMEM((tm, dff), BF16)] + _ffn_weight_scratch(d, dff, ck),
        compiler_params=_cparams(("arbitrary", "arbitrary")),
        name="ffn",
    )(x, ml, mc, g, w1, w3, w2)


def _gelu_tanh(x):
    return 0.5 * x * (1.0 + jnp.tanh(0.7978845608028654 * (x + 0.044715 * (x * x * x))))


def _head_norm(x, gmat, g):
    ms = jnp.dot((x * x).astype(BF16), gmat, preferred_element_type=F32)
    return x * lax.rsqrt(ms + EPS) * g


def _dup_heads(x):
    low = lax.broadcasted_iota(jnp.int32, x.shape, 1) < AT_DH
    sw = pltpu.roll(x, AT_DH, 1)
    return jnp.concatenate([jnp.where(low, x, sw), jnp.where(low, sw, x)], axis=1)


def _rope(x, cos, sin_a, sin_b):
    parts = []
    for j in range(x.shape[1] // LANES):
        xj = x[:, j * LANES:(j + 1) * LANES]
        parts.append(xj * cos + pltpu.roll(xj, LANES - ROPE_Q, 1) * sin_a
                     + pltpu.roll(xj, ROPE_Q, 1) * sin_b)
    return jnp.concatenate(parts, axis=1)


def _gate_views(gt_ref):
    return (gt_ref.at[:, 0:D_RNN], gt_ref.at[:, D_RNN:D_RNN + ML_W], gt_ref.at[:, D_RNN + ML_W:])


def _proj_kernel(x_ref, ml_ref, mc_ref, g_ref, wh_ref, wt_ref, gmat_ref, qg_ref, kg_ref, gb_ref,
                 cos_ref, sa_ref, sb_ref,
                 rgx_o, mq_o, mk_o, mv_o, aq_o, ak_o, av_o, gt_o, mg_o,
                 tail, *, seq, tm, d_model):
    rgg_o, mo_o, br_o = _gate_views(gt_o)
    @pl.when(jnp.logical_and(pl.program_id(0) == 0, pl.program_id(1) == 0))
    def _():
        for r in range(0, d_model, LANES):
            tail[r:r + LANES, :] = wt_ref[r:r + LANES, N_GATE:]

    t0 = pl.program_id(1) * tm
    rows = t0 + lax.broadcasted_iota(jnp.int32, (tm, 1), 0)
    is_ctx = rows >= seq
    h = _ln_mod(x_ref[...], g_ref[...], _row_mod(ml_ref, mc_ref, is_ctx, 3),
                _row_mod(ml_ref, mc_ref, is_ctx, 4)).astype(BF16)

    def seg(c0, w, ref=wh_ref):
        return jnp.dot(h, ref[:, c0:c0 + w], preferred_element_type=F32)

    rgx_o[...] = seg(P_RGX, D_RNN).reshape(rgx_o.shape)
    rgg_o[...] = _gelu_tanh(seg(P_RGG, D_RNN)).astype(BF16)
    mq_o[...] = seg(P_MQ, ML_W).astype(BF16)
    mk_o[...] = (seg(P_MK, ML_W) * (ML_DH ** -0.5)).astype(BF16)
    mv_o[...] = seg(P_MV, ML_W).astype(BF16)
    mo_o[...] = _sigmoid(seg(P_MO, ML_W)).astype(BF16)
    cos, sin_a, sin_b = cos_ref[...], sa_ref[...], sb_ref[...]
    gmat = gmat_ref[...]
    q = _head_norm(seg(T_AQ, AT_W, tail), gmat, qg_ref[...])
    aq_o[...] = (_rope(q, cos, sin_a, sin_b) * (AT_DH ** -0.5 * LOG2E)).astype(BF16)
    k = _head_norm(seg(T_AK, AT_KVW, tail), gmat[:AT_KVW, :AT_KVW], kg_ref[...])
    ak_o[...] = _dup_heads(_rope(k, cos, sin_a, sin_b)).astype(BF16)
    av_o[...] = _dup_heads(seg(T_AV, AT_KVW, tail)).astype(BF16)
    for c0 in range(0, N_BRANCH * d_model, GATE_SEG):
        br_o[:, c0:c0 + GATE_SEG] = _sigmoid(seg(T_BR + c0, GATE_SEG, tail)).astype(BF16)
    mg_o[...] = seg(0, LANES, wt_ref) + gb_ref[...]


def _proj_call(x, ml, mc, g, wh, wt, gmat, qg, kg, gb, cos, sin_a, sin_b, *, layer, seq):
    bsz, t, d = x.shape
    ntail = wt.shape[-1] - N_GATE
    tm = _pick_tile(t, (384, 128))
    kern = functools.partial(_proj_kernel, seq=seq, tm=tm, d_model=d)
    widths = [(D_RNN, F32), (ML_W, BF16), (ML_W, BF16), (ML_W, BF16),
              (AT_W, BF16), (2 * AT_KVW, BF16), (2 * AT_KVW, BF16),
              (D_RNN + ML_W + N_BRANCH * d, BF16), (LANES, F32)]
    tab_spec = pl.BlockSpec((tm, LANES), lambda b, i: (i, 0))
    out_specs = [pl.BlockSpec((None, tm, wd), lambda b, i: (b, i, 0)) for wd, _ in widths]
    out_shape = [jax.ShapeDtypeStruct((bsz, t, wd), dt) for wd, dt in widths]
    out_specs[0] = pl.BlockSpec((tm // SUBLANES, None, SUBLANES, D_RNN), lambda b, i: (i, b, 0, 0))
    out_shape[0] = jax.ShapeDtypeStruct((t // SUBLANES, bsz, SUBLANES, D_RNN), F32)
    return pl.pallas_call(
        kern,
        grid=(bsz, t // tm),
        in_specs=[
            pl.BlockSpec((None, tm, d), lambda b, i: (b, i, 0)),
            pl.BlockSpec((None, N_MOD, d), lambda b, i: (b, 0, 0)),
            _const_spec((N_MOD, d)),
            _const_spec((1, d)),
            _const_spec((d, wh.shape[-1]), (layer,)),
            _const_spec((d, wt.shape[-1]), (layer,)),
            _const_spec((AT_W, AT_W)),
            _const_spec((1, AT_W)),
            _const_spec((1, AT_KVW)),
            _const_spec((1, LANES)),
            tab_spec, tab_spec, tab_spec,
        ],
        out_specs=out_specs,
        out_shape=out_shape,
        scratch_shapes=[pltpu.VMEM((d, ntail), BF16)],
        compiler_params=_cparams(("arbitrary", "arbitrary")),
        name="in_proj",
    )(x, ml, mc, g, wh, wt, gmat, qg, kg, gb, cos, sin_a, sin_b)


def _scan_chunk(s, rev, nl, nc):
    if rev:
        return jnp.where(s < nc, nl + nc - 1 - s, nl - 1 - (s - nc))
    return jnp.where(s < nc, nl + s, s - nc)


def _rglru_kernel(xf_ref, xfp_ref, xfn_ref, xb_ref, xbp_ref, xbn_ref, cw_ref, cb_ref, w_ref,
                  bias_ref, clam_ref, hf_ref, hb_ref, ext, a_s, b_s, h_s, hstate,
                  *, nl, nc, tc, sub):
    s = pl.program_id(0)
    bsz, c = hstate.shape[1], hstate.shape[2]
    ntb = tc // SUBLANES

    @pl.when(s == 0)
    def _():
        hstate[...] = jnp.zeros_like(hstate)

    cw = cw_ref[...]
    cb = cb_ref[...]
    for d, (x_ref, xp_ref, xn_ref) in enumerate(((xf_ref, xfp_ref, xfn_ref),
                                                (xb_ref, xbp_ref, xbn_ref))):
        chunk = _scan_chunk(s, d == 1, nl, nc)
        first = jnp.logical_or(chunk == 0, chunk == nl)
        last = jnp.logical_or(chunk == nl - 1, chunk == nl + nc - 1)
        prev = jnp.swapaxes(xp_ref[0], 0, 1)[SUBLANES - CONV_LEFT:]
        nxt = jnp.swapaxes(xn_ref[0], 0, 1)[:CONV_RIGHT]
        ext[d, 0:CONV_LEFT] = jnp.where(first, 0.0, prev)
        for i in range(ntb):
            o = CONV_LEFT + i * SUBLANES
            ext[d, o:o + SUBLANES] = jnp.swapaxes(x_ref[i], 0, 1)
        ext[d, CONV_LEFT + tc:CONV_LEFT + tc + CONV_RIGHT] = jnp.where(last, 0.0, nxt)

        def gates(i, carry, d=d):
            t0 = pl.multiple_of(i * sub, sub)
            u = cb
            for k in range(CONV_W):
                u = u + ext[d, pl.ds(t0 + k, sub)] * cw[k:k + 1, :]
            u2 = u.reshape(sub * bsz, c)
            z = jnp.dot(u2.astype(BF16), w_ref[d], preferred_element_type=F32) + bias_ref[d]
            a = jnp.exp2(clam_ref[d] * jnp.tanh(z[:, :c]) + clam_ref[d])
            gi = 0.5 * jnp.tanh(z[:, c:]) + 0.5
            s1 = 1.0 - a * a
            bb = (s1 * lax.rsqrt(jnp.maximum(s1, TINY))) * (gi * u2)
            a_s[d, pl.ds(t0, sub)] = a.reshape(sub, bsz, c)
            b_s[d, pl.ds(t0, sub)] = bb.reshape(sub, bsz, c)
            return carry

        lax.fori_loop(0, tc // sub, gates, 0)

    def step(k, hs):
        hf, hb = hs
        kb = tc - 1 - k
        hf = a_s[0, k] * hf + b_s[0, k]
        hb = a_s[1, kb] * hb + b_s[1, kb]
        h_s[0, k] = hf
        h_s[1, kb] = hb
        return hf, hb

    hf, hb = lax.fori_loop(0, tc, step, (hstate[0], hstate[1]), unroll=8)
    hstate[0] = hf
    hstate[1] = hb
    for d, o_ref in enumerate((hf_ref, hb_ref)):
        for i in range(ntb):
            o_ref[i] = jnp.swapaxes(h_s[d, i * SUBLANES:(i + 1) * SUBLANES], 0, 1)


def _rglru_call(rgx, cw, cb, w, bias, clam, *, seq):
    ntb_all, bsz, _, c = rgx.shape
    t = ntb_all * SUBLANES
    tc = 128
    sub = tc
    assert seq % tc == 0 and (t - seq) % tc == 0
    nl, nc = seq // tc, (t - seq) // tc
    ntb = tc // SUBLANES

    def specs(rev):
        chunk = functools.partial(_scan_chunk, rev=rev, nl=nl, nc=nc)
        return [
            pl.BlockSpec((ntb, bsz, SUBLANES, c), lambda s: (chunk(s), 0, 0, 0)),
            pl.BlockSpec((1, bsz, SUBLANES, c),
                         lambda s: (jnp.maximum(chunk(s) * ntb - 1, 0), 0, 0, 0)),
            pl.BlockSpec((1, bsz, SUBLANES, c),
                         lambda s: (jnp.minimum((chunk(s) + 1) * ntb, ntb_all - 1), 0, 0, 0)),
        ]

    kern = functools.partial(_rglru_kernel, nl=nl, nc=nc, tc=tc, sub=sub)
    fspec, bspec = specs(False), specs(True)
    return pl.pallas_call(
        kern,
        grid=(nl + nc,),
        in_specs=fspec + bspec + [
            _const_spec((CONV_W, c)),
            _const_spec((1, c)),
            _const_spec((2, c, 2 * c)),
            _const_spec((2, 1, 2 * c)),
            _const_spec((2, 1, c)),
        ],
        out_specs=[fspec[0], bspec[0]],
        out_shape=[jax.ShapeDtypeStruct(rgx.shape, F32)] * 2,
        scratch_shapes=[
            pltpu.VMEM((2, tc + CONV_W - 1, bsz, c), F32),
            pltpu.VMEM((2, tc, bsz, c), F32),
            pltpu.VMEM((2, tc, bsz, c), F32),
            pltpu.VMEM((2, tc, bsz, c), F32),
            pltpu.VMEM((2, bsz, c), F32),
        ],
        compiler_params=_cparams(("arbitrary",)),
        name="rglru",
    )(rgx, rgx, rgx, rgx, rgx, rgx, cw, cb, w, bias, clam)


def _log_sigmoid(x):
    return -(jnp.maximum(-x, 0.0) + jnp.log1p(jnp.exp(-jnp.abs(x))))


def _lane_scan(x, op, ident, rev):
    n = x.shape[1]
    lane = lax.broadcasted_iota(jnp.int32, x.shape, 1)
    sh = 1
    while sh < n:
        if rev:
            x = op(x, jnp.where(lane < n - sh, pltpu.roll(x, n - sh, 1), ident))
        else:
            x = op(x, jnp.where(lane >= sh, pltpu.roll(x, sh, 1), ident))
        sh *= 2
    return x


def _mlgate_kernel(g_ref, row_o, col_o, cb_s, gg_s, ml_s, tot_s, wm_s, mo_s, mn_s, *, nl, nc):
    L = ML_CHUNK
    H = ML_HEADS
    U = ML_UNITS
    is_bwd = lax.broadcasted_iota(jnp.int32, (U, L), 0) >= H
    for c in range(nl + nc):
        lanes = slice(c * L, (c + 1) * L)
        g16 = g_ref[:, lanes]
        li = jnp.concatenate([g16[0:H], g16[2 * H:3 * H]], axis=0)
        lf = _log_sigmoid(jnp.concatenate([g16[H:2 * H], g16[3 * H:4 * H]], axis=0))
        pre = _lane_scan(lf, jnp.add, 0.0, False)
        suf = _lane_scan(lf, jnp.add, 0.0, True)
        tot = pre + suf - lf
        cb = jnp.where(is_bwd, suf, pre)
        gg = li - cb
        pmax = _lane_scan(gg, jnp.maximum, NEG, False)
        smax = _lane_scan(gg, jnp.maximum, NEG, True)
        cb_s[:, lanes] = cb
        gg_s[:, lanes] = gg
        ml_s[:, lanes] = cb + jnp.where(is_bwd, smax, pmax)
        tot_s[:, lanes] = tot
        wm_s[:, lanes] = tot + jnp.maximum(pmax, smax)

    m = jnp.zeros((U, L), F32)
    for s in range(nl + nc):
        cf = nl + s if s < nc else s - nc
        cr = nl + nc - 1 - s if s < nc else nl - 1 - (s - nc)
        lf_, lr_ = slice(cf * L, (cf + 1) * L), slice(cr * L, (cr + 1) * L)
        tot = jnp.where(is_bwd, tot_s[:, lr_], tot_s[:, lf_])
        wm = jnp.where(is_bwd, wm_s[:, lr_], wm_s[:, lf_])
        m_new = jnp.maximum(tot + m, wm)
        mo_s[0:H, lf_] = m[0:H]
        mo_s[H:U, lr_] = m[H:U]
        mn_s[0:H, lf_] = m_new[0:H]
        mn_s[H:U, lr_] = m_new[H:U]
        m = m_new

    cb = cb_s[...]
    gg = gg_s[...]
    tot = tot_s[...]
    mo = mo_s[...]
    mn = mn_s[...]
    m_t = jnp.maximum(cb + mo, ml_s[...])
    row_o[0:U, :] = gg
    row_o[U:2 * U, :] = jnp.exp(tot + gg - mn)
    row_o[2 * U:3 * U, :] = jnp.exp(tot + mo - mn)
    row_o[3 * U:4 * U, :] = mo
    col_o[0:U, :] = cb - m_t
    col_o[U:2 * U, :] = jnp.exp(-m_t)


def _mlgate_call(gt, *, seq):
    bsz, ng, t = gt.shape
    nl, nc = seq // ML_CHUNK, (t - seq) // ML_CHUNK
    kern = functools.partial(_mlgate_kernel, nl=nl, nc=nc)
    rout = pl.BlockSpec((None, 4 * ML_UNITS, t), lambda b: (b, 0, 0))
    cout = pl.BlockSpec((None, 2 * ML_UNITS, t), lambda b: (b, 0, 0))
    return pl.pallas_call(
        kern,
        grid=(bsz,),
        in_specs=[pl.BlockSpec((None, ng, t), lambda b: (b, 0, 0))],
        out_specs=[rout, cout],
        out_shape=[jax.ShapeDtypeStruct((bsz, 4 * ML_UNITS, t), F32),
                   jax.ShapeDtypeStruct((bsz, 2 * ML_UNITS, t), F32)],
        scratch_shapes=[pltpu.VMEM((ML_UNITS, t), F32)] * 7,
        compiler_params=_cparams(("arbitrary",)),
        name="mlstm_gates",
    )(gt)


def _mlstm_kernel(qf_ref, ktf_ref, vf_ref, rf_ref, cf_ref, qb_ref, ktb_ref, vb_ref, rb_ref, cb_ref,
                  hf_ref, hb_ref, c_st):
    s = pl.program_id(1)

    @pl.when(s == 0)
    def _():
        c_st[...] = jnp.zeros_like(c_st)

    L, dh, U = ML_CHUNK, ML_DH, ML_UNITS
    ti = lax.broadcasted_iota(jnp.int32, (L, L), 0)
    si = lax.broadcasted_iota(jnp.int32, (L, L), 1)
    ones = jnp.ones((L, dh), BF16)
    for bb in range(qf_ref.shape[0]):
        for d, (q_ref, kt_ref, v_ref, r_ref, c_ref, o_ref) in enumerate(
                ((qf_ref, ktf_ref, vf_ref, rf_ref, cf_ref, hf_ref),
                 (qb_ref, ktb_ref, vb_ref, rb_ref, cb_ref, hb_ref))):
            tri = (si >= ti) if d == 1 else (si <= ti)
            rows = r_ref[bb]
            cols = c_ref[bb]
            for hd in range(ML_HEADS):
                u = d * ML_HEADS + hd
                sl = slice(hd * dh, (hd + 1) * dh)
                q = q_ref[bb, :, sl]
                kt = kt_ref[bb, sl, :]
                v_ext = jnp.concatenate([v_ref[bb, :, sl], ones], axis=1)
                g_row = rows[u:u + 1, :]
                w_row = rows[U + u:U + u + 1, :]
                a_row = rows[2 * U + u:2 * U + u + 1, :]
                m_row = rows[3 * U + u:3 * U + u + 1, :]
                xb = jnp.broadcast_to(cols[:, u:u + 1], (L, L))
                emt = cols[:, U + u:U + u + 1]
                c_old = c_st[bb * U + u]
                qkc = jnp.dot(q, jnp.concatenate([kt, c_old.astype(BF16)], axis=1),
                              preferred_element_type=F32)
                dm = jnp.exp(jnp.where(tri, xb + g_row, NEG))
                dec = jnp.exp(xb + m_row)
                sc = (qkc[:, :L] * dm).astype(BF16)
                ktw = (kt.astype(F32) * w_row).astype(BF16)
                sv = jnp.dot(jnp.concatenate([sc, ktw], axis=0), v_ext,
                             preferred_element_type=F32)
                ab = sv[:L] + jnp.concatenate([dec, dec], axis=1) * qkc[:, L:]
                o_ref[bb, :, sl] = ab[:, :dh] / jnp.maximum(jnp.abs(ab[:, dh:]), emt)
                c_st[bb * U + u] = jnp.concatenate([a_row, a_row], axis=1) * c_old + sv[L:]


def _mlstm_call(mq, mkt, mv, rowp, colp, *, seq):
    bsz, t, w = mq.shape
    nl, nc = seq // ML_CHUNK, (t - seq) // ML_CHUNK
    cf = functools.partial(_scan_chunk, rev=False, nl=nl, nc=nc)
    cr = functools.partial(_scan_chunk, rev=True, nl=nl, nc=nc)
    nr = rowp.shape[1]
    bpb = 8 if bsz % 8 == 0 else 1

    def tok(ch, wd):
        return pl.BlockSpec((bpb, ML_CHUNK, wd), lambda b, s: (b, ch(s), 0))

    def tr(ch, rows):
        return pl.BlockSpec((bpb, rows, ML_CHUNK), lambda b, s: (b, 0, ch(s)))

    def side(ch):
        return [tok(ch, w), tr(ch, w), tok(ch, w), tr(ch, nr), tok(ch, colp.shape[2])]

    return pl.pallas_call(
        _mlstm_kernel,
        grid=(bsz // bpb, nl + nc),
        in_specs=side(cf) + side(cr),
        out_specs=[tok(cf, w), tok(cr, w)],
        out_shape=[jax.ShapeDtypeStruct((bsz, t, w), F32)] * 2,
        scratch_shapes=[pltpu.VMEM((bpb * ML_UNITS, ML_DH, 2 * ML_DH), F32)],
        compiler_params=_cparams(("arbitrary", "arbitrary")),
        name="mlstm",
    )(mq, mkt, mv, rowp, colp, mq, mkt, mv, rowp, colp)


def _attn_kernel(sink_ref, q_ref, k_ref, v_ref, o_ref, vext, *, seq, qbs):
    kv = pl.program_id(1)
    i = pl.program_id(2)
    blk = ATT_BLOCK
    t = k_ref.shape[0]
    ctx = t - seq
    nlb = seq // blk
    band = 3 * blk

    @pl.when(i == 0)
    def _():
        vext[:, 0:LANES] = v_ref[...]
        vext[:, LANES:2 * LANES] = jnp.ones((t, LANES), BF16)

    lane = lax.broadcasted_iota(jnp.int32, (blk, LANES), 1)
    low = lane < AT_DH
    zero = jnp.zeros((blk, LANES), BF16)
    rows = AT_G * blk
    gi = lax.broadcasted_iota(jnp.int32, (rows, 1), 0) // blk
    snk = jnp.zeros((rows, 1), F32)
    for g in range(AT_G):
        snk = jnp.where(gi == g, sink_ref[kv * AT_G + g] * LOG2E, snk)
    ti = lax.broadcasted_iota(jnp.int32, (blk, band), 0)
    ci = lax.broadcasted_iota(jnp.int32, (blk, band), 1)
    nt = (((1,), (1,)), ((), ()))
    k_ctx = k_ref[seq:t, :]
    v_ctx = vext[seq:t, :]

    for j in range(qbs):
        qb = i * qbs + j
        s0 = pl.multiple_of(jnp.clip((qb - 1) * blk, 0, t - band), blk)
        rel = ci - ti + (s0 - qb * blk + blk)
        lim = jnp.where(qb < nlb, seq - s0, 0)
        ok = jnp.logical_and(jnp.logical_and(rel >= 0, rel <= 2 * blk), ci < lim)
        bias = jnp.where(ok, 0.0, NEG)
        bias = jnp.concatenate([bias] * AT_G, axis=0)

        qs = []
        for h2 in range(AT_G // 2):
            qj = q_ref[j * blk:(j + 1) * blk, h2 * LANES:(h2 + 1) * LANES]
            qs += [jnp.where(low, qj, zero), jnp.where(low, zero, qj)]
        q_all = jnp.concatenate(qs, axis=0)
        s_band = lax.dot_general(q_all, k_ref[pl.ds(s0, band), :], nt,
                                 preferred_element_type=F32) + bias
        s_ctx = lax.dot_general(q_all, k_ctx, nt, preferred_element_type=F32)
        m = jnp.maximum(jnp.maximum(jnp.max(s_band, axis=1, keepdims=True),
                                    jnp.max(s_ctx, axis=1, keepdims=True)), snk)
        p_band = jnp.exp2(s_band - m).astype(BF16)
        p_ctx = jnp.exp2(s_ctx - m).astype(BF16)
        o2 = (jnp.dot(p_band, vext[pl.ds(s0, band), :], preferred_element_type=F32)
              + jnp.dot(p_ctx, v_ctx, preferred_element_type=F32))
        o2 = o2[:, :LANES] / (o2[:, LANES:] + jnp.exp2(snk - m))
        for h2 in range(AT_G // 2):
            a = o2[(2 * h2) * blk:(2 * h2 + 1) * blk, :]
            b = o2[(2 * h2 + 1) * blk:(2 * h2 + 2) * blk, :]
            o_ref[j * blk:(j + 1) * blk, h2 * LANES:(h2 + 1) * LANES] = (
                jnp.where(low, a, b).astype(o_ref.dtype))


def _attn_call(sink, aq, akd, avd, *, seq):
    bsz, t, _ = aq.shape
    blk = ATT_BLOCK
    qbs = 3
    assert t % (qbs * blk) == 0 and t >= 3 * blk
    gw = AT_G * AT_DH
    kern = functools.partial(_attn_kernel, seq=seq, qbs=qbs)
    whole = pl.BlockSpec((None, t, LANES), lambda b, kv, i: (b, 0, kv))
    return pl.pallas_call(
        kern,
        grid=(bsz, AT_KV, t // (qbs * blk)),
        in_specs=[
            pl.BlockSpec(memory_space=pltpu.SMEM),
            pl.BlockSpec((None, qbs * blk, gw), lambda b, kv, i: (b, i, kv)),
            whole, whole,
        ],
        out_specs=pl.BlockSpec((None, qbs * blk, gw), lambda b, kv, i: (b, i, kv)),
        out_shape=jax.ShapeDtypeStruct((bsz, t, AT_W), BF16),
        scratch_shapes=[pltpu.VMEM((t, 2 * LANES), BF16)],
        compiler_params=_cparams(("arbitrary", "arbitrary", "arbitrary")),
        name="attn",
    )(sink, aq, akd, avd)


def _merge_kernel(x_ref, ml_ref, mc_ref, rf_ref, rb_ref, hf_ref, hb_ref, yc_ref, gt_ref,
                  ng_ref, wb_ref, wo_ref, g_ref, w1_hbm, w3_hbm, w2_hbm, o_ref, act_ref,
                  w1b, w3b, w2b, stg_c, stg_r, sem, *, lead, seq, tm, ck):
    gg_ref, mo_ref, br_ref = _gate_views(gt_ref)
    first = jnp.logical_and(pl.program_id(0) == 0, pl.program_id(1) == 0)
    stream = _FfnWeightStream((w1_hbm, w3_hbm, w2_hbm), lead, (w1b, w3b, w2b),
                              stg_c, stg_r, sem, ck)

    @pl.when(first)
    def _():
        stream.prime()

    t0 = pl.program_id(1) * tm
    rows = t0 + lax.broadcasted_iota(jnp.int32, (tm, 1), 0)
    is_ctx = rows >= seq
    d = x_ref.shape[1]
    rsum = (rf_ref[...] + rb_ref[...]).reshape(tm, D_RNN)
    ya = (rsum * gg_ref[...].astype(F32)).astype(BF16)
    hsum = hf_ref[...] + hb_ref[...]
    parts = []
    for hd in range(ML_HEADS):
        hh = hsum[:, hd * ML_DH:(hd + 1) * ML_DH]
        parts.append(hh * lax.rsqrt(jnp.mean(hh * hh, axis=-1, keepdims=True) + EPS))
    yb = (jnp.concatenate(parts, axis=1) * ng_ref[...] * mo_ref[...].astype(F32)).astype(BF16)
    m = (br_ref[:, 0:d].astype(F32) * jnp.dot(ya, wb_ref[0], preferred_element_type=F32)
         + br_ref[:, d:2 * d].astype(F32) * jnp.dot(yb, wb_ref[1], preferred_element_type=F32)
         + br_ref[:, 2 * d:3 * d].astype(F32) * jnp.dot(yc_ref[...], wb_ref[2],
                                                       preferred_element_type=F32))
    y = jnp.dot(m.astype(BF16), wo_ref[...], preferred_element_type=F32)
    x1 = x_ref[...] + _row_mod(ml_ref, mc_ref, is_ctx, 5) * y
    @pl.when(first)
    def _():
        o_ref[...] = _ffn_body(x1, ml_ref, mc_ref, is_ctx, g_ref, w1b, w3b, w2b, act_ref,
                               base=6, ck=ck, stream=stream)

    @pl.when(jnp.logical_not(first))
    def _():
        o_ref[...] = _ffn_body(x1, ml_ref, mc_ref, is_ctx, g_ref, w1b, w3b, w2b, act_ref,
                               base=6, ck=ck)


def _merge_call(x, ml, mc, rf, rb, hf, hb, yc, gates, ng, wb, wo, g, w1, w3, w2,
                *, layer, seq, out_rows):
    bsz, t, d = x.shape
    dff = w1.shape[-1]
    tm = _pick_tile(t, (384, 128))
    ck = _pick_tile(dff, (256, 128))
    kern = functools.partial(_merge_kernel, lead=(layer, 1), seq=seq, tm=tm, ck=ck)
    hbm = pl.BlockSpec(memory_space=pl.ANY)

    def tok(wd):
        return pl.BlockSpec((None, tm, wd), lambda b, i: (b, i, 0))

    rspec = pl.BlockSpec((tm // SUBLANES, None, SUBLANES, D_RNN), lambda b, i: (i, b, 0, 0))

    return pl.pallas_call(
        kern,
        grid=(bsz, pl.cdiv(out_rows, tm)),
        in_specs=[
            tok(d),
            pl.BlockSpec((None, N_MOD, d), lambda b, i: (b, 0, 0)),
            _const_spec((N_MOD, d)),
            rspec, rspec, tok(ML_W), tok(ML_W), tok(AT_W), tok(gates.shape[-1]),
            _const_spec((1, ML_W)),
            _const_spec((N_BRANCH, BRANCH_W, d), (layer,)),
            _const_spec((d, d), (layer,)),
            _const_spec((1, d)),
            hbm, hbm, hbm,
        ],
        out_specs=tok(d),
        out_shape=jax.ShapeDtypeStruct((bsz, out_rows, d), F32),
        scratch_shapes=[pltpu.VMEM((tm, dff), BF16)] + _ffn_weight_scratch(d, dff, ck),
        compiler_params=_cparams(("arbitrary", "arbitrary")),
        name="merge_ffn",
    )(x, ml, mc, rf, rb, hf, hb, yc, gates, ng, wb, wo, g, w1, w3, w2)


def _block_diag(w):
    n, bi, bj = w.shape
    eye = jnp.eye(n, dtype=w.dtype)
    return (eye[:, None, :, None] * w[:, :, None, :]).reshape(n * bi, n * bj)


def _rope_tables(seq, t):
    rows = seq // GRID_W
    row = jnp.repeat(jnp.arange(rows), GRID_W).astype(F32)
    col = jnp.broadcast_to(jnp.arange(GRID_W), (rows, GRID_W)).reshape(-1).astype(F32)
    half = AT_DH // 2
    inv = ROPE_BASE ** (-jnp.arange(0, half, 2, dtype=F32) / half)
    ar = row[:, None] * inv
    ac = col[:, None] * inv
    ang = jnp.concatenate([ar, ar, ac, ac], axis=-1)
    cos = jnp.concatenate([jnp.cos(ang), jnp.ones((t - seq, AT_DH), F32)], axis=0)
    sin = jnp.concatenate([jnp.sin(ang), jnp.zeros((t - seq, AT_DH), F32)], axis=0)
    cos = jnp.tile(cos, (1, LANES // AT_DH))
    sin = jnp.tile(sin, (1, LANES // AT_DH))
    first_half = (jnp.arange(LANES) % (2 * ROPE_Q)) < ROPE_Q
    sin_a = jnp.where(first_half, -sin, 0.0)
    sin_b = jnp.where(first_half, 0.0, sin)
    return cos, sin_a, sin_b


def kernel(x, c, ctx, c_ctx, ada_w, ada_b, norm_g, ffn_w1, ffn_w3, ffn_w2, w_in, rg_conv_w,
           rg_conv_b, rg_wa, rg_ba, rg_wi, rg_bi, rg_lam, ml_gate_b, ml_norm_g, at_qn_g,
           at_kn_g, at_sink, w_branch, w_out):
    bsz, seq, d = x.shape
    nctx = ctx.shape[1]
    t = seq + nctx
    depth = ada_w.shape[0]

    xs = jnp.concatenate([x, ctx], axis=1)
    mod_rows = 2 * SUBLANES
    cc = jnp.concatenate([c, c_ctx[None, :], jnp.zeros((mod_rows - bsz - 1, d), F32)], axis=0)
    mods = _ada_call(cc, ada_w, ada_b).reshape(depth, mod_rows, N_MOD, d)
    cos, sin_a, sin_b = _rope_tables(seq, t)
    gmat = _block_diag(jnp.full((AT_HEADS, AT_DH, AT_DH), 1.0 / AT_DH, F32)).astype(BF16)
    w1, w3, w2 = ffn_w1, ffn_w3, ffn_w2
    w_head, w_tail = w_in[..., :P_HEAD].astype(BF16), w_in[..., P_HEAD:].astype(BF16)
    w_br, w_o = w_branch.astype(BF16), w_out.astype(BF16)

    for l in range(depth):
        ml = mods[l, :bsz]
        mc = mods[l, bsz]
        last = l == depth - 1

        xs = _ffn_call(xs, ml, mc, norm_g[l, 0][None, :], w1, w3, w2,
                       lead=(l, 0), base=0, seq=seq, out_rows=t)

        qg = jnp.tile(at_qn_g[l], AT_HEADS)[None, :]
        kg = jnp.tile(at_kn_g[l], AT_KV)[None, :]
        gb = jnp.concatenate([ml_gate_b[l], jnp.zeros((LANES - N_GATE,), F32)])[None, :]
        (rgx, mq, mk, mv, aq, akd, avd, gates, mg) = _proj_call(
            xs, ml, mc, norm_g[l, 1][None, :], w_head, w_tail, gmat, qg, kg, gb,
            cos, sin_a, sin_b, layer=l, seq=seq)

        wcat = jnp.stack([jnp.concatenate([_block_diag(rg_wa[l, dr]), _block_diag(rg_wi[l, dr])],
                                          axis=1) for dr in range(2)])
        wcat = (0.5 * wcat).astype(BF16)
        bias = 0.5 * jnp.concatenate([rg_ba[l], rg_bi[l]], axis=1)[:, None, :]
        clam = (-0.5 * LOG2E * LRU_C * jax.nn.softplus(-rg_lam[l]))[:, None, :]
        rhf, rhb = _rglru_call(rgx, rg_conv_w[l], rg_conv_b[l][None, :], wcat, bias, clam,
                               seq=seq)

        gt = jnp.swapaxes(mg[:, :, :N_GATE], 1, 2)
        rowp, colsrc = _mlgate_call(gt, seq=seq)
        colp = jnp.swapaxes(colsrc, 1, 2)
        mkt = jnp.swapaxes(mk, 1, 2)
        mhf, mhb = _mlstm_call(mq, mkt, mv, rowp, colp, seq=seq)

        yc = _attn_call(at_sink[l], aq, akd, avd, seq=seq)

        xs = _merge_call(xs, ml, mc, rhf, rhb, mhf, mhb, yc, gates, ml_norm_g[l][None, :],
                         w_br, w_o, norm_g[l, 2][None, :], w1, w3, w2,
                         layer=l, seq=seq, out_rows=seq if last else t)
    return xs
```

```python
import functools

import jax
import jax.numpy as jnp
from jax import lax
from jax.experimental import pallas as pl
from jax.experimental.pallas import tpu as pltpu

F32 = jnp.float32
BF16 = jnp.bfloat16

EPS = 1e-6
NEG = -1e30
TINY = 1e-36
LOG2E = 1.4426950408889634
N_MOD = 9
GRID_W = 64
ROPE_BASE = 10000.0

D_RNN = 512
RNN_BLOCKS = 8
RNN_BLOCK = D_RNN // RNN_BLOCKS
CONV_W = 4
CONV_LEFT = 2
CONV_RIGHT = CONV_W - 1 - CONV_LEFT
LRU_C = 8.0

ML_HEADS = 4
ML_DH = 128
ML_W = ML_HEADS * ML_DH
ML_CHUNK = 128
ML_UNITS = 2 * ML_HEADS

AT_HEADS = 8
AT_KV = 2
AT_DH = 64
AT_G = AT_HEADS // AT_KV
AT_W = AT_HEADS * AT_DH
AT_KVW = AT_KV * AT_DH
ATT_BLOCK = 128
ROPE_Q = AT_DH // 4

N_BRANCH = 3
BRANCH_W = 512

LANES = 128
SUBLANES = 8
VMEM_LIMIT = 56 * 1024 * 1024

P_RGX = 0
P_RGG = P_RGX + D_RNN
P_MQ = P_RGG + D_RNN
P_MK = P_MQ + ML_W
P_MV = P_MK + ML_W
P_MO = P_MV + ML_W
P_HEAD = P_MO + ML_W
N_GATE = 4 * ML_HEADS
T_AQ = 0
T_AK = T_AQ + AT_W
T_AV = T_AK + AT_KVW
T_BR = T_AV + AT_KVW
GATE_SEG = 512


def _cparams(sem):
    return pltpu.CompilerParams(dimension_semantics=sem, vmem_limit_bytes=VMEM_LIMIT)


def _const_spec(shape, lead=()):
    nd = len(shape)
    idx = tuple(lead) + (0,) * nd
    return pl.BlockSpec((None,) * len(lead) + tuple(shape), lambda *_: idx,
                        pipeline_mode=pl.Buffered(1))


def _pick_tile(total, candidates):
    for c in candidates:
        if total % c == 0:
            return c
    raise ValueError(f"no tile for {total}")


def _sigmoid(x):
    return jax.nn.sigmoid(x)


def _ln_mod(x, g, shift, scale):
    ms = jnp.mean(x * x, axis=-1, keepdims=True)
    return x * lax.rsqrt(ms + EPS) * g * (1.0 + scale) + shift


def _row_mod(ml_ref, mc_ref, is_ctx, i):
    return jnp.where(is_ctx, mc_ref[i:i + 1, :], ml_ref[i:i + 1, :])


def _ada_kernel(cc_ref, w_ref, b_ref, o_ref):
    cc = cc_ref[...]
    s = cc * _sigmoid(cc)
    o_ref[...] = jnp.dot(s.astype(BF16), w_ref[...].astype(BF16),
                         preferred_element_type=F32) + b_ref[...]


def _ada_call(cc, ada_w, ada_b):
    depth, d, nout = ada_w.shape
    rows = cc.shape[0]
    tn = _pick_tile(nout, (1536, 1024, 512, 256, 128))
    return pl.pallas_call(
        _ada_kernel,
        grid=(depth, nout // tn),
        in_specs=[
            pl.BlockSpec((rows, d), lambda l, j: (0, 0)),
            pl.BlockSpec((None, d, tn), lambda l, j: (l, 0, j)),
            pl.BlockSpec((None, 1, tn), lambda l, j: (l, 0, j)),
        ],
        out_specs=pl.BlockSpec((None, rows, tn), lambda l, j: (l, 0, j)),
        out_shape=jax.ShapeDtypeStruct((depth, rows, nout), F32),
        compiler_params=_cparams(("arbitrary", "arbitrary")),
        name="ada_mod",
    )(cc, ada_w, ada_b.reshape(depth, 1, nout))


def _ffn_body(x, ml_ref, mc_ref, is_ctx, g_ref, w1_ref, w3_ref, w2_ref, act_ref, *, base, ck,
              stream=None):
    h = _ln_mod(x, g_ref[...], _row_mod(ml_ref, mc_ref, is_ctx, base),
                _row_mod(ml_ref, mc_ref, is_ctx, base + 1)).astype(BF16)
    dff = w1_ref.shape[1]
    nck = dff // ck
    if stream is not None:
        stream.land(0)
        stream.refill(0)
    for j in range(nck):
        if stream is not None and j + 1 < nck:
            stream.land(j + 1)
        a = jnp.dot(h, w1_ref[:, j * ck:(j + 1) * ck], preferred_element_type=F32)
        b = jnp.dot(h, w3_ref[:, j * ck:(j + 1) * ck], preferred_element_type=F32)
        act_ref[:, j * ck:(j + 1) * ck] = (a * _sigmoid(a) * b).astype(BF16)
        if stream is not None and j + 1 < nck:
            stream.refill(j + 1)
    y = jnp.dot(act_ref[...], w2_ref[...], preferred_element_type=F32)
    return x + 0.5 * _row_mod(ml_ref, mc_ref, is_ctx, base + 2) * y


def _ffn_weight_scratch(d, dff, ck):
    return [pltpu.VMEM((d, dff), BF16), pltpu.VMEM((d, dff), BF16), pltpu.VMEM((dff, d), BF16),
            pltpu.VMEM((2, 2, d, ck), F32), pltpu.VMEM((2, ck, d), F32),
            pltpu.SemaphoreType.DMA((2, 3))]


class _FfnWeightStream:
    def __init__(self, hbm, lead, resident, stg_c, stg_r, sem, ck):
        self.hbm, self.lead, self.resident = hbm, lead, resident
        self.stg_c, self.stg_r, self.sem, self.ck = stg_c, stg_r, sem, ck
        self.nck = resident[0].shape[1] // ck

    def _stages(self, j):
        s = j % 2
        return (self.stg_c.at[s, 0], self.stg_c.at[s, 1], self.stg_r.at[s])

    def _copies(self, j):
        l0, l1 = self.lead
        sl = slice(j * self.ck, (j + 1) * self.ck)
        srcs = (self.hbm[0].at[l0, l1, :, sl], self.hbm[1].at[l0, l1, :, sl],
                self.hbm[2].at[l0, l1, sl, :])
        return [pltpu.make_async_copy(src, stg, self.sem.at[j % 2, i])
                for i, (src, stg) in enumerate(zip(srcs, self._stages(j)))]

    def start(self, j):
        for cp in self._copies(j):
            cp.start()

    def prime(self):
        self.start(0)
        if self.nck > 1:
            self.start(1)

    def land(self, j):
        sl = slice(j * self.ck, (j + 1) * self.ck)
        for cp in self._copies(j):
            cp.wait()
        w1b, w3b, w2b = self.resident
        s1, s3, s2 = self._stages(j)
        w1b[:, sl] = s1[...].astype(BF16)
        w3b[:, sl] = s3[...].astype(BF16)
        w2b[sl, :] = s2[...].astype(BF16)

    def refill(self, j):
        if j + 2 < self.nck:
            self.start(j + 2)


def _ffn_kernel(x_ref, ml_ref, mc_ref, g_ref, w1_hbm, w3_hbm, w2_hbm, o_ref, act_ref,
                w1b, w3b, w2b, stg_c, stg_r, sem, *, lead, base, seq, tm, ck):
    t0 = pl.program_id(1) * tm
    rows = t0 + lax.broadcasted_iota(jnp.int32, (tm, 1), 0)
    is_ctx = rows >= seq
    first = jnp.logical_and(pl.program_id(0) == 0, pl.program_id(1) == 0)

    @pl.when(first)
    def _():
        stream = _FfnWeightStream((w1_hbm, w3_hbm, w2_hbm), lead, (w1b, w3b, w2b),
                                  stg_c, stg_r, sem, ck)
        stream.prime()
        o_ref[...] = _ffn_body(x_ref[...], ml_ref, mc_ref, is_ctx, g_ref, w1b, w3b, w2b,
                               act_ref, base=base, ck=ck, stream=stream)

    @pl.when(jnp.logical_not(first))
    def _():
        o_ref[...] = _ffn_body(x_ref[...], ml_ref, mc_ref, is_ctx, g_ref, w1b, w3b, w2b,
                               act_ref, base=base, ck=ck)


def _ffn_call(x, ml, mc, g, w1, w3, w2, *, lead, base, seq, out_rows):
    bsz, t, d = x.shape
    dff = w1.shape[-1]
    tm = _pick_tile(t, (768, 384, 128))
    ck = _pick_tile(dff, (256, 128))
    kern = functools.partial(_ffn_kernel, lead=lead, base=base, seq=seq, tm=tm, ck=ck)
    hbm = pl.BlockSpec(memory_space=pl.ANY)
    return pl.pallas_call(
        kern,
        grid=(bsz, pl.cdiv(out_rows, tm)),
        in_specs=[
            pl.BlockSpec((None, tm, d), lambda b, i: (b, i, 0)),
            pl.BlockSpec((None, N_MOD, d), lambda b, i: (b, 0, 0)),
            _const_spec((N_MOD, d)),
            _const_spec((1, d)),
            hbm, hbm, hbm,
        ],
        out_specs=pl.BlockSpec((None, tm, d), lambda b, i: (b, i, 0)),
        out_shape=jax.ShapeDtypeStruct((bsz, out_rows, d), F32),
        scratch_shapes=[pltpu.VMEM((tm, dff), BF16)] + _ffn_weight_scratch(d, dff, ck),
        compiler_params=_cparams(("arbitrary", "arbitrary")),
        name="ffn",
    )(x, ml, mc, g, w1, w3, w2)


def _gelu_tanh(x):
    return 0.5 * x * (1.0 + jnp.tanh(0.7978845608028654 * (x + 0.044715 * (x * x * x))))


def _head_norm(x, gmat, g):
    ms = jnp.dot((x * x).astype(BF16), gmat, preferred_element_type=F32)
    return x * lax.rsqrt(ms + EPS) * g


def _dup_heads(x):
    low = lax.broadcasted_iota(jnp.int32, x.shape, 1) < AT_DH
    sw = pltpu.roll(x, AT_DH, 1)
    return jnp.concatenate([jnp.where(low, x, sw), jnp.where(low, sw, x)], axis=1)


def _rope(x, cos, sin_a, sin_b):
    parts = []
    for j in range(x.shape[1] // LANES):
        xj = x[:, j * LANES:(j + 1) * LANES]
        parts.append(xj * cos + pltpu.roll(xj, LANES - ROPE_Q, 1) * sin_a
                     + pltpu.roll(xj, ROPE_Q, 1) * sin_b)
    return jnp.concatenate(parts, axis=1)


def _proj_kernel(x_ref, ml_ref, mc_ref, g_ref, wh_ref, wt_ref, gmat_ref, qg_ref, kg_ref, gb_ref,
                 cos_ref, sa_ref, sb_ref,
                 rgx_o, rgg_o, mq_o, mk_o, mv_o, mo_o, aq_o, ak_o, av_o, br_o, mg_o,
                 tail, *, seq, tm, d_model):
    @pl.when(jnp.logical_and(pl.program_id(0) == 0, pl.program_id(1) == 0))
    def _():
        for r in range(0, d_model, LANES):
            tail[r:r + LANES, :] = wt_ref[r:r + LANES, N_GATE:]

    t0 = pl.program_id(1) * tm
    rows = t0 + lax.broadcasted_iota(jnp.int32, (tm, 1), 0)
    is_ctx = rows >= seq
    h = _ln_mod(x_ref[...], g_ref[...], _row_mod(ml_ref, mc_ref, is_ctx, 3),
                _row_mod(ml_ref, mc_ref, is_ctx, 4)).astype(BF16)

    def seg(c0, w, ref=wh_ref):
        return jnp.dot(h, ref[:, c0:c0 + w], preferred_element_type=F32)

    rgx_o[...] = seg(P_RGX, D_RNN).reshape(rgx_o.shape)
    rgg_o[...] = _gelu_tanh(seg(P_RGG, D_RNN)).astype(BF16)
    mq_o[...] = seg(P_MQ, ML_W).astype(BF16)
    mk_o[...] = (seg(P_MK, ML_W) * (ML_DH ** -0.5)).astype(BF16)
    mv_o[...] = seg(P_MV, ML_W).astype(BF16)
    mo_o[...] = _sigmoid(seg(P_MO, ML_W)).astype(BF16)
    cos, sin_a, sin_b = cos_ref[...], sa_ref[...], sb_ref[...]
    gmat = gmat_ref[...]
    q = _head_norm(seg(T_AQ, AT_W, tail), gmat, qg_ref[...])
    aq_o[...] = (_rope(q, cos, sin_a, sin_b) * (AT_DH ** -0.5 * LOG2E)).astype(BF16)
    k = _head_norm(seg(T_AK, AT_KVW, tail), gmat[:AT_KVW, :AT_KVW], kg_ref[...])
    ak_o[...] = _dup_heads(_rope(k, cos, sin_a, sin_b)).astype(BF16)
    av_o[...] = _dup_heads(seg(T_AV, AT_KVW, tail)).astype(BF16)
    for c0 in range(0, N_BRANCH * d_model, GATE_SEG):
        br_o[:, c0:c0 + GATE_SEG] = _sigmoid(seg(T_BR + c0, GATE_SEG, tail)).astype(BF16)
    mg_o[...] = seg(0, LANES, wt_ref) + gb_ref[...]


def _proj_call(x, ml, mc, g, wh, wt, gmat, qg, kg, gb, cos, sin_a, sin_b, *, layer, seq):
    bsz, t, d = x.shape
    ntail = wt.shape[-1] - N_GATE
    tm = _pick_tile(t, (384, 128))
    kern = functools.partial(_proj_kernel, seq=seq, tm=tm, d_model=d)
    widths = [(D_RNN, F32), (D_RNN, BF16), (ML_W, BF16), (ML_W, BF16), (ML_W, BF16),
              (ML_W, BF16), (AT_W, BF16), (2 * AT_KVW, BF16), (2 * AT_KVW, BF16),
              (N_BRANCH * d, BF16), (LANES, F32)]
    tab_spec = pl.BlockSpec((tm, LANES), lambda b, i: (i, 0))
    out_specs = [pl.BlockSpec((None, tm, wd), lambda b, i: (b, i, 0)) for wd, _ in widths]
    out_shape = [jax.ShapeDtypeStruct((bsz, t, wd), dt) for wd, dt in widths]
    out_specs[0] = pl.BlockSpec((tm // SUBLANES, None, SUBLANES, D_RNN), lambda b, i: (i, b, 0, 0))
    out_shape[0] = jax.ShapeDtypeStruct((t // SUBLANES, bsz, SUBLANES, D_RNN), F32)
    return pl.pallas_call(
        kern,
        grid=(bsz, t // tm),
        in_specs=[
            pl.BlockSpec((None, tm, d), lambda b, i: (b, i, 0)),
            pl.BlockSpec((None, N_MOD, d), lambda b, i: (b, 0, 0)),
            _const_spec((N_MOD, d)),
            _const_spec((1, d)),
            _const_spec((d, wh.shape[-1]), (layer,)),
            _const_spec((d, wt.shape[-1]), (layer,)),
            _const_spec((AT_W, AT_W)),
            _const_spec((1, AT_W)),
            _const_spec((1, AT_KVW)),
            _const_spec((1, LANES)),
            tab_spec, tab_spec, tab_spec,
        ],
        out_specs=out_specs,
        out_shape=out_shape,
        scratch_shapes=[pltpu.VMEM((d, ntail), BF16)],
        compiler_params=_cparams(("arbitrary", "arbitrary")),
        name="in_proj",
    )(x, ml, mc, g, wh, wt, gmat, qg, kg, gb, cos, sin_a, sin_b)


def _scan_chunk(s, rev, nl, nc):
    if rev:
        return jnp.where(s < nc, nl + nc - 1 - s, nl - 1 - (s - nc))
    return jnp.where(s < nc, nl + s, s - nc)


def _rglru_kernel(xf_ref, xfp_ref, xfn_ref, xb_ref, xbp_ref, xbn_ref, cw_ref, cb_ref, w_ref,
                  bias_ref, clam_ref, hf_ref, hb_ref, ext, a_s, b_s, h_s, hstate,
                  *, nl, nc, tc, sub):
    s = pl.program_id(0)
    bsz, c = hstate.shape[1], hstate.shape[2]
    ntb = tc // SUBLANES

    @pl.when(s == 0)
    def _():
        hstate[...] = jnp.zeros_like(hstate)

    cw = cw_ref[...]
    cb = cb_ref[...]
    for d, (x_ref, xp_ref, xn_ref) in enumerate(((xf_ref, xfp_ref, xfn_ref),
                                                (xb_ref, xbp_ref, xbn_ref))):
        chunk = _scan_chunk(s, d == 1, nl, nc)
        first = jnp.logical_or(chunk == 0, chunk == nl)
        last = jnp.logical_or(chunk == nl - 1, chunk == nl + nc - 1)
        prev = jnp.swapaxes(xp_ref[0], 0, 1)[SUBLANES - CONV_LEFT:]
        nxt = jnp.swapaxes(xn_ref[0], 0, 1)[:CONV_RIGHT]
        ext[d, 0:CONV_LEFT] = jnp.where(first, 0.0, prev)
        for i in range(ntb):
            o = CONV_LEFT + i * SUBLANES
            ext[d, o:o + SUBLANES] = jnp.swapaxes(x_ref[i], 0, 1)
        ext[d, CONV_LEFT + tc:CONV_LEFT + tc + CONV_RIGHT] = jnp.where(last, 0.0, nxt)

        def gates(i, carry, d=d):
            t0 = pl.multiple_of(i * sub, sub)
            u = cb
            for k in range(CONV_W):
                u = u + ext[d, pl.ds(t0 + k, sub)] * cw[k:k + 1, :]
            u2 = u.reshape(sub * bsz, c)
            z = jnp.dot(u2.astype(BF16), w_ref[d], preferred_element_type=F32) + bias_ref[d]
            a = jnp.exp2(clam_ref[d] * jnp.tanh(z[:, :c]) + clam_ref[d])
            gi = 0.5 * jnp.tanh(z[:, c:]) + 0.5
            s1 = 1.0 - a * a
            bb = (s1 * lax.rsqrt(jnp.maximum(s1, TINY))) * (gi * u2)
            a_s[d, pl.ds(t0, sub)] = a.reshape(sub, bsz, c)
            b_s[d, pl.ds(t0, sub)] = bb.reshape(sub, bsz, c)
            return carry

        lax.fori_loop(0, tc // sub, gates, 0)

    def step(k, hs):
        hf, hb = hs
        kb = tc - 1 - k
        hf = a_s[0, k] * hf + b_s[0, k]
        hb = a_s[1, kb] * hb + b_s[1, kb]
        h_s[0, k] = hf
        h_s[1, kb] = hb
        return hf, hb

    hf, hb = lax.fori_loop(0, tc, step, (hstate[0], hstate[1]), unroll=8)
    hstate[0] = hf
    hstate[1] = hb
    for d, o_ref in enumerate((hf_ref, hb_ref)):
        for i in range(ntb):
            o_ref[i] = jnp.swapaxes(h_s[d, i * SUBLANES:(i + 1) * SUBLANES], 0, 1)


def _rglru_call(rgx, cw, cb, w, bias, clam, *, seq):
    ntb_all, bsz, _, c = rgx.shape
    t = ntb_all * SUBLANES
    tc = 128
    sub = tc
    assert seq % tc == 0 and (t - seq) % tc == 0
    nl, nc = seq // tc, (t - seq) // tc
    ntb = tc // SUBLANES

    def specs(rev):
        chunk = functools.partial(_scan_chunk, rev=rev, nl=nl, nc=nc)
        return [
            pl.BlockSpec((ntb, bsz, SUBLANES, c), lambda s: (chunk(s), 0, 0, 0)),
            pl.BlockSpec((1, bsz, SUBLANES, c),
                         lambda s: (jnp.maximum(chunk(s) * ntb - 1, 0), 0, 0, 0)),
            pl.BlockSpec((1, bsz, SUBLANES, c),
                         lambda s: (jnp.minimum((chunk(s) + 1) * ntb, ntb_all - 1), 0, 0, 0)),
        ]

    kern = functools.partial(_rglru_kernel, nl=nl, nc=nc, tc=tc, sub=sub)
    fspec, bspec = specs(False), specs(True)
    return pl.pallas_call(
        kern,
        grid=(nl + nc,),
        in_specs=fspec + bspec + [
            _const_spec((CONV_W, c)),
            _const_spec((1, c)),
            _const_spec((2, c, 2 * c)),
            _const_spec((2, 1, 2 * c)),
            _const_spec((2, 1, c)),
        ],
        out_specs=[fspec[0], bspec[0]],
        out_shape=[jax.ShapeDtypeStruct(rgx.shape, F32)] * 2,
        scratch_shapes=[
            pltpu.VMEM((2, tc + CONV_W - 1, bsz, c), F32),
            pltpu.VMEM((2, tc, bsz, c), F32),
            pltpu.VMEM((2, tc, bsz, c), F32),
            pltpu.VMEM((2, tc, bsz, c), F32),
            pltpu.VMEM((2, bsz, c), F32),
        ],
        compiler_params=_cparams(("arbitrary",)),
        name="rglru",
    )(rgx, rgx, rgx, rgx, rgx, rgx, cw, cb, w, bias, clam)


def _log_sigmoid(x):
    return -(jnp.maximum(-x, 0.0) + jnp.log1p(jnp.exp(-jnp.abs(x))))


def _lane_scan(x, op, ident, rev):
    n = x.shape[1]
    lane = lax.broadcasted_iota(jnp.int32, x.shape, 1)
    sh = 1
    while sh < n:
        if rev:
            x = op(x, jnp.where(lane < n - sh, pltpu.roll(x, n - sh, 1), ident))
        else:
            x = op(x, jnp.where(lane >= sh, pltpu.roll(x, sh, 1), ident))
        sh *= 2
    return x


def _mlgate_kernel(g_ref, row_o, col_o, cb_s, gg_s, ml_s, tot_s, wm_s, mo_s, mn_s, *, nl, nc):
    L = ML_CHUNK
    H = ML_HEADS
    U = ML_UNITS
    is_bwd = lax.broadcasted_iota(jnp.int32, (U, L), 0) >= H
    for c in range(nl + nc):
        lanes = slice(c * L, (c + 1) * L)
        g16 = g_ref[:, lanes]
        li = jnp.concatenate([g16[0:H], g16[2 * H:3 * H]], axis=0)
        lf = _log_sigmoid(jnp.concatenate([g16[H:2 * H], g16[3 * H:4 * H]], axis=0))
        pre = _lane_scan(lf, jnp.add, 0.0, False)
        suf = _lane_scan(lf, jnp.add, 0.0, True)
        tot = pre + suf - lf
        cb = jnp.where(is_bwd, suf, pre)
        gg = li - cb
        pmax = _lane_scan(gg, jnp.maximum, NEG, False)
        smax = _lane_scan(gg, jnp.maximum, NEG, True)
        cb_s[:, lanes] = cb
        gg_s[:, lanes] = gg
        ml_s[:, lanes] = cb + jnp.where(is_bwd, smax, pmax)
        tot_s[:, lanes] = tot
        wm_s[:, lanes] = tot + jnp.maximum(pmax, smax)

    m = jnp.zeros((U, L), F32)
    for s in range(nl + nc):
        cf = nl + s if s < nc else s - nc
        cr = nl + nc - 1 - s if s < nc else nl - 1 - (s - nc)
        lf_, lr_ = slice(cf * L, (cf + 1) * L), slice(cr * L, (cr + 1) * L)
        tot = jnp.where(is_bwd, tot_s[:, lr_], tot_s[:, lf_])
        wm = jnp.where(is_bwd, wm_s[:, lr_], wm_s[:, lf_])
        m_new = jnp.maximum(tot + m, wm)
        mo_s[0:H, lf_] = m[0:H]
        mo_s[H:U, lr_] = m[H:U]
        mn_s[0:H, lf_] = m_new[0:H]
        mn_s[H:U, lr_] = m_new[H:U]
        m = m_new

    cb = cb_s[...]
    gg = gg_s[...]
    tot = tot_s[...]
    mo = mo_s[...]
    mn = mn_s[...]
    m_t = jnp.maximum(cb + mo, ml_s[...])
    row_o[0:U, :] = gg
    row_o[U:2 * U, :] = jnp.exp(tot + gg - mn)
    row_o[2 * U:3 * U, :] = jnp.exp(tot + mo - mn)
    row_o[3 * U:4 * U, :] = mo
    col_o[0:U, :] = cb - m_t
    col_o[U:2 * U, :] = jnp.exp(-m_t)


def _mlgate_call(gt, *, seq):
    bsz, ng, t = gt.shape
    nl, nc = seq // ML_CHUNK, (t - seq) // ML_CHUNK
    kern = functools.partial(_mlgate_kernel, nl=nl, nc=nc)
    rout = pl.BlockSpec((None, 4 * ML_UNITS, t), lambda b: (b, 0, 0))
    cout = pl.BlockSpec((None, 2 * ML_UNITS, t), lambda b: (b, 0, 0))
    return pl.pallas_call(
        kern,
        grid=(bsz,),
        in_specs=[pl.BlockSpec((None, ng, t), lambda b: (b, 0, 0))],
        out_specs=[rout, cout],
        out_shape=[jax.ShapeDtypeStruct((bsz, 4 * ML_UNITS, t), F32),
                   jax.ShapeDtypeStruct((bsz, 2 * ML_UNITS, t), F32)],
        scratch_shapes=[pltpu.VMEM((ML_UNITS, t), F32)] * 7,
        compiler_params=_cparams(("arbitrary",)),
        name="mlstm_gates",
    )(gt)


def _mlstm_kernel(qf_ref, ktf_ref, vf_ref, rf_ref, cf_ref, qb_ref, ktb_ref, vb_ref, rb_ref, cb_ref,
                  hf_ref, hb_ref, c_st):
    s = pl.program_id(1)

    @pl.when(s == 0)
    def _():
        c_st[...] = jnp.zeros_like(c_st)

    L, dh, U = ML_CHUNK, ML_DH, ML_UNITS
    ti = lax.broadcasted_iota(jnp.int32, (L, L), 0)
    si = lax.broadcasted_iota(jnp.int32, (L, L), 1)
    ones = jnp.ones((L, dh), BF16)
    for bb in range(qf_ref.shape[0]):
        for d, (q_ref, kt_ref, v_ref, r_ref, c_ref, o_ref) in enumerate(
                ((qf_ref, ktf_ref, vf_ref, rf_ref, cf_ref, hf_ref),
                 (qb_ref, ktb_ref, vb_ref, rb_ref, cb_ref, hb_ref))):
            tri = (si >= ti) if d == 1 else (si <= ti)
            rows = r_ref[bb]
            cols = c_ref[bb]
            for hd in range(ML_HEADS):
                u = d * ML_HEADS + hd
                sl = slice(hd * dh, (hd + 1) * dh)
                q = q_ref[bb, :, sl]
                kt = kt_ref[bb, sl, :]
                v_ext = jnp.concatenate([v_ref[bb, :, sl], ones], axis=1)
                g_row = rows[u:u + 1, :]
                w_row = rows[U + u:U + u + 1, :]
                a_row = rows[2 * U + u:2 * U + u + 1, :]
                m_row = rows[3 * U + u:3 * U + u + 1, :]
                xb = jnp.broadcast_to(cols[:, u:u + 1], (L, L))
                emt = cols[:, U + u:U + u + 1]
                c_old = c_st[bb * U + u]
                qkc = jnp.dot(q, jnp.concatenate([kt, c_old.astype(BF16)], axis=1),
                              preferred_element_type=F32)
                dm = jnp.exp(jnp.where(tri, xb + g_row, NEG))
                dec = jnp.exp(xb + m_row)
                sc = (qkc[:, :L] * dm).astype(BF16)
                ktw = (kt.astype(F32) * w_row).astype(BF16)
                sv = jnp.dot(jnp.concatenate([sc, ktw], axis=0), v_ext,
                             preferred_element_type=F32)
                ab = sv[:L] + jnp.concatenate([dec, dec], axis=1) * qkc[:, L:]
                o_ref[bb, :, sl] = ab[:, :dh] / jnp.maximum(jnp.abs(ab[:, dh:]), emt)
                c_st[bb * U + u] = jnp.concatenate([a_row, a_row], axis=1) * c_old + sv[L:]


def _mlstm_call(mq, mkt, mv, rowp, colp, *, seq):
    bsz, t, w = mq.shape
    nl, nc = seq // ML_CHUNK, (t - seq) // ML_CHUNK
    cf = functools.partial(_scan_chunk, rev=False, nl=nl, nc=nc)
    cr = functools.partial(_scan_chunk, rev=True, nl=nl, nc=nc)
    nr = rowp.shape[1]
    bpb = 8 if bsz % 8 == 0 else 1

    def tok(ch, wd):
        return pl.BlockSpec((bpb, ML_CHUNK, wd), lambda b, s: (b, ch(s), 0))

    def tr(ch, rows):
        return pl.BlockSpec((bpb, rows, ML_CHUNK), lambda b, s: (b, 0, ch(s)))

    def side(ch):
        return [tok(ch, w), tr(ch, w), tok(ch, w), tr(ch, nr), tok(ch, colp.shape[2])]

    return pl.pallas_call(
        _mlstm_kernel,
        grid=(bsz // bpb, nl + nc),
        in_specs=side(cf) + side(cr),
        out_specs=[tok(cf, w), tok(cr, w)],
        out_shape=[jax.ShapeDtypeStruct((bsz, t, w), F32)] * 2,
        scratch_shapes=[pltpu.VMEM((bpb * ML_UNITS, ML_DH, 2 * ML_DH), F32)],
        compiler_params=_cparams(("arbitrary", "arbitrary")),
        name="mlstm",
    )(mq, mkt, mv, rowp, colp, mq, mkt, mv, rowp, colp)


def _attn_kernel(sink_ref, q_ref, k_ref, v_ref, o_ref, vext, *, seq, qbs):
    kv = pl.program_id(1)
    i = pl.program_id(2)
    blk = ATT_BLOCK
    t = k_ref.shape[0]
    ctx = t - seq
    nlb = seq // blk
    band = 3 * blk

    @pl.when(i == 0)
    def _():
        vext[:, 0:LANES] = v_ref[...]
        vext[:, LANES:2 * LANES] = jnp.ones((t, LANES), BF16)

    lane = lax.broadcasted_iota(jnp.int32, (blk, LANES), 1)
    low = lane < AT_DH
    zero = jnp.zeros((blk, LANES), BF16)
    rows = AT_G * blk
    gi = lax.broadcasted_iota(jnp.int32, (rows, 1), 0) // blk
    snk = jnp.zeros((rows, 1), F32)
    for g in range(AT_G):
        snk = jnp.where(gi == g, sink_ref[kv * AT_G + g] * LOG2E, snk)
    ti = lax.broadcasted_iota(jnp.int32, (blk, band), 0)
    ci = lax.broadcasted_iota(jnp.int32, (blk, band), 1)
    nt = (((1,), (1,)), ((), ()))
    k_ctx = k_ref[seq:t, :]
    v_ctx = vext[seq:t, :]

    for j in range(qbs):
        qb = i * qbs + j
        s0 = pl.multiple_of(jnp.clip((qb - 1) * blk, 0, t - band), blk)
        rel = ci - ti + (s0 - qb * blk + blk)
        lim = jnp.where(qb < nlb, seq - s0, 0)
        ok = jnp.logical_and(jnp.logical_and(rel >= 0, rel <= 2 * blk), ci < lim)
        bias = jnp.where(ok, 0.0, NEG)
        bias = jnp.concatenate([bias] * AT_G, axis=0)

        qs = []
        for h2 in range(AT_G // 2):
            qj = q_ref[j * blk:(j + 1) * blk, h2 * LANES:(h2 + 1) * LANES]
            qs += [jnp.where(low, qj, zero), jnp.where(low, zero, qj)]
        q_all = jnp.concatenate(qs, axis=0)
        s_band = lax.dot_general(q_all, k_ref[pl.ds(s0, band), :], nt,
                                 preferred_element_type=F32) + bias
        s_ctx = lax.dot_general(q_all, k_ctx, nt, preferred_element_type=F32)
        m = jnp.maximum(jnp.maximum(jnp.max(s_band, axis=1, keepdims=True),
                                    jnp.max(s_ctx, axis=1, keepdims=True)), snk)
        p_band = jnp.exp2(s_band - m).astype(BF16)
        p_ctx = jnp.exp2(s_ctx - m).astype(BF16)
        o2 = (jnp.dot(p_band, vext[pl.ds(s0, band), :], preferred_element_type=F32)
              + jnp.dot(p_ctx, v_ctx, preferred_element_type=F32))
        o2 = o2[:, :LANES] / (o2[:, LANES:] + jnp.exp2(snk - m))
        for h2 in range(AT_G // 2):
            a = o2[(2 * h2) * blk:(2 * h2 + 1) * blk, :]
            b = o2[(2 * h2 + 1) * blk:(2 * h2 + 2) * blk, :]
            o_ref[j * blk:(j + 1) * blk, h2 * LANES:(h2 + 1) * LANES] = (
                jnp.where(low, a, b).astype(o_ref.dtype))


def _attn_call(sink, aq, akd, avd, *, seq):
    bsz, t, _ = aq.shape
    blk = ATT_BLOCK
    qbs = t // blk
    assert t % (qbs * blk) == 0 and t >= 3 * blk
    gw = AT_G * AT_DH
    kern = functools.partial(_attn_kernel, seq=seq, qbs=qbs)
    whole = pl.BlockSpec((None, t, LANES), lambda b, kv, i: (b, 0, kv))
    return pl.pallas_call(
        kern,
        grid=(bsz, AT_KV, t // (qbs * blk)),
        in_specs=[
            pl.BlockSpec(memory_space=pltpu.SMEM),
            pl.BlockSpec((None, qbs * blk, gw), lambda b, kv, i: (b, i, kv)),
            whole, whole,
        ],
        out_specs=pl.BlockSpec((None, qbs * blk, gw), lambda b, kv, i: (b, i, kv)),
        out_shape=jax.ShapeDtypeStruct((bsz, t, AT_W), BF16),
        scratch_shapes=[pltpu.VMEM((t, 2 * LANES), BF16)],
        compiler_params=_cparams(("arbitrary", "arbitrary", "arbitrary")),
        name="attn",
    )(sink, aq, akd, avd)


def _merge_kernel(x_ref, ml_ref, mc_ref, rf_ref, rb_ref, gg_ref, hf_ref, hb_ref, mo_ref, yc_ref,
                  br_ref, ng_ref, wb_ref, wo_ref, g_ref, w1_hbm, w3_hbm, w2_hbm, o_ref, act_ref,
                  w1b, w3b, w2b, stg_c, stg_r, sem, *, lead, seq, tm, ck):
    first = jnp.logical_and(pl.program_id(0) == 0, pl.program_id(1) == 0)
    stream = _FfnWeightStream((w1_hbm, w3_hbm, w2_hbm), lead, (w1b, w3b, w2b),
                              stg_c, stg_r, sem, ck)

    @pl.when(first)
    def _():
        stream.prime()

    t0 = pl.program_id(1) * tm
    rows = t0 + lax.broadcasted_iota(jnp.int32, (tm, 1), 0)
    is_ctx = rows >= seq
    d = x_ref.shape[1]
    rsum = (rf_ref[...] + rb_ref[...]).reshape(tm, D_RNN)
    ya = (rsum * gg_ref[...].astype(F32)).astype(BF16)
    hsum = hf_ref[...] + hb_ref[...]
    parts = []
    for hd in range(ML_HEADS):
        hh = hsum[:, hd * ML_DH:(hd + 1) * ML_DH]
        parts.append(hh * lax.rsqrt(jnp.mean(hh * hh, axis=-1, keepdims=True) + EPS))
    yb = (jnp.concatenate(parts, axis=1) * ng_ref[...] * mo_ref[...].astype(F32)).astype(BF16)
    m = (br_ref[:, 0:d].astype(F32) * jnp.dot(ya, wb_ref[0], preferred_element_type=F32)
         + br_ref[:, d:2 * d].astype(F32) * jnp.dot(yb, wb_ref[1], preferred_element_type=F32)
         + br_ref[:, 2 * d:3 * d].astype(F32) * jnp.dot(yc_ref[...], wb_ref[2],
                                                       preferred_element_type=F32))
    y = jnp.dot(m.astype(BF16), wo_ref[...], preferred_element_type=F32)
    x1 = x_ref[...] + _row_mod(ml_ref, mc_ref, is_ctx, 5) * y
    @pl.when(first)
    def _():
        o_ref[...] = _ffn_body(x1, ml_ref, mc_ref, is_ctx, g_ref, w1b, w3b, w2b, act_ref,
                               base=6, ck=ck, stream=stream)

    @pl.when(jnp.logical_not(first))
    def _():
        o_ref[...] = _ffn_body(x1, ml_ref, mc_ref, is_ctx, g_ref, w1b, w3b, w2b, act_ref,
                               base=6, ck=ck)


def _merge_call(x, ml, mc, rf, rb, gg, hf, hb, mo, yc, br, ng, wb, wo, g, w1, w3, w2,
                *, layer, seq, out_rows):
    bsz, t, d = x.shape
    dff = w1.shape[-1]
    tm = _pick_tile(t, (384, 128))
    ck = _pick_tile(dff, (256, 128))
    kern = functools.partial(_merge_kernel, lead=(layer, 1), seq=seq, tm=tm, ck=ck)
    hbm = pl.BlockSpec(memory_space=pl.ANY)

    def tok(wd):
        return pl.BlockSpec((None, tm, wd), lambda b, i: (b, i, 0))

    rspec = pl.BlockSpec((tm // SUBLANES, None, SUBLANES, D_RNN), lambda b, i: (i, b, 0, 0))

    return pl.pallas_call(
        kern,
        grid=(bsz, pl.cdiv(out_rows, tm)),
        in_specs=[
            tok(d),
            pl.BlockSpec((None, N_MOD, d), lambda b, i: (b, 0, 0)),
            _const_spec((N_MOD, d)),
            rspec, rspec, tok(D_RNN), tok(ML_W), tok(ML_W), tok(ML_W), tok(AT_W),
            tok(N_BRANCH * d),
            _const_spec((1, ML_W)),
            _const_spec((N_BRANCH, BRANCH_W, d), (layer,)),
            _const_spec((d, d), (layer,)),
            _const_spec((1, d)),
            hbm, hbm, hbm,
        ],
        out_specs=tok(d),
        out_shape=jax.ShapeDtypeStruct((bsz, out_rows, d), F32),
        scratch_shapes=[pltpu.VMEM((tm, dff), BF16)] + _ffn_weight_scratch(d, dff, ck),
        compiler_params=_cparams(("arbitrary", "arbitrary")),
        name="merge_ffn",
    )(x, ml, mc, rf, rb, gg, hf, hb, mo, yc, br, ng, wb, wo, g, w1, w3, w2)


def _block_diag(w):
    n, bi, bj = w.shape
    eye = jnp.eye(n, dtype=w.dtype)
    return (eye[:, None, :, None] * w[:, :, None, :]).reshape(n * bi, n * bj)


def _rope_tables(seq, t):
    rows = seq // GRID_W
    row = jnp.repeat(jnp.arange(rows), GRID_W).astype(F32)
    col = jnp.broadcast_to(jnp.arange(GRID_W), (rows, GRID_W)).reshape(-1).astype(F32)
    half = AT_DH // 2
    inv = ROPE_BASE ** (-jnp.arange(0, half, 2, dtype=F32) / half)
    ar = row[:, None] * inv
    ac = col[:, None] * inv
    ang = jnp.concatenate([ar, ar, ac, ac], axis=-1)
    cos = jnp.concatenate([jnp.cos(ang), jnp.ones((t - seq, AT_DH), F32)], axis=0)
    sin = jnp.concatenate([jnp.sin(ang), jnp.zeros((t - seq, AT_DH), F32)], axis=0)
    cos = jnp.tile(cos, (1, LANES // AT_DH))
    sin = jnp.tile(sin, (1, LANES // AT_DH))
    first_half = (jnp.arange(LANES) % (2 * ROPE_Q)) < ROPE_Q
    sin_a = jnp.where(first_half, -sin, 0.0)
    sin_b = jnp.where(first_half, 0.0, sin)
    return cos, sin_a, sin_b


def kernel(x, c, ctx, c_ctx, ada_w, ada_b, norm_g, ffn_w1, ffn_w3, ffn_w2, w_in, rg_conv_w,
           rg_conv_b, rg_wa, rg_ba, rg_wi, rg_bi, rg_lam, ml_gate_b, ml_norm_g, at_qn_g,
           at_kn_g, at_sink, w_branch, w_out):
    bsz, seq, d = x.shape
    nctx = ctx.shape[1]
    t = seq + nctx
    depth = ada_w.shape[0]

    xs = jnp.concatenate([x, ctx], axis=1)
    mod_rows = 2 * SUBLANES
    cc = jnp.concatenate([c, c_ctx[None, :], jnp.zeros((mod_rows - bsz - 1, d), F32)], axis=0)
    mods = _ada_call(cc, ada_w, ada_b).reshape(depth, mod_rows, N_MOD, d)
    cos, sin_a, sin_b = _rope_tables(seq, t)
    gmat = _block_diag(jnp.full((AT_HEADS, AT_DH, AT_DH), 1.0 / AT_DH, F32)).astype(BF16)
    w1, w3, w2 = ffn_w1, ffn_w3, ffn_w2
    w_head, w_tail = w_in[..., :P_HEAD].astype(BF16), w_in[..., P_HEAD:].astype(BF16)
    w_br, w_o = w_branch.astype(BF16), w_out.astype(BF16)

    for l in range(depth):
        ml = mods[l, :bsz]
        mc = mods[l, bsz]
        last = l == depth - 1

        xs = _ffn_call(xs, ml, mc, norm_g[l, 0][None, :], w1, w3, w2,
                       lead=(l, 0), base=0, seq=seq, out_rows=t)

        qg = jnp.tile(at_qn_g[l], AT_HEADS)[None, :]
        kg = jnp.tile(at_kn_g[l], AT_KV)[None, :]
        gb = jnp.concatenate([ml_gate_b[l], jnp.zeros((LANES - N_GATE,), F32)])[None, :]
        (rgx, rgg, mq, mk, mv, mo, aq, akd, avd, br, mg) = _proj_call(
            xs, ml, mc, norm_g[l, 1][None, :], w_head, w_tail, gmat, qg, kg, gb,
            cos, sin_a, sin_b, layer=l, seq=seq)

        wcat = jnp.stack([jnp.concatenate([_block_diag(rg_wa[l, dr]), _block_diag(rg_wi[l, dr])],
                                          axis=1) for dr in range(2)])
        wcat = (0.5 * wcat).astype(BF16)
        bias = 0.5 * jnp.concatenate([rg_ba[l], rg_bi[l]], axis=1)[:, None, :]
        clam = (-0.5 * LOG2E * LRU_C * jax.nn.softplus(-rg_lam[l]))[:, None, :]
        rhf, rhb = _rglru_call(rgx, rg_conv_w[l], rg_conv_b[l][None, :], wcat, bias, clam,
                               seq=seq)

        gt = jnp.swapaxes(mg[:, :, :N_GATE], 1, 2)
        rowp, colsrc = _mlgate_call(gt, seq=seq)
        colp = jnp.swapaxes(colsrc, 1, 2)
        mkt = jnp.swapaxes(mk, 1, 2)
        mhf, mhb = _mlstm_call(mq, mkt, mv, rowp, colp, seq=seq)

        yc = _attn_call(at_sink[l], aq, akd, avd, seq=seq)

        xs = _merge_call(xs, ml, mc, rhf, rhb, rgg, mhf, mhb, mo, yc, br, ml_norm_g[l][None, :],
                         w_br, w_o, norm_g[l, 2][None, :], w1, w3, w2,
                         layer=l, seq=seq, out_rows=seq if last else t)
    return xs
```

```python
import functools

import jax
import jax.numpy as jnp
from jax import lax
from jax.experimental import pallas as pl
from jax.experimental.pallas import tpu as pltpu

F32 = jnp.float32
BF16 = jnp.bfloat16

EPS = 1e-6
NEG = -1e30
TINY = 1e-36
LOG2E = 1.4426950408889634
N_MOD = 9
GRID_W = 64
ROPE_BASE = 10000.0

D_RNN = 512
RNN_BLOCKS = 8
RNN_BLOCK = D_RNN // RNN_BLOCKS
CONV_W = 4
CONV_LEFT = 2
CONV_RIGHT = CONV_W - 1 - CONV_LEFT
LRU_C = 8.0

ML_HEADS = 4
ML_DH = 128
ML_W = ML_HEADS * ML_DH
ML_CHUNK = 128
ML_UNITS = 2 * ML_HEADS

AT_HEADS = 8
AT_KV = 2
AT_DH = 64
AT_G = AT_HEADS // AT_KV
AT_W = AT_HEADS * AT_DH
AT_KVW = AT_KV * AT_DH
ATT_BLOCK = 128
ROPE_Q = AT_DH // 4

N_BRANCH = 3
BRANCH_W = 512

LANES = 128
SUBLANES = 8
VMEM_LIMIT = 56 * 1024 * 1024

P_RGX = 0
P_RGG = P_RGX + D_RNN
P_MQ = P_RGG + D_RNN
P_MK = P_MQ + ML_W
P_MV = P_MK + ML_W
P_MO = P_MV + ML_W
P_HEAD = P_MO + ML_W
N_GATE = 4 * ML_HEADS
T_AQ = 0
T_AK = T_AQ + AT_W
T_AV = T_AK + AT_KVW
T_BR = T_AV + AT_KVW
GATE_SEG = 512


def _cparams(sem):
    return pltpu.CompilerParams(dimension_semantics=sem, vmem_limit_bytes=VMEM_LIMIT)


def _const_spec(shape, lead=()):
    nd = len(shape)
    idx = tuple(lead) + (0,) * nd
    return pl.BlockSpec((None,) * len(lead) + tuple(shape), lambda *_: idx,
                        pipeline_mode=pl.Buffered(1))


def _pick_tile(total, candidates):
    for c in candidates:
        if total % c == 0:
            return c
    raise ValueError(f"no tile for {total}")


def _sigmoid(x):
    return jax.nn.sigmoid(x)


def _ln_mod(x, g, shift, scale):
    ms = jnp.mean(x * x, axis=-1, keepdims=True)
    return x * lax.rsqrt(ms + EPS) * g * (1.0 + scale) + shift


def _row_mod(ml_ref, mc_ref, is_ctx, i):
    return jnp.where(is_ctx, mc_ref[i:i + 1, :], ml_ref[i:i + 1, :])


def _ada_kernel(cc_ref, w_ref, b_ref, o_ref):
    cc = cc_ref[...]
    s = cc * _sigmoid(cc)
    o_ref[...] = jnp.dot(s.astype(BF16), w_ref[...].astype(BF16),
                         preferred_element_type=F32) + b_ref[...]


def _ada_call(cc, ada_w, ada_b):
    depth, d, nout = ada_w.shape
    rows = cc.shape[0]
    tn = _pick_tile(nout, (1536, 1024, 512, 256, 128))
    return pl.pallas_call(
        _ada_kernel,
        grid=(depth, nout // tn),
        in_specs=[
            pl.BlockSpec((rows, d), lambda l, j: (0, 0)),
            pl.BlockSpec((None, d, tn), lambda l, j: (l, 0, j)),
            pl.BlockSpec((None, 1, tn), lambda l, j: (l, 0, j)),
        ],
        out_specs=pl.BlockSpec((None, rows, tn), lambda l, j: (l, 0, j)),
        out_shape=jax.ShapeDtypeStruct((depth, rows, nout), F32),
        compiler_params=_cparams(("arbitrary", "arbitrary")),
        name="ada_mod",
    )(cc, ada_w, ada_b.reshape(depth, 1, nout))


def _ffn_body(x, ml_ref, mc_ref, is_ctx, g_ref, w1_ref, w3_ref, w2_ref, act_ref, *, base, ck,
              stream=None):
    h = _ln_mod(x, g_ref[...], _row_mod(ml_ref, mc_ref, is_ctx, base),
                _row_mod(ml_ref, mc_ref, is_ctx, base + 1)).astype(BF16)
    dff = w1_ref.shape[1]
    nck = dff // ck
    if stream is not None:
        stream.land(0)
        stream.refill(0)
    for j in range(nck):
        if stream is not None and j + 1 < nck:
            stream.land(j + 1)
        a = jnp.dot(h, w1_ref[:, j * ck:(j + 1) * ck], preferred_element_type=F32)
        b = jnp.dot(h, w3_ref[:, j * ck:(j + 1) * ck], preferred_element_type=F32)
        act_ref[:, j * ck:(j + 1) * ck] = (a * _sigmoid(a) * b).astype(BF16)
        if stream is not None and j + 1 < nck:
            stream.refill(j + 1)
    y = jnp.dot(act_ref[...], w2_ref[...], preferred_element_type=F32)
    return x + 0.5 * _row_mod(ml_ref, mc_ref, is_ctx, base + 2) * y


def _ffn_weight_scratch(d, dff, ck):
    return [pltpu.VMEM((d, dff), BF16), pltpu.VMEM((d, dff), BF16), pltpu.VMEM((dff, d), BF16),
            pltpu.VMEM((2, 2, d, ck), F32), pltpu.VMEM((2, ck, d), F32),
            pltpu.SemaphoreType.DMA((2, 3))]


class _FfnWeightStream:
    def __init__(self, hbm, lead, resident, stg_c, stg_r, sem, ck):
        self.hbm, self.lead, self.resident = hbm, lead, resident
        self.stg_c, self.stg_r, self.sem, self.ck = stg_c, stg_r, sem, ck
        self.nck = resident[0].shape[1] // ck

    def _stages(self, j):
        s = j % 2
        return (self.stg_c.at[s, 0], self.stg_c.at[s, 1], self.stg_r.at[s])

    def _copies(self, j):
        l0, l1 = self.lead
        sl = slice(j * self.ck, (j + 1) * self.ck)
        srcs = (self.hbm[0].at[l0, l1, :, sl], self.hbm[1].at[l0, l1, :, sl],
                self.hbm[2].at[l0, l1, sl, :])
        return [pltpu.make_async_copy(src, stg, self.sem.at[j % 2, i])
                for i, (src, stg) in enumerate(zip(srcs, self._stages(j)))]

    def start(self, j):
        for cp in self._copies(j):
            cp.start()

    def prime(self):
        self.start(0)
        if self.nck > 1:
            self.start(1)

    def land(self, j):
        sl = slice(j * self.ck, (j + 1) * self.ck)
        for cp in self._copies(j):
            cp.wait()
        w1b, w3b, w2b = self.resident
        s1, s3, s2 = self._stages(j)
        w1b[:, sl] = s1[...].astype(BF16)
        w3b[:, sl] = s3[...].astype(BF16)
        w2b[sl, :] = s2[...].astype(BF16)

    def refill(self, j):
        if j + 2 < self.nck:
            self.start(j + 2)


def _ffn_kernel(x_ref, *refs, lead, base, seq, tm, ck, split=False):
    if split:
        c_ref, *refs = refs
    (ml_ref, mc_ref, g_ref, w1_hbm, w3_hbm, w2_hbm, o_ref, act_ref,
     w1b, w3b, w2b, stg_c, stg_r, sem) = refs
    t0 = pl.program_id(1) * tm
    rows = t0 + lax.broadcasted_iota(jnp.int32, (tm, 1), 0)
    is_ctx = rows >= seq
    first = jnp.logical_and(pl.program_id(0) == 0, pl.program_id(1) == 0)

    def tile():
        if split:
            return jnp.where(t0 >= seq, c_ref[...], x_ref[...])
        return x_ref[...]

    @pl.when(first)
    def _():
        stream = _FfnWeightStream((w1_hbm, w3_hbm, w2_hbm), lead, (w1b, w3b, w2b),
                                  stg_c, stg_r, sem, ck)
        stream.prime()
        o_ref[...] = _ffn_body(tile(), ml_ref, mc_ref, is_ctx, g_ref, w1b, w3b, w2b,
                               act_ref, base=base, ck=ck, stream=stream)

    @pl.when(jnp.logical_not(first))
    def _():
        o_ref[...] = _ffn_body(tile(), ml_ref, mc_ref, is_ctx, g_ref, w1b, w3b, w2b,
                               act_ref, base=base, ck=ck)


def _ffn_call(x, ml, mc, g, w1, w3, w2, *, lead, base, seq, out_rows, ctx=None):
    bsz, t, d = x.shape
    dff = w1.shape[-1]
    ck = _pick_tile(dff, (256, 128))
    split = ctx is not None
    if split:
        tm = ctx.shape[1]
        assert seq == t and seq % tm == 0
        nlat = seq // tm
        x_specs = [pl.BlockSpec((None, tm, d), lambda b, i: (b, jnp.minimum(i, nlat - 1), 0)),
                   pl.BlockSpec((None, tm, d), lambda b, i: (b, 0, 0))]
        xs = (x, ctx)
    else:
        tm = _pick_tile(t, (768, 384, 128))
        x_specs = [pl.BlockSpec((None, tm, d), lambda b, i: (b, i, 0))]
        xs = (x,)
    kern = functools.partial(_ffn_kernel, lead=lead, base=base, seq=seq, tm=tm, ck=ck,
                             split=split)
    hbm = pl.BlockSpec(memory_space=pl.ANY)
    return pl.pallas_call(
        kern,
        grid=(bsz, pl.cdiv(out_rows, tm)),
        in_specs=x_specs + [
            pl.BlockSpec((None, N_MOD, d), lambda b, i: (b, 0, 0)),
            _const_spec((N_MOD, d)),
            _const_spec((1, d)),
            hbm, hbm, hbm,
        ],
        out_specs=pl.BlockSpec((None, tm, d), lambda b, i: (b, i, 0)),
        out_shape=jax.ShapeDtypeStruct((bsz, out_rows, d), F32),
        scratch_shapes=[pltpu.VMEM((tm, dff), BF16)] + _ffn_weight_scratch(d, dff, ck),
        compiler_params=_cparams(("arbitrary", "arbitrary")),
        name="ffn",
    )(*xs, ml, mc, g, w1, w3, w2)


def _gelu_tanh(x):
    return 0.5 * x * (1.0 + jnp.tanh(0.7978845608028654 * (x + 0.044715 * (x * x * x))))


def _head_norm(x, gmat, g):
    ms = jnp.dot((x * x).astype(BF16), gmat, preferred_element_type=F32)
    return x * lax.rsqrt(ms + EPS) * g


def _dup_heads(x):
    low = lax.broadcasted_iota(jnp.int32, x.shape, 1) < AT_DH
    sw = pltpu.roll(x, AT_DH, 1)
    return jnp.concatenate([jnp.where(low, x, sw), jnp.where(low, sw, x)], axis=1)


def _rope(x, cos, sin_a, sin_b):
    parts = []
    for j in range(x.shape[1] // LANES):
        xj = x[:, j * LANES:(j + 1) * LANES]
        parts.append(xj * cos + pltpu.roll(xj, LANES - ROPE_Q, 1) * sin_a
                     + pltpu.roll(xj, ROPE_Q, 1) * sin_b)
    return jnp.concatenate(parts, axis=1)


def _proj_kernel(x_ref, ml_ref, mc_ref, g_ref, wh_ref, wt_ref, gmat_ref, qg_ref, kg_ref, gb_ref,
                 cos_ref, sa_ref, sb_ref,
                 rgx_o, rgg_o, mq_o, mk_o, mv_o, mo_o, aq_o, ak_o, av_o, br_o, mg_o,
                 tail, *, seq, tm, d_model):
    @pl.when(jnp.logical_and(pl.program_id(0) == 0, pl.program_id(1) == 0))
    def _():
        for r in range(0, d_model, LANES):
            tail[r:r + LANES, :] = wt_ref[r:r + LANES, N_GATE:]

    t0 = pl.program_id(1) * tm
    rows = t0 + lax.broadcasted_iota(jnp.int32, (tm, 1), 0)
    is_ctx = rows >= seq
    h = _ln_mod(x_ref[...], g_ref[...], _row_mod(ml_ref, mc_ref, is_ctx, 3),
                _row_mod(ml_ref, mc_ref, is_ctx, 4)).astype(BF16)

    def seg(c0, w, ref=wh_ref):
        return jnp.dot(h, ref[:, c0:c0 + w], preferred_element_type=F32)

    rgx_o[...] = seg(P_RGX, D_RNN).reshape(rgx_o.shape)
    rgg_o[...] = _gelu_tanh(seg(P_RGG, D_RNN)).astype(BF16)
    mq_o[...] = seg(P_MQ, ML_W).astype(BF16)
    mk_o[...] = (seg(P_MK, ML_W) * (ML_DH ** -0.5)).astype(BF16)
    mv_o[...] = seg(P_MV, ML_W).astype(BF16)
    mo_o[...] = _sigmoid(seg(P_MO, ML_W)).astype(BF16)
    cos, sin_a, sin_b = cos_ref[...], sa_ref[...], sb_ref[...]
    gmat = gmat_ref[...]
    q = _head_norm(seg(T_AQ, AT_W, tail), gmat, qg_ref[...])
    aq_o[...] = (_rope(q, cos, sin_a, sin_b) * (AT_DH ** -0.5 * LOG2E)).astype(BF16)
    k = _head_norm(seg(T_AK, AT_KVW, tail), gmat[:AT_KVW, :AT_KVW], kg_ref[...])
    ak_o[...] = _dup_heads(_rope(k, cos, sin_a, sin_b)).astype(BF16)
    av_o[...] = _dup_heads(seg(T_AV, AT_KVW, tail)).astype(BF16)
    for c0 in range(0, N_BRANCH * d_model, GATE_SEG):
        br_o[:, c0:c0 + GATE_SEG] = _sigmoid(seg(T_BR + c0, GATE_SEG, tail)).astype(BF16)
    mg_o[...] = seg(0, LANES, wt_ref) + gb_ref[...]


def _proj_call(x, ml, mc, g, wh, wt, gmat, qg, kg, gb, cos, sin_a, sin_b, *, layer, seq):
    bsz, t, d = x.shape
    ntail = wt.shape[-1] - N_GATE
    tm = _pick_tile(t, (384, 128))
    kern = functools.partial(_proj_kernel, seq=seq, tm=tm, d_model=d)
    widths = [(D_RNN, F32), (D_RNN, BF16), (ML_W, BF16), (ML_W, BF16), (ML_W, BF16),
              (ML_W, BF16), (AT_W, BF16), (2 * AT_KVW, BF16), (2 * AT_KVW, BF16),
              (N_BRANCH * d, BF16), (LANES, F32)]
    tab_spec = pl.BlockSpec((tm, LANES), lambda b, i: (i, 0))
    out_specs = [pl.BlockSpec((None, tm, wd), lambda b, i: (b, i, 0)) for wd, _ in widths]
    out_shape = [jax.ShapeDtypeStruct((bsz, t, wd), dt) for wd, dt in widths]
    out_specs[0] = pl.BlockSpec((tm // SUBLANES, None, SUBLANES, D_RNN), lambda b, i: (i, b, 0, 0))
    out_shape[0] = jax.ShapeDtypeStruct((t // SUBLANES, bsz, SUBLANES, D_RNN), F32)
    return pl.pallas_call(
        kern,
        grid=(bsz, t // tm),
        in_specs=[
            pl.BlockSpec((None, tm, d), lambda b, i: (b, i, 0)),
            pl.BlockSpec((None, N_MOD, d), lambda b, i: (b, 0, 0)),
            _const_spec((N_MOD, d)),
            _const_spec((1, d)),
            _const_spec((d, wh.shape[-1]), (layer,)),
            _const_spec((d, wt.shape[-1]), (layer,)),
            _const_spec((AT_W, AT_W)),
            _const_spec((1, AT_W)),
            _const_spec((1, AT_KVW)),
            _const_spec((1, LANES)),
            tab_spec, tab_spec, tab_spec,
        ],
        out_specs=out_specs,
        out_shape=out_shape,
        scratch_shapes=[pltpu.VMEM((d, ntail), BF16)],
        compiler_params=_cparams(("arbitrary", "arbitrary")),
        name="in_proj",
    )(x, ml, mc, g, wh, wt, gmat, qg, kg, gb, cos, sin_a, sin_b)


def _scan_chunk(s, rev, nl, nc):
    if rev:
        return jnp.where(s < nc, nl + nc - 1 - s, nl - 1 - (s - nc))
    return jnp.where(s < nc, nl + s, s - nc)


def _rglru_kernel(xf_ref, xfp_ref, xfn_ref, xb_ref, xbp_ref, xbn_ref, cw_ref, cb_ref, w_ref,
                  bias_ref, clam_ref, hf_ref, hb_ref, ext, a_s, b_s, h_s, hstate,
                  *, nl, nc, tc, sub):
    s = pl.program_id(0)
    bsz, c = hstate.shape[1], hstate.shape[2]
    ntb = tc // SUBLANES

    @pl.when(s == 0)
    def _():
        hstate[...] = jnp.zeros_like(hstate)

    cw = cw_ref[...]
    cb = cb_ref[...]
    for d, (x_ref, xp_ref, xn_ref) in enumerate(((xf_ref, xfp_ref, xfn_ref),
                                                (xb_ref, xbp_ref, xbn_ref))):
        chunk = _scan_chunk(s, d == 1, nl, nc)
        first = jnp.logical_or(chunk == 0, chunk == nl)
        last = jnp.logical_or(chunk == nl - 1, chunk == nl + nc - 1)
        prev = jnp.swapaxes(xp_ref[0], 0, 1)[SUBLANES - CONV_LEFT:]
        nxt = jnp.swapaxes(xn_ref[0], 0, 1)[:CONV_RIGHT]
        ext[d, 0:CONV_LEFT] = jnp.where(first, 0.0, prev)
        for i in range(ntb):
            o = CONV_LEFT + i * SUBLANES
            ext[d, o:o + SUBLANES] = jnp.swapaxes(x_ref[i], 0, 1)
        ext[d, CONV_LEFT + tc:CONV_LEFT + tc + CONV_RIGHT] = jnp.where(last, 0.0, nxt)

        def gates(i, carry, d=d):
            t0 = pl.multiple_of(i * sub, sub)
            u = cb
            for k in range(CONV_W):
                u = u + ext[d, pl.ds(t0 + k, sub)] * cw[k:k + 1, :]
            u2 = u.reshape(sub * bsz, c)
            z = jnp.dot(u2.astype(BF16), w_ref[d], preferred_element_type=F32) + bias_ref[d]
            a = jnp.exp2(clam_ref[d] * jnp.tanh(z[:, :c]) + clam_ref[d])
            gi = 0.5 * jnp.tanh(z[:, c:]) + 0.5
            s1 = 1.0 - a * a
            bb = (s1 * lax.rsqrt(jnp.maximum(s1, TINY))) * (gi * u2)
            a_s[d, pl.ds(t0, sub)] = a.reshape(sub, bsz, c)
            b_s[d, pl.ds(t0, sub)] = bb.reshape(sub, bsz, c)
            return carry

        lax.fori_loop(0, tc // sub, gates, 0)

    def step(k, hs):
        hf, hb = hs
        kb = tc - 1 - k
        hf = a_s[0, k] * hf + b_s[0, k]
        hb = a_s[1, kb] * hb + b_s[1, kb]
        h_s[0, k] = hf
        h_s[1, kb] = hb
        return hf, hb

    hf, hb = lax.fori_loop(0, tc, step, (hstate[0], hstate[1]), unroll=8)
    hstate[0] = hf
    hstate[1] = hb
    for d, o_ref in enumerate((hf_ref, hb_ref)):
        for i in range(ntb):
            o_ref[i] = jnp.swapaxes(h_s[d, i * SUBLANES:(i + 1) * SUBLANES], 0, 1)


def _rglru_call(rgx, cw, cb, w, bias, clam, *, seq):
    ntb_all, bsz, _, c = rgx.shape
    t = ntb_all * SUBLANES
    tc = 128
    sub = tc
    assert seq % tc == 0 and (t - seq) % tc == 0
    nl, nc = seq // tc, (t - seq) // tc
    ntb = tc // SUBLANES

    def specs(rev):
        chunk = functools.partial(_scan_chunk, rev=rev, nl=nl, nc=nc)
        return [
            pl.BlockSpec((ntb, bsz, SUBLANES, c), lambda s: (chunk(s), 0, 0, 0)),
            pl.BlockSpec((1, bsz, SUBLANES, c),
                         lambda s: (jnp.maximum(chunk(s) * ntb - 1, 0), 0, 0, 0)),
            pl.BlockSpec((1, bsz, SUBLANES, c),
                         lambda s: (jnp.minimum((chunk(s) + 1) * ntb, ntb_all - 1), 0, 0, 0)),
        ]

    kern = functools.partial(_rglru_kernel, nl=nl, nc=nc, tc=tc, sub=sub)
    fspec, bspec = specs(False), specs(True)
    return pl.pallas_call(
        kern,
        grid=(nl + nc,),
        in_specs=fspec + bspec + [
            _const_spec((CONV_W, c)),
            _const_spec((1, c)),
            _const_spec((2, c, 2 * c)),
            _const_spec((2, 1, 2 * c)),
            _const_spec((2, 1, c)),
        ],
        out_specs=[fspec[0], bspec[0]],
        out_shape=[jax.ShapeDtypeStruct(rgx.shape, F32)] * 2,
        scratch_shapes=[
            pltpu.VMEM((2, tc + CONV_W - 1, bsz, c), F32),
            pltpu.VMEM((2, tc, bsz, c), F32),
            pltpu.VMEM((2, tc, bsz, c), F32),
            pltpu.VMEM((2, tc, bsz, c), F32),
            pltpu.VMEM((2, bsz, c), F32),
        ],
        compiler_params=_cparams(("arbitrary",)),
        name="rglru",
    )(rgx, rgx, rgx, rgx, rgx, rgx, cw, cb, w, bias, clam)


def _log_sigmoid(x):
    return -(jnp.maximum(-x, 0.0) + jnp.log1p(jnp.exp(-jnp.abs(x))))


def _lane_scan(x, op, ident, rev):
    n = x.shape[1]
    lane = lax.broadcasted_iota(jnp.int32, x.shape, 1)
    sh = 1
    while sh < n:
        if rev:
            x = op(x, jnp.where(lane < n - sh, pltpu.roll(x, n - sh, 1), ident))
        else:
            x = op(x, jnp.where(lane >= sh, pltpu.roll(x, sh, 1), ident))
        sh *= 2
    return x


def _mlgate_kernel(g_ref, row_o, col_o, cb_s, gg_s, ml_s, tot_s, wm_s, mo_s, mn_s, *, nl, nc):
    L = ML_CHUNK
    H = ML_HEADS
    U = ML_UNITS
    is_bwd = lax.broadcasted_iota(jnp.int32, (U, L), 0) >= H
    for c in range(nl + nc):
        lanes = slice(c * L, (c + 1) * L)
        g16 = g_ref[:, lanes]
        li = jnp.concatenate([g16[0:H], g16[2 * H:3 * H]], axis=0)
        lf = _log_sigmoid(jnp.concatenate([g16[H:2 * H], g16[3 * H:4 * H]], axis=0))
        pre = _lane_scan(lf, jnp.add, 0.0, False)
        suf = _lane_scan(lf, jnp.add, 0.0, True)
        tot = pre + suf - lf
        cb = jnp.where(is_bwd, suf, pre)
        gg = li - cb
        pmax = _lane_scan(gg, jnp.maximum, NEG, False)
        smax = _lane_scan(gg, jnp.maximum, NEG, True)
        cb_s[:, lanes] = cb
        gg_s[:, lanes] = gg
        ml_s[:, lanes] = cb + jnp.where(is_bwd, smax, pmax)
        tot_s[:, lanes] = tot
        wm_s[:, lanes] = tot + jnp.maximum(pmax, smax)

    m = jnp.zeros((U, L), F32)
    for s in range(nl + nc):
        cf = nl + s if s < nc else s - nc
        cr = nl + nc - 1 - s if s < nc else nl - 1 - (s - nc)
        lf_, lr_ = slice(cf * L, (cf + 1) * L), slice(cr * L, (cr + 1) * L)
        tot = jnp.where(is_bwd, tot_s[:, lr_], tot_s[:, lf_])
        wm = jnp.where(is_bwd, wm_s[:, lr_], wm_s[:, lf_])
        m_new = jnp.maximum(tot + m, wm)
        mo_s[0:H, lf_] = m[0:H]
        mo_s[H:U, lr_] = m[H:U]
        mn_s[0:H, lf_] = m_new[0:H]
        mn_s[H:U, lr_] = m_new[H:U]
        m = m_new

    cb = cb_s[...]
    gg = gg_s[...]
    tot = tot_s[...]
    mo = mo_s[...]
    mn = mn_s[...]
    m_t = jnp.maximum(cb + mo, ml_s[...])
    row_o[0:U, :] = gg
    row_o[U:2 * U, :] = jnp.exp(tot + gg - mn)
    row_o[2 * U:3 * U, :] = jnp.exp(tot + mo - mn)
    row_o[3 * U:4 * U, :] = mo
    col_o[0:U, :] = cb - m_t
    col_o[U:2 * U, :] = jnp.exp(-m_t)


def _mlgate_call(gt, *, seq):
    bsz, ng, t = gt.shape
    nl, nc = seq // ML_CHUNK, (t - seq) // ML_CHUNK
    kern = functools.partial(_mlgate_kernel, nl=nl, nc=nc)
    rout = pl.BlockSpec((None, 4 * ML_UNITS, t), lambda b: (b, 0, 0))
    cout = pl.BlockSpec((None, 2 * ML_UNITS, t), lambda b: (b, 0, 0))
    return pl.pallas_call(
        kern,
        grid=(bsz,),
        in_specs=[pl.BlockSpec((None, ng, t), lambda b: (b, 0, 0))],
        out_specs=[rout, cout],
        out_shape=[jax.ShapeDtypeStruct((bsz, 4 * ML_UNITS, t), F32),
                   jax.ShapeDtypeStruct((bsz, 2 * ML_UNITS, t), F32)],
        scratch_shapes=[pltpu.VMEM((ML_UNITS, t), F32)] * 7,
        compiler_params=_cparams(("arbitrary",)),
        name="mlstm_gates",
    )(gt)


def _mlstm_kernel(qf_ref, ktf_ref, vf_ref, rf_ref, cf_ref, qb_ref, ktb_ref, vb_ref, rb_ref, cb_ref,
                  hf_ref, hb_ref, c_st):
    s = pl.program_id(1)

    @pl.when(s == 0)
    def _():
        c_st[...] = jnp.zeros_like(c_st)

    L, dh, U = ML_CHUNK, ML_DH, ML_UNITS
    ti = lax.broadcasted_iota(jnp.int32, (L, L), 0)
    si = lax.broadcasted_iota(jnp.int32, (L, L), 1)
    ones = jnp.ones((L, dh), BF16)
    for bb in range(qf_ref.shape[0]):
        for d, (q_ref, kt_ref, v_ref, r_ref, c_ref, o_ref) in enumerate(
                ((qf_ref, ktf_ref, vf_ref, rf_ref, cf_ref, hf_ref),
                 (qb_ref, ktb_ref, vb_ref, rb_ref, cb_ref, hb_ref))):
            tri = (si >= ti) if d == 1 else (si <= ti)
            rows = r_ref[bb]
            cols = c_ref[bb]
            for hd in range(ML_HEADS):
                u = d * ML_HEADS + hd
                sl = slice(hd * dh, (hd + 1) * dh)
                q = q_ref[bb, :, sl]
                kt = kt_ref[bb, sl, :]
                v_ext = jnp.concatenate([v_ref[bb, :, sl], ones], axis=1)
                g_row = rows[u:u + 1, :]
                w_row = rows[U + u:U + u + 1, :]
                a_row = rows[2 * U + u:2 * U + u + 1, :]
                m_row = rows[3 * U + u:3 * U + u + 1, :]
                xb = jnp.broadcast_to(cols[:, u:u + 1], (L, L))
                emt = cols[:, U + u:U + u + 1]
                c_old = c_st[bb * U + u]
                qkc = jnp.dot(q, jnp.concatenate([kt, c_old.astype(BF16)], axis=1),
                              preferred_element_type=F32)
                dm = jnp.exp(jnp.where(tri, xb + g_row, NEG))
                dec = jnp.exp(xb + m_row)
                sc = (qkc[:, :L] * dm).astype(BF16)
                ktw = (kt.astype(F32) * w_row).astype(BF16)
                sv = jnp.dot(jnp.concatenate([sc, ktw], axis=0), v_ext,
                             preferred_element_type=F32)
                ab = sv[:L] + jnp.concatenate([dec, dec], axis=1) * qkc[:, L:]
                o_ref[bb, :, sl] = ab[:, :dh] / jnp.maximum(jnp.abs(ab[:, dh:]), emt)
                c_st[bb * U + u] = jnp.concatenate([a_row, a_row], axis=1) * c_old + sv[L:]


def _mlstm_call(mq, mkt, mv, rowp, colp, *, seq):
    bsz, t, w = mq.shape
    nl, nc = seq // ML_CHUNK, (t - seq) // ML_CHUNK
    cf = functools.partial(_scan_chunk, rev=False, nl=nl, nc=nc)
    cr = functools.partial(_scan_chunk, rev=True, nl=nl, nc=nc)
    nr = rowp.shape[1]
    bpb = 8 if bsz % 8 == 0 else 1

    def tok(ch, wd):
        return pl.BlockSpec((bpb, ML_CHUNK, wd), lambda b, s: (b, ch(s), 0))

    def tr(ch, rows):
        return pl.BlockSpec((bpb, rows, ML_CHUNK), lambda b, s: (b, 0, ch(s)))

    def side(ch):
        return [tok(ch, w), tr(ch, w), tok(ch, w), tr(ch, nr), tok(ch, colp.shape[2])]

    return pl.pallas_call(
        _mlstm_kernel,
        grid=(bsz // bpb, nl + nc),
        in_specs=side(cf) + side(cr),
        out_specs=[tok(cf, w), tok(cr, w)],
        out_shape=[jax.ShapeDtypeStruct((bsz, t, w), F32)] * 2,
        scratch_shapes=[pltpu.VMEM((bpb * ML_UNITS, ML_DH, 2 * ML_DH), F32)],
        compiler_params=_cparams(("arbitrary", "arbitrary")),
        name="mlstm",
    )(mq, mkt, mv, rowp, colp, mq, mkt, mv, rowp, colp)


def _attn_kernel(sink_ref, q_ref, k_ref, v_ref, o_ref, vext, *, seq, qbs):
    kv = pl.program_id(1)
    i = pl.program_id(2)
    blk = ATT_BLOCK
    t = k_ref.shape[0]
    ctx = t - seq
    nlb = seq // blk
    band = 3 * blk

    @pl.when(i == 0)
    def _():
        vext[:, 0:LANES] = v_ref[...]
        vext[:, LANES:2 * LANES] = jnp.ones((t, LANES), BF16)

    lane = lax.broadcasted_iota(jnp.int32, (blk, LANES), 1)
    low = lane < AT_DH
    zero = jnp.zeros((blk, LANES), BF16)
    rows = AT_G * blk
    gi = lax.broadcasted_iota(jnp.int32, (rows, 1), 0) // blk
    snk = jnp.zeros((rows, 1), F32)
    for g in range(AT_G):
        snk = jnp.where(gi == g, sink_ref[kv * AT_G + g] * LOG2E, snk)
    ti = lax.broadcasted_iota(jnp.int32, (blk, band), 0)
    ci = lax.broadcasted_iota(jnp.int32, (blk, band), 1)
    nt = (((1,), (1,)), ((), ()))
    k_ctx = k_ref[seq:t, :]
    v_ctx = vext[seq:t, :]

    for j in range(qbs):
        qb = i * qbs + j
        s0 = pl.multiple_of(jnp.clip((qb - 1) * blk, 0, t - band), blk)
        rel = ci - ti + (s0 - qb * blk + blk)
        lim = jnp.where(qb < nlb, seq - s0, 0)
        ok = jnp.logical_and(jnp.logical_and(rel >= 0, rel <= 2 * blk), ci < lim)
        bias = jnp.where(ok, 0.0, NEG)
        bias = jnp.concatenate([bias] * AT_G, axis=0)

        qs = []
        for h2 in range(AT_G // 2):
            qj = q_ref[j * blk:(j + 1) * blk, h2 * LANES:(h2 + 1) * LANES]
            qs += [jnp.where(low, qj, zero), jnp.where(low, zero, qj)]
        q_all = jnp.concatenate(qs, axis=0)
        s_band = lax.dot_general(q_all, k_ref[pl.ds(s0, band), :], nt,
                                 preferred_element_type=F32) + bias
        s_ctx = lax.dot_general(q_all, k_ctx, nt, preferred_element_type=F32)
        m = jnp.maximum(jnp.maximum(jnp.max(s_band, axis=1, keepdims=True),
                                    jnp.max(s_ctx, axis=1, keepdims=True)), snk)
        p_band = jnp.exp2(s_band - m).astype(BF16)
        p_ctx = jnp.exp2(s_ctx - m).astype(BF16)
        o2 = (jnp.dot(p_band, vext[pl.ds(s0, band), :], preferred_element_type=F32)
              + jnp.dot(p_ctx, v_ctx, preferred_element_type=F32))
        o2 = o2[:, :LANES] / (o2[:, LANES:] + jnp.exp2(snk - m))
        for h2 in range(AT_G // 2):
            a = o2[(2 * h2) * blk:(2 * h2 + 1) * blk, :]
            b = o2[(2 * h2 + 1) * blk:(2 * h2 + 2) * blk, :]
            o_ref[j * blk:(j + 1) * blk, h2 * LANES:(h2 + 1) * LANES] = (
                jnp.where(low, a, b).astype(o_ref.dtype))


def _attn_call(sink, aq, akd, avd, *, seq):
    bsz, t, _ = aq.shape
    blk = ATT_BLOCK
    qbs = t // blk
    assert t % (qbs * blk) == 0 and t >= 3 * blk
    gw = AT_G * AT_DH
    kern = functools.partial(_attn_kernel, seq=seq, qbs=qbs)
    whole = pl.BlockSpec((None, t, LANES), lambda b, kv, i: (b, 0, kv))
    return pl.pallas_call(
        kern,
        grid=(bsz, AT_KV, t // (qbs * blk)),
        in_specs=[
            pl.BlockSpec(memory_space=pltpu.SMEM),
            pl.BlockSpec((None, qbs * blk, gw), lambda b, kv, i: (b, i, kv)),
            whole, whole,
        ],
        out_specs=pl.BlockSpec((None, qbs * blk, gw), lambda b, kv, i: (b, i, kv)),
        out_shape=jax.ShapeDtypeStruct((bsz, t, AT_W), BF16),
        scratch_shapes=[pltpu.VMEM((t, 2 * LANES), BF16)],
        compiler_params=_cparams(("arbitrary", "arbitrary", "arbitrary")),
        name="attn",
    )(sink, aq, akd, avd)


def _merge_kernel(x_ref, ml_ref, mc_ref, rf_ref, rb_ref, gg_ref, hf_ref, hb_ref, mo_ref, yc_ref,
                  br_ref, ng_ref, wb_ref, wo_ref, g_ref, w1_hbm, w3_hbm, w2_hbm, o_ref, act_ref,
                  w1b, w3b, w2b, stg_c, stg_r, sem, *, lead, seq, tm, ck):
    first = jnp.logical_and(pl.program_id(0) == 0, pl.program_id(1) == 0)
    stream = _FfnWeightStream((w1_hbm, w3_hbm, w2_hbm), lead, (w1b, w3b, w2b),
                              stg_c, stg_r, sem, ck)

    @pl.when(first)
    def _():
        stream.prime()

    t0 = pl.program_id(1) * tm
    rows = t0 + lax.broadcasted_iota(jnp.int32, (tm, 1), 0)
    is_ctx = rows >= seq
    d = x_ref.shape[1]
    rsum = (rf_ref[...] + rb_ref[...]).reshape(tm, D_RNN)
    ya = (rsum * gg_ref[...].astype(F32)).astype(BF16)
    hsum = hf_ref[...] + hb_ref[...]
    parts = []
    for hd in range(ML_HEADS):
        hh = hsum[:, hd * ML_DH:(hd + 1) * ML_DH]
        parts.append(hh * lax.rsqrt(jnp.mean(hh * hh, axis=-1, keepdims=True) + EPS))
    yb = (jnp.concatenate(parts, axis=1) * ng_ref[...] * mo_ref[...].astype(F32)).astype(BF16)
    m = (br_ref[:, 0:d].astype(F32) * jnp.dot(ya, wb_ref[0], preferred_element_type=F32)
         + br_ref[:, d:2 * d].astype(F32) * jnp.dot(yb, wb_ref[1], preferred_element_type=F32)
         + br_ref[:, 2 * d:3 * d].astype(F32) * jnp.dot(yc_ref[...], wb_ref[2],
                                                       preferred_element_type=F32))
    y = jnp.dot(m.astype(BF16), wo_ref[...], preferred_element_type=F32)
    x1 = x_ref[...] + _row_mod(ml_ref, mc_ref, is_ctx, 5) * y
    @pl.when(first)
    def _():
        o_ref[...] = _ffn_body(x1, ml_ref, mc_ref, is_ctx, g_ref, w1b, w3b, w2b, act_ref,
                               base=6, ck=ck, stream=stream)

    @pl.when(jnp.logical_not(first))
    def _():
        o_ref[...] = _ffn_body(x1, ml_ref, mc_ref, is_ctx, g_ref, w1b, w3b, w2b, act_ref,
                               base=6, ck=ck)


def _merge_call(x, ml, mc, rf, rb, gg, hf, hb, mo, yc, br, ng, wb, wo, g, w1, w3, w2,
                *, layer, seq, out_rows):
    bsz, t, d = x.shape
    dff = w1.shape[-1]
    tm = _pick_tile(t, (384, 128))
    ck = _pick_tile(dff, (256, 128))
    kern = functools.partial(_merge_kernel, lead=(layer, 1), seq=seq, tm=tm, ck=ck)
    hbm = pl.BlockSpec(memory_space=pl.ANY)

    def tok(wd):
        return pl.BlockSpec((None, tm, wd), lambda b, i: (b, i, 0))

    rspec = pl.BlockSpec((tm // SUBLANES, None, SUBLANES, D_RNN), lambda b, i: (i, b, 0, 0))

    return pl.pallas_call(
        kern,
        grid=(bsz, pl.cdiv(out_rows, tm)),
        in_specs=[
            tok(d),
            pl.BlockSpec((None, N_MOD, d), lambda b, i: (b, 0, 0)),
            _const_spec((N_MOD, d)),
            rspec, rspec, tok(D_RNN), tok(ML_W), tok(ML_W), tok(ML_W), tok(AT_W),
            tok(N_BRANCH * d),
            _const_spec((1, ML_W)),
            _const_spec((N_BRANCH, BRANCH_W, d), (layer,)),
            _const_spec((d, d), (layer,)),
            _const_spec((1, d)),
            hbm, hbm, hbm,
        ],
        out_specs=tok(d),
        out_shape=jax.ShapeDtypeStruct((bsz, out_rows, d), F32),
        scratch_shapes=[pltpu.VMEM((tm, dff), BF16)] + _ffn_weight_scratch(d, dff, ck),
        compiler_params=_cparams(("arbitrary", "arbitrary")),
        name="merge_ffn",
    )(x, ml, mc, rf, rb, gg, hf, hb, mo, yc, br, ng, wb, wo, g, w1, w3, w2)


def _block_diag(w):
    n, bi, bj = w.shape
    eye = jnp.eye(n, dtype=w.dtype)
    return (eye[:, None, :, None] * w[:, :, None, :]).reshape(n * bi, n * bj)


def _rope_tables(seq, t):
    rows = seq // GRID_W
    row = jnp.repeat(jnp.arange(rows), GRID_W).astype(F32)
    col = jnp.broadcast_to(jnp.arange(GRID_W), (rows, GRID_W)).reshape(-1).astype(F32)
    half = AT_DH // 2
    inv = ROPE_BASE ** (-jnp.arange(0, half, 2, dtype=F32) / half)
    ar = row[:, None] * inv
    ac = col[:, None] * inv
    ang = jnp.concatenate([ar, ar, ac, ac], axis=-1)
    cos = jnp.concatenate([jnp.cos(ang), jnp.ones((t - seq, AT_DH), F32)], axis=0)
    sin = jnp.concatenate([jnp.sin(ang), jnp.zeros((t - seq, AT_DH), F32)], axis=0)
    cos = jnp.tile(cos, (1, LANES // AT_DH))
    sin = jnp.tile(sin, (1, LANES // AT_DH))
    first_half = (jnp.arange(LANES) % (2 * ROPE_Q)) < ROPE_Q
    sin_a = jnp.where(first_half, -sin, 0.0)
    sin_b = jnp.where(first_half, 0.0, sin)
    return cos, sin_a, sin_b


def kernel(x, c, ctx, c_ctx, ada_w, ada_b, norm_g, ffn_w1, ffn_w3, ffn_w2, w_in, rg_conv_w,
           rg_conv_b, rg_wa, rg_ba, rg_wi, rg_bi, rg_lam, ml_gate_b, ml_norm_g, at_qn_g,
           at_kn_g, at_sink, w_branch, w_out):
    bsz, seq, d = x.shape
    nctx = ctx.shape[1]
    t = seq + nctx
    depth = ada_w.shape[0]

    xs = x
    mod_rows = 2 * SUBLANES
    cc = jnp.concatenate([c, c_ctx[None, :], jnp.zeros((mod_rows - bsz - 1, d), F32)], axis=0)
    mods = _ada_call(cc, ada_w, ada_b).reshape(depth, mod_rows, N_MOD, d)
    cos, sin_a, sin_b = _rope_tables(seq, t)
    gmat = _block_diag(jnp.full((AT_HEADS, AT_DH, AT_DH), 1.0 / AT_DH, F32)).astype(BF16)
    w1, w3, w2 = ffn_w1, ffn_w3, ffn_w2
    w_head, w_tail = w_in[..., :P_HEAD].astype(BF16), w_in[..., P_HEAD:].astype(BF16)
    w_br, w_o = w_branch.astype(BF16), w_out.astype(BF16)

    for l in range(depth):
        ml = mods[l, :bsz]
        mc = mods[l, bsz]
        last = l == depth - 1

        xs = _ffn_call(xs, ml, mc, norm_g[l, 0][None, :], w1, w3, w2,
                       lead=(l, 0), base=0, seq=seq, out_rows=t, ctx=ctx if l == 0 else None)

        qg = jnp.tile(at_qn_g[l], AT_HEADS)[None, :]
        kg = jnp.tile(at_kn_g[l], AT_KV)[None, :]
        gb = jnp.concatenate([ml_gate_b[l], jnp.zeros((LANES - N_GATE,), F32)])[None, :]
        (rgx, rgg, mq, mk, mv, mo, aq, akd, avd, br, mg) = _proj_call(
            xs, ml, mc, norm_g[l, 1][None, :], w_head, w_tail, gmat, qg, kg, gb,
            cos, sin_a, sin_b, layer=l, seq=seq)

        wcat = jnp.stack([jnp.concatenate([_block_diag(rg_wa[l, dr]), _block_diag(rg_wi[l, dr])],
                                          axis=1) for dr in range(2)])
        wcat = (0.5 * wcat).astype(BF16)
        bias = 0.5 * jnp.concatenate([rg_ba[l], rg_bi[l]], axis=1)[:, None, :]
        clam = (-0.5 * LOG2E * LRU_C * jax.nn.softplus(-rg_lam[l]))[:, None, :]
        rhf, rhb = _rglru_call(rgx, rg_conv_w[l], rg_conv_b[l][None, :], wcat, bias, clam,
                               seq=seq)

        gt = jnp.swapaxes(mg[:, :, :N_GATE], 1, 2)
        rowp, colsrc = _mlgate_call(gt, seq=seq)
        colp = jnp.swapaxes(colsrc, 1, 2)
        mkt = jnp.swapaxes(mk, 1, 2)
        mhf, mhb = _mlstm_call(mq, mkt, mv, rowp, colp, seq=seq)

        yc = _attn_call(at_sink[l], aq, akd, avd, seq=seq)

        xs = _merge_call(xs, ml, mc, rhf, rhb, rgg, mhf, mhb, mo, yc, br, ml_norm_g[l][None, :],
                         w_br, w_o, norm_g[l, 2][None, :], w1, w3, w2,
                         layer=l, seq=seq, out_rows=seq if last else t)
    return xs
```
